```python
import math
import jax, jax.numpy as jnp
from jax import lax
import numpy as np

D_MODEL = 1024
BATCH = 8
SEQ = 2048
DEPTH = 1
DEC_BATCH = 8
DEC_SEQ = 64
PAST_LEN = 1024

CHUNK = 64
WINDOW = 128
WIN_CHUNKS = WINDOW // CHUNK
N_HEADS = 8
N_KV_HEADS = 2
HEAD_DIM = 64
Q_PER_KV = N_HEADS // N_KV_HEADS
ATTN_WIDTH = N_HEADS * HEAD_DIM
KV_WIDTH = N_KV_HEADS * HEAD_DIM
ROT_DIM = HEAD_DIM // 4
ROPE_THETA = 500000.0
SSM_WIDTH = D_MODEL // 2
SSM_GROUP = 16
SSM_GROUPS = SSM_WIDTH // SSM_GROUP
SSM_STATE = 64
PLE_DIM = 256
N_BRANCH = 2
IN_WIDTH = 2 * ATTN_WIDTH + 2 * KV_WIDTH + 2 * SSM_WIDTH + N_BRANCH * D_MODEL
EPS = 1e-6

kernel_name = "hybrid_swa_sink_s5_streaming_step"


def rmsnorm(x, gain):
    xf = x.astype(jnp.float32)
    y = xf * lax.rsqrt(jnp.mean(xf * xf, axis=-1, keepdims=True) + EPS) * gain.astype(jnp.float32)
    return y.astype(x.dtype)


def rope_partial(x, pos):
    half = ROT_DIM // 2
    inv = jnp.power(ROPE_THETA, -jnp.arange(half, dtype=jnp.float32) * 2.0 / ROT_DIM)
    ang = pos.astype(jnp.float32)[:, None] * inv[None, :]
    cos = jnp.cos(ang)[None, :, None, :]
    sin = jnp.sin(ang)[None, :, None, :]
    xf = x.astype(jnp.float32)
    x1, x2, rest = xf[..., :half], xf[..., half:ROT_DIM], xf[..., ROT_DIM:]
    out = jnp.concatenate([x1 * cos - x2 * sin, x2 * cos + x1 * sin, rest], axis=-1)
    return out.astype(x.dtype)


def sink_attention(q, k, v, valid, sinks):
    f32 = jnp.float32
    s = jnp.einsum('bnqhgd,bnkhd->bnhgqk', q.astype(f32), k.astype(f32)) * (HEAD_DIM ** -0.5)
    if valid is not None:
        s = jnp.where(valid[None, :, None, None, None, :], s, -jnp.inf)
    sink = sinks.astype(f32).reshape(N_KV_HEADS, Q_PER_KV)[None, None, :, :, None, None]
    m = jnp.maximum(jnp.max(s, axis=-1, keepdims=True), sink)
    e = jnp.exp(s - m)
    w = e / (jnp.sum(e, axis=-1, keepdims=True) + jnp.exp(sink - m))
    o = jnp.einsum('bnhgqk,bnkhd->bnqhgd', w, v.astype(f32))
    return o.astype(q.dtype)


def prompt_attention(q, k, v, sinks):
    b, t = q.shape[0], q.shape[1]
    nc = t // CHUNK
    pad = WIN_CHUNKS * CHUNK
    qc = q.reshape(b, nc, CHUNK, N_KV_HEADS, Q_PER_KV, HEAD_DIM)

    def band(x):
        xp = jnp.pad(x, ((0, 0), (pad, 0), (0, 0), (0, 0)))
        xc = xp.reshape(b, nc + WIN_CHUNKS, CHUNK, N_KV_HEADS, HEAD_DIM)
        return jnp.concatenate([xc[:, j:j + nc] for j in range(WIN_CHUNKS + 1)], axis=2)

    key_pos = (jnp.arange(nc)[:, None] - WIN_CHUNKS) * CHUNK + jnp.arange((WIN_CHUNKS + 1) * CHUNK)[None, :]
    o = sink_attention(qc, band(k), band(v), key_pos >= 0, sinks)
    return o.reshape(b, t, ATTN_WIDTH), k[:, -WINDOW:], v[:, -WINDOW:]


def sample_attention(q, k, v, cache_k, cache_v, sinks):
    b, s = q.shape[0], q.shape[1]
    kk = jnp.concatenate([cache_k.astype(k.dtype), k], axis=1)
    vv = jnp.concatenate([cache_v.astype(v.dtype), v], axis=1)
    qs = q.reshape(b, 1, s, N_KV_HEADS, Q_PER_KV, HEAD_DIM)
    o = sink_attention(qs, kk[:, None], vv[:, None], None, sinks)
    return o.reshape(b, s, ATTN_WIDTH), kk[:, -WINDOW:], vv[:, -WINDOW:]


def ssm_branch(u, h0_re, h0_im, a_re, a_im, log_dt, b_re, b_im, c_re, c_im, d_skip, w_glu):
    f32 = jnp.float32
    bsz, t, _ = u.shape
    lam = lax.complex(a_re.astype(f32), a_im.astype(f32))
    dt = jnp.exp(log_dt.astype(f32))[:, None]
    a_bar = jnp.exp(lam * dt)
    bmat = lax.complex(b_re.astype(f32), b_im.astype(f32))
    b_bar = ((a_bar - 1.0) / lam)[..., None] * bmat
    cmat = lax.complex(c_re.astype(f32), c_im.astype(f32))
    uf = u.astype(f32)
    ug = uf.reshape(bsz, t, SSM_GROUPS, SSM_GROUP).astype(jnp.complex64)
    bu = jnp.einsum('gpc,btgc->btgp', b_bar, ug)
    if h0_re is not None:
        h0 = lax.complex(h0_re.astype(f32), h0_im.astype(f32))
        bu = bu.at[:, 0].add(a_bar[None] * h0)
    a_seq = jnp.broadcast_to(a_bar, bu.shape)

    def combine(left, right):
        a1, b1 = left
        a2, b2 = right
        return a1 * a2, a2 * b1 + b2

    _, h = lax.associative_scan(combine, (a_seq, bu), axis=1)
    y = jnp.einsum('gcp,btgp->btgc', cmat, h).real.reshape(bsz, t, SSM_WIDTH)
    y = y + d_skip.astype(f32) * uf
    z = jax.nn.gelu(y)
    z = z * jax.nn.sigmoid(z @ w_glu.astype(f32))
    h_last = h[:, -1]
    return z.astype(u.dtype), h_last.real, h_last.imag


def trunk_layer(h, p, pos, attend, h0_re, h0_im, norm_gain, w_in, w_o_attn,
                ssm_a_re, ssm_a_im, ssm_log_dt, ssm_b_re, ssm_b_im, ssm_c_re, ssm_c_im,
                ssm_d, ssm_w_glu, w_o_ssm, w_out, w_ple_gate, w_ple_proj):
    b, t, _ = h.shape
    xn = rmsnorm(h, norm_gain)
    z = xn @ w_in
    o0 = ATTN_WIDTH
    o1 = o0 + KV_WIDTH
    o2 = o1 + KV_WIDTH
    o3 = o2 + ATTN_WIDTH
    o4 = o3 + SSM_WIDTH
    o5 = o4 + SSM_WIDTH
    o6 = o5 + D_MODEL
    q = rope_partial(z[..., :o0].reshape(b, t, N_HEADS, HEAD_DIM), pos)
    k = rope_partial(z[..., o0:o1].reshape(b, t, N_KV_HEADS, HEAD_DIM), pos)
    v = z[..., o1:o2].reshape(b, t, N_KV_HEADS, HEAD_DIM)
    z_attn = z[..., o2:o3]
    u = z[..., o3:o4]
    z_ssm = z[..., o4:o5]
    g_attn = z[..., o5:o6]
    g_ssm = z[..., o6:]

    attn, k_state, v_state = attend(q, k, v)
    ssm, s_re, s_im = ssm_branch(u, h0_re, h0_im, ssm_a_re, ssm_a_im, ssm_log_dt, ssm_b_re,
                                 ssm_b_im, ssm_c_re, ssm_c_im, ssm_d, ssm_w_glu)
    ya = (attn * jax.nn.silu(z_attn)) @ w_o_attn
    ys = (ssm * jax.nn.silu(z_ssm)) @ w_o_ssm
    merged = jax.nn.sigmoid(g_attn) * ya + jax.nn.sigmoid(g_ssm) * ys
    h = h + merged @ w_out
    h = h + jax.nn.sigmoid(h @ w_ple_gate) * (p @ w_ple_proj)
    return h, k_state, v_state, s_re, s_im


def setup_inputs(seed: int = 0) -> dict:
    key = jax.random.key(seed)
    ks = jax.random.split(key, 32)
    nrm = jax.random.normal
    f32 = jnp.float32
    d = {}
    d["x_prompt"] = nrm(ks[0], (BATCH, SEQ, D_MODEL), f32)
    d["x_sample"] = nrm(ks[1], (DEC_BATCH, DEC_SEQ, D_MODEL), f32)
    d["p_prompt"] = nrm(ks[2], (DEPTH, BATCH, SEQ, PLE_DIM), f32)
    d["p_sample"] = nrm(ks[3], (DEPTH, DEC_BATCH, DEC_SEQ, PLE_DIM), f32)
    d["cache_attn_k"] = nrm(ks[4], (DEPTH, DEC_BATCH, WINDOW, N_KV_HEADS, HEAD_DIM), f32)
    d["cache_attn_v"] = nrm(ks[5], (DEPTH, DEC_BATCH, WINDOW, N_KV_HEADS, HEAD_DIM), f32)
    d["state_ssm_re"] = 0.1 * nrm(ks[6], (DEPTH, DEC_BATCH, SSM_GROUPS, SSM_STATE), f32)
    d["state_ssm_im"] = 0.1 * nrm(ks[7], (DEPTH, DEC_BATCH, SSM_GROUPS, SSM_STATE), f32)
    d["norm_gain"] = 1.0 + 0.02 * nrm(ks[8], (DEPTH, D_MODEL), f32)
    d["w_in"] = nrm(ks[9], (DEPTH, D_MODEL, IN_WIDTH), f32) * D_MODEL ** -0.5
    d["attn_sinks"] = 0.5 * nrm(ks[10], (DEPTH, N_HEADS), f32)
    d["w_o_attn"] = nrm(ks[11], (DEPTH, ATTN_WIDTH, D_MODEL), f32) * ATTN_WIDTH ** -0.5
    d["ssm_a_re"] = -0.5 + 0.01 * nrm(ks[12], (DEPTH, SSM_GROUPS, SSM_STATE), f32)
    d["ssm_a_im"] = (math.pi * jnp.arange(SSM_STATE, dtype=f32))[None, None, :] + 0.01 * nrm(ks[13], (DEPTH, SSM_GROUPS, SSM_STATE), f32)
    d["ssm_log_dt"] = jax.random.uniform(ks[14], (DEPTH, SSM_GROUPS), f32, math.log(1e-3), math.log(1e-1))
    d["ssm_b_re"] = nrm(ks[15], (DEPTH, SSM_GROUPS, SSM_STATE, SSM_GROUP), f32) * (2 * SSM_GROUP) ** -0.5
    d["ssm_b_im"] = nrm(ks[16], (DEPTH, SSM_GROUPS, SSM_STATE, SSM_GROUP), f32) * (2 * SSM_GROUP) ** -0.5
    d["ssm_c_re"] = nrm(ks[17], (DEPTH, SSM_GROUPS, SSM_GROUP, SSM_STATE), f32) * SSM_STATE ** -0.5
    d["ssm_c_im"] = nrm(ks[18], (DEPTH, SSM_GROUPS, SSM_GROUP, SSM_STATE), f32) * SSM_STATE ** -0.5
    d["ssm_d"] = nrm(ks[19], (DEPTH, SSM_WIDTH), f32)
    d["ssm_w_glu"] = nrm(ks[20], (DEPTH, SSM_WIDTH, SSM_WIDTH), f32) * SSM_WIDTH ** -0.5
    d["w_o_ssm"] = nrm(ks[21], (DEPTH, SSM_WIDTH, D_MODEL), f32) * SSM_WIDTH ** -0.5
    d["w_out"] = nrm(ks[22], (DEPTH, D_MODEL, D_MODEL), f32) * D_MODEL ** -0.5
    d["w_ple_gate"] = nrm(ks[23], (DEPTH, D_MODEL, D_MODEL), f32) * D_MODEL ** -0.5
    d["w_ple_proj"] = nrm(ks[24], (DEPTH, PLE_DIM, D_MODEL), f32) * PLE_DIM ** -0.5
    d["final_norm_gain"] = 1.0 + 0.02 * nrm(ks[25], (D_MODEL,), f32)
    return d


def reference(x_prompt, x_sample, p_prompt, p_sample, cache_attn_k, cache_attn_v,
              state_ssm_re, state_ssm_im, norm_gain, w_in, attn_sinks, w_o_attn,
              ssm_a_re, ssm_a_im, ssm_log_dt, ssm_b_re, ssm_b_im, ssm_c_re, ssm_c_im,
              ssm_d, ssm_w_glu, w_o_ssm, w_out, w_ple_gate, w_ple_proj, final_norm_gain):
    pos_p = jnp.arange(x_prompt.shape[1])
    pos_s = PAST_LEN + jnp.arange(x_sample.shape[1])
    h_p, h_s = x_prompt, x_sample
    kp_l, vp_l, rp_l, ip_l, ks_l, vs_l, rs_l, is_l = [], [], [], [], [], [], [], []
    for i in range(DEPTH):
        lw = (norm_gain[i], w_in[i], w_o_attn[i], ssm_a_re[i], ssm_a_im[i], ssm_log_dt[i],
              ssm_b_re[i], ssm_b_im[i], ssm_c_re[i], ssm_c_im[i], ssm_d[i], ssm_w_glu[i],
              w_o_ssm[i], w_out[i], w_ple_gate[i], w_ple_proj[i])
        sinks = attn_sinks[i]
        ck, cv = cache_attn_k[i], cache_attn_v[i]
        h_p, kp, vp, rp, ip = trunk_layer(
            h_p, p_prompt[i], pos_p,
            lambda q, k, v: prompt_attention(q, k, v, sinks),
            None, None, *lw)
        h_s, ks_, vs_, rs_, is_ = trunk_layer(
            h_s, p_sample[i], pos_s,
            lambda q, k, v: sample_attention(q, k, v, ck, cv, sinks),
            state_ssm_re[i], state_ssm_im[i], *lw)
        kp_l.append(kp); vp_l.append(vp); rp_l.append(rp); ip_l.append(ip)
        ks_l.append(ks_); vs_l.append(vs_); rs_l.append(rs_); is_l.append(is_)
    y_prompt = rmsnorm(h_p, final_norm_gain)
    y_sample = rmsnorm(h_s, final_norm_gain)
    new_k_prompt = jnp.stack(kp_l)
    new_v_prompt = jnp.stack(vp_l)
    new_ssm_re_prompt = jnp.stack(rp_l)
    new_ssm_im_prompt = jnp.stack(ip_l)
    new_k_sample = jnp.stack(ks_l)
    new_v_sample = jnp.stack(vs_l)
    new_ssm_re_sample = jnp.stack(rs_l)
    new_ssm_im_sample = jnp.stack(is_l)
    return (y_prompt, y_sample, new_k_prompt, new_v_prompt, new_ssm_re_prompt, new_ssm_im_prompt,
            new_k_sample, new_v_sample, new_ssm_re_sample, new_ssm_im_sample)
```

```python
import functools

import numpy as np
import jax
import jax.numpy as jnp
from jax import lax
from jax.experimental import pallas as pl
from jax.experimental.pallas import tpu as pltpu

F32 = jnp.float32
BF16 = jnp.bfloat16

LANES = 128
SUBLANES = 8
V7X_VMEM_BYTES = 64 * 1024 * 1024

D_MODEL = 1024
CHUNK = 64
WINDOW = 128
N_HEADS = 8
N_KV_HEADS = 2
HEAD_DIM = 64
Q_PER_KV = N_HEADS // N_KV_HEADS
ATTN_WIDTH = N_HEADS * HEAD_DIM
KV_WIDTH = N_KV_HEADS * HEAD_DIM
ROT_DIM = HEAD_DIM // 4
ROPE_THETA = 500000.0
SSM_WIDTH = D_MODEL // 2
SSM_GROUP = 16
SSM_GROUPS = SSM_WIDTH // SSM_GROUP
SSM_STATE = 64
PLE_DIM = 256
PAST_LEN = 1024
EPS = 1e-6

O_Q = 0
O_K = O_Q + ATTN_WIDTH
O_V = O_K + KV_WIDTH
O_ZA = O_V + KV_WIDTH
O_U = O_ZA + ATTN_WIDTH
O_ZS = O_U + SSM_WIDTH
O_GA = O_ZS + SSM_WIDTH
O_GS = O_GA + D_MODEL
IN_WIDTH = O_GS + D_MODEL

KEYS = WINDOW + CHUNK
LAGS = SUBLANES
PAIRS = SSM_GROUPS // 2
PAIR_K = 2 * LAGS * SSM_GROUP
PAIR_N = 2 * 2 * SSM_STATE
U_TILES = SSM_WIDTH // LANES
PAIRS_PER_TILE = PAIRS // U_TILES
SLOT = 2 * SSM_GROUP

PROJ_ROWS = 256
ATTN_ROWS = 512
SSM_ROWS = 256
OUT_ROWS = 256


def _sigmoid(x):
    return 1.0 / (1.0 + jnp.exp(-x))


def _const_spec(shape):
    zeros = (0,) * len(shape)
    return pl.BlockSpec(shape, lambda *_: zeros, pipeline_mode=pl.Buffered(1))


def _params(vmem_bytes, n_grid):
    return pltpu.CompilerParams(
        dimension_semantics=("arbitrary",) * n_grid,
        vmem_limit_bytes=min(int(vmem_bytes), V7X_VMEM_BYTES - 8 * 1024 * 1024),
    )


def _proj_kernel(x_ref, gain_ref, cos_ref, sina_ref, sinb_ref, w_ref,
                 q_ref, k_ref, v_ref, sa_ref, u_ref, sz_ref, ga_ref, gs_ref):
    x = x_ref[...]
    ms = jnp.mean(x * x, axis=-1, keepdims=True)
    xn = (x * lax.rsqrt(ms + EPS) * gain_ref[...]).astype(BF16)

    def seg(a, b):
        return jnp.dot(xn, w_ref[:, a:b], preferred_element_type=F32)

    cos, sina, sinb = cos_ref[...], sina_ref[...], sinb_ref[...]

    def rope(t):
        return (t * cos + pltpu.roll(t, ROT_DIM // 2, 1) * sina
                + pltpu.roll(t, LANES - ROT_DIM // 2, 1) * sinb)

    zq = seg(O_Q, O_K)
    for j in range(ATTN_WIDTH // LANES):
        sl = slice(j * LANES, (j + 1) * LANES)
        q_ref[:, sl] = (rope(zq[:, sl]) * (HEAD_DIM ** -0.5)).astype(BF16)
    k_ref[...] = rope(seg(O_K, O_V))
    v_ref[...] = seg(O_V, O_ZA)
    za = seg(O_ZA, O_U)
    sa_ref[...] = za * _sigmoid(za)
    u_ref[...] = seg(O_U, O_ZS)
    zs = seg(O_ZS, O_GA)
    sz_ref[...] = zs * _sigmoid(zs)
    ga_ref[...] = _sigmoid(seg(O_GA, O_GS))
    gs_ref[...] = _sigmoid(seg(O_GS, IN_WIDTH))


def _proj(x2d, gain, cos, sina, sinb, w_in, seq):
    n = x2d.shape[0]
    tm = min(PROJ_ROWS, n)
    tab_rows = cos.shape[0]
    tab_tiles = tab_rows // tm
    assert n % tm == 0 and tab_rows % tm == 0

    def row_spec(w):
        return pl.BlockSpec((tm, w), lambda i: (i, 0))

    tab_spec = pl.BlockSpec((tm, LANES), lambda i: (i % tab_tiles, 0))
    widths = (ATTN_WIDTH, KV_WIDTH, KV_WIDTH, ATTN_WIDTH, SSM_WIDTH, SSM_WIDTH, D_MODEL, D_MODEL)
    dtypes = (BF16,) + (F32,) * 7
    vmem = (2 * tm * D_MODEL * 4 + D_MODEL * IN_WIDTH * 2 + 3 * 2 * tm * LANES * 4
            + 3 * tm * IN_WIDTH * 4 + tm * D_MODEL * 8)
    return pl.pallas_call(
        _proj_kernel,
        grid=(n // tm,),
        in_specs=[row_spec(D_MODEL), _const_spec((1, D_MODEL)), tab_spec, tab_spec, tab_spec,
                  _const_spec((D_MODEL, IN_WIDTH))],
        out_specs=[row_spec(w) for w in widths],
        out_shape=[jax.ShapeDtypeStruct((n, w), d) for w, d in zip(widths, dtypes)],
        compiler_params=_params(vmem, 1),
        name=f"proj_{seq}",
    )(x2d, gain, cos, sina, sinb, w_in)


def _attn_kernel(sinks_ref, q_ref, k_ref, v_ref, sa_ref, o_ref, *, tq, mask_prefix):
    t = pl.program_id(1)
    lo = lax.broadcasted_iota(jnp.int32, (KEYS, LANES), 1) < HEAD_DIM
    top = lax.broadcasted_iota(jnp.int32, (2 * CHUNK, 1), 0) < CHUNK
    nt = (((1,), (1,)), ((), ()))

    for c in range(tq // CHUNK):
        rows = slice(c * CHUNK, (c + 1) * CHUNK)
        row0 = pl.multiple_of(t * tq + c * CHUNK, CHUNK)
        kk = k_ref[0, pl.ds(row0, KEYS), :]
        vv = v_ref[0, pl.ds(row0, KEYS), :]
        kr = pltpu.roll(kk, HEAD_DIM, 1)
        vr = pltpu.roll(vv, HEAD_DIM, 1)
        if mask_prefix:
            valid = row0 + lax.broadcasted_iota(jnp.int32, (1, KEYS), 1) >= WINDOW

        for kv in range(N_KV_HEADS):
            c0 = kv * Q_PER_KV * HEAD_DIM
            qs = jnp.concatenate([q_ref[0, rows, c0:c0 + LANES],
                                  q_ref[0, rows, c0 + LANES:c0 + 2 * LANES]], axis=0)
            k_lo, k_hi = (kk, kr) if kv == 0 else (kr, kk)
            v_lo, v_hi = (vv, vr) if kv == 0 else (vr, vv)

            def head(ksrc, vsrc, low, sink_top, sink_bot):
                kx = jnp.where(lo == low, ksrc, 0.0).astype(BF16)
                vx = jnp.where(lo == low, vsrc, 0.0).astype(BF16)
                s = lax.dot_general(qs, kx, nt, preferred_element_type=F32)
                if mask_prefix:
                    s = jnp.where(valid, s, -jnp.inf)
                sink = jnp.where(top, sink_top, sink_bot)
                m = jnp.maximum(jnp.max(s, axis=1, keepdims=True), sink)
                e = jnp.exp(s - m)
                den = jnp.sum(e, axis=1, keepdims=True) + jnp.exp(sink - m)
                return jnp.dot(e.astype(BF16), vx, preferred_element_type=F32) / den

            h0 = kv * Q_PER_KV
            o = (head(k_lo, v_lo, True, sinks_ref[h0], sinks_ref[h0 + 2])
                 + head(k_hi, v_hi, False, sinks_ref[h0 + 1], sinks_ref[h0 + 3]))
            o_ref[0, rows, c0:c0 + LANES] = o[:CHUNK] * sa_ref[0, rows, c0:c0 + LANES]
            o_ref[0, rows, c0 + LANES:c0 + 2 * LANES] = (
                o[CHUNK:] * sa_ref[0, rows, c0 + LANES:c0 + 2 * LANES])


def _attn(sinks, q, kpad, vpad, sa, mask_prefix, seq):
    b, t, _ = q.shape
    tq = min(ATTN_ROWS, t)
    assert t % tq == 0 and kpad.shape[1] == t + WINDOW
    row_spec = pl.BlockSpec((1, tq, ATTN_WIDTH), lambda i, j: (i, j, 0))
    kv_spec = pl.BlockSpec((1, t + WINDOW, KV_WIDTH), lambda i, j: (i, 0, 0))
    vmem = (2 * tq * ATTN_WIDTH * (2 + 4 + 4) + 2 * 2 * (t + WINDOW) * KV_WIDTH * 4
            + 16 * 2 * CHUNK * 2 * LANES * 4 + 8 * KEYS * LANES * 4)
    return pl.pallas_call(
        functools.partial(_attn_kernel, tq=tq, mask_prefix=mask_prefix),
        grid=(b, t // tq),
        in_specs=[pl.BlockSpec(memory_space=pltpu.SMEM), row_spec, kv_spec, kv_spec, row_spec],
        out_specs=row_spec,
        out_shape=jax.ShapeDtypeStruct((b, t, ATTN_WIDTH), F32),
        compiler_params=_params(vmem, 2),
        name=f"attn_{seq}",
    )(sinks, q, kpad, vpad, sa)


def _ssm_kernel(u_ref, sz_ref, h0r_ref, h0i_ref, wlag_ref, a8r_ref, a8i_ref, pr_ref, pi_ref,
                ck_ref, d_ref, wglu_ref, xs_ref, hr_ref, hi_ref,
                ubuf, cr_s, ci_s, hs, *, tt, has_state):
    t = pl.program_id(1)

    @pl.when(t == 0)
    def _():
        ubuf[0:LAGS, :] = jnp.zeros((LAGS, SSM_WIDTH), F32)
        if has_state:
            for q in range(PAIRS):
                h0r, h0i = h0r_ref[0, q:q + 1, :], h0i_ref[0, q:q + 1, :]
                pr, pi = pr_ref[q], pi_ref[q]
                cr_s[q] = pr * h0r - pi * h0i
                ci_s[q] = pr * h0i + pi * h0r
        else:
            cr_s[...] = jnp.zeros(cr_s.shape, F32)
            ci_s[...] = jnp.zeros(ci_s.shape, F32)

    ubuf[LAGS:LAGS + tt, :] = u_ref[0]
    slot = lax.broadcasted_iota(jnp.int32, (tt, LANES), 1) // SLOT

    ys = []
    for k in range(U_TILES):
        rolled = []
        for s in range(LAGS):
            us = ubuf[LAGS - s:LAGS - s + tt, k * LANES:(k + 1) * LANES]
            if s % PAIRS_PER_TILE:
                us = pltpu.roll(us, SLOT * (s % PAIRS_PER_TILE), 1)
            rolled.append(us)

        for sg in range(PAIRS_PER_TILE):
            q = k * PAIRS_PER_TILE + sg

            def stack(base):
                out = rolled[base + (-sg) % PAIRS_PER_TILE]
                for i in range(1, PAIRS_PER_TILE):
                    out = jnp.where(slot == i, rolled[base + (i - sg) % PAIRS_PER_TILE], out)
                return out

            xl = jnp.concatenate([stack(0), stack(PAIRS_PER_TILE)], axis=1).astype(BF16)
            w = jnp.dot(xl, wlag_ref[q], preferred_element_type=F32)
            ar, ai = a8r_ref[q:q + 1, :], a8i_ref[q:q + 1, :]
            cr, ci = cr_s[q], ci_s[q]
            for b in range(tt // SUBLANES):
                blk = slice(b * SUBLANES, (b + 1) * SUBLANES)
                hr = w[blk, :LANES] + cr
                hi = w[blk, LANES:] + ci
                hs[blk, q * PAIR_N:q * PAIR_N + LANES] = hr
                hs[blk, q * PAIR_N + LANES:(q + 1) * PAIR_N] = hi
                cr = ar * hr - ai * hi
                ci = ar * hi + ai * hr
            cr_s[q] = cr
            ci_s[q] = ci
            hr_ref[0, q:q + 1, :] = hr[SUBLANES - 1:, :]
            hi_ref[0, q:q + 1, :] = hi[SUBLANES - 1:, :]

        cols = slice(k * PAIRS_PER_TILE * PAIR_N, (k + 1) * PAIRS_PER_TILE * PAIR_N)
        ys.append(jnp.dot(hs[:, cols].astype(BF16), ck_ref[k], preferred_element_type=F32))

    u = u_ref[0]
    y = jnp.concatenate(ys, axis=1) + d_ref[...] * u
    z = jax.nn.gelu(y)
    g = jnp.dot(z.astype(BF16), wglu_ref[...], preferred_element_type=F32)
    xs_ref[0] = z * _sigmoid(g) * sz_ref[0]
    ubuf[0:LAGS, :] = ubuf[tt:tt + LAGS, :]


def _ssm(u, sz, h0r, h0i, consts, d_skip, w_glu, has_state, seq):
    b, t, _ = u.shape
    tt = min(SSM_ROWS, t)
    assert t % tt == 0
    wlag, a8r, a8i, pr, pi, ck = consts
    row_spec = pl.BlockSpec((1, tt, SSM_WIDTH), lambda i, j: (i, j, 0))
    st_spec = pl.BlockSpec((1, PAIRS, LANES), lambda i, j: (i, 0, 0))
    n_state = PAIRS * PAIR_N
    vmem = (3 * 2 * tt * SSM_WIDTH * 4 + wlag.size * 2 + ck.size * 2 + w_glu.size * 2
            + tt * n_state * 4 + (LAGS + 2) * tt * SSM_WIDTH * 4 + 4 * tt * SSM_WIDTH * 4
            + 4 * tt * PAIR_N * 4 + 16 * PAIRS * SUBLANES * LANES * 4)
    return pl.pallas_call(
        functools.partial(_ssm_kernel, tt=tt, has_state=has_state),
        grid=(b, t // tt),
        in_specs=[row_spec, row_spec, st_spec, st_spec,
                  _const_spec(wlag.shape), _const_spec(a8r.shape), _const_spec(a8i.shape),
                  _const_spec(pr.shape), _const_spec(pi.shape), _const_spec(ck.shape),
                  _const_spec((1, SSM_WIDTH)), _const_spec(w_glu.shape)],
        out_specs=[row_spec, st_spec, st_spec],
        out_shape=[jax.ShapeDtypeStruct((b, t, SSM_WIDTH), F32),
                   jax.ShapeDtypeStruct((b, PAIRS, LANES), F32),
                   jax.ShapeDtypeStruct((b, PAIRS, LANES), F32)],
        scratch_shapes=[pltpu.VMEM((tt + LAGS, SSM_WIDTH), F32),
                        pltpu.VMEM((PAIRS, SUBLANES, LANES), F32),
                        pltpu.VMEM((PAIRS, SUBLANES, LANES), F32),
                        pltpu.VMEM((tt, n_state), F32)],
        compiler_params=_params(vmem, 2),
        name=f"ssm_{seq}",
    )(u, sz, h0r, h0i, wlag, a8r, a8i, pr, pi, ck, d_skip, w_glu)


def _ssm_constants(a_re, a_im, log_dt, b_re, b_im, c_re, c_im):
    lr, li = a_re.astype(F32), a_im.astype(F32)
    dt = jnp.exp(log_dt.astype(F32))[:, None]
    xr, xi = lr * dt, li * dt

    def apow(n):
        mag = jnp.exp(xr * n)
        return mag * jnp.cos(xi * n), mag * jnp.sin(xi * n)

    ar, ai = apow(1.0)
    nr, ni = ar - 1.0, ai
    den = lr * lr + li * li
    fr, fi = (nr * lr + ni * li) / den, (ni * lr - nr * li) / den
    br, bi = b_re.astype(F32), b_im.astype(F32)
    bbr = fr[..., None] * br - fi[..., None] * bi
    bbi = fr[..., None] * bi + fi[..., None] * br
    lag = jnp.arange(LAGS, dtype=F32)[:, None, None]
    er, ei = apow(lag)
    wl_r = er[..., None] * bbr[None] - ei[..., None] * bbi[None]
    wl_i = er[..., None] * bbi[None] + ei[..., None] * bbr[None]

    qq = np.arange(PAIRS)
    s_idx = (PAIRS_PER_TILE * np.arange(2)[None, :, None, None]
             + (np.arange(PAIRS_PER_TILE)[None, None, :, None] - qq[:, None, None, None]) % PAIRS_PER_TILE)
    g_idx = 2 * qq[:, None, None, None] + np.arange(2)[None, None, None, :]
    eye2 = jnp.eye(2, dtype=F32)

    def spread(wl):
        sel = jnp.swapaxes(wl[s_idx, g_idx], -1, -2)
        full = sel[:, :, :, :, :, None, :] * eye2[None, None, None, :, None, :, None]
        return full.reshape(PAIRS, PAIR_K, 2 * SSM_STATE)

    wlag = jnp.concatenate([spread(wl_r), spread(wl_i)], axis=-1).astype(BF16)

    a8r, a8i = (a.reshape(PAIRS, LANES) for a in apow(float(LAGS)))
    pwr, pwi = (jnp.swapaxes(a.reshape(LAGS, PAIRS, LANES), 0, 1) for a in apow(lag + 1.0))

    def ctab(c):
        return jnp.swapaxes(c.astype(F32).reshape(U_TILES, PAIRS_PER_TILE, 2, SSM_GROUP, SSM_STATE), -1, -2)

    cv = jnp.stack([ctab(c_re), -ctab(c_im)], axis=2)
    eye4 = jnp.eye(PAIRS_PER_TILE, dtype=F32)
    ck = (cv[:, :, :, :, :, None, None, :]
          * eye4[None, :, None, None, None, :, None, None]
          * eye2[None, None, None, :, None, None, :, None])
    ck = ck.reshape(U_TILES, PAIRS_PER_TILE * PAIR_N, LANES).astype(BF16)
    return wlag, a8r, a8i, pwr, pwi, ck


def _out_kernel(xa_ref, xs_ref, ga_ref, gs_ref, x_ref, p_ref, woa_ref, wos_ref, wout_ref,
                wpg_ref, wpp_ref, fg_ref, y_ref):
    def mm(a, w_ref):
        return jnp.dot(a.astype(BF16), w_ref[...], preferred_element_type=F32)

    merged = ga_ref[...] * mm(xa_ref[...], woa_ref) + gs_ref[...] * mm(xs_ref[...], wos_ref)
    h = x_ref[...] + mm(merged, wout_ref)
    h = h + _sigmoid(mm(h, wpg_ref)) * mm(p_ref[...], wpp_ref)
    ms = jnp.mean(h * h, axis=-1, keepdims=True)
    y_ref[...] = h * lax.rsqrt(ms + EPS) * fg_ref[...]


def _out(xa, xs, ga, gs, x2d, p2d, woa, wos, wout, wpg, wpp, fgain, seq):
    n = x2d.shape[0]
    tm = min(OUT_ROWS, n)
    assert n % tm == 0

    def row_spec(w):
        return pl.BlockSpec((tm, w), lambda i: (i, 0))

    weights = (woa, wos, wout, wpg, wpp)
    vmem = (2 * tm * (2 * ATTN_WIDTH + 4 * D_MODEL + PLE_DIM) * 4 + sum(w.size for w in weights) * 2
            + 8 * tm * D_MODEL * 4)
    return pl.pallas_call(
        _out_kernel,
        grid=(n // tm,),
        in_specs=[row_spec(ATTN_WIDTH), row_spec(SSM_WIDTH), row_spec(D_MODEL), row_spec(D_MODEL),
                  row_spec(D_MODEL), row_spec(PLE_DIM)]
                 + [_const_spec(w.shape) for w in weights] + [_const_spec((1, D_MODEL))],
        out_specs=row_spec(D_MODEL),
        out_shape=jax.ShapeDtypeStruct((n, D_MODEL), F32),
        compiler_params=_params(vmem, 1),
        name=f"out_{seq}",
    )(xa, xs, ga, gs, x2d, p2d, woa, wos, wout, wpg, wpp, fgain)


def _rope_tables(pos, rows):
    half = ROT_DIM // 2
    inv = jnp.power(ROPE_THETA, -jnp.arange(half, dtype=F32) * 2.0 / ROT_DIM)
    ang = pos.astype(F32)[:, None] * inv[None, :]
    cos, sin = jnp.cos(ang), jnp.sin(ang)
    t = pos.shape[0]
    rest = HEAD_DIM - ROT_DIM
    ones, zeros, zh = jnp.ones((t, rest), F32), jnp.zeros((t, rest), F32), jnp.zeros((t, half), F32)
    cos_h = jnp.concatenate([cos, cos, ones], axis=1)
    sina_h = jnp.concatenate([zh, sin, zeros], axis=1)
    sinb_h = jnp.concatenate([-sin, zh, zeros], axis=1)
    reps = (max(rows // t, 1), LANES // HEAD_DIM)
    return tuple(jnp.tile(a, reps) for a in (cos_h, sina_h, sinb_h))


def _layer(x, p, pos, k_prefix, v_prefix, mask_prefix, h0r, h0i, wts, ssm_consts, seq):
    b, t, _ = x.shape
    (gain, w_in, sinks, woa, d_skip, w_glu, wos, wout, wpg, wpp, fgain) = wts
    x2d = x.reshape(b * t, D_MODEL)
    cos, sina, sinb = _rope_tables(pos, min(PROJ_ROWS, b * t))
    q, k, v, sa, u, sz, ga, gs = _proj(x2d, gain, cos, sina, sinb, w_in, seq)
    k3 = k.reshape(b, t, KV_WIDTH)
    v3 = v.reshape(b, t, KV_WIDTH)
    kpad = jnp.concatenate([k_prefix, k3], axis=1)
    vpad = jnp.concatenate([v_prefix, v3], axis=1)
    xa = _attn(sinks, q.reshape(b, t, ATTN_WIDTH), kpad, vpad, sa.reshape(b, t, ATTN_WIDTH),
               mask_prefix, seq)
    has_state = h0r is not None
    if not has_state:
        h0r = h0i = jnp.zeros((b, PAIRS, LANES), F32)
    xs, hr, hi = _ssm(u.reshape(b, t, SSM_WIDTH), sz.reshape(b, t, SSM_WIDTH), h0r, h0i,
                      ssm_consts, d_skip, w_glu, has_state, seq)
    y = _out(xa.reshape(b * t, ATTN_WIDTH), xs.reshape(b * t, SSM_WIDTH), ga, gs, x2d,
             p.reshape(b * t, PLE_DIM), woa, wos, wout, wpg, wpp, fgain, seq)
    k_state = kpad[:, -WINDOW:].reshape(1, b, WINDOW, N_KV_HEADS, HEAD_DIM)
    v_state = vpad[:, -WINDOW:].reshape(1, b, WINDOW, N_KV_HEADS, HEAD_DIM)
    s_re = hr.reshape(1, b, SSM_GROUPS, SSM_STATE)
    s_im = hi.reshape(1, b, SSM_GROUPS, SSM_STATE)
    return y.reshape(b, t, D_MODEL), k_state, v_state, s_re, s_im


def kernel(x_prompt, x_sample, p_prompt, p_sample, cache_attn_k, cache_attn_v, state_ssm_re,
           state_ssm_im, norm_gain, w_in, attn_sinks, w_o_attn, ssm_a_re, ssm_a_im, ssm_log_dt,
           ssm_b_re, ssm_b_im, ssm_c_re, ssm_c_im, ssm_d, ssm_w_glu, w_o_ssm, w_out,
           w_ple_gate, w_ple_proj, final_norm_gain):
    assert norm_gain.shape[0] == 1, "single-layer model"
    bp, tp, _ = x_prompt.shape
    bs, ts, _ = x_sample.shape
    wts = (norm_gain[0].reshape(1, D_MODEL).astype(F32), w_in[0].astype(BF16),
           attn_sinks[0].astype(F32), w_o_attn[0].astype(BF16),
           ssm_d[0].reshape(1, SSM_WIDTH).astype(F32), ssm_w_glu[0].astype(BF16),
           w_o_ssm[0].astype(BF16), w_out[0].astype(BF16), w_ple_gate[0].astype(BF16),
           w_ple_proj[0].astype(BF16), final_norm_gain.reshape(1, D_MODEL).astype(F32))
    consts = _ssm_constants(ssm_a_re[0], ssm_a_im[0], ssm_log_dt[0], ssm_b_re[0], ssm_b_im[0],
                            ssm_c_re[0], ssm_c_im[0])

    zeros_kv = jnp.zeros((bp, WINDOW, KV_WIDTH), F32)
    y_p, kp, vp, rp, ip = _layer(x_prompt, p_prompt[0], jnp.arange(tp), zeros_kv, zeros_kv, True,
                                 None, None, wts, consts, "prompt")
    ck = cache_attn_k[0].reshape(bs, WINDOW, KV_WIDTH).astype(F32)
    cv = cache_attn_v[0].reshape(bs, WINDOW, KV_WIDTH).astype(F32)
    h0r = state_ssm_re[0].reshape(bs, PAIRS, LANES).astype(F32)
    h0i = state_ssm_im[0].reshape(bs, PAIRS, LANES).astype(F32)
    y_s, ks, vs, rs, is_ = _layer(x_sample, p_sample[0], PAST_LEN + jnp.arange(ts), ck, cv, False,
                                  h0r, h0i, wts, consts, "sample")
    return (y_p, y_s, kp, vp, rp, ip, ks, vs, rs, is_)
```

```python
import functools

import numpy as np
import jax
import jax.numpy as jnp
from jax import lax
from jax.experimental import pallas as pl
from jax.experimental.pallas import tpu as pltpu

F32 = jnp.float32
BF16 = jnp.bfloat16

LANES = 128
SUBLANES = 8
V7X_VMEM_BYTES = 64 * 1024 * 1024

D_MODEL = 1024
CHUNK = 64
WINDOW = 128
N_HEADS = 8
N_KV_HEADS = 2
HEAD_DIM = 64
Q_PER_KV = N_HEADS // N_KV_HEADS
ATTN_WIDTH = N_HEADS * HEAD_DIM
KV_WIDTH = N_KV_HEADS * HEAD_DIM
ROT_DIM = HEAD_DIM // 4
ROPE_THETA = 500000.0
SSM_WIDTH = D_MODEL // 2
SSM_GROUP = 16
SSM_GROUPS = SSM_WIDTH // SSM_GROUP
SSM_STATE = 64
PLE_DIM = 256
PAST_LEN = 1024
EPS = 1e-6

O_Q = 0
O_K = O_Q + ATTN_WIDTH
O_V = O_K + KV_WIDTH
O_ZA = O_V + KV_WIDTH
O_U = O_ZA + ATTN_WIDTH
O_ZS = O_U + SSM_WIDTH
O_GA = O_ZS + SSM_WIDTH
O_GS = O_GA + D_MODEL
IN_WIDTH = O_GS + D_MODEL

KEYS = WINDOW + CHUNK
LAGS = SUBLANES
PAIRS = SSM_GROUPS // 2
PAIR_K = 2 * LAGS * SSM_GROUP
PAIR_N = 2 * 2 * SSM_STATE
U_TILES = SSM_WIDTH // LANES
PAIRS_PER_TILE = PAIRS // U_TILES
SLOT = 2 * SSM_GROUP

PROJ_ROWS = 256
ATTN_ROWS = 512
SSM_ROWS = 256
OUT_ROWS = 256


def _sigmoid(x):
    return 1.0 / (1.0 + jnp.exp(-x))


def _const_spec(shape):
    zeros = (0,) * len(shape)
    return pl.BlockSpec(shape, lambda *_: zeros, pipeline_mode=pl.Buffered(1))


def _params(vmem_bytes, n_grid):
    return pltpu.CompilerParams(
        dimension_semantics=("arbitrary",) * n_grid,
        vmem_limit_bytes=min(int(vmem_bytes), V7X_VMEM_BYTES - 8 * 1024 * 1024),
    )


def _proj_kernel(x_ref, gain_ref, cos_ref, sina_ref, sinb_ref, w_ref,
                 q_ref, k_ref, v_ref, sa_ref, u_ref, sz_ref, ga_ref, gs_ref):
    x = x_ref[...]
    ms = jnp.mean(x * x, axis=-1, keepdims=True)
    xn = (x * lax.rsqrt(ms + EPS) * gain_ref[...]).astype(BF16)

    def seg(a, b):
        return jnp.dot(xn, w_ref[:, a:b], preferred_element_type=F32)

    cos, sina, sinb = cos_ref[...], sina_ref[...], sinb_ref[...]

    def rope(t):
        return (t * cos + pltpu.roll(t, ROT_DIM // 2, 1) * sina
                + pltpu.roll(t, LANES - ROT_DIM // 2, 1) * sinb)

    zq = seg(O_Q, O_K)
    for j in range(ATTN_WIDTH // LANES):
        sl = slice(j * LANES, (j + 1) * LANES)
        q_ref[:, sl] = (rope(zq[:, sl]) * (HEAD_DIM ** -0.5)).astype(BF16)
    k_ref[...] = rope(seg(O_K, O_V))
    v_ref[...] = seg(O_V, O_ZA)
    za = seg(O_ZA, O_U)
    sa_ref[...] = za * _sigmoid(za)
    u_ref[...] = seg(O_U, O_ZS)
    zs = seg(O_ZS, O_GA)
    sz_ref[...] = zs * _sigmoid(zs)
    ga_ref[...] = _sigmoid(seg(O_GA, O_GS))
    gs_ref[...] = _sigmoid(seg(O_GS, IN_WIDTH))


def _proj(x2d, gain, cos, sina, sinb, w_in, seq):
    n = x2d.shape[0]
    tm = min(PROJ_ROWS, n)
    tab_rows = cos.shape[0]
    tab_tiles = tab_rows // tm
    assert n % tm == 0 and tab_rows % tm == 0

    def row_spec(w):
        return pl.BlockSpec((tm, w), lambda i: (i, 0))

    tab_spec = pl.BlockSpec((tm, LANES), lambda i: (i % tab_tiles, 0))
    widths = (ATTN_WIDTH, KV_WIDTH, KV_WIDTH, ATTN_WIDTH, SSM_WIDTH, SSM_WIDTH, D_MODEL, D_MODEL)
    dtypes = (BF16,) + (F32,) * 7
    vmem = (2 * tm * D_MODEL * 4 + D_MODEL * IN_WIDTH * 2 + 3 * 2 * tm * LANES * 4
            + 3 * tm * IN_WIDTH * 4 + tm * D_MODEL * 8)
    return pl.pallas_call(
        _proj_kernel,
        grid=(n // tm,),
        in_specs=[row_spec(D_MODEL), _const_spec((1, D_MODEL)), tab_spec, tab_spec, tab_spec,
                  _const_spec((D_MODEL, IN_WIDTH))],
        out_specs=[row_spec(w) for w in widths],
        out_shape=[jax.ShapeDtypeStruct((n, w), d) for w, d in zip(widths, dtypes)],
        compiler_params=_params(vmem, 1),
        name=f"proj_{seq}",
    )(x2d, gain, cos, sina, sinb, w_in)


def _attn_kernel(sinks_ref, q_ref, k_ref, v_ref, sa_ref, o_ref, *, tq, mask_prefix):
    t = pl.program_id(1)
    lo_k = lax.broadcasted_iota(jnp.int32, (KEYS, LANES), 1) < HEAD_DIM
    lo_q = lax.broadcasted_iota(jnp.int32, (CHUNK, LANES), 1) < HEAD_DIM
    head_row = lax.broadcasted_iota(jnp.int32, (Q_PER_KV * CHUNK, 1), 0) // CHUNK
    nt = (((1,), (1,)), ((), ()))
    units = [(c, kv) for c in range(tq // CHUNK) for kv in range(N_KV_HEADS)]

    windows = {}

    def window(c):
        if c not in windows:
            row0 = pl.multiple_of(t * tq + c * CHUNK, CHUNK)
            kk = k_ref[0, pl.ds(row0, KEYS), :]
            vv = v_ref[0, pl.ds(row0, KEYS), :]
            windows[c] = (row0, kk, vv, pltpu.roll(kk, HEAD_DIM, 1), pltpu.roll(vv, HEAD_DIM, 1))
        return windows[c]

    def scores(c, kv):
        rows = slice(c * CHUNK, (c + 1) * CHUNK)
        row0, kk, vv, kr, vr = window(c)
        k2 = (jnp.where(lo_k, kk, kr) if kv == 0 else jnp.where(lo_k, kr, kk)).astype(BF16)
        v2 = (jnp.where(lo_k, vv, vr) if kv == 0 else jnp.where(lo_k, vr, vv)).astype(BF16)
        c0 = kv * Q_PER_KV * HEAD_DIM
        zero = jnp.zeros((CHUNK, LANES), BF16)
        parts = []
        for j in range(Q_PER_KV // 2):
            qt = q_ref[0, rows, c0 + j * LANES:c0 + (j + 1) * LANES]
            parts += [jnp.where(lo_q, qt, zero), jnp.where(lo_q, zero, qt)]
        s = lax.dot_general(jnp.concatenate(parts, axis=0), k2, nt, preferred_element_type=F32)
        if mask_prefix:
            valid = row0 + lax.broadcasted_iota(jnp.int32, (1, KEYS), 1) >= WINDOW
            s = jnp.where(valid, s, -jnp.inf)
        return s, v2

    def softmax(s, kv):
        h0 = kv * Q_PER_KV
        sink = jnp.where(head_row == 0, sinks_ref[h0],
                         jnp.where(head_row == 1, sinks_ref[h0 + 1],
                                   jnp.where(head_row == 2, sinks_ref[h0 + 2], sinks_ref[h0 + 3])))
        m = jnp.maximum(jnp.max(s, axis=1, keepdims=True), sink)
        e = jnp.exp(s - m)
        den = jnp.sum(e, axis=1, keepdims=True) + jnp.exp(sink - m)
        return e.astype(BF16), den

    def output(e, den, v2, c, kv):
        rows = slice(c * CHUNK, (c + 1) * CHUNK)
        c0 = kv * Q_PER_KV * HEAD_DIM
        o = jnp.dot(e, v2, preferred_element_type=F32) / den
        for j in range(Q_PER_KV // 2):
            cols = slice(c0 + j * LANES, c0 + (j + 1) * LANES)
            even = o[2 * j * CHUNK:(2 * j + 1) * CHUNK]
            odd = o[(2 * j + 1) * CHUNK:(2 * j + 2) * CHUNK]
            o_ref[0, rows, cols] = jnp.where(lo_q, even, odd) * sa_ref[0, rows, cols]

    n = len(units)
    st = {}
    for i in range(n + 2):
        if i < n:
            st[i] = scores(*units[i])
        if 0 <= i - 1 < n:
            s, v2 = st[i - 1]
            st[i - 1] = softmax(s, units[i - 1][1]) + (v2,)
        if 0 <= i - 2 < n:
            e, den, v2 = st.pop(i - 2)
            output(e, den, v2, *units[i - 2])


def _attn(sinks, q, kpad, vpad, sa, mask_prefix, seq):
    b, t, _ = q.shape
    tq = min(ATTN_ROWS, t)
    assert t % tq == 0 and kpad.shape[1] == t + WINDOW
    row_spec = pl.BlockSpec((1, tq, ATTN_WIDTH), lambda i, j: (i, j, 0))
    kv_spec = pl.BlockSpec((1, t + WINDOW, KV_WIDTH), lambda i, j: (i, 0, 0))
    vmem = (2 * tq * ATTN_WIDTH * (2 + 4 + 4) + 2 * 2 * (t + WINDOW) * KV_WIDTH * 4
            + 16 * 2 * CHUNK * 2 * LANES * 4 + 8 * KEYS * LANES * 4)
    return pl.pallas_call(
        functools.partial(_attn_kernel, tq=tq, mask_prefix=mask_prefix),
        grid=(b, t // tq),
        in_specs=[pl.BlockSpec(memory_space=pltpu.SMEM), row_spec, kv_spec, kv_spec, row_spec],
        out_specs=row_spec,
        out_shape=jax.ShapeDtypeStruct((b, t, ATTN_WIDTH), F32),
        compiler_params=_params(vmem, 2),
        name=f"attn_{seq}",
    )(sinks, q, kpad, vpad, sa)


def _ssm_kernel(u_ref, sz_ref, h0r_ref, h0i_ref, wlag_ref, a8r_ref, a8i_ref, pr_ref, pi_ref,
                ck_ref, d_ref, wglu_ref, xs_ref, hr_ref, hi_ref,
                ubuf, cr_s, ci_s, hs, *, tt, has_state):
    t = pl.program_id(1)

    @pl.when(t == 0)
    def _():
        ubuf[0:LAGS, :] = jnp.zeros((LAGS, SSM_WIDTH), F32)
        if has_state:
            for q in range(PAIRS):
                h0r, h0i = h0r_ref[0, q:q + 1, :], h0i_ref[0, q:q + 1, :]
                pr, pi = pr_ref[q], pi_ref[q]
                cr_s[q] = pr * h0r - pi * h0i
                ci_s[q] = pr * h0i + pi * h0r
        else:
            cr_s[...] = jnp.zeros(cr_s.shape, F32)
            ci_s[...] = jnp.zeros(ci_s.shape, F32)

    ubuf[LAGS:LAGS + tt, :] = u_ref[0]
    slot = lax.broadcasted_iota(jnp.int32, (tt, LANES), 1) // SLOT

    ys = []
    for k in range(U_TILES):
        rolled = []
        for s in range(LAGS):
            us = ubuf[LAGS - s:LAGS - s + tt, k * LANES:(k + 1) * LANES]
            if s % PAIRS_PER_TILE:
                us = pltpu.roll(us, SLOT * (s % PAIRS_PER_TILE), 1)
            rolled.append(us)

        for sg in range(PAIRS_PER_TILE):
            q = k * PAIRS_PER_TILE + sg

            def stack(base):
                out = rolled[base + (-sg) % PAIRS_PER_TILE]
                for i in range(1, PAIRS_PER_TILE):
                    out = jnp.where(slot == i, rolled[base + (i - sg) % PAIRS_PER_TILE], out)
                return out

            xl = jnp.concatenate([stack(0), stack(PAIRS_PER_TILE)], axis=1).astype(BF16)
            w = jnp.dot(xl, wlag_ref[q], preferred_element_type=F32)
            ar, ai = a8r_ref[q:q + 1, :], a8i_ref[q:q + 1, :]
            cr, ci = cr_s[q], ci_s[q]
            for b in range(tt // SUBLANES):
                blk = slice(b * SUBLANES, (b + 1) * SUBLANES)
                hr = w[blk, :LANES] + cr
                hi = w[blk, LANES:] + ci
                hs[blk, q * PAIR_N:q * PAIR_N + LANES] = hr
                hs[blk, q * PAIR_N + LANES:(q + 1) * PAIR_N] = hi
                cr = ar * hr - ai * hi
                ci = ar * hi + ai * hr
            cr_s[q] = cr
            ci_s[q] = ci
            hr_ref[0, q:q + 1, :] = hr[SUBLANES - 1:, :]
            hi_ref[0, q:q + 1, :] = hi[SUBLANES - 1:, :]

        cols = slice(k * PAIRS_PER_TILE * PAIR_N, (k + 1) * PAIRS_PER_TILE * PAIR_N)
        ys.append(jnp.dot(hs[:, cols].astype(BF16), ck_ref[k], preferred_element_type=F32))

    u = u_ref[0]
    y = jnp.concatenate(ys, axis=1) + d_ref[...] * u
    z = jax.nn.gelu(y)
    g = jnp.dot(z.astype(BF16), wglu_ref[...], preferred_element_type=F32)
    xs_ref[0] = z * _sigmoid(g) * sz_ref[0]
    ubuf[0:LAGS, :] = ubuf[tt:tt + LAGS, :]


def _ssm(u, sz, h0r, h0i, consts, d_skip, w_glu, has_state, seq):
    b, t, _ = u.shape
    tt = min(SSM_ROWS, t)
    assert t % tt == 0
    wlag, a8r, a8i, pr, pi, ck = consts
    row_spec = pl.BlockSpec((1, tt, SSM_WIDTH), lambda i, j: (i, j, 0))
    st_spec = pl.BlockSpec((1, PAIRS, LANES), lambda i, j: (i, 0, 0))
    n_state = PAIRS * PAIR_N
    vmem = (3 * 2 * tt * SSM_WIDTH * 4 + wlag.size * 2 + ck.size * 2 + w_glu.size * 2
            + tt * n_state * 4 + (LAGS + 2) * tt * SSM_WIDTH * 4 + 4 * tt * SSM_WIDTH * 4
            + 4 * tt * PAIR_N * 4 + 16 * PAIRS * SUBLANES * LANES * 4)
    return pl.pallas_call(
        functools.partial(_ssm_kernel, tt=tt, has_state=has_state),
        grid=(b, t // tt),
        in_specs=[row_spec, row_spec, st_spec, st_spec,
                  _const_spec(wlag.shape), _const_spec(a8r.shape), _const_spec(a8i.shape),
                  _const_spec(pr.shape), _const_spec(pi.shape), _const_spec(ck.shape),
                  _const_spec((1, SSM_WIDTH)), _const_spec(w_glu.shape)],
        out_specs=[row_spec, st_spec, st_spec],
        out_shape=[jax.ShapeDtypeStruct((b, t, SSM_WIDTH), F32),
                   jax.ShapeDtypeStruct((b, PAIRS, LANES), F32),
                   jax.ShapeDtypeStruct((b, PAIRS, LANES), F32)],
        scratch_shapes=[pltpu.VMEM((tt + LAGS, SSM_WIDTH), F32),
                        pltpu.VMEM((PAIRS, SUBLANES, LANES), F32),
                        pltpu.VMEM((PAIRS, SUBLANES, LANES), F32),
                        pltpu.VMEM((tt, n_state), F32)],
        compiler_params=_params(vmem, 2),
        name=f"ssm_{seq}",
    )(u, sz, h0r, h0i, wlag, a8r, a8i, pr, pi, ck, d_skip, w_glu)


def _ssm_constants(a_re, a_im, log_dt, b_re, b_im, c_re, c_im):
    lr, li = a_re.astype(F32), a_im.astype(F32)
    dt = jnp.exp(log_dt.astype(F32))[:, None]
    xr, xi = lr * dt, li * dt

    def apow(n):
        mag = jnp.exp(xr * n)
        return mag * jnp.cos(xi * n), mag * jnp.sin(xi * n)

    ar, ai = apow(1.0)
    nr, ni = ar - 1.0, ai
    den = lr * lr + li * li
    fr, fi = (nr * lr + ni * li) / den, (ni * lr - nr * li) / den
    br, bi = b_re.astype(F32), b_im.astype(F32)
    bbr = fr[..., None] * br - fi[..., None] * bi
    bbi = fr[..., None] * bi + fi[..., None] * br
    lag = jnp.arange(LAGS, dtype=F32)[:, None, None]
    er, ei = apow(lag)
    wl_r = er[..., None] * bbr[None] - ei[..., None] * bbi[None]
    wl_i = er[..., None] * bbi[None] + ei[..., None] * bbr[None]

    qq = np.arange(PAIRS)
    s_idx = (PAIRS_PER_TILE * np.arange(2)[None, :, None, None]
             + (np.arange(PAIRS_PER_TILE)[None, None, :, None] - qq[:, None, None, None]) % PAIRS_PER_TILE)
    g_idx = 2 * qq[:, None, None, None] + np.arange(2)[None, None, None, :]
    eye2 = jnp.eye(2, dtype=F32)

    def spread(wl):
        sel = jnp.swapaxes(wl[s_idx, g_idx], -1, -2)
        full = sel[:, :, :, :, :, None, :] * eye2[None, None, None, :, None, :, None]
        return full.reshape(PAIRS, PAIR_K, 2 * SSM_STATE)

    wlag = jnp.concatenate([spread(wl_r), spread(wl_i)], axis=-1).astype(BF16)

    a8r, a8i = (a.reshape(PAIRS, LANES) for a in apow(float(LAGS)))
    pwr, pwi = (jnp.swapaxes(a.reshape(LAGS, PAIRS, LANES), 0, 1) for a in apow(lag + 1.0))

    def ctab(c):
        return jnp.swapaxes(c.astype(F32).reshape(U_TILES, PAIRS_PER_TILE, 2, SSM_GROUP, SSM_STATE), -1, -2)

    cv = jnp.stack([ctab(c_re), -ctab(c_im)], axis=2)
    eye4 = jnp.eye(PAIRS_PER_TILE, dtype=F32)
    ck = (cv[:, :, :, :, :, None, None, :]
          * eye4[None, :, None, None, None, :, None, None]
          * eye2[None, None, None, :, None, None, :, None])
    ck = ck.reshape(U_TILES, PAIRS_PER_TILE * PAIR_N, LANES).astype(BF16)
    return wlag, a8r, a8i, pwr, pwi, ck


def _out_kernel(xa_ref, xs_ref, ga_ref, gs_ref, x_ref, p_ref, woa_ref, wos_ref, wout_ref,
                wpg_ref, wpp_ref, fg_ref, y_ref):
    def mm(a, w_ref):
        return jnp.dot(a.astype(BF16), w_ref[...], preferred_element_type=F32)

    merged = ga_ref[...] * mm(xa_ref[...], woa_ref) + gs_ref[...] * mm(xs_ref[...], wos_ref)
    h = x_ref[...] + mm(merged, wout_ref)
    h = h + _sigmoid(mm(h, wpg_ref)) * mm(p_ref[...], wpp_ref)
    ms = jnp.mean(h * h, axis=-1, keepdims=True)
    y_ref[...] = h * lax.rsqrt(ms + EPS) * fg_ref[...]


def _out(xa, xs, ga, gs, x2d, p2d, woa, wos, wout, wpg, wpp, fgain, seq):
    n = x2d.shape[0]
    tm = min(OUT_ROWS, n)
    assert n % tm == 0

    def row_spec(w):
        return pl.BlockSpec((tm, w), lambda i: (i, 0))

    weights = (woa, wos, wout, wpg, wpp)
    vmem = (2 * tm * (2 * ATTN_WIDTH + 4 * D_MODEL + PLE_DIM) * 4 + sum(w.size for w in weights) * 2
            + 8 * tm * D_MODEL * 4)
    return pl.pallas_call(
        _out_kernel,
        grid=(n // tm,),
        in_specs=[row_spec(ATTN_WIDTH), row_spec(SSM_WIDTH), row_spec(D_MODEL), row_spec(D_MODEL),
                  row_spec(D_MODEL), row_spec(PLE_DIM)]
                 + [_const_spec(w.shape) for w in weights] + [_const_spec((1, D_MODEL))],
        out_specs=row_spec(D_MODEL),
        out_shape=jax.ShapeDtypeStruct((n, D_MODEL), F32),
        compiler_params=_params(vmem, 1),
        name=f"out_{seq}",
    )(xa, xs, ga, gs, x2d, p2d, woa, wos, wout, wpg, wpp, fgain)


def _rope_tables(pos, rows):
    half = ROT_DIM // 2
    inv = jnp.power(ROPE_THETA, -jnp.arange(half, dtype=F32) * 2.0 / ROT_DIM)
    ang = pos.astype(F32)[:, None] * inv[None, :]
    cos, sin = jnp.cos(ang), jnp.sin(ang)
    t = pos.shape[0]
    rest = HEAD_DIM - ROT_DIM
    ones, zeros, zh = jnp.ones((t, rest), F32), jnp.zeros((t, rest), F32), jnp.zeros((t, half), F32)
    cos_h = jnp.concatenate([cos, cos, ones], axis=1)
    sina_h = jnp.concatenate([zh, sin, zeros], axis=1)
    sinb_h = jnp.concatenate([-sin, zh, zeros], axis=1)
    reps = (max(rows // t, 1), LANES // HEAD_DIM)
    return tuple(jnp.tile(a, reps) for a in (cos_h, sina_h, sinb_h))


def _layer(x, p, pos, k_prefix, v_prefix, mask_prefix, h0r, h0i, wts, ssm_consts, seq):
    b, t, _ = x.shape
    (gain, w_in, sinks, woa, d_skip, w_glu, wos, wout, wpg, wpp, fgain) = wts
    x2d = x.reshape(b * t, D_MODEL)
    cos, sina, sinb = _rope_tables(pos, min(PROJ_ROWS, b * t))
    q, k, v, sa, u, sz, ga, gs = _proj(x2d, gain, cos, sina, sinb, w_in, seq)
    k3 = k.reshape(b, t, KV_WIDTH)
    v3 = v.reshape(b, t, KV_WIDTH)
    kpad = jnp.concatenate([k_prefix, k3], axis=1)
    vpad = jnp.concatenate([v_prefix, v3], axis=1)
    xa = _attn(sinks, q.reshape(b, t, ATTN_WIDTH), kpad, vpad, sa.reshape(b, t, ATTN_WIDTH),
               mask_prefix, seq)
    has_state = h0r is not None
    if not has_state:
        h0r = h0i = jnp.zeros((b, PAIRS, LANES), F32)
    xs, hr, hi = _ssm(u.reshape(b, t, SSM_WIDTH), sz.reshape(b, t, SSM_WIDTH), h0r, h0i,
                      ssm_consts, d_skip, w_glu, has_state, seq)
    y = _out(xa.reshape(b * t, ATTN_WIDTH), xs.reshape(b * t, SSM_WIDTH), ga, gs, x2d,
             p.reshape(b * t, PLE_DIM), woa, wos, wout, wpg, wpp, fgain, seq)
    k_state = kpad[:, -WINDOW:].reshape(1, b, WINDOW, N_KV_HEADS, HEAD_DIM)
    v_state = vpad[:, -WINDOW:].reshape(1, b, WINDOW, N_KV_HEADS, HEAD_DIM)
    s_re = hr.reshape(1, b, SSM_GROUPS, SSM_STATE)
    s_im = hi.reshape(1, b, SSM_GROUPS, SSM_STATE)
    return y.reshape(b, t, D_MODEL), k_state, v_state, s_re, s_im


def kernel(x_prompt, x_sample, p_prompt, p_sample, cache_attn_k, cache_attn_v, state_ssm_re,
           state_ssm_im, norm_gain, w_in, attn_sinks, w_o_attn, ssm_a_re, ssm_a_im, ssm_log_dt,
           ssm_b_re, ssm_b_im, ssm_c_re, ssm_c_im, ssm_d, ssm_w_glu, w_o_ssm, w_out,
           w_ple_gate, w_ple_proj, final_norm_gain):
    assert norm_gain.shape[0] == 1, "single-layer model"
    bp, tp, _ = x_prompt.shape
    bs, ts, _ = x_sample.shape
    wts = (norm_gain[0].reshape(1, D_MODEL).astype(F32), w_in[0].astype(BF16),
           attn_sinks[0].astype(F32), w_o_attn[0].astype(BF16),
           ssm_d[0].reshape(1, SSM_WIDTH).astype(F32), ssm_w_glu[0].astype(BF16),
           w_o_ssm[0].astype(BF16), w_out[0].astype(BF16), w_ple_gate[0].astype(BF16),
           w_ple_proj[0].astype(BF16), final_norm_gain.reshape(1, D_MODEL).astype(F32))
    consts = _ssm_constants(ssm_a_re[0], ssm_a_im[0], ssm_log_dt[0], ssm_b_re[0], ssm_b_im[0],
                            ssm_c_re[0], ssm_c_im[0])

    zeros_kv = jnp.zeros((bp, WINDOW, KV_WIDTH), F32)
    y_p, kp, vp, rp, ip = _layer(x_prompt, p_prompt[0], jnp.arange(tp), zeros_kv, zeros_kv, True,
                                 None, None, wts, consts, "prompt")
    ck = cache_attn_k[0].reshape(bs, WINDOW, KV_WIDTH).astype(F32)
    cv = cache_attn_v[0].reshape(bs, WINDOW, KV_WIDTH).astype(F32)
    h0r = state_ssm_re[0].reshape(bs, PAIRS, LANES).astype(F32)
    h0i = state_ssm_im[0].reshape(bs, PAIRS, LANES).astype(F32)
    y_s, ks, vs, rs, is_ = _layer(x_sample, p_sample[0], PAST_LEN + jnp.arange(ts), ck, cv, False,
                                  h0r, h0i, wts, consts, "sample")
    return (y_p, y_s, kp, vp, rp, ip, ks, vs, rs, is_)
```

```python
import functools

import numpy as np
import jax
import jax.numpy as jnp
from jax import lax
from jax.experimental import pallas as pl
from jax.experimental.pallas import tpu as pltpu

F32 = jnp.float32
BF16 = jnp.bfloat16

LANES = 128
SUBLANES = 8
V7X_VMEM_BYTES = 64 * 1024 * 1024

D_MODEL = 1024
CHUNK = 64
WINDOW = 128
N_HEADS = 8
N_KV_HEADS = 2
HEAD_DIM = 64
Q_PER_KV = N_HEADS // N_KV_HEADS
LOG2E = 1.4426950408889634
Q_SCALE = HEAD_DIM ** -0.5 * LOG2E
ATTN_WIDTH = N_HEADS * HEAD_DIM
KV_WIDTH = N_KV_HEADS * HEAD_DIM
ROT_DIM = HEAD_DIM // 4
ROPE_THETA = 500000.0
SSM_WIDTH = D_MODEL // 2
SSM_GROUP = 16
SSM_GROUPS = SSM_WIDTH // SSM_GROUP
SSM_STATE = 64
PLE_DIM = 256
PAST_LEN = 1024
EPS = 1e-6

O_Q = 0
O_K = O_Q + ATTN_WIDTH
O_V = O_K + KV_WIDTH
O_ZA = O_V + KV_WIDTH
O_U = O_ZA + ATTN_WIDTH
O_ZS = O_U + SSM_WIDTH
O_GA = O_ZS + SSM_WIDTH
O_GS = O_GA + D_MODEL
IN_WIDTH = O_GS + D_MODEL

QM_WIDTH = N_HEADS * LANES
KV2_WIDTH = N_KV_HEADS * LANES
KEYS = WINDOW + CHUNK
LAGS = SUBLANES
PAIRS = SSM_GROUPS // 2
PAIR_K = 2 * LAGS * SSM_GROUP
PAIR_N = 2 * 2 * SSM_STATE
U_TILES = SSM_WIDTH // LANES
PAIRS_PER_TILE = PAIRS // U_TILES
SLOT = 2 * SSM_GROUP
assert PAIRS_PER_TILE == 4 and LAGS == 2 * PAIRS_PER_TILE

PROJ_ROWS = 256
ATTN_ROWS = 512
SSM_ROWS = 256
OUT_ROWS = 256


def _sigmoid(x):
    return 1.0 / (1.0 + jnp.exp(-x))


def _const_spec(shape):
    zeros = (0,) * len(shape)
    return pl.BlockSpec(shape, lambda *_: zeros, pipeline_mode=pl.Buffered(1))


def _params(vmem_bytes, n_grid):
    return pltpu.CompilerParams(
        dimension_semantics=("arbitrary",) * n_grid,
        vmem_limit_bytes=min(int(vmem_bytes), V7X_VMEM_BYTES - 8 * 1024 * 1024),
    )


def _proj_kernel(x_ref, gain_ref, cos_ref, sina_ref, sinb_ref, w_ref,
                 q_ref, k_ref, v_ref, k2_ref, v2_ref, sa_ref, u_ref, sz_ref, ga_ref, gs_ref):
    x = x_ref[...]
    ms = jnp.mean(x * x, axis=-1, keepdims=True)
    xn = (x * lax.rsqrt(ms + EPS) * gain_ref[...]).astype(BF16)

    def seg(a, b):
        return jnp.dot(xn, w_ref[:, a:b], preferred_element_type=F32)

    cos, sina, sinb = cos_ref[...], sina_ref[...], sinb_ref[...]

    def rope(t):
        return (t * cos + pltpu.roll(t, ROT_DIM // 2, 1) * sina
                + pltpu.roll(t, LANES - ROT_DIM // 2, 1) * sinb)

    lo = lax.broadcasted_iota(jnp.int32, (x.shape[0], LANES), 1) < HEAD_DIM

    zq = seg(O_Q, O_K)
    for j in range(ATTN_WIDTH // LANES):
        qt = rope(zq[:, j * LANES:(j + 1) * LANES]) * Q_SCALE
        q_ref[:, 2 * j * LANES:(2 * j + 1) * LANES] = jnp.where(lo, qt, 0.0).astype(BF16)
        q_ref[:, (2 * j + 1) * LANES:(2 * j + 2) * LANES] = jnp.where(lo, 0.0, qt).astype(BF16)

    def both_halves(t, dst_ref):
        tr = pltpu.roll(t, HEAD_DIM, 1)
        dst_ref[:, :LANES] = jnp.where(lo, t, tr).astype(BF16)
        dst_ref[:, LANES:] = jnp.where(lo, tr, t).astype(BF16)

    kk = rope(seg(O_K, O_V))
    k_ref[...] = kk
    both_halves(kk, k2_ref)
    vv = seg(O_V, O_ZA)
    v_ref[...] = vv
    both_halves(vv, v2_ref)
    za = seg(O_ZA, O_U)
    sa_ref[...] = za * _sigmoid(za)
    u_ref[...] = seg(O_U, O_ZS)
    zs = seg(O_ZS, O_GA)
    sz_ref[...] = zs * _sigmoid(zs)
    ga_ref[...] = _sigmoid(seg(O_GA, O_GS))
    gs_ref[...] = _sigmoid(seg(O_GS, IN_WIDTH))


def _proj(x2d, gain, cos, sina, sinb, w_in, seq):
    n = x2d.shape[0]
    tm = min(PROJ_ROWS, n)
    tab_rows = cos.shape[0]
    tab_tiles = tab_rows // tm
    assert n % tm == 0 and tab_rows % tm == 0

    def row_spec(w):
        return pl.BlockSpec((tm, w), lambda i: (i, 0))

    tab_spec = pl.BlockSpec((tm, LANES), lambda i: (i % tab_tiles, 0))
    widths = (QM_WIDTH, KV_WIDTH, KV_WIDTH, KV2_WIDTH, KV2_WIDTH,
              ATTN_WIDTH, SSM_WIDTH, SSM_WIDTH, D_MODEL, D_MODEL)
    dtypes = (BF16, F32, F32, BF16, BF16) + (F32,) * 5
    vmem = (2 * tm * D_MODEL * 4 + D_MODEL * IN_WIDTH * 2 + 3 * 2 * tm * LANES * 4
            + 3 * tm * IN_WIDTH * 4 + tm * D_MODEL * 8)
    return pl.pallas_call(
        _proj_kernel,
        grid=(n // tm,),
        in_specs=[row_spec(D_MODEL), _const_spec((1, D_MODEL)), tab_spec, tab_spec, tab_spec,
                  _const_spec((D_MODEL, IN_WIDTH))],
        out_specs=[row_spec(w) for w in widths],
        out_shape=[jax.ShapeDtypeStruct((n, w), d) for w, d in zip(widths, dtypes)],
        compiler_params=_params(vmem, 1),
        name=f"proj_{seq}",
    )(x2d, gain, cos, sina, sinb, w_in)


def _attn_kernel(sinks_ref, q_ref, k_ref, v_ref, sa_ref, o_ref, *, tq, mask_prefix):
    t = pl.program_id(1)
    lo_q = lax.broadcasted_iota(jnp.int32, (CHUNK, LANES), 1) < HEAD_DIM
    head_row = lax.broadcasted_iota(jnp.int32, (Q_PER_KV * CHUNK, 1), 0) // CHUNK
    nt = (((1,), (1,)), ((), ()))
    units = [(c, kv) for c in range(tq // CHUNK) for kv in range(N_KV_HEADS)]

    def scores(c, kv):
        rows = slice(c * CHUNK, (c + 1) * CHUNK)
        row0 = pl.multiple_of(t * tq + c * CHUNK, CHUNK)
        cols = slice(kv * LANES, (kv + 1) * LANES)
        k2 = k_ref[0, pl.ds(row0, KEYS), cols]
        v2 = v_ref[0, pl.ds(row0, KEYS), cols]
        h0 = kv * Q_PER_KV
        qm = jnp.concatenate([q_ref[0, rows, (h0 + h) * LANES:(h0 + h + 1) * LANES]
                              for h in range(Q_PER_KV)], axis=0)
        s = lax.dot_general(qm, k2, nt, preferred_element_type=F32)
        if mask_prefix and c * CHUNK < WINDOW:
            valid = row0 + lax.broadcasted_iota(jnp.int32, (1, KEYS), 1) >= WINDOW
            s = jnp.where(valid, s, -jnp.inf)
        return s, v2

    def softmax(s, kv):
        h0 = kv * Q_PER_KV
        sk = [sinks_ref[h0 + h] * LOG2E for h in range(Q_PER_KV)]
        sink = jnp.where(head_row == 0, sk[0],
                         jnp.where(head_row == 1, sk[1], jnp.where(head_row == 2, sk[2], sk[3])))
        m = jnp.maximum(jnp.max(s, axis=1, keepdims=True), sink)
        e = jnp.exp2(s - m)
        den = jnp.sum(e, axis=1, keepdims=True) + jnp.exp2(sink - m)
        return e.astype(BF16), den

    def output(e, den, v2, c, kv):
        rows = slice(c * CHUNK, (c + 1) * CHUNK)
        c0 = kv * Q_PER_KV * HEAD_DIM
        o = jnp.dot(e, v2, preferred_element_type=F32) / den
        for j in range(Q_PER_KV // 2):
            cols = slice(c0 + j * LANES, c0 + (j + 1) * LANES)
            even = o[2 * j * CHUNK:(2 * j + 1) * CHUNK]
            odd = o[(2 * j + 1) * CHUNK:(2 * j + 2) * CHUNK]
            o_ref[0, rows, cols] = jnp.where(lo_q, even, odd) * sa_ref[0, rows, cols]

    n = len(units)
    st = {}
    for i in range(n + 2):
        if i < n:
            st[i] = scores(*units[i])
        if 0 <= i - 1 < n:
            s, v2 = st[i - 1]
            st[i - 1] = softmax(s, units[i - 1][1]) + (v2,)
        if 0 <= i - 2 < n:
            e, den, v2 = st.pop(i - 2)
            output(e, den, v2, *units[i - 2])


def _attn(sinks, qm, k2pad, v2pad, sa, mask_prefix, seq):
    b, t, _ = qm.shape
    tq = min(ATTN_ROWS, t)
    assert t % tq == 0 and k2pad.shape[1] == t + WINDOW

    def row_spec(w):
        return pl.BlockSpec((1, tq, w), lambda i, j: (i, j, 0))

    kv_spec = pl.BlockSpec((1, t + WINDOW, KV2_WIDTH), lambda i, j: (i, 0, 0))
    vmem = (2 * tq * (QM_WIDTH * 2 + 2 * ATTN_WIDTH * 4) + 2 * 2 * (t + WINDOW) * KV2_WIDTH * 2
            + 8 * Q_PER_KV * CHUNK * 2 * LANES * 4)
    return pl.pallas_call(
        functools.partial(_attn_kernel, tq=tq, mask_prefix=mask_prefix),
        grid=(b, t // tq),
        in_specs=[pl.BlockSpec(memory_space=pltpu.SMEM), row_spec(QM_WIDTH), kv_spec, kv_spec,
                  row_spec(ATTN_WIDTH)],
        out_specs=row_spec(ATTN_WIDTH),
        out_shape=jax.ShapeDtypeStruct((b, t, ATTN_WIDTH), F32),
        compiler_params=_params(vmem, 2),
        name=f"attn_{seq}",
    )(sinks, qm, k2pad, v2pad, sa)


def _ssm_kernel(u_ref, sz_ref, h0r_ref, h0i_ref, wlag_ref, a8r_ref, a8i_ref, pr_ref, pi_ref,
                ck_ref, d_ref, wglu_ref, xs_ref, hr_ref, hi_ref,
                ubuf, cr_s, ci_s, hs, *, tt, has_state):
    t = pl.program_id(1)

    @pl.when(t == 0)
    def _():
        ubuf[0:LAGS, :] = jnp.zeros((LAGS, SSM_WIDTH), F32)
        if has_state:
            for q in range(PAIRS):
                h0r, h0i = h0r_ref[0, q:q + 1, :], h0i_ref[0, q:q + 1, :]
                pr, pi = pr_ref[q], pi_ref[q]
                cr_s[q] = pr * h0r - pi * h0i
                ci_s[q] = pr * h0i + pi * h0r
        else:
            cr_s[...] = jnp.zeros(cr_s.shape, F32)
            ci_s[...] = jnp.zeros(ci_s.shape, F32)

    ubuf[LAGS:LAGS + tt, :] = u_ref[0]
    slot = lax.broadcasted_iota(jnp.int32, (tt, LANES), 1) // SLOT
    to_low = [((slot + PAIRS_PER_TILE - s) % PAIRS_PER_TILE) < 2 for s in range(2)]
    same_parity = [((slot + sg) % 2) == 0 for sg in range(2)]

    def route(r):
        low = [jnp.where(to_low[s], r[s], r[s + 2]) for s in range(2)]
        high = [jnp.where(to_low[s], r[s + 2], r[s]) for s in range(2)]
        return [jnp.where(same_parity[sg % 2], src[0], src[1])
                for sg, src in zip(range(PAIRS_PER_TILE), (low, low, high, high))]

    ys = []
    for k in range(U_TILES):
        ub = ubuf[:, k * LANES:(k + 1) * LANES]
        rolled = []
        for s in range(LAGS):
            us = ub[LAGS:] if s == 0 else pltpu.roll(ub, s, 0)[LAGS:]
            if s % PAIRS_PER_TILE:
                us = pltpu.roll(us, SLOT * (s % PAIRS_PER_TILE), 1)
            rolled.append(us)
        halves = (route(rolled[:PAIRS_PER_TILE]), route(rolled[PAIRS_PER_TILE:]))

        for sg in range(PAIRS_PER_TILE):
            q = k * PAIRS_PER_TILE + sg
            xl = jnp.concatenate([halves[0][sg], halves[1][sg]], axis=1).astype(BF16)
            w = jnp.dot(xl, wlag_ref[q], preferred_element_type=F32)
            ar, ai = a8r_ref[q:q + 1, :], a8i_ref[q:q + 1, :]
            cr, ci = cr_s[q], ci_s[q]
            for b in range(tt // SUBLANES):
                blk = slice(b * SUBLANES, (b + 1) * SUBLANES)
                hr = w[blk, :LANES] + cr
                hi = w[blk, LANES:] + ci
                hs[blk, q * PAIR_N:q * PAIR_N + LANES] = hr
                hs[blk, q * PAIR_N + LANES:(q + 1) * PAIR_N] = hi
                cr = ar * hr - ai * hi
                ci = ar * hi + ai * hr
            cr_s[q] = cr
            ci_s[q] = ci
            hr_ref[0, q:q + 1, :] = hr[SUBLANES - 1:, :]
            hi_ref[0, q:q + 1, :] = hi[SUBLANES - 1:, :]

        cols = slice(k * PAIRS_PER_TILE * PAIR_N, (k + 1) * PAIRS_PER_TILE * PAIR_N)
        ys.append(jnp.dot(hs[:, cols].astype(BF16), ck_ref[k], preferred_element_type=F32))

    u = u_ref[0]
    y = jnp.concatenate(ys, axis=1) + d_ref[...] * u
    z = jax.nn.gelu(y)
    g = jnp.dot(z.astype(BF16), wglu_ref[...], preferred_element_type=F32)
    xs_ref[0] = z * _sigmoid(g) * sz_ref[0]
    ubuf[0:LAGS, :] = ubuf[tt:tt + LAGS, :]


def _ssm(u, sz, h0r, h0i, consts, d_skip, w_glu, has_state, seq):
    b, t, _ = u.shape
    tt = min(SSM_ROWS, t)
    assert t % tt == 0
    wlag, a8r, a8i, pr, pi, ck = consts
    row_spec = pl.BlockSpec((1, tt, SSM_WIDTH), lambda i, j: (i, j, 0))
    st_spec = pl.BlockSpec((1, PAIRS, LANES), lambda i, j: (i, 0, 0))
    n_state = PAIRS * PAIR_N
    vmem = (3 * 2 * tt * SSM_WIDTH * 4 + wlag.size * 2 + ck.size * 2 + w_glu.size * 2
            + tt * n_state * 4 + (LAGS + 2) * tt * SSM_WIDTH * 4 + 4 * tt * SSM_WIDTH * 4
            + 4 * tt * PAIR_N * 4 + 16 * PAIRS * SUBLANES * LANES * 4)
    return pl.pallas_call(
        functools.partial(_ssm_kernel, tt=tt, has_state=has_state),
        grid=(b, t // tt),
        in_specs=[row_spec, row_spec, st_spec, st_spec,
                  _const_spec(wlag.shape), _const_spec(a8r.shape), _const_spec(a8i.shape),
                  _const_spec(pr.shape), _const_spec(pi.shape), _const_spec(ck.shape),
                  _const_spec((1, SSM_WIDTH)), _const_spec(w_glu.shape)],
        out_specs=[row_spec, st_spec, st_spec],
        out_shape=[jax.ShapeDtypeStruct((b, t, SSM_WIDTH), F32),
                   jax.ShapeDtypeStruct((b, PAIRS, LANES), F32),
                   jax.ShapeDtypeStruct((b, PAIRS, LANES), F32)],
        scratch_shapes=[pltpu.VMEM((tt + LAGS, SSM_WIDTH), F32),
                        pltpu.VMEM((PAIRS, SUBLANES, LANES), F32),
                        pltpu.VMEM((PAIRS, SUBLANES, LANES), F32),
                        pltpu.VMEM((tt, n_state), F32)],
        compiler_params=_params(vmem, 2),
        name=f"ssm_{seq}",
    )(u, sz, h0r, h0i, wlag, a8r, a8i, pr, pi, ck, d_skip, w_glu)


def _ssm_constants(a_re, a_im, log_dt, b_re, b_im, c_re, c_im):
    dt = jnp.exp(log_dt.astype(F32))[:, None]
    lr = a_re.astype(F32).reshape(PAIRS, 1, LANES)
    li = a_im.astype(F32).reshape(PAIRS, 1, LANES)
    xr = (a_re.astype(F32) * dt).reshape(PAIRS, 1, LANES)
    xi = (a_im.astype(F32) * dt).reshape(PAIRS, 1, LANES)

    def apow(n):
        mag = jnp.exp(xr * n)
        return mag * jnp.cos(xi * n), mag * jnp.sin(xi * n)

    ar, ai = apow(1.0)
    nr, ni = ar - 1.0, ai
    den = lr * lr + li * li
    fr, fi = (nr * lr + ni * li) / den, (ni * lr - nr * li) / den

    qq, rr = np.arange(PAIRS)[:, None], np.arange(PAIR_K)[None, :]
    half, slot_i, gl = rr // LANES, (rr % LANES) // SLOT, (rr % SLOT) // SSM_GROUP
    lag_tab = (PAIRS_PER_TILE * half + (slot_i - qq) % PAIRS_PER_TILE).astype(np.float32)[:, :, None]
    same_group = (gl[0][:, None] == (np.arange(LANES) // SSM_STATE)[None, :]).astype(np.float32)

    def b_rows(bm):
        t = jnp.transpose(bm.astype(F32).reshape(PAIRS, 2, SSM_STATE, SSM_GROUP), (0, 3, 1, 2))
        t = t.reshape(PAIRS, 1, SSM_GROUP, LANES)
        t = jnp.broadcast_to(t, (PAIRS, PAIR_K // SSM_GROUP, SSM_GROUP, LANES))
        return t.reshape(PAIRS, PAIR_K, LANES)

    br, bi = b_rows(b_re), b_rows(b_im)
    bbr, bbi = fr * br - fi * bi, fr * bi + fi * br
    er, ei = apow(lag_tab)
    wlag = jnp.concatenate([(er * bbr - ei * bbi) * same_group,
                            (er * bbi + ei * bbr) * same_group], axis=-1).astype(BF16)

    a8r, a8i = (a.reshape(PAIRS, LANES) for a in apow(float(LAGS)))
    pwr, pwi = apow(np.arange(1, LAGS + 1, dtype=np.float32)[None, :, None])

    def c_cols(c):
        t = jnp.transpose(c.astype(F32).reshape(U_TILES, LANES // SSM_GROUP, SSM_GROUP, SSM_STATE),
                          (0, 3, 1, 2))
        return t.reshape(U_TILES, 1, 1, 1, SSM_STATE, LANES)

    cols_group = np.arange(LANES) // SSM_GROUP
    rows_group = 2 * np.arange(PAIRS_PER_TILE)[:, None] + np.arange(2)[None, :]
    c_mask = (rows_group[:, None, :, None, None] == cols_group[None, None, None, None, :])
    c_mask = c_mask.astype(np.float32)[None]
    ck = jnp.concatenate([c_cols(c_re) * c_mask, -c_cols(c_im) * c_mask], axis=2)
    ck = ck.reshape(U_TILES, PAIRS_PER_TILE * PAIR_N, LANES).astype(BF16)
    return wlag, a8r, a8i, pwr, pwi, ck


def _out_kernel(xa_ref, xs_ref, ga_ref, gs_ref, x_ref, p_ref, woa_ref, wos_ref, wout_ref,
                wpg_ref, wpp_ref, fg_ref, y_ref):
    def mm(a, w_ref):
        return jnp.dot(a.astype(BF16), w_ref[...], preferred_element_type=F32)

    merged = ga_ref[...] * mm(xa_ref[...], woa_ref) + gs_ref[...] * mm(xs_ref[...], wos_ref)
    h = x_ref[...] + mm(merged, wout_ref)
    h = h + _sigmoid(mm(h, wpg_ref)) * mm(p_ref[...], wpp_ref)
    ms = jnp.mean(h * h, axis=-1, keepdims=True)
    y_ref[...] = h * lax.rsqrt(ms + EPS) * fg_ref[...]


def _out(xa, xs, ga, gs, x2d, p2d, woa, wos, wout, wpg, wpp, fgain, seq):
    n = x2d.shape[0]
    tm = min(OUT_ROWS, n)
    assert n % tm == 0

    def row_spec(w):
        return pl.BlockSpec((tm, w), lambda i: (i, 0))

    weights = (woa, wos, wout, wpg, wpp)
    vmem = (2 * tm * (2 * ATTN_WIDTH + 4 * D_MODEL + PLE_DIM) * 4 + sum(w.size for w in weights) * 2
            + 8 * tm * D_MODEL * 4)
    return pl.pallas_call(
        _out_kernel,
        grid=(n // tm,),
        in_specs=[row_spec(ATTN_WIDTH), row_spec(SSM_WIDTH), row_spec(D_MODEL), row_spec(D_MODEL),
                  row_spec(D_MODEL), row_spec(PLE_DIM)]
                 + [_const_spec(w.shape) for w in weights] + [_const_spec((1, D_MODEL))],
        out_specs=row_spec(D_MODEL),
        out_shape=jax.ShapeDtypeStruct((n, D_MODEL), F32),
        compiler_params=_params(vmem, 1),
        name=f"out_{seq}",
    )(xa, xs, ga, gs, x2d, p2d, woa, wos, wout, wpg, wpp, fgain)


def _rope_tables(pos0, t, rows):
    half = ROT_DIM // 2
    d = np.arange(LANES) % HEAD_DIM
    inv = jnp.power(ROPE_THETA, -jnp.asarray(d % half, F32) * 2.0 / ROT_DIM)
    pos = (pos0 + jnp.arange(t)).astype(F32)
    ang = pos[:, None] * inv[None, :]
    cos, sin = jnp.cos(ang), jnp.sin(ang)
    cos_t = jnp.where((d < ROT_DIM)[None, :], cos, 1.0)
    sina = jnp.where(((d >= half) & (d < ROT_DIM))[None, :], sin, 0.0)
    sinb = jnp.where((d < half)[None, :], -sin, 0.0)
    reps = (max(rows // t, 1), 1)
    return tuple(jnp.tile(a, reps) for a in (cos_t, sina, sinb))


def _both_halves(a):
    h0, h1 = a[..., :HEAD_DIM], a[..., HEAD_DIM:]
    return jnp.concatenate([h0, h0, h1, h1], axis=-1).astype(BF16)


def _layer(x, p, pos0, k_prefix, v_prefix, h0r, h0i, wts, ssm_consts, seq):
    b, t, _ = x.shape
    (gain, w_in, sinks, woa, d_skip, w_glu, wos, wout, wpg, wpp, fgain) = wts
    x2d = x.reshape(b * t, D_MODEL)
    cos, sina, sinb = _rope_tables(pos0, t, min(PROJ_ROWS, b * t))
    qm, k, v, k2, v2, sa, u, sz, ga, gs = _proj(x2d, gain, cos, sina, sinb, w_in, seq)
    k3 = k.reshape(b, t, KV_WIDTH)
    v3 = v.reshape(b, t, KV_WIDTH)
    mask_prefix = k_prefix is None
    if mask_prefix:
        k2_prefix = v2_prefix = jnp.zeros((b, WINDOW, KV2_WIDTH), BF16)
    else:
        k2_prefix, v2_prefix = _both_halves(k_prefix), _both_halves(v_prefix)
    k2pad = jnp.concatenate([k2_prefix, k2.reshape(b, t, KV2_WIDTH)], axis=1)
    v2pad = jnp.concatenate([v2_prefix, v2.reshape(b, t, KV2_WIDTH)], axis=1)
    xa = _attn(sinks, qm.reshape(b, t, QM_WIDTH), k2pad, v2pad, sa.reshape(b, t, ATTN_WIDTH),
               mask_prefix, seq)
    has_state = h0r is not None
    if not has_state:
        h0r = h0i = jnp.zeros((b, PAIRS, LANES), F32)
    xs, hr, hi = _ssm(u.reshape(b, t, SSM_WIDTH), sz.reshape(b, t, SSM_WIDTH), h0r, h0i,
                      ssm_consts, d_skip, w_glu, has_state, seq)
    y = _out(xa.reshape(b * t, ATTN_WIDTH), xs.reshape(b * t, SSM_WIDTH), ga, gs, x2d,
             p.reshape(b * t, PLE_DIM), woa, wos, wout, wpg, wpp, fgain, seq)

    def last_window(prefix, new):
        if t >= WINDOW:
            rows = new[:, t - WINDOW:]
        else:
            rows = jnp.concatenate([prefix[:, t:], new], axis=1)
        return rows.reshape(1, b, WINDOW, N_KV_HEADS, HEAD_DIM)

    s_re = hr.reshape(1, b, SSM_GROUPS, SSM_STATE)
    s_im = hi.reshape(1, b, SSM_GROUPS, SSM_STATE)
    return (y.reshape(b, t, D_MODEL), last_window(k_prefix, k3), last_window(v_prefix, v3), s_re, s_im)


def kernel(x_prompt, x_sample, p_prompt, p_sample, cache_attn_k, cache_attn_v, state_ssm_re,
           state_ssm_im, norm_gain, w_in, attn_sinks, w_o_attn, ssm_a_re, ssm_a_im, ssm_log_dt,
           ssm_b_re, ssm_b_im, ssm_c_re, ssm_c_im, ssm_d, ssm_w_glu, w_o_ssm, w_out,
           w_ple_gate, w_ple_proj, final_norm_gain):
    assert norm_gain.shape[0] == 1, "single-layer model"
    bs = x_sample.shape[0]
    wts = (norm_gain[0].reshape(1, D_MODEL).astype(F32), w_in[0].astype(BF16),
           attn_sinks[0].astype(F32), w_o_attn[0].astype(BF16),
           ssm_d[0].reshape(1, SSM_WIDTH).astype(F32), ssm_w_glu[0].astype(BF16),
           w_o_ssm[0].astype(BF16), w_out[0].astype(BF16), w_ple_gate[0].astype(BF16),
           w_ple_proj[0].astype(BF16), final_norm_gain.reshape(1, D_MODEL).astype(F32))
    consts = _ssm_constants(ssm_a_re[0], ssm_a_im[0], ssm_log_dt[0], ssm_b_re[0], ssm_b_im[0],
                            ssm_c_re[0], ssm_c_im[0])

    y_p, kp, vp, rp, ip = _layer(x_prompt, p_prompt[0], 0, None, None, None, None, wts, consts, "prompt")
    ck = cache_attn_k[0].reshape(bs, WINDOW, KV_WIDTH).astype(F32)
    cv = cache_attn_v[0].reshape(bs, WINDOW, KV_WIDTH).astype(F32)
    h0r = state_ssm_re[0].reshape(bs, PAIRS, LANES).astype(F32)
    h0i = state_ssm_im[0].reshape(bs, PAIRS, LANES).astype(F32)
    y_s, ks, vs, rs, is_ = _layer(x_sample, p_sample[0], PAST_LEN, ck, cv, h0r, h0i, wts, consts, "sample")
    return (y_p, y_s, kp, vp, rp, ip, ks, vs, rs, is_)
```

```python
import functools

import numpy as np
import jax
import jax.numpy as jnp
from jax import lax
from jax.experimental import pallas as pl
from jax.experimental.pallas import tpu as pltpu

F32 = jnp.float32
BF16 = jnp.bfloat16

LANES = 128
SUBLANES = 8
V7X_VMEM_BYTES = 64 * 1024 * 1024

D_MODEL = 1024
CHUNK = 64
WINDOW = 128
N_HEADS = 8
N_KV_HEADS = 2
HEAD_DIM = 64
Q_PER_KV = N_HEADS // N_KV_HEADS
LOG2E = 1.4426950408889634
Q_SCALE = HEAD_DIM ** -0.5 * LOG2E
ATTN_WIDTH = N_HEADS * HEAD_DIM
KV_WIDTH = N_KV_HEADS * HEAD_DIM
ROT_DIM = HEAD_DIM // 4
ROPE_THETA = 500000.0
SSM_WIDTH = D_MODEL // 2
SSM_GROUP = 16
SSM_GROUPS = SSM_WIDTH // SSM_GROUP
SSM_STATE = 64
PLE_DIM = 256
PAST_LEN = 1024
EPS = 1e-6

O_Q = 0
O_K = O_Q + ATTN_WIDTH
O_V = O_K + KV_WIDTH
O_ZA = O_V + KV_WIDTH
O_U = O_ZA + ATTN_WIDTH
O_ZS = O_U + SSM_WIDTH
O_GA = O_ZS + SSM_WIDTH
O_GS = O_GA + D_MODEL
IN_WIDTH = O_GS + D_MODEL

QM_WIDTH = N_HEADS * LANES
KV2_WIDTH = N_KV_HEADS * LANES
KEYS = WINDOW + CHUNK
LAGS = SUBLANES
PAIRS = SSM_GROUPS // 2
PAIR_K = 2 * LAGS * SSM_GROUP
PAIR_N = 2 * 2 * SSM_STATE
N_STATE = PAIRS * PAIR_N
U_TILES = SSM_WIDTH // LANES
PAIRS_PER_TILE = PAIRS // U_TILES
SLOT = 2 * SSM_GROUP
assert PAIRS_PER_TILE == 4 and LAGS == 2 * PAIRS_PER_TILE

LAYER_ROWS = 256
PROJ_ROWS = 256
ATTN_ROWS = 512
SSM_ROWS = 256
OUT_ROWS = 256


def _sigmoid(x):
    return 1.0 / (1.0 + jnp.exp(-x))


def _const_spec(shape):
    zeros = (0,) * len(shape)
    return pl.BlockSpec(shape, lambda *_: zeros, pipeline_mode=pl.Buffered(1))


def _params(vmem_bytes, n_grid):
    return pltpu.CompilerParams(
        dimension_semantics=("arbitrary",) * n_grid,
        vmem_limit_bytes=min(int(vmem_bytes), V7X_VMEM_BYTES - 8 * 1024 * 1024),
    )


def _proj_stage(x, gain, cos, sina, sinb, w_ref):
    ms = jnp.mean(x * x, axis=-1, keepdims=True)
    xn = (x * lax.rsqrt(ms + EPS) * gain).astype(BF16)

    def seg(a, b):
        return jnp.dot(xn, w_ref[:, a:b], preferred_element_type=F32)

    def rope(t):
        return (t * cos + pltpu.roll(t, ROT_DIM // 2, 1) * sina
                + pltpu.roll(t, LANES - ROT_DIM // 2, 1) * sinb)

    lo = lax.broadcasted_iota(jnp.int32, (x.shape[0], LANES), 1) < HEAD_DIM

    zq = seg(O_Q, O_K)
    qh = []
    for j in range(ATTN_WIDTH // LANES):
        qt = rope(zq[:, j * LANES:(j + 1) * LANES]) * Q_SCALE
        qh += [jnp.where(lo, qt, 0.0).astype(BF16), jnp.where(lo, 0.0, qt).astype(BF16)]

    def both_halves(t):
        tr = pltpu.roll(t, HEAD_DIM, 1)
        return [jnp.where(lo, t, tr).astype(BF16), jnp.where(lo, tr, t).astype(BF16)]

    kk = rope(seg(O_K, O_V))
    vv = seg(O_V, O_ZA)
    za = seg(O_ZA, O_U)
    u = seg(O_U, O_ZS)
    zs = seg(O_ZS, O_GA)
    return dict(qh=qh, k=kk, v=vv, k2=both_halves(kk), v2=both_halves(vv),
                sa=za * _sigmoid(za), u=u, sz=zs * _sigmoid(zs),
                ga=_sigmoid(seg(O_GA, O_GS)), gs=_sigmoid(seg(O_GS, IN_WIDTH)))


def _attn_stage(sinks_ref, n_chunks, get_q, get_kv, get_valid, emit):
    lo_q = lax.broadcasted_iota(jnp.int32, (CHUNK, LANES), 1) < HEAD_DIM
    head_row = lax.broadcasted_iota(jnp.int32, (Q_PER_KV * CHUNK, 1), 0) // CHUNK
    nt = (((1,), (1,)), ((), ()))
    units = [(c, kv) for c in range(n_chunks) for kv in range(N_KV_HEADS)]

    def scores(c, kv):
        k2, v2 = get_kv(c, kv)
        qm = jnp.concatenate([get_q(c, kv * Q_PER_KV + h) for h in range(Q_PER_KV)], axis=0)
        s = lax.dot_general(qm, k2, nt, preferred_element_type=F32)
        valid = get_valid(c)
        if valid is not None:
            s = jnp.where(valid, s, -jnp.inf)
        return s, v2

    def softmax(s, kv):
        sk = [sinks_ref[kv * Q_PER_KV + h] * LOG2E for h in range(Q_PER_KV)]
        sink = jnp.where(head_row == 0, sk[0],
                         jnp.where(head_row == 1, sk[1], jnp.where(head_row == 2, sk[2], sk[3])))
        m = jnp.maximum(jnp.max(s, axis=1, keepdims=True), sink)
        e = jnp.exp2(s - m)
        den = jnp.sum(e, axis=1, keepdims=True) + jnp.exp2(sink - m)
        return e.astype(BF16), den

    def output(e, den, v2, c, kv):
        o = jnp.dot(e, v2, preferred_element_type=F32) / den
        for j in range(Q_PER_KV // 2):
            even = o[2 * j * CHUNK:(2 * j + 1) * CHUNK]
            odd = o[(2 * j + 1) * CHUNK:(2 * j + 2) * CHUNK]
            emit(c, kv * (Q_PER_KV // 2) + j, jnp.where(lo_q, even, odd))

    n = len(units)
    st = {}
    for i in range(n + 2):
        if i < n:
            st[i] = scores(*units[i])
        if 0 <= i - 1 < n:
            s, v2 = st[i - 1]
            st[i - 1] = softmax(s, units[i - 1][1]) + (v2,)
        if 0 <= i - 2 < n:
            e, den, v2 = st.pop(i - 2)
            output(e, den, v2, *units[i - 2])


def _ssm_reset(ubuf, cr_s, ci_s, h0=None):
    ubuf[0:LAGS, :] = jnp.zeros((LAGS, SSM_WIDTH), F32)
    if h0 is None:
        cr_s[...] = jnp.zeros(cr_s.shape, F32)
        ci_s[...] = jnp.zeros(ci_s.shape, F32)
    else:
        h0r_ref, h0i_ref, pr_ref, pi_ref = h0
        for q in range(PAIRS):
            h0r, h0i = h0r_ref[0, q:q + 1, :], h0i_ref[0, q:q + 1, :]
            pr, pi = pr_ref[q], pi_ref[q]
            cr_s[q] = pr * h0r - pi * h0i
            ci_s[q] = pr * h0i + pi * h0r


def _ssm_stage(ubuf, cr_s, ci_s, hs, wlag_ref, a8r_ref, a8i_ref, ck_ref, hr_ref, hi_ref, tt):
    slot = lax.broadcasted_iota(jnp.int32, (tt, LANES), 1) // SLOT
    to_low = [((slot + PAIRS_PER_TILE - s) % PAIRS_PER_TILE) < 2 for s in range(2)]
    same_parity = [((slot + sg) % 2) == 0 for sg in range(2)]

    def route(r):
        low = [jnp.where(to_low[s], r[s], r[s + 2]) for s in range(2)]
        high = [jnp.where(to_low[s], r[s + 2], r[s]) for s in range(2)]
        return [jnp.where(same_parity[sg % 2], src[0], src[1])
                for sg, src in zip(range(PAIRS_PER_TILE), (low, low, high, high))]

    ys = []
    for k in range(U_TILES):
        ub = ubuf[:, k * LANES:(k + 1) * LANES]
        rolled = []
        for s in range(LAGS):
            us = ub[LAGS:] if s == 0 else pltpu.roll(ub, s, 0)[LAGS:]
            if s % PAIRS_PER_TILE:
                us = pltpu.roll(us, SLOT * (s % PAIRS_PER_TILE), 1)
            rolled.append(us)
        halves = (route(rolled[:PAIRS_PER_TILE]), route(rolled[PAIRS_PER_TILE:]))

        for sg in range(PAIRS_PER_TILE):
            q = k * PAIRS_PER_TILE + sg
            xl = jnp.concatenate([halves[0][sg], halves[1][sg]], axis=1).astype(BF16)
            w = jnp.dot(xl, wlag_ref[q], preferred_element_type=F32)
            ar, ai = a8r_ref[q:q + 1, :], a8i_ref[q:q + 1, :]
            cr, ci = cr_s[q], ci_s[q]
            for b in range(tt // SUBLANES):
                blk = slice(b * SUBLANES, (b + 1) * SUBLANES)
                hr = w[blk, :LANES] + cr
                hi = w[blk, LANES:] + ci
                hs[blk, q * PAIR_N:q * PAIR_N + LANES] = hr
                hs[blk, q * PAIR_N + LANES:(q + 1) * PAIR_N] = hi
                cr = ar * hr - ai * hi
                ci = ar * hi + ai * hr
            cr_s[q] = cr
            ci_s[q] = ci
            hr_ref[0, q:q + 1, :] = hr[SUBLANES - 1:, :]
            hi_ref[0, q:q + 1, :] = hi[SUBLANES - 1:, :]

        cols = slice(k * PAIRS_PER_TILE * PAIR_N, (k + 1) * PAIRS_PER_TILE * PAIR_N)
        ys.append(jnp.dot(hs[:, cols].astype(BF16), ck_ref[k], preferred_element_type=F32))
    return jnp.concatenate(ys, axis=1)


def _glu_stage(y, u, sz, d, wglu_ref):
    z = jax.nn.gelu(y + d * u)
    g = jnp.dot(z.astype(BF16), wglu_ref[...], preferred_element_type=F32)
    return z * _sigmoid(g) * sz


def _out_stage(xa, xs, ga, gs, x, p, woa_ref, wos_ref, wout_ref, wpg_ref, wpp_ref, fgain):
    def mm(a, w_ref):
        return jnp.dot(a.astype(BF16), w_ref[...], preferred_element_type=F32)

    merged = ga * mm(xa, woa_ref) + gs * mm(xs, wos_ref)
    h = x + mm(merged, wout_ref)
    h = h + _sigmoid(mm(h, wpg_ref)) * mm(p, wpp_ref)
    ms = jnp.mean(h * h, axis=-1, keepdims=True)
    return h * lax.rsqrt(ms + EPS) * fgain


def _layer_kernel(sinks_ref, x_ref, p_ref, cos_ref, sina_ref, sinb_ref, gain_ref, w_in_ref,
                  wlag_ref, a8r_ref, a8i_ref, ck_ref, d_ref, wglu_ref,
                  woa_ref, wos_ref, wout_ref, wpg_ref, wpp_ref, fg_ref,
                  y_ref, k_ref, v_ref, hr_ref, hi_ref,
                  kbuf, vbuf, xa_s, ubuf, cr_s, ci_s, hs, *, tt):
    t = pl.program_id(1)

    @pl.when(t == 0)
    def _():
        kbuf[0:WINDOW, :] = jnp.zeros((WINDOW, KV2_WIDTH), BF16)
        vbuf[0:WINDOW, :] = jnp.zeros((WINDOW, KV2_WIDTH), BF16)
        _ssm_reset(ubuf, cr_s, ci_s)

    x = x_ref[0]
    pj = _proj_stage(x, gain_ref[...], cos_ref[...], sina_ref[...], sinb_ref[...], w_in_ref)
    k_ref[0] = pj["k"]
    v_ref[0] = pj["v"]
    for j in range(N_KV_HEADS):
        kbuf[WINDOW:WINDOW + tt, j * LANES:(j + 1) * LANES] = pj["k2"][j]
        vbuf[WINDOW:WINDOW + tt, j * LANES:(j + 1) * LANES] = pj["v2"][j]
    ubuf[LAGS:LAGS + tt, :] = pj["u"]

    def get_kv(c, kv):
        rows, cols = slice(c * CHUNK, c * CHUNK + KEYS), slice(kv * LANES, (kv + 1) * LANES)
        return kbuf[rows, cols], vbuf[rows, cols]

    def get_valid(c):
        if c * CHUNK >= WINDOW:
            return None
        in_seq = c * CHUNK + lax.broadcasted_iota(jnp.int32, (1, KEYS), 1) >= WINDOW
        return jnp.logical_or(in_seq, t > 0)

    def emit(c, tile, o):
        rows, cols = slice(c * CHUNK, (c + 1) * CHUNK), slice(tile * LANES, (tile + 1) * LANES)
        xa_s[rows, cols] = (o * pj["sa"][rows, cols]).astype(BF16)

    _attn_stage(sinks_ref, tt // CHUNK, lambda c, h: pj["qh"][h][c * CHUNK:(c + 1) * CHUNK],
                get_kv, get_valid, emit)

    y_ssm = _ssm_stage(ubuf, cr_s, ci_s, hs, wlag_ref, a8r_ref, a8i_ref, ck_ref, hr_ref, hi_ref, tt)
    xs = _glu_stage(y_ssm, pj["u"], pj["sz"], d_ref[...], wglu_ref)
    y_ref[0] = _out_stage(xa_s[...], xs, pj["ga"], pj["gs"], x, p_ref[0],
                          woa_ref, wos_ref, wout_ref, wpg_ref, wpp_ref, fg_ref[...])

    kbuf[0:WINDOW, :] = kbuf[tt:tt + WINDOW, :]
    vbuf[0:WINDOW, :] = vbuf[tt:tt + WINDOW, :]
    ubuf[0:LAGS, :] = ubuf[tt:tt + LAGS, :]


def _layer_fused(x, p, tabs, wts, consts):
    b, t, _ = x.shape
    tt = min(LAYER_ROWS, t)
    assert t % tt == 0 and tt >= WINDOW and tabs[0].shape[0] == t
    (gain, w_in, sinks, woa, d_skip, w_glu, wos, wout, wpg, wpp, fgain) = wts
    wlag, a8r, a8i, _, _, ck = consts

    def row_spec(w):
        return pl.BlockSpec((1, tt, w), lambda i, j: (i, j, 0))

    tab_spec = pl.BlockSpec((tt, LANES), lambda i, j: (j, 0))
    st_spec = pl.BlockSpec((1, PAIRS, LANES), lambda i, j: (i, 0, 0))
    consts_in = (gain, w_in, wlag, a8r, a8i, ck, d_skip, w_glu, woa, wos, wout, wpg, wpp, fgain)
    vmem = (sum(a.size * a.dtype.itemsize for a in consts_in)
            + 2 * tt * (2 * D_MODEL + PLE_DIM + 2 * KV_WIDTH + 3 * LANES) * 4
            + 2 * (WINDOW + tt) * KV2_WIDTH * 2 + tt * ATTN_WIDTH * 2 + (tt + LAGS) * SSM_WIDTH * 4
            + tt * N_STATE * 4 + 3 * tt * IN_WIDTH * 4)
    y, k, v, hr, hi = pl.pallas_call(
        functools.partial(_layer_kernel, tt=tt),
        grid=(b, t // tt),
        in_specs=[pl.BlockSpec(memory_space=pltpu.SMEM), row_spec(D_MODEL), row_spec(PLE_DIM),
                  tab_spec, tab_spec, tab_spec]
                 + [_const_spec(a.shape) for a in consts_in],
        out_specs=[row_spec(D_MODEL), row_spec(KV_WIDTH), row_spec(KV_WIDTH), st_spec, st_spec],
        out_shape=[jax.ShapeDtypeStruct((b, t, D_MODEL), F32),
                   jax.ShapeDtypeStruct((b, t, KV_WIDTH), F32),
                   jax.ShapeDtypeStruct((b, t, KV_WIDTH), F32),
                   jax.ShapeDtypeStruct((b, PAIRS, LANES), F32),
                   jax.ShapeDtypeStruct((b, PAIRS, LANES), F32)],
        scratch_shapes=[pltpu.VMEM((WINDOW + tt, KV2_WIDTH), BF16),
                        pltpu.VMEM((WINDOW + tt, KV2_WIDTH), BF16),
                        pltpu.VMEM((tt, ATTN_WIDTH), BF16),
                        pltpu.VMEM((tt + LAGS, SSM_WIDTH), F32),
                        pltpu.VMEM((PAIRS, SUBLANES, LANES), F32),
                        pltpu.VMEM((PAIRS, SUBLANES, LANES), F32),
                        pltpu.VMEM((tt, N_STATE), F32)],
        compiler_params=_params(vmem, 2),
        name="layer_prompt",
    )(sinks, x, p, *tabs, *consts_in)
    return y, k, v, hr, hi


def _proj_kernel(x_ref, gain_ref, cos_ref, sina_ref, sinb_ref, w_ref,
                 q_ref, k_ref, v_ref, k2_ref, v2_ref, sa_ref, u_ref, sz_ref, ga_ref, gs_ref):
    pj = _proj_stage(x_ref[...], gain_ref[...], cos_ref[...], sina_ref[...], sinb_ref[...], w_ref)
    for h in range(N_HEADS):
        q_ref[:, h * LANES:(h + 1) * LANES] = pj["qh"][h]
    for j in range(N_KV_HEADS):
        k2_ref[:, j * LANES:(j + 1) * LANES] = pj["k2"][j]
        v2_ref[:, j * LANES:(j + 1) * LANES] = pj["v2"][j]
    k_ref[...] = pj["k"]
    v_ref[...] = pj["v"]
    sa_ref[...] = pj["sa"]
    u_ref[...] = pj["u"]
    sz_ref[...] = pj["sz"]
    ga_ref[...] = pj["ga"]
    gs_ref[...] = pj["gs"]


def _proj(x2d, gain, cos, sina, sinb, w_in, seq):
    n = x2d.shape[0]
    tm = min(PROJ_ROWS, n)
    tab_rows = cos.shape[0]
    tab_tiles = tab_rows // tm
    assert n % tm == 0 and tab_rows % tm == 0

    def row_spec(w):
        return pl.BlockSpec((tm, w), lambda i: (i, 0))

    tab_spec = pl.BlockSpec((tm, LANES), lambda i: (i % tab_tiles, 0))
    widths = (QM_WIDTH, KV_WIDTH, KV_WIDTH, KV2_WIDTH, KV2_WIDTH,
              ATTN_WIDTH, SSM_WIDTH, SSM_WIDTH, D_MODEL, D_MODEL)
    dtypes = (BF16, F32, F32, BF16, BF16) + (F32,) * 5
    vmem = (2 * tm * D_MODEL * 4 + D_MODEL * IN_WIDTH * 2 + 3 * 2 * tm * LANES * 4
            + 3 * tm * IN_WIDTH * 4 + tm * D_MODEL * 8)
    return pl.pallas_call(
        _proj_kernel,
        grid=(n // tm,),
        in_specs=[row_spec(D_MODEL), _const_spec((1, D_MODEL)), tab_spec, tab_spec, tab_spec,
                  _const_spec((D_MODEL, IN_WIDTH))],
        out_specs=[row_spec(w) for w in widths],
        out_shape=[jax.ShapeDtypeStruct((n, w), d) for w, d in zip(widths, dtypes)],
        compiler_params=_params(vmem, 1),
        name=f"proj_{seq}",
    )(x2d, gain, cos, sina, sinb, w_in)


def _attn_kernel(sinks_ref, q_ref, k_ref, v_ref, sa_ref, o_ref, *, tq):
    t = pl.program_id(1)

    def rows(c):
        return slice(c * CHUNK, (c + 1) * CHUNK)

    def get_kv(c, kv):
        row0 = pl.multiple_of(t * tq + c * CHUNK, CHUNK)
        cols = slice(kv * LANES, (kv + 1) * LANES)
        return k_ref[0, pl.ds(row0, KEYS), cols], v_ref[0, pl.ds(row0, KEYS), cols]

    def emit(c, tile, o):
        cols = slice(tile * LANES, (tile + 1) * LANES)
        o_ref[0, rows(c), cols] = o * sa_ref[0, rows(c), cols]

    _attn_stage(sinks_ref, tq // CHUNK, lambda c, h: q_ref[0, rows(c), h * LANES:(h + 1) * LANES],
                get_kv, lambda c: None, emit)


def _attn(sinks, qm, k2pad, v2pad, sa, seq):
    b, t, _ = qm.shape
    tq = min(ATTN_ROWS, t)
    assert t % tq == 0 and k2pad.shape[1] == t + WINDOW

    def row_spec(w):
        return pl.BlockSpec((1, tq, w), lambda i, j: (i, j, 0))

    kv_spec = pl.BlockSpec((1, t + WINDOW, KV2_WIDTH), lambda i, j: (i, 0, 0))
    vmem = (2 * tq * (QM_WIDTH * 2 + 2 * ATTN_WIDTH * 4) + 2 * 2 * (t + WINDOW) * KV2_WIDTH * 2
            + 8 * Q_PER_KV * CHUNK * 2 * LANES * 4)
    return pl.pallas_call(
        functools.partial(_attn_kernel, tq=tq),
        grid=(b, t // tq),
        in_specs=[pl.BlockSpec(memory_space=pltpu.SMEM), row_spec(QM_WIDTH), kv_spec, kv_spec,
                  row_spec(ATTN_WIDTH)],
        out_specs=row_spec(ATTN_WIDTH),
        out_shape=jax.ShapeDtypeStruct((b, t, ATTN_WIDTH), F32),
        compiler_params=_params(vmem, 2),
        name=f"attn_{seq}",
    )(sinks, qm, k2pad, v2pad, sa)


def _ssm_kernel(u_ref, sz_ref, h0r_ref, h0i_ref, wlag_ref, a8r_ref, a8i_ref, pr_ref, pi_ref,
                ck_ref, d_ref, wglu_ref, xs_ref, hr_ref, hi_ref, ubuf, cr_s, ci_s, hs, *, tt):
    @pl.when(pl.program_id(1) == 0)
    def _():
        _ssm_reset(ubuf, cr_s, ci_s, (h0r_ref, h0i_ref, pr_ref, pi_ref))

    u = u_ref[0]
    ubuf[LAGS:LAGS + tt, :] = u
    y = _ssm_stage(ubuf, cr_s, ci_s, hs, wlag_ref, a8r_ref, a8i_ref, ck_ref, hr_ref, hi_ref, tt)
    xs_ref[0] = _glu_stage(y, u, sz_ref[0], d_ref[...], wglu_ref)
    ubuf[0:LAGS, :] = ubuf[tt:tt + LAGS, :]


def _ssm(u, sz, h0r, h0i, consts, d_skip, w_glu, seq):
    b, t, _ = u.shape
    tt = min(SSM_ROWS, t)
    assert t % tt == 0
    wlag, a8r, a8i, pr, pi, ck = consts
    row_spec = pl.BlockSpec((1, tt, SSM_WIDTH), lambda i, j: (i, j, 0))
    st_spec = pl.BlockSpec((1, PAIRS, LANES), lambda i, j: (i, 0, 0))
    vmem = (3 * 2 * tt * SSM_WIDTH * 4 + wlag.size * 2 + ck.size * 2 + w_glu.size * 2
            + tt * N_STATE * 4 + (LAGS + 2) * tt * SSM_WIDTH * 4 + 4 * tt * SSM_WIDTH * 4
            + 4 * tt * PAIR_N * 4 + 16 * PAIRS * SUBLANES * LANES * 4)
    return pl.pallas_call(
        functools.partial(_ssm_kernel, tt=tt),
        grid=(b, t // tt),
        in_specs=[row_spec, row_spec, st_spec, st_spec,
                  _const_spec(wlag.shape), _const_spec(a8r.shape), _const_spec(a8i.shape),
                  _const_spec(pr.shape), _const_spec(pi.shape), _const_spec(ck.shape),
                  _const_spec((1, SSM_WIDTH)), _const_spec(w_glu.shape)],
        out_specs=[row_spec, st_spec, st_spec],
        out_shape=[jax.ShapeDtypeStruct((b, t, SSM_WIDTH), F32),
                   jax.ShapeDtypeStruct((b, PAIRS, LANES), F32),
                   jax.ShapeDtypeStruct((b, PAIRS, LANES), F32)],
        scratch_shapes=[pltpu.VMEM((tt + LAGS, SSM_WIDTH), F32),
                        pltpu.VMEM((PAIRS, SUBLANES, LANES), F32),
                        pltpu.VMEM((PAIRS, SUBLANES, LANES), F32),
                        pltpu.VMEM((tt, N_STATE), F32)],
        compiler_params=_params(vmem, 2),
        name=f"ssm_{seq}",
    )(u, sz, h0r, h0i, wlag, a8r, a8i, pr, pi, ck, d_skip, w_glu)


def _out_kernel(xa_ref, xs_ref, ga_ref, gs_ref, x_ref, p_ref, woa_ref, wos_ref, wout_ref,
                wpg_ref, wpp_ref, fg_ref, y_ref):
    y_ref[...] = _out_stage(xa_ref[...], xs_ref[...], ga_ref[...], gs_ref[...], x_ref[...], p_ref[...],
                            woa_ref, wos_ref, wout_ref, wpg_ref, wpp_ref, fg_ref[...])


def _out(xa, xs, ga, gs, x2d, p2d, woa, wos, wout, wpg, wpp, fgain, seq):
    n = x2d.shape[0]
    tm = min(OUT_ROWS, n)
    assert n % tm == 0

    def row_spec(w):
        return pl.BlockSpec((tm, w), lambda i: (i, 0))

    weights = (woa, wos, wout, wpg, wpp)
    vmem = (2 * tm * (2 * ATTN_WIDTH + 4 * D_MODEL + PLE_DIM) * 4 + sum(w.size for w in weights) * 2
            + 8 * tm * D_MODEL * 4)
    return pl.pallas_call(
        _out_kernel,
        grid=(n // tm,),
        in_specs=[row_spec(ATTN_WIDTH), row_spec(SSM_WIDTH), row_spec(D_MODEL), row_spec(D_MODEL),
                  row_spec(D_MODEL), row_spec(PLE_DIM)]
                 + [_const_spec(w.shape) for w in weights] + [_const_spec((1, D_MODEL))],
        out_specs=row_spec(D_MODEL),
        out_shape=jax.ShapeDtypeStruct((n, D_MODEL), F32),
        compiler_params=_params(vmem, 1),
        name=f"out_{seq}",
    )(xa, xs, ga, gs, x2d, p2d, woa, wos, wout, wpg, wpp, fgain)


def _both_halves(a):
    h0, h1 = a[..., :HEAD_DIM], a[..., HEAD_DIM:]
    return jnp.concatenate([h0, h0, h1, h1], axis=-1).astype(BF16)


def _layer_split(x, p, tabs, k_prefix, v_prefix, h0r, h0i, wts, consts, seq):
    b, t, _ = x.shape
    assert t <= WINDOW
    (gain, w_in, sinks, woa, d_skip, w_glu, wos, wout, wpg, wpp, fgain) = wts
    x2d = x.reshape(b * t, D_MODEL)
    qm, k, v, k2, v2, sa, u, sz, ga, gs = _proj(x2d, gain, *tabs, w_in, seq)
    k2pad = jnp.concatenate([_both_halves(k_prefix), k2.reshape(b, t, KV2_WIDTH)], axis=1)
    v2pad = jnp.concatenate([_both_halves(v_prefix), v2.reshape(b, t, KV2_WIDTH)], axis=1)
    xa = _attn(sinks, qm.reshape(b, t, QM_WIDTH), k2pad, v2pad, sa.reshape(b, t, ATTN_WIDTH), seq)
    xs, hr, hi = _ssm(u.reshape(b, t, SSM_WIDTH), sz.reshape(b, t, SSM_WIDTH), h0r, h0i,
                      consts, d_skip, w_glu, seq)
    y = _out(xa.reshape(b * t, ATTN_WIDTH), xs.reshape(b * t, SSM_WIDTH), ga, gs, x2d,
             p.reshape(b * t, PLE_DIM), woa, wos, wout, wpg, wpp, fgain, seq)
    k_new = jnp.concatenate([k_prefix[:, t:], k.reshape(b, t, KV_WIDTH)], axis=1)
    v_new = jnp.concatenate([v_prefix[:, t:], v.reshape(b, t, KV_WIDTH)], axis=1)
    return y.reshape(b, t, D_MODEL), k_new, v_new, hr, hi


def _ssm_constants(a_re, a_im, log_dt, b_re, b_im, c_re, c_im):
    dt = jnp.exp(log_dt.astype(F32))[:, None]
    lr = a_re.astype(F32).reshape(PAIRS, 1, LANES)
    li = a_im.astype(F32).reshape(PAIRS, 1, LANES)
    xr = (a_re.astype(F32) * dt).reshape(PAIRS, 1, LANES)
    xi = (a_im.astype(F32) * dt).reshape(PAIRS, 1, LANES)

    def apow(n):
        mag = jnp.exp(xr * n)
        return mag * jnp.cos(xi * n), mag * jnp.sin(xi * n)

    ar, ai = apow(1.0)
    nr, ni = ar - 1.0, ai
    den = lr * lr + li * li
    fr, fi = (nr * lr + ni * li) / den, (ni * lr - nr * li) / den

    qq, rr = np.arange(PAIRS)[:, None], np.arange(PAIR_K)[None, :]
    half, slot_i, gl = rr // LANES, (rr % LANES) // SLOT, (rr % SLOT) // SSM_GROUP
    lag_tab = (PAIRS_PER_TILE * half + (slot_i - qq) % PAIRS_PER_TILE).astype(np.float32)[:, :, None]
    same_group = (gl[0][:, None] == (np.arange(LANES) // SSM_STATE)[None, :]).astype(np.float32)

    def b_rows(bm):
        t = jnp.transpose(bm.astype(F32).reshape(PAIRS, 2, SSM_STATE, SSM_GROUP), (0, 3, 1, 2))
        t = t.reshape(PAIRS, 1, SSM_GROUP, LANES)
        t = jnp.broadcast_to(t, (PAIRS, PAIR_K // SSM_GROUP, SSM_GROUP, LANES))
        return t.reshape(PAIRS, PAIR_K, LANES)

    br, bi = b_rows(b_re), b_rows(b_im)
    bbr, bbi = fr * br - fi * bi, fr * bi + fi * br
    er, ei = apow(lag_tab)
    wlag = jnp.concatenate([(er * bbr - ei * bbi) * same_group,
                            (er * bbi + ei * bbr) * same_group], axis=-1).astype(BF16)

    a8r, a8i = (a.reshape(PAIRS, LANES) for a in apow(float(LAGS)))
    pwr, pwi = apow(np.arange(1, LAGS + 1, dtype=np.float32)[None, :, None])

    def c_cols(c):
        t = jnp.transpose(c.astype(F32).reshape(U_TILES, LANES // SSM_GROUP, SSM_GROUP, SSM_STATE),
                          (0, 3, 1, 2))
        return t.reshape(U_TILES, 1, 1, 1, SSM_STATE, LANES)

    cols_group = np.arange(LANES) // SSM_GROUP
    rows_group = 2 * np.arange(PAIRS_PER_TILE)[:, None] + np.arange(2)[None, :]
    c_mask = (rows_group[:, None, :, None, None] == cols_group[None, None, None, None, :])
    c_mask = c_mask.astype(np.float32)[None]
    ck = jnp.concatenate([c_cols(c_re) * c_mask, -c_cols(c_im) * c_mask], axis=2)
    ck = ck.reshape(U_TILES, PAIRS_PER_TILE * PAIR_N, LANES).astype(BF16)
    return wlag, a8r, a8i, pwr, pwi, ck


def _rope_tables(pos0, t, rows):
    half = ROT_DIM // 2
    d = np.arange(LANES) % HEAD_DIM
    inv = jnp.power(ROPE_THETA, -jnp.asarray(d % half, F32) * 2.0 / ROT_DIM)
    pos = (pos0 + jnp.arange(t)).astype(F32)
    ang = pos[:, None] * inv[None, :]
    cos, sin = jnp.cos(ang), jnp.sin(ang)
    cos_t = jnp.where((d < ROT_DIM)[None, :], cos, 1.0)
    sina = jnp.where(((d >= half) & (d < ROT_DIM))[None, :], sin, 0.0)
    sinb = jnp.where((d < half)[None, :], -sin, 0.0)
    reps = (max(rows // t, 1), 1)
    return tuple(jnp.tile(a, reps) for a in (cos_t, sina, sinb))


def kernel(x_prompt, x_sample, p_prompt, p_sample, cache_attn_k, cache_attn_v, state_ssm_re,
           state_ssm_im, norm_gain, w_in, attn_sinks, w_o_attn, ssm_a_re, ssm_a_im, ssm_log_dt,
           ssm_b_re, ssm_b_im, ssm_c_re, ssm_c_im, ssm_d, ssm_w_glu, w_o_ssm, w_out,
           w_ple_gate, w_ple_proj, final_norm_gain):
    assert norm_gain.shape[0] == 1, "single-layer model"
    bp, tp, _ = x_prompt.shape
    bs, ts, _ = x_sample.shape
    wts = (norm_gain[0].reshape(1, D_MODEL).astype(F32), w_in[0].astype(BF16),
           attn_sinks[0].astype(F32), w_o_attn[0].astype(BF16),
           ssm_d[0].reshape(1, SSM_WIDTH).astype(F32), ssm_w_glu[0].astype(BF16),
           w_o_ssm[0].astype(BF16), w_out[0].astype(BF16), w_ple_gate[0].astype(BF16),
           w_ple_proj[0].astype(BF16), final_norm_gain.reshape(1, D_MODEL).astype(F32))
    consts = _ssm_constants(ssm_a_re[0], ssm_a_im[0], ssm_log_dt[0], ssm_b_re[0], ssm_b_im[0],
                            ssm_c_re[0], ssm_c_im[0])

    y_p, k_p, v_p, hr_p, hi_p = _layer_fused(x_prompt, p_prompt[0], _rope_tables(0, tp, tp), wts, consts)

    ck = cache_attn_k[0].reshape(bs, WINDOW, KV_WIDTH).astype(F32)
    cv = cache_attn_v[0].reshape(bs, WINDOW, KV_WIDTH).astype(F32)
    h0r = state_ssm_re[0].reshape(bs, PAIRS, LANES).astype(F32)
    h0i = state_ssm_im[0].reshape(bs, PAIRS, LANES).astype(F32)
    tabs_s = _rope_tables(PAST_LEN, ts, min(PROJ_ROWS, bs * ts))
    y_s, k_s, v_s, hr_s, hi_s = _layer_split(x_sample, p_sample[0], tabs_s, ck, cv, h0r, h0i,
                                             wts, consts, "sample")

    def kv_out(a, b):
        return a.reshape(1, b, WINDOW, N_KV_HEADS, HEAD_DIM)

    def st_out(a, b):
        return a.reshape(1, b, SSM_GROUPS, SSM_STATE)

    return (y_p, y_s, kv_out(k_p[:, tp - WINDOW:], bp), kv_out(v_p[:, tp - WINDOW:], bp),
            st_out(hr_p, bp), st_out(hi_p, bp), kv_out(k_s, bs), kv_out(v_s, bs),
            st_out(hr_s, bs), st_out(hi_s, bs))
```

```python
import functools

import numpy as np
import jax
import jax.numpy as jnp
from jax import lax
from jax.experimental import pallas as pl
from jax.experimental.pallas import tpu as pltpu

F32 = jnp.float32
BF16 = jnp.bfloat16

LANES = 128
SUBLANES = 8
V7X_VMEM_BYTES = 64 * 1024 * 1024

D_MODEL = 1024
CHUNK = 64
WINDOW = 128
N_HEADS = 8
N_KV_HEADS = 2
HEAD_DIM = 64
Q_PER_KV = N_HEADS // N_KV_HEADS
LOG2E = 1.4426950408889634
Q_SCALE = HEAD_DIM ** -0.5 * LOG2E
ATTN_WIDTH = N_HEADS * HEAD_DIM
KV_WIDTH = N_KV_HEADS * HEAD_DIM
ROT_DIM = HEAD_DIM // 4
ROPE_THETA = 500000.0
SSM_WIDTH = D_MODEL // 2
SSM_GROUP = 16
SSM_GROUPS = SSM_WIDTH // SSM_GROUP
SSM_STATE = 64
PLE_DIM = 256
PAST_LEN = 1024
EPS = 1e-6

O_Q = 0
O_K = O_Q + ATTN_WIDTH
O_V = O_K + KV_WIDTH
O_ZA = O_V + KV_WIDTH
O_U = O_ZA + ATTN_WIDTH
O_ZS = O_U + SSM_WIDTH
O_GA = O_ZS + SSM_WIDTH
O_GS = O_GA + D_MODEL
IN_WIDTH = O_GS + D_MODEL

QM_WIDTH = N_HEADS * LANES
KV2_WIDTH = N_KV_HEADS * LANES
KEYS = WINDOW + CHUNK
LAGS = SUBLANES
PAIRS = SSM_GROUPS // 2
PAIR_K = 2 * LAGS * SSM_GROUP
PAIR_N = 2 * 2 * SSM_STATE
N_STATE = PAIRS * PAIR_N
U_TILES = SSM_WIDTH // LANES
PAIRS_PER_TILE = PAIRS // U_TILES
SLOT = 2 * SSM_GROUP
BF16_ROWS = 2 * SUBLANES
assert PAIRS_PER_TILE == 4 and LAGS == 2 * PAIRS_PER_TILE

LAYER_ROWS = 512
PROJ_ROWS = 256
ATTN_ROWS = 512
SSM_ROWS = 256
OUT_ROWS = 256


def _sigmoid(x):
    return 1.0 / (1.0 + jnp.exp(-x))


def _const_spec(shape):
    zeros = (0,) * len(shape)
    return pl.BlockSpec(shape, lambda *_: zeros, pipeline_mode=pl.Buffered(1))


def _params(vmem_bytes, n_grid):
    return pltpu.CompilerParams(
        dimension_semantics=("arbitrary",) * n_grid,
        vmem_limit_bytes=min(int(vmem_bytes), V7X_VMEM_BYTES - 8 * 1024 * 1024),
    )


def _run(steps):
    for step in steps:
        step()


def _spread(main, other):
    merged, j = [], 0
    for i, step in enumerate(main):
        while j < len(other) and j * len(main) <= i * len(other):
            merged.append(other[j])
            j += 1
        merged.append(step)
    return merged + other[j:]


def _proj_steps(get_x, gain, get_tabs, w_ref, o, store=None):
    st = {}

    def norm():
        x = get_x()
        ms = jnp.mean(x * x, axis=-1, keepdims=True)
        st["xn"] = (x * lax.rsqrt(ms + EPS) * gain).astype(BF16)
        st["lo"] = lax.broadcasted_iota(jnp.int32, (x.shape[0], LANES), 1) < HEAD_DIM

    def seg(a, b):
        return jnp.dot(st["xn"], w_ref[:, a:b], preferred_element_type=F32)

    def rope(t):
        cos, sina, sinb = get_tabs()
        return (t * cos + pltpu.roll(t, ROT_DIM // 2, 1) * sina
                + pltpu.roll(t, LANES - ROT_DIM // 2, 1) * sinb)

    def both_halves(t):
        tr = pltpu.roll(t, HEAD_DIM, 1)
        return [jnp.where(st["lo"], t, tr).astype(BF16), jnp.where(st["lo"], tr, t).astype(BF16)]

    def done(name):
        if store is not None:
            store(name)

    def q():
        zq = seg(O_Q, O_K)
        o["qh"] = []
        for j in range(ATTN_WIDTH // LANES):
            qt = rope(zq[:, j * LANES:(j + 1) * LANES]) * Q_SCALE
            o["qh"] += [jnp.where(st["lo"], qt, 0.0).astype(BF16),
                        jnp.where(st["lo"], 0.0, qt).astype(BF16)]

    def k():
        o["k"] = rope(seg(O_K, O_V))
        o["k2"] = both_halves(o["k"])
        done("k")

    def v():
        o["v"] = seg(O_V, O_ZA)
        o["v2"] = both_halves(o["v"])
        done("v")

    def za():
        z = seg(O_ZA, O_U)
        o["sa"] = z * _sigmoid(z)

    def u():
        o["u"] = seg(O_U, O_ZS)
        done("u")

    def zs():
        z = seg(O_ZS, O_GA)
        o["sz"] = z * _sigmoid(z)

    def ga():
        o["ga"] = _sigmoid(seg(O_GA, O_GS))

    def gs():
        o["gs"] = _sigmoid(seg(O_GS, IN_WIDTH))

    return [norm, q, k, v, za, u, zs, ga, gs]


def _attn_steps(sinks_ref, chunks, get_q, get_kv, get_valid, emit):
    nt = (((1,), (1,)), ((), ()))
    units = [(c, kv) for c in chunks for kv in range(N_KV_HEADS)]
    n = len(units)
    st = {}

    def scores(c, kv):
        k2, v2 = get_kv(c, kv)
        qm = jnp.concatenate([get_q(c, kv * Q_PER_KV + h) for h in range(Q_PER_KV)], axis=0)
        s = lax.dot_general(qm, k2, nt, preferred_element_type=F32)
        valid = get_valid(c)
        if valid is not None:
            s = jnp.where(valid, s, -jnp.inf)
        return s, v2

    def softmax(s, kv):
        head_row = lax.broadcasted_iota(jnp.int32, (Q_PER_KV * CHUNK, 1), 0) // CHUNK
        sk = [sinks_ref[kv * Q_PER_KV + h] * LOG2E for h in range(Q_PER_KV)]
        sink = jnp.where(head_row == 0, sk[0],
                         jnp.where(head_row == 1, sk[1], jnp.where(head_row == 2, sk[2], sk[3])))
        m = jnp.maximum(jnp.max(s, axis=1, keepdims=True), sink)
        e = jnp.exp2(s - m)
        den = jnp.sum(e, axis=1, keepdims=True) + jnp.exp2(sink - m)
        return e.astype(BF16), den

    def output(e, den, v2, c, kv):
        lo_q = lax.broadcasted_iota(jnp.int32, (CHUNK, LANES), 1) < HEAD_DIM
        o = jnp.dot(e, v2, preferred_element_type=F32) / den
        for j in range(Q_PER_KV // 2):
            even = o[2 * j * CHUNK:(2 * j + 1) * CHUNK]
            odd = o[(2 * j + 1) * CHUNK:(2 * j + 2) * CHUNK]
            emit(c, kv * (Q_PER_KV // 2) + j, jnp.where(lo_q, even, odd))

    def make(i):
        def step():
            if i < n:
                st[i] = scores(*units[i])
            if 0 <= i - 1 < n:
                s, v2 = st[i - 1]
                st[i - 1] = softmax(s, units[i - 1][1]) + (v2,)
            if 0 <= i - 2 < n:
                e, den, v2 = st.pop(i - 2)
                output(e, den, v2, *units[i - 2])
        return step

    return [make(i) for i in range(n + 2)]


def _ssm_reset(ubuf, cr_s, ci_s, h0=None):
    ubuf[0:LAGS, :] = jnp.zeros((LAGS, SSM_WIDTH), F32)
    if h0 is None:
        cr_s[...] = jnp.zeros(cr_s.shape, F32)
        ci_s[...] = jnp.zeros(ci_s.shape, F32)
    else:
        h0r_ref, h0i_ref, pr_ref, pi_ref = h0
        for q in range(PAIRS):
            h0r, h0i = h0r_ref[0, q:q + 1, :], h0i_ref[0, q:q + 1, :]
            pr, pi = pr_ref[q], pi_ref[q]
            cr_s[q] = pr * h0r - pi * h0i
            ci_s[q] = pr * h0i + pi * h0r


def _ssm_steps(ubuf, row0, tt, cr_s, ci_s, hs, wlag_ref, a8r_ref, a8i_ref, ck_ref, hr_ref, hi_ref, o):
    st = {"ys": []}

    def setup():
        slot = lax.broadcasted_iota(jnp.int32, (tt, LANES), 1) // SLOT
        st["to_low"] = [((slot + PAIRS_PER_TILE - s) % PAIRS_PER_TILE) < 2 for s in range(2)]
        st["same_parity"] = [((slot + sg) % 2) == 0 for sg in range(2)]

    def route(r):
        low = [jnp.where(st["to_low"][s], r[s], r[s + 2]) for s in range(2)]
        high = [jnp.where(st["to_low"][s], r[s + 2], r[s]) for s in range(2)]
        return [jnp.where(st["same_parity"][sg % 2], src[0], src[1])
                for sg, src in zip(range(PAIRS_PER_TILE), (low, low, high, high))]

    def lag_copies(k):
        def step():
            ub = ubuf[row0:row0 + LAGS + tt, k * LANES:(k + 1) * LANES]
            rolled = []
            for s in range(LAGS):
                us = ub[LAGS:] if s == 0 else pltpu.roll(ub, s, 0)[LAGS:]
                if s % PAIRS_PER_TILE:
                    us = pltpu.roll(us, SLOT * (s % PAIRS_PER_TILE), 1)
                rolled.append(us)
            st["halves"] = (route(rolled[:PAIRS_PER_TILE]), route(rolled[PAIRS_PER_TILE:]))
        return step

    def pair(k, sg):
        def step():
            q = k * PAIRS_PER_TILE + sg
            xl = jnp.concatenate([st["halves"][0][sg], st["halves"][1][sg]], axis=1).astype(BF16)
            w = jnp.dot(xl, wlag_ref[q], preferred_element_type=F32)
            ar, ai = a8r_ref[q:q + 1, :], a8i_ref[q:q + 1, :]
            cr, ci = cr_s[q], ci_s[q]
            for b2 in range(tt // BF16_ROWS):
                hrs, his = [], []
                for b in (2 * b2, 2 * b2 + 1):
                    blk = slice(b * SUBLANES, (b + 1) * SUBLANES)
                    hr = w[blk, :LANES] + cr
                    hi = w[blk, LANES:] + ci
                    cr = ar * hr - ai * hi
                    ci = ar * hi + ai * hr
                    hrs.append(hr)
                    his.append(hi)
                blk2 = slice(b2 * BF16_ROWS, (b2 + 1) * BF16_ROWS)
                hs[blk2, q * PAIR_N:q * PAIR_N + LANES] = jnp.concatenate(hrs, axis=0).astype(BF16)
                hs[blk2, q * PAIR_N + LANES:(q + 1) * PAIR_N] = jnp.concatenate(his, axis=0).astype(BF16)
            cr_s[q] = cr
            ci_s[q] = ci
            hr_ref[0, q:q + 1, :] = hr[SUBLANES - 1:, :]
            hi_ref[0, q:q + 1, :] = hi[SUBLANES - 1:, :]
        return step

    def c_proj(k):
        def step():
            cols = slice(k * PAIRS_PER_TILE * PAIR_N, (k + 1) * PAIRS_PER_TILE * PAIR_N)
            st["ys"].append(jnp.dot(hs[:, cols], ck_ref[k], preferred_element_type=F32))
            if k == U_TILES - 1:
                o["y"] = jnp.concatenate(st["ys"], axis=1)
        return step

    steps = [setup]
    for k in range(U_TILES):
        steps += [lag_copies(k)] + [pair(k, sg) for sg in range(PAIRS_PER_TILE)] + [c_proj(k)]
    return steps


def _glu(y, u, sz, d, wglu_ref):
    z = jax.nn.gelu(y + d * u)
    g = jnp.dot(z.astype(BF16), wglu_ref[...], preferred_element_type=F32)
    return z * _sigmoid(g) * sz


def _out_steps(src, woa_ref, wos_ref, wout_ref, wpg_ref, wpp_ref, fgain, emit):
    st = {}

    def mm(a, w_ref):
        return jnp.dot(a.astype(BF16), w_ref[...], preferred_element_type=F32)

    def branches():
        st["merged"] = src["ga"]() * mm(src["xa"](), woa_ref) + src["gs"]() * mm(src["xs"](), wos_ref)

    def residual():
        st["h"] = src["x"]() + mm(st.pop("merged"), wout_ref)

    def embed_gate():
        h = st.pop("h")
        st["h"] = h + _sigmoid(mm(h, wpg_ref)) * mm(src["p"](), wpp_ref)

    def norm():
        h = st.pop("h")
        ms = jnp.mean(h * h, axis=-1, keepdims=True)
        emit(h * lax.rsqrt(ms + EPS) * fgain)

    return [branches, residual, embed_gate, norm]


def _layer_kernel(sinks_ref, x_ref, p_ref, cos_ref, sina_ref, sinb_ref, gain_ref, w_in_ref,
                  wlag_ref, a8r_ref, a8i_ref, ck_ref, d_ref, wglu_ref,
                  woa_ref, wos_ref, wout_ref, wpg_ref, wpp_ref, fg_ref,
                  y_ref, k_ref, v_ref, hr_ref, hi_ref,
                  kbuf, vbuf, xa_s, ubuf, cr_s, ci_s, hs, *, tt):
    t = pl.program_id(1)
    half = tt // 2

    @pl.when(t == 0)
    def _():
        kbuf[0:WINDOW, :] = jnp.zeros((WINDOW, KV2_WIDTH), BF16)
        vbuf[0:WINDOW, :] = jnp.zeros((WINDOW, KV2_WIDTH), BF16)
        _ssm_reset(ubuf, cr_s, ci_s)

    pj, so = [{}, {}], [{}, {}]

    def rows(h):
        return slice(h * half, (h + 1) * half)

    def proj(h):
        def store(name):
            if name == "u":
                ubuf[LAGS + h * half:LAGS + (h + 1) * half, :] = pj[h]["u"]
                return
            full_ref, buf = (k_ref, kbuf) if name == "k" else (v_ref, vbuf)
            full_ref[0, rows(h)] = pj[h][name]
            for j in range(N_KV_HEADS):
                buf[WINDOW + h * half:WINDOW + (h + 1) * half, j * LANES:(j + 1) * LANES] = pj[h][name + "2"][j]

        return _proj_steps(lambda: x_ref[0, rows(h)], gain_ref[...],
                           lambda: (cos_ref[rows(h)], sina_ref[rows(h)], sinb_ref[rows(h)]),
                           w_in_ref, pj[h], store)

    def mid(h):
        def get_q(c, head):
            r0 = c * CHUNK - h * half
            return pj[h]["qh"][head][r0:r0 + CHUNK]

        def get_kv(c, kv):
            krows, cols = slice(c * CHUNK, c * CHUNK + KEYS), slice(kv * LANES, (kv + 1) * LANES)
            return kbuf[krows, cols], vbuf[krows, cols]

        def get_valid(c):
            if c * CHUNK >= WINDOW:
                return None
            in_seq = c * CHUNK + lax.broadcasted_iota(jnp.int32, (1, KEYS), 1) >= WINDOW
            return jnp.logical_or(in_seq, t > 0)

        def emit(c, tile, o):
            r0, cols = c * CHUNK - h * half, slice(tile * LANES, (tile + 1) * LANES)
            xa_s[c * CHUNK:(c + 1) * CHUNK, cols] = (o * pj[h]["sa"][r0:r0 + CHUNK, cols]).astype(BF16)

        def glu():
            so[h]["xs"] = _glu(so[h].pop("y"), pj[h]["u"], pj[h]["sz"], d_ref[...], wglu_ref)

        chunks = range(h * half // CHUNK, (h + 1) * half // CHUNK)
        return (_attn_steps(sinks_ref, chunks, get_q, get_kv, get_valid, emit)
                + _ssm_steps(ubuf, h * half, half, cr_s, ci_s, hs, wlag_ref, a8r_ref, a8i_ref, ck_ref,
                             hr_ref, hi_ref, so[h])
                + [glu])

    def out(h):
        src = dict(xa=lambda: xa_s[rows(h)], xs=lambda: so[h]["xs"], ga=lambda: pj[h]["ga"],
                   gs=lambda: pj[h]["gs"], x=lambda: x_ref[0, rows(h)], p=lambda: p_ref[0, rows(h)])

        def emit(y):
            y_ref[0, rows(h)] = y

        return _out_steps(src, woa_ref, wos_ref, wout_ref, wpg_ref, wpp_ref, fg_ref[...], emit)

    _run(proj(0))
    _run(_spread(mid(0), proj(1)))
    _run(_spread(mid(1), out(0)))
    _run(out(1))

    kbuf[0:WINDOW, :] = kbuf[tt:tt + WINDOW, :]
    vbuf[0:WINDOW, :] = vbuf[tt:tt + WINDOW, :]
    ubuf[0:LAGS, :] = ubuf[tt:tt + LAGS, :]


def _layer_fused(x, p, tabs, wts, consts):
    b, t, _ = x.shape
    tt = min(LAYER_ROWS, t)
    assert t % tt == 0 and tt // 2 >= WINDOW and (tt // 2) % BF16_ROWS == 0 and tabs[0].shape[0] == t
    (gain, w_in, sinks, woa, d_skip, w_glu, wos, wout, wpg, wpp, fgain) = wts
    wlag, a8r, a8i, _, _, ck = consts

    def row_spec(w):
        return pl.BlockSpec((1, tt, w), lambda i, j: (i, j, 0))

    tab_spec = pl.BlockSpec((tt, LANES), lambda i, j: (j, 0))
    st_spec = pl.BlockSpec((1, PAIRS, LANES), lambda i, j: (i, 0, 0))
    consts_in = (gain, w_in, wlag, a8r, a8i, ck, d_skip, w_glu, woa, wos, wout, wpg, wpp, fgain)
    vmem = (sum(a.size * a.dtype.itemsize for a in consts_in)
            + 2 * tt * (2 * D_MODEL + PLE_DIM + 2 * KV_WIDTH + 3 * LANES) * 4
            + 2 * (WINDOW + tt) * KV2_WIDTH * 2 + tt * ATTN_WIDTH * 2 + (tt + LAGS) * SSM_WIDTH * 4
            + (tt // 2) * N_STATE * 2 + 3 * tt * IN_WIDTH * 4)
    y, k, v, hr, hi = pl.pallas_call(
        functools.partial(_layer_kernel, tt=tt),
        grid=(b, t // tt),
        in_specs=[pl.BlockSpec(memory_space=pltpu.SMEM), row_spec(D_MODEL), row_spec(PLE_DIM),
                  tab_spec, tab_spec, tab_spec]
                 + [_const_spec(a.shape) for a in consts_in],
        out_specs=[row_spec(D_MODEL), row_spec(KV_WIDTH), row_spec(KV_WIDTH), st_spec, st_spec],
        out_shape=[jax.ShapeDtypeStruct((b, t, D_MODEL), F32),
                   jax.ShapeDtypeStruct((b, t, KV_WIDTH), F32),
                   jax.ShapeDtypeStruct((b, t, KV_WIDTH), F32),
                   jax.ShapeDtypeStruct((b, PAIRS, LANES), F32),
                   jax.ShapeDtypeStruct((b, PAIRS, LANES), F32)],
        scratch_shapes=[pltpu.VMEM((WINDOW + tt, KV2_WIDTH), BF16),
                        pltpu.VMEM((WINDOW + tt, KV2_WIDTH), BF16),
                        pltpu.VMEM((tt, ATTN_WIDTH), BF16),
                        pltpu.VMEM((tt + LAGS, SSM_WIDTH), F32),
                        pltpu.VMEM((PAIRS, SUBLANES, LANES), F32),
                        pltpu.VMEM((PAIRS, SUBLANES, LANES), F32),
                        pltpu.VMEM((tt // 2, N_STATE), BF16)],
        compiler_params=_params(vmem, 2),
        name="layer_prompt",
    )(sinks, x, p, *tabs, *consts_in)
    return y, k, v, hr, hi


def _proj_kernel(x_ref, gain_ref, cos_ref, sina_ref, sinb_ref, w_ref,
                 q_ref, k_ref, v_ref, k2_ref, v2_ref, sa_ref, u_ref, sz_ref, ga_ref, gs_ref):
    pj = {}
    _run(_proj_steps(lambda: x_ref[...], gain_ref[...],
                     lambda: (cos_ref[...], sina_ref[...], sinb_ref[...]), w_ref, pj))
    for h in range(N_HEADS):
        q_ref[:, h * LANES:(h + 1) * LANES] = pj["qh"][h]
    for j in range(N_KV_HEADS):
        k2_ref[:, j * LANES:(j + 1) * LANES] = pj["k2"][j]
        v2_ref[:, j * LANES:(j + 1) * LANES] = pj["v2"][j]
    k_ref[...] = pj["k"]
    v_ref[...] = pj["v"]
    sa_ref[...] = pj["sa"]
    u_ref[...] = pj["u"]
    sz_ref[...] = pj["sz"]
    ga_ref[...] = pj["ga"]
    gs_ref[...] = pj["gs"]


def _proj(x2d, gain, cos, sina, sinb, w_in, seq):
    n = x2d.shape[0]
    tm = min(PROJ_ROWS, n)
    tab_rows = cos.shape[0]
    tab_tiles = tab_rows // tm
    assert n % tm == 0 and tab_rows % tm == 0

    def row_spec(w):
        return pl.BlockSpec((tm, w), lambda i: (i, 0))

    tab_spec = pl.BlockSpec((tm, LANES), lambda i: (i % tab_tiles, 0))
    widths = (QM_WIDTH, KV_WIDTH, KV_WIDTH, KV2_WIDTH, KV2_WIDTH,
              ATTN_WIDTH, SSM_WIDTH, SSM_WIDTH, D_MODEL, D_MODEL)
    dtypes = (BF16, F32, F32, BF16, BF16) + (F32,) * 5
    vmem = (2 * tm * D_MODEL * 4 + D_MODEL * IN_WIDTH * 2 + 3 * 2 * tm * LANES * 4
            + 3 * tm * IN_WIDTH * 4 + tm * D_MODEL * 8)
    return pl.pallas_call(
        _proj_kernel,
        grid=(n // tm,),
        in_specs=[row_spec(D_MODEL), _const_spec((1, D_MODEL)), tab_spec, tab_spec, tab_spec,
                  _const_spec((D_MODEL, IN_WIDTH))],
        out_specs=[row_spec(w) for w in widths],
        out_shape=[jax.ShapeDtypeStruct((n, w), d) for w, d in zip(widths, dtypes)],
        compiler_params=_params(vmem, 1),
        name=f"proj_{seq}",
    )(x2d, gain, cos, sina, sinb, w_in)


def _attn_kernel(sinks_ref, q_ref, k_ref, v_ref, sa_ref, o_ref, *, tq):
    t = pl.program_id(1)

    def rows(c):
        return slice(c * CHUNK, (c + 1) * CHUNK)

    def get_kv(c, kv):
        row0 = pl.multiple_of(t * tq + c * CHUNK, CHUNK)
        cols = slice(kv * LANES, (kv + 1) * LANES)
        return k_ref[0, pl.ds(row0, KEYS), cols], v_ref[0, pl.ds(row0, KEYS), cols]

    def emit(c, tile, o):
        cols = slice(tile * LANES, (tile + 1) * LANES)
        o_ref[0, rows(c), cols] = o * sa_ref[0, rows(c), cols]

    _run(_attn_steps(sinks_ref, range(tq // CHUNK),
                     lambda c, h: q_ref[0, rows(c), h * LANES:(h + 1) * LANES],
                     get_kv, lambda c: None, emit))


def _attn(sinks, qm, k2pad, v2pad, sa, seq):
    b, t, _ = qm.shape
    tq = min(ATTN_ROWS, t)
    assert t % tq == 0 and k2pad.shape[1] == t + WINDOW

    def row_spec(w):
        return pl.BlockSpec((1, tq, w), lambda i, j: (i, j, 0))

    kv_spec = pl.BlockSpec((1, t + WINDOW, KV2_WIDTH), lambda i, j: (i, 0, 0))
    vmem = (2 * tq * (QM_WIDTH * 2 + 2 * ATTN_WIDTH * 4) + 2 * 2 * (t + WINDOW) * KV2_WIDTH * 2
            + 8 * Q_PER_KV * CHUNK * 2 * LANES * 4)
    return pl.pallas_call(
        functools.partial(_attn_kernel, tq=tq),
        grid=(b, t // tq),
        in_specs=[pl.BlockSpec(memory_space=pltpu.SMEM), row_spec(QM_WIDTH), kv_spec, kv_spec,
                  row_spec(ATTN_WIDTH)],
        out_specs=row_spec(ATTN_WIDTH),
        out_shape=jax.ShapeDtypeStruct((b, t, ATTN_WIDTH), F32),
        compiler_params=_params(vmem, 2),
        name=f"attn_{seq}",
    )(sinks, qm, k2pad, v2pad, sa)


def _ssm_kernel(u_ref, sz_ref, h0r_ref, h0i_ref, wlag_ref, a8r_ref, a8i_ref, pr_ref, pi_ref,
                ck_ref, d_ref, wglu_ref, xs_ref, hr_ref, hi_ref, ubuf, cr_s, ci_s, hs, *, tt):
    @pl.when(pl.program_id(1) == 0)
    def _():
        _ssm_reset(ubuf, cr_s, ci_s, (h0r_ref, h0i_ref, pr_ref, pi_ref))

    u = u_ref[0]
    ubuf[LAGS:LAGS + tt, :] = u
    so = {}
    _run(_ssm_steps(ubuf, 0, tt, cr_s, ci_s, hs, wlag_ref, a8r_ref, a8i_ref, ck_ref, hr_ref, hi_ref, so))
    xs_ref[0] = _glu(so["y"], u, sz_ref[0], d_ref[...], wglu_ref)
    ubuf[0:LAGS, :] = ubuf[tt:tt + LAGS, :]


def _ssm(u, sz, h0r, h0i, consts, d_skip, w_glu, seq):
    b, t, _ = u.shape
    tt = min(SSM_ROWS, t)
    assert t % tt == 0 and tt % BF16_ROWS == 0
    wlag, a8r, a8i, pr, pi, ck = consts
    row_spec = pl.BlockSpec((1, tt, SSM_WIDTH), lambda i, j: (i, j, 0))
    st_spec = pl.BlockSpec((1, PAIRS, LANES), lambda i, j: (i, 0, 0))
    vmem = (3 * 2 * tt * SSM_WIDTH * 4 + wlag.size * 2 + ck.size * 2 + w_glu.size * 2
            + tt * N_STATE * 2 + (LAGS + 2) * tt * SSM_WIDTH * 4 + 4 * tt * SSM_WIDTH * 4
            + 4 * tt * PAIR_N * 4 + 16 * PAIRS * SUBLANES * LANES * 4)
    return pl.pallas_call(
        functools.partial(_ssm_kernel, tt=tt),
        grid=(b, t // tt),
        in_specs=[row_spec, row_spec, st_spec, st_spec,
                  _const_spec(wlag.shape), _const_spec(a8r.shape), _const_spec(a8i.shape),
                  _const_spec(pr.shape), _const_spec(pi.shape), _const_spec(ck.shape),
                  _const_spec((1, SSM_WIDTH)), _const_spec(w_glu.shape)],
        out_specs=[row_spec, st_spec, st_spec],
        out_shape=[jax.ShapeDtypeStruct((b, t, SSM_WIDTH), F32),
                   jax.ShapeDtypeStruct((b, PAIRS, LANES), F32),
                   jax.ShapeDtypeStruct((b, PAIRS, LANES), F32)],
        scratch_shapes=[pltpu.VMEM((tt + LAGS, SSM_WIDTH), F32),
                        pltpu.VMEM((PAIRS, SUBLANES, LANES), F32),
                        pltpu.VMEM((PAIRS, SUBLANES, LANES), F32),
                        pltpu.VMEM((tt, N_STATE), BF16)],
        compiler_params=_params(vmem, 2),
        name=f"ssm_{seq}",
    )(u, sz, h0r, h0i, wlag, a8r, a8i, pr, pi, ck, d_skip, w_glu)


def _out_kernel(xa_ref, xs_ref, ga_ref, gs_ref, x_ref, p_ref, woa_ref, wos_ref, wout_ref,
                wpg_ref, wpp_ref, fg_ref, y_ref):
    src = dict(xa=lambda: xa_ref[...], xs=lambda: xs_ref[...], ga=lambda: ga_ref[...],
               gs=lambda: gs_ref[...], x=lambda: x_ref[...], p=lambda: p_ref[...])

    def emit(y):
        y_ref[...] = y

    _run(_out_steps(src, woa_ref, wos_ref, wout_ref, wpg_ref, wpp_ref, fg_ref[...], emit))


def _out(xa, xs, ga, gs, x2d, p2d, woa, wos, wout, wpg, wpp, fgain, seq):
    n = x2d.shape[0]
    tm = min(OUT_ROWS, n)
    assert n % tm == 0

    def row_spec(w):
        return pl.BlockSpec((tm, w), lambda i: (i, 0))

    weights = (woa, wos, wout, wpg, wpp)
    vmem = (2 * tm * (2 * ATTN_WIDTH + 4 * D_MODEL + PLE_DIM) * 4 + sum(w.size for w in weights) * 2
            + 8 * tm * D_MODEL * 4)
    return pl.pallas_call(
        _out_kernel,
        grid=(n // tm,),
        in_specs=[row_spec(ATTN_WIDTH), row_spec(SSM_WIDTH), row_spec(D_MODEL), row_spec(D_MODEL),
                  row_spec(D_MODEL), row_spec(PLE_DIM)]
                 + [_const_spec(w.shape) for w in weights] + [_const_spec((1, D_MODEL))],
        out_specs=row_spec(D_MODEL),
        out_shape=jax.ShapeDtypeStruct((n, D_MODEL), F32),
        compiler_params=_params(vmem, 1),
        name=f"out_{seq}",
    )(xa, xs, ga, gs, x2d, p2d, woa, wos, wout, wpg, wpp, fgain)


def _both_halves(a):
    h0, h1 = a[..., :HEAD_DIM], a[..., HEAD_DIM:]
    return jnp.concatenate([h0, h0, h1, h1], axis=-1).astype(BF16)


def _layer_split(x, p, tabs, k_prefix, v_prefix, h0r, h0i, wts, consts, seq):
    b, t, _ = x.shape
    assert t <= WINDOW
    (gain, w_in, sinks, woa, d_skip, w_glu, wos, wout, wpg, wpp, fgain) = wts
    x2d = x.reshape(b * t, D_MODEL)
    qm, k, v, k2, v2, sa, u, sz, ga, gs = _proj(x2d, gain, *tabs, w_in, seq)
    k2pad = jnp.concatenate([_both_halves(k_prefix), k2.reshape(b, t, KV2_WIDTH)], axis=1)
    v2pad = jnp.concatenate([_both_halves(v_prefix), v2.reshape(b, t, KV2_WIDTH)], axis=1)
    xa = _attn(sinks, qm.reshape(b, t, QM_WIDTH), k2pad, v2pad, sa.reshape(b, t, ATTN_WIDTH), seq)
    xs, hr, hi = _ssm(u.reshape(b, t, SSM_WIDTH), sz.reshape(b, t, SSM_WIDTH), h0r, h0i,
                      consts, d_skip, w_glu, seq)
    y = _out(xa.reshape(b * t, ATTN_WIDTH), xs.reshape(b * t, SSM_WIDTH), ga, gs, x2d,
             p.reshape(b * t, PLE_DIM), woa, wos, wout, wpg, wpp, fgain, seq)
    k_new = jnp.concatenate([k_prefix[:, t:], k.reshape(b, t, KV_WIDTH)], axis=1)
    v_new = jnp.concatenate([v_prefix[:, t:], v.reshape(b, t, KV_WIDTH)], axis=1)
    return y.reshape(b, t, D_MODEL), k_new, v_new, hr, hi


def _ssm_constants(a_re, a_im, log_dt, b_re, b_im, c_re, c_im):
    dt = jnp.exp(log_dt.astype(F32))[:, None]
    lr = a_re.astype(F32).reshape(PAIRS, 1, LANES)
    li = a_im.astype(F32).reshape(PAIRS, 1, LANES)
    xr = (a_re.astype(F32) * dt).reshape(PAIRS, 1, LANES)
    xi = (a_im.astype(F32) * dt).reshape(PAIRS, 1, LANES)

    def apow(n):
        mag = jnp.exp(xr * n)
        return mag * jnp.cos(xi * n), mag * jnp.sin(xi * n)

    ar, ai = apow(1.0)
    nr, ni = ar - 1.0, ai
    den = lr * lr + li * li
    fr, fi = (nr * lr + ni * li) / den, (ni * lr - nr * li) / den

    qq, rr = np.arange(PAIRS)[:, None], np.arange(PAIR_K)[None, :]
    half, slot_i, gl = rr // LANES, (rr % LANES) // SLOT, (rr % SLOT) // SSM_GROUP
    lag_tab = (PAIRS_PER_TILE * half + (slot_i - qq) % PAIRS_PER_TILE).astype(np.float32)[:, :, None]
    same_group = (gl[0][:, None] == (np.arange(LANES) // SSM_STATE)[None, :]).astype(np.float32)

    def b_rows(bm):
        t = jnp.transpose(bm.astype(F32).reshape(PAIRS, 2, SSM_STATE, SSM_GROUP), (0, 3, 1, 2))
        t = t.reshape(PAIRS, 1, SSM_GROUP, LANES)
        t = jnp.broadcast_to(t, (PAIRS, PAIR_K // SSM_GROUP, SSM_GROUP, LANES))
        return t.reshape(PAIRS, PAIR_K, LANES)

    br, bi = b_rows(b_re), b_rows(b_im)
    bbr, bbi = fr * br - fi * bi, fr * bi + fi * br
    er, ei = apow(lag_tab)
    wlag = jnp.concatenate([(er * bbr - ei * bbi) * same_group,
                            (er * bbi + ei * bbr) * same_group], axis=-1).astype(BF16)

    a8r, a8i = (a.reshape(PAIRS, LANES) for a in apow(float(LAGS)))
    pwr, pwi = apow(np.arange(1, LAGS + 1, dtype=np.float32)[None, :, None])

    def c_cols(c):
        t = jnp.transpose(c.astype(F32).reshape(U_TILES, LANES // SSM_GROUP, SSM_GROUP, SSM_STATE),
                          (0, 3, 1, 2))
        return t.reshape(U_TILES, 1, 1, 1, SSM_STATE, LANES)

    cols_group = np.arange(LANES) // SSM_GROUP
    rows_group = 2 * np.arange(PAIRS_PER_TILE)[:, None] + np.arange(2)[None, :]
    c_mask = (rows_group[:, None, :, None, None] == cols_group[None, None, None, None, :])
    c_mask = c_mask.astype(np.float32)[None]
    ck = jnp.concatenate([c_cols(c_re) * c_mask, -c_cols(c_im) * c_mask], axis=2)
    ck = ck.reshape(U_TILES, PAIRS_PER_TILE * PAIR_N, LANES).astype(BF16)
    return wlag, a8r, a8i, pwr, pwi, ck


def _rope_tables(pos0, t, rows):
    half = ROT_DIM // 2
    d = np.arange(LANES) % HEAD_DIM
    inv = jnp.power(ROPE_THETA, -jnp.asarray(d % half, F32) * 2.0 / ROT_DIM)
    pos = (pos0 + jnp.arange(t)).astype(F32)
    ang = pos[:, None] * inv[None, :]
    cos, sin = jnp.cos(ang), jnp.sin(ang)
    cos_t = jnp.where((d < ROT_DIM)[None, :], cos, 1.0)
    sina = jnp.where(((d >= half) & (d < ROT_DIM))[None, :], sin, 0.0)
    sinb = jnp.where((d < half)[None, :], -sin, 0.0)
    reps = (max(rows // t, 1), 1)
    return tuple(jnp.tile(a, reps) for a in (cos_t, sina, sinb))


def kernel(x_prompt, x_sample, p_prompt, p_sample, cache_attn_k, cache_attn_v, state_ssm_re,
           state_ssm_im, norm_gain, w_in, attn_sinks, w_o_attn, ssm_a_re, ssm_a_im, ssm_log_dt,
           ssm_b_re, ssm_b_im, ssm_c_re, ssm_c_im, ssm_d, ssm_w_glu, w_o_ssm, w_out,
           w_ple_gate, w_ple_proj, final_norm_gain):
    assert norm_gain.shape[0] == 1, "single-layer model"
    bp, tp, _ = x_prompt.shape
    bs, ts, _ = x_sample.shape
    wts = (norm_gain[0].reshape(1, D_MODEL).astype(F32), w_in[0].astype(BF16),
           attn_sinks[0].astype(F32), w_o_attn[0].astype(BF16),
           ssm_d[0].reshape(1, SSM_WIDTH).astype(F32), ssm_w_glu[0].astype(BF16),
           w_o_ssm[0].astype(BF16), w_out[0].astype(BF16), w_ple_gate[0].astype(BF16),
           w_ple_proj[0].astype(BF16), final_norm_gain.reshape(1, D_MODEL).astype(F32))
    consts = _ssm_constants(ssm_a_re[0], ssm_a_im[0], ssm_log_dt[0], ssm_b_re[0], ssm_b_im[0],
                            ssm_c_re[0], ssm_c_im[0])

    y_p, k_p, v_p, hr_p, hi_p = _layer_fused(x_prompt, p_prompt[0], _rope_tables(0, tp, tp), wts, consts)

    ck = cache_attn_k[0].reshape(bs, WINDOW, KV_WIDTH).astype(F32)
    cv = cache_attn_v[0].reshape(bs, WINDOW, KV_WIDTH).astype(F32)
    h0r = state_ssm_re[0].reshape(bs, PAIRS, LANES).astype(F32)
    h0i = state_ssm_im[0].reshape(bs, PAIRS, LANES).astype(F32)
    tabs_s = _rope_tables(PAST_LEN, ts, min(PROJ_ROWS, bs * ts))
    y_s, k_s, v_s, hr_s, hi_s = _layer_split(x_sample, p_sample[0], tabs_s, ck, cv, h0r, h0i,
                                             wts, consts, "sample")

    def kv_out(a, b):
        return a.reshape(1, b, WINDOW, N_KV_HEADS, HEAD_DIM)

    def st_out(a, b):
        return a.reshape(1, b, SSM_GROUPS, SSM_STATE)

    return (y_p, y_s, kv_out(k_p[:, tp - WINDOW:], bp), kv_out(v_p[:, tp - WINDOW:], bp),
            st_out(hr_p, bp), st_out(hi_p, bp), kv_out(k_s, bs), kv_out(v_s, bs),
            st_out(hr_s, bs), st_out(hi_s, bs))
```

```python
import functools

import numpy as np
import jax
import jax.numpy as jnp
from jax import lax
from jax.experimental import pallas as pl
from jax.experimental.pallas import tpu as pltpu

F32 = jnp.float32
BF16 = jnp.bfloat16

LANES = 128
SUBLANES = 8
V7X_VMEM_BYTES = 64 * 1024 * 1024

D_MODEL = 1024
CHUNK = 64
WINDOW = 128
N_HEADS = 8
N_KV_HEADS = 2
HEAD_DIM = 64
Q_PER_KV = N_HEADS // N_KV_HEADS
LOG2E = 1.4426950408889634
Q_SCALE = HEAD_DIM ** -0.5 * LOG2E
ATTN_WIDTH = N_HEADS * HEAD_DIM
KV_WIDTH = N_KV_HEADS * HEAD_DIM
ROT_DIM = HEAD_DIM // 4
ROPE_THETA = 500000.0
SSM_WIDTH = D_MODEL // 2
SSM_GROUP = 16
SSM_GROUPS = SSM_WIDTH // SSM_GROUP
SSM_STATE = 64
PLE_DIM = 256
PAST_LEN = 1024
EPS = 1e-6

O_Q = 0
O_K = O_Q + ATTN_WIDTH
O_V = O_K + KV_WIDTH
O_ZA = O_V + KV_WIDTH
O_U = O_ZA + ATTN_WIDTH
O_ZS = O_U + SSM_WIDTH
O_GA = O_ZS + SSM_WIDTH
O_GS = O_GA + D_MODEL
IN_WIDTH = O_GS + D_MODEL

QM_WIDTH = N_HEADS * LANES
KV2_WIDTH = N_KV_HEADS * LANES
KEYS = WINDOW + CHUNK
LAGS = SUBLANES
PAIRS = SSM_GROUPS // 2
PAIR_K = 2 * LAGS * SSM_GROUP
PAIR_N = 2 * 2 * SSM_STATE
N_STATE = PAIRS * PAIR_N
U_TILES = SSM_WIDTH // LANES
PAIRS_PER_TILE = PAIRS // U_TILES
SLOT = 2 * SSM_GROUP
BF16_ROWS = 2 * SUBLANES
assert PAIRS_PER_TILE == 4 and LAGS == 2 * PAIRS_PER_TILE

LAYER_ROWS = 512
PROJ_ROWS = 256
ATTN_ROWS = 512
SSM_ROWS = 256
OUT_ROWS = 256


def _sigmoid(x):
    return 1.0 / (1.0 + jnp.exp(-x))


def _const_spec(shape):
    zeros = (0,) * len(shape)
    return pl.BlockSpec(shape, lambda *_: zeros, pipeline_mode=pl.Buffered(1))


def _params(vmem_bytes, n_grid):
    return pltpu.CompilerParams(
        dimension_semantics=("arbitrary",) * n_grid,
        vmem_limit_bytes=min(int(vmem_bytes), V7X_VMEM_BYTES - 8 * 1024 * 1024),
    )


def _run(steps):
    for step in steps:
        step()


def _spread(main, other):
    merged, j = [], 0
    for i, step in enumerate(main):
        while j < len(other) and j * len(main) <= i * len(other):
            merged.append(other[j])
            j += 1
        merged.append(step)
    return merged + other[j:]


def _proj_steps(get_x, gain, get_tabs, w_ref, o, store=None):
    st = {}

    def norm():
        x = get_x()
        ms = jnp.mean(x * x, axis=-1, keepdims=True)
        st["xn"] = (x * lax.rsqrt(ms + EPS) * gain).astype(BF16)
        st["lo"] = lax.broadcasted_iota(jnp.int32, (x.shape[0], LANES), 1) < HEAD_DIM

    def seg(a, b):
        return jnp.dot(st["xn"], w_ref[:, a:b], preferred_element_type=F32)

    def rope(t):
        cos, sina, sinb = get_tabs()
        return (t * cos + pltpu.roll(t, ROT_DIM // 2, 1) * sina
                + pltpu.roll(t, LANES - ROT_DIM // 2, 1) * sinb)

    def both_halves(t):
        tr = pltpu.roll(t, HEAD_DIM, 1)
        return [jnp.where(st["lo"], t, tr).astype(BF16), jnp.where(st["lo"], tr, t).astype(BF16)]

    def done(name):
        if store is not None:
            store(name)

    def q():
        zq = seg(O_Q, O_K)
        o["qh"] = []
        for j in range(ATTN_WIDTH // LANES):
            qt = rope(zq[:, j * LANES:(j + 1) * LANES]) * Q_SCALE
            o["qh"] += [jnp.where(st["lo"], qt, 0.0).astype(BF16),
                        jnp.where(st["lo"], 0.0, qt).astype(BF16)]

    def kv():
        z = seg(O_K, O_ZA)
        o["k"] = rope(z[:, :KV_WIDTH])
        o["k2"] = both_halves(o["k"])
        done("k")
        o["v"] = z[:, KV_WIDTH:]
        o["v2"] = both_halves(o["v"])
        done("v")

    def za():
        z = seg(O_ZA, O_U)
        o["sa"] = z * _sigmoid(z)

    def u():
        o["u"] = seg(O_U, O_ZS)
        done("u")

    def zs():
        z = seg(O_ZS, O_GA)
        o["sz"] = z * _sigmoid(z)

    def ga():
        o["ga"] = _sigmoid(seg(O_GA, O_GS))

    def gs():
        o["gs"] = _sigmoid(seg(O_GS, IN_WIDTH))

    return [norm, q, kv, za, u, zs, ga, gs]


def _attn_steps(sinks_ref, chunks, get_q, get_kv, get_valid, emit):
    nt = (((1,), (1,)), ((), ()))
    units = [(c, kv) for c in chunks for kv in range(N_KV_HEADS)]
    n = len(units)
    st = {}

    def scores(c, kv):
        k2, v2 = get_kv(c, kv)
        qm = jnp.concatenate([get_q(c, kv * Q_PER_KV + h) for h in range(Q_PER_KV)], axis=0)
        s = lax.dot_general(qm, k2, nt, preferred_element_type=F32)
        valid = get_valid(c)
        if valid is not None:
            s = jnp.where(valid, s, -jnp.inf)
        return s, v2

    def softmax(s, kv):
        head_row = lax.broadcasted_iota(jnp.int32, (Q_PER_KV * CHUNK, 1), 0) // CHUNK
        sk = [sinks_ref[kv * Q_PER_KV + h] * LOG2E for h in range(Q_PER_KV)]
        sink = jnp.where(head_row == 0, sk[0],
                         jnp.where(head_row == 1, sk[1], jnp.where(head_row == 2, sk[2], sk[3])))
        m = jnp.maximum(jnp.max(s, axis=1, keepdims=True), sink)
        return jnp.exp2(s - m).astype(BF16), jnp.exp2(sink - m)

    def output(e, sink_term, v2, c, kv):
        lo_q = lax.broadcasted_iota(jnp.int32, (CHUNK, LANES), 1) < HEAD_DIM
        ones = jnp.ones((KEYS, LANES), BF16)
        pv = jnp.dot(e, jnp.concatenate([v2, ones], axis=1), preferred_element_type=F32)
        o = pv[:, :LANES] / (pv[:, LANES:] + sink_term)
        for j in range(Q_PER_KV // 2):
            even = o[2 * j * CHUNK:(2 * j + 1) * CHUNK]
            odd = o[(2 * j + 1) * CHUNK:(2 * j + 2) * CHUNK]
            emit(c, kv * (Q_PER_KV // 2) + j, jnp.where(lo_q, even, odd))

    def make(i):
        def step():
            if i < n:
                st[i] = scores(*units[i])
            if 0 <= i - 1 < n:
                s, v2 = st[i - 1]
                st[i - 1] = softmax(s, units[i - 1][1]) + (v2,)
            if 0 <= i - 2 < n:
                e, den, v2 = st.pop(i - 2)
                output(e, den, v2, *units[i - 2])
        return step

    return [make(i) for i in range(n + 2)]


def _ssm_reset(ubuf, cr_s, ci_s, h0=None):
    ubuf[0:LAGS, :] = jnp.zeros((LAGS, SSM_WIDTH), F32)
    if h0 is None:
        cr_s[...] = jnp.zeros(cr_s.shape, F32)
        ci_s[...] = jnp.zeros(ci_s.shape, F32)
    else:
        h0r_ref, h0i_ref, pr_ref, pi_ref = h0
        for q in range(PAIRS):
            h0r, h0i = h0r_ref[0, q:q + 1, :], h0i_ref[0, q:q + 1, :]
            pr, pi = pr_ref[q], pi_ref[q]
            cr_s[q] = pr * h0r - pi * h0i
            ci_s[q] = pr * h0i + pi * h0r


def _ssm_steps(ubuf, row0, tt, cr_s, ci_s, hs, wlag_ref, a8r_ref, a8i_ref, ck_ref, hr_ref, hi_ref, o):
    st = {"ys": []}

    def setup():
        slot = lax.broadcasted_iota(jnp.int32, (tt, LANES), 1) // SLOT
        st["to_low"] = [((slot + PAIRS_PER_TILE - s) % PAIRS_PER_TILE) < 2 for s in range(2)]
        st["same_parity"] = [((slot + sg) % 2) == 0 for sg in range(2)]

    def route(r):
        low = [jnp.where(st["to_low"][s], r[s], r[s + 2]) for s in range(2)]
        high = [jnp.where(st["to_low"][s], r[s + 2], r[s]) for s in range(2)]
        return [jnp.where(st["same_parity"][sg % 2], src[0], src[1])
                for sg, src in zip(range(PAIRS_PER_TILE), (low, low, high, high))]

    def lag_copies(k):
        def step():
            ub = ubuf[row0:row0 + LAGS + tt, k * LANES:(k + 1) * LANES]
            rolled = []
            for s in range(LAGS):
                us = ub[LAGS:] if s == 0 else pltpu.roll(ub, s, 0)[LAGS:]
                if s % PAIRS_PER_TILE:
                    us = pltpu.roll(us, SLOT * (s % PAIRS_PER_TILE), 1)
                rolled.append(us)
            st["halves"] = (route(rolled[:PAIRS_PER_TILE]), route(rolled[PAIRS_PER_TILE:]))
        return step

    def pair(k, sg):
        def step():
            q = k * PAIRS_PER_TILE + sg
            xl = jnp.concatenate([st["halves"][0][sg], st["halves"][1][sg]], axis=1).astype(BF16)
            w = jnp.dot(xl, wlag_ref[q], preferred_element_type=F32)
            ar, ai = a8r_ref[q:q + 1, :], a8i_ref[q:q + 1, :]
            cr, ci = cr_s[q], ci_s[q]
            for b2 in range(tt // BF16_ROWS):
                hrs, his = [], []
                for b in (2 * b2, 2 * b2 + 1):
                    blk = slice(b * SUBLANES, (b + 1) * SUBLANES)
                    hr = w[blk, :LANES] + cr
                    hi = w[blk, LANES:] + ci
                    cr = ar * hr - ai * hi
                    ci = ar * hi + ai * hr
                    hrs.append(hr)
                    his.append(hi)
                blk2 = slice(b2 * BF16_ROWS, (b2 + 1) * BF16_ROWS)
                hs[blk2, q * PAIR_N:q * PAIR_N + LANES] = jnp.concatenate(hrs, axis=0).astype(BF16)
                hs[blk2, q * PAIR_N + LANES:(q + 1) * PAIR_N] = jnp.concatenate(his, axis=0).astype(BF16)
            cr_s[q] = cr
            ci_s[q] = ci
            hr_ref[0, q:q + 1, :] = hr[SUBLANES - 1:, :]
            hi_ref[0, q:q + 1, :] = hi[SUBLANES - 1:, :]
        return step

    def c_proj(k):
        def step():
            cols = slice(k * PAIRS_PER_TILE * PAIR_N, (k + 1) * PAIRS_PER_TILE * PAIR_N)
            st["ys"].append(jnp.dot(hs[:, cols], ck_ref[k], preferred_element_type=F32))
            if k == U_TILES - 1:
                o["y"] = jnp.concatenate(st["ys"], axis=1)
        return step

    steps = [setup]
    for k in range(U_TILES):
        steps += [lag_copies(k)] + [pair(k, sg) for sg in range(PAIRS_PER_TILE)] + [c_proj(k)]
    return steps


def _glu(y, u, sz, d, wglu_ref):
    z = jax.nn.gelu(y + d * u)
    g = jnp.dot(z.astype(BF16), wglu_ref[...], preferred_element_type=F32)
    return z * _sigmoid(g) * sz


def _out_steps(src, woa_ref, wos_ref, wout_ref, wpg_ref, wpp_ref, fgain, emit):
    st = {}

    def mm(a, w_ref):
        return jnp.dot(a.astype(BF16), w_ref[...], preferred_element_type=F32)

    def branches():
        st["merged"] = src["ga"]() * mm(src["xa"](), woa_ref) + src["gs"]() * mm(src["xs"](), wos_ref)

    def residual():
        st["h"] = src["x"]() + mm(st.pop("merged"), wout_ref)

    def embed_gate():
        h = st.pop("h")
        st["h"] = h + _sigmoid(mm(h, wpg_ref)) * mm(src["p"](), wpp_ref)

    def norm():
        h = st.pop("h")
        ms = jnp.mean(h * h, axis=-1, keepdims=True)
        emit(h * lax.rsqrt(ms + EPS) * fgain)

    return [branches, residual, embed_gate, norm]


def _layer_kernel(sinks_ref, x_ref, p_ref, cos_ref, sina_ref, sinb_ref, gain_ref, w_in_ref,
                  wlag_ref, a8r_ref, a8i_ref, ck_ref, d_ref, wglu_ref,
                  woa_ref, wos_ref, wout_ref, wpg_ref, wpp_ref, fg_ref,
                  y_ref, k_ref, v_ref, hr_ref, hi_ref,
                  kbuf, vbuf, xa_s, ubuf, cr_s, ci_s, hs, *, tt):
    t = pl.program_id(1)
    half = tt // 2

    @pl.when(t == 0)
    def _():
        kbuf[0:WINDOW, :] = jnp.zeros((WINDOW, KV2_WIDTH), BF16)
        vbuf[0:WINDOW, :] = jnp.zeros((WINDOW, KV2_WIDTH), BF16)
        _ssm_reset(ubuf, cr_s, ci_s)

    pj, so = [{}, {}], [{}, {}]

    def rows(h):
        return slice(h * half, (h + 1) * half)

    def proj(h):
        def store(name):
            if name == "u":
                ubuf[LAGS + h * half:LAGS + (h + 1) * half, :] = pj[h]["u"]
                return
            full_ref, buf = (k_ref, kbuf) if name == "k" else (v_ref, vbuf)
            full_ref[0, rows(h)] = pj[h][name]
            for j in range(N_KV_HEADS):
                buf[WINDOW + h * half:WINDOW + (h + 1) * half, j * LANES:(j + 1) * LANES] = pj[h][name + "2"][j]

        return _proj_steps(lambda: x_ref[0, rows(h)], gain_ref[...],
                           lambda: (cos_ref[rows(h)], sina_ref[rows(h)], sinb_ref[rows(h)]),
                           w_in_ref, pj[h], store)

    def mid(h):
        def get_q(c, head):
            r0 = c * CHUNK - h * half
            return pj[h]["qh"][head][r0:r0 + CHUNK]

        def get_kv(c, kv):
            krows, cols = slice(c * CHUNK, c * CHUNK + KEYS), slice(kv * LANES, (kv + 1) * LANES)
            return kbuf[krows, cols], vbuf[krows, cols]

        def get_valid(c):
            if c * CHUNK >= WINDOW:
                return None
            in_seq = c * CHUNK + lax.broadcasted_iota(jnp.int32, (1, KEYS), 1) >= WINDOW
            return jnp.logical_or(in_seq, t > 0)

        def emit(c, tile, o):
            r0, cols = c * CHUNK - h * half, slice(tile * LANES, (tile + 1) * LANES)
            xa_s[c * CHUNK:(c + 1) * CHUNK, cols] = (o * pj[h]["sa"][r0:r0 + CHUNK, cols]).astype(BF16)

        def glu():
            so[h]["xs"] = _glu(so[h].pop("y"), pj[h]["u"], pj[h]["sz"], d_ref[...], wglu_ref)

        chunks = range(h * half // CHUNK, (h + 1) * half // CHUNK)
        return (_attn_steps(sinks_ref, chunks, get_q, get_kv, get_valid, emit)
                + _ssm_steps(ubuf, h * half, half, cr_s, ci_s, hs, wlag_ref, a8r_ref, a8i_ref, ck_ref,
                             hr_ref, hi_ref, so[h])
                + [glu])

    def out(h):
        src = dict(xa=lambda: xa_s[rows(h)], xs=lambda: so[h]["xs"], ga=lambda: pj[h]["ga"],
                   gs=lambda: pj[h]["gs"], x=lambda: x_ref[0, rows(h)], p=lambda: p_ref[0, rows(h)])

        def emit(y):
            y_ref[0, rows(h)] = y

        return _out_steps(src, woa_ref, wos_ref, wout_ref, wpg_ref, wpp_ref, fg_ref[...], emit)

    _run(proj(0))
    _run(_spread(mid(0), proj(1)))
    _run(_spread(mid(1), out(0)))
    _run(out(1))

    kbuf[0:WINDOW, :] = kbuf[tt:tt + WINDOW, :]
    vbuf[0:WINDOW, :] = vbuf[tt:tt + WINDOW, :]
    ubuf[0:LAGS, :] = ubuf[tt:tt + LAGS, :]


def _layer_fused(x, p, tabs, wts, consts):
    b, t, _ = x.shape
    tt = min(LAYER_ROWS, t)
    assert t % tt == 0 and tt // 2 >= WINDOW and (tt // 2) % BF16_ROWS == 0 and tabs[0].shape[0] == t
    (gain, w_in, sinks, woa, d_skip, w_glu, wos, wout, wpg, wpp, fgain) = wts
    wlag, a8r, a8i, _, _, ck = consts

    def row_spec(w):
        return pl.BlockSpec((1, tt, w), lambda i, j: (i, j, 0))

    tab_spec = pl.BlockSpec((tt, LANES), lambda i, j: (j, 0))
    st_spec = pl.BlockSpec((1, PAIRS, LANES), lambda i, j: (i, 0, 0))
    consts_in = (gain, w_in, wlag, a8r, a8i, ck, d_skip, w_glu, woa, wos, wout, wpg, wpp, fgain)
    vmem = (sum(a.size * a.dtype.itemsize for a in consts_in)
            + 2 * tt * (2 * D_MODEL + PLE_DIM + 2 * KV_WIDTH + 3 * LANES) * 4
            + 2 * (WINDOW + tt) * KV2_WIDTH * 2 + tt * ATTN_WIDTH * 2 + (tt + LAGS) * SSM_WIDTH * 4
            + (tt // 2) * N_STATE * 2 + 3 * tt * IN_WIDTH * 4)
    y, k, v, hr, hi = pl.pallas_call(
        functools.partial(_layer_kernel, tt=tt),
        grid=(b, t // tt),
        in_specs=[pl.BlockSpec(memory_space=pltpu.SMEM), row_spec(D_MODEL), row_spec(PLE_DIM),
                  tab_spec, tab_spec, tab_spec]
                 + [_const_spec(a.shape) for a in consts_in],
        out_specs=[row_spec(D_MODEL), row_spec(KV_WIDTH), row_spec(KV_WIDTH), st_spec, st_spec],
        out_shape=[jax.ShapeDtypeStruct((b, t, D_MODEL), F32),
                   jax.ShapeDtypeStruct((b, t, KV_WIDTH), F32),
                   jax.ShapeDtypeStruct((b, t, KV_WIDTH), F32),
                   jax.ShapeDtypeStruct((b, PAIRS, LANES), F32),
                   jax.ShapeDtypeStruct((b, PAIRS, LANES), F32)],
        scratch_shapes=[pltpu.VMEM((WINDOW + tt, KV2_WIDTH), BF16),
                        pltpu.VMEM((WINDOW + tt, KV2_WIDTH), BF16),
                        pltpu.VMEM((tt, ATTN_WIDTH), BF16),
                        pltpu.VMEM((tt + LAGS, SSM_WIDTH), F32),
                        pltpu.VMEM((PAIRS, SUBLANES, LANES), F32),
                        pltpu.VMEM((PAIRS, SUBLANES, LANES), F32),
                        pltpu.VMEM((tt // 2, N_STATE), BF16)],
        compiler_params=_params(vmem, 2),
        name="layer_prompt",
    )(sinks, x, p, *tabs, *consts_in)
    return y, k, v, hr, hi


def _proj_kernel(x_ref, gain_ref, cos_ref, sina_ref, sinb_ref, w_ref,
                 q_ref, k_ref, v_ref, k2_ref, v2_ref, sa_ref, u_ref, sz_ref, ga_ref, gs_ref):
    pj = {}
    _run(_proj_steps(lambda: x_ref[...], gain_ref[...],
                     lambda: (cos_ref[...], sina_ref[...], sinb_ref[...]), w_ref, pj))
    for h in range(N_HEADS):
        q_ref[:, h * LANES:(h + 1) * LANES] = pj["qh"][h]
    for j in range(N_KV_HEADS):
        k2_ref[:, j * LANES:(j + 1) * LANES] = pj["k2"][j]
        v2_ref[:, j * LANES:(j + 1) * LANES] = pj["v2"][j]
    k_ref[...] = pj["k"]
    v_ref[...] = pj["v"]
    sa_ref[...] = pj["sa"]
    u_ref[...] = pj["u"]
    sz_ref[...] = pj["sz"]
    ga_ref[...] = pj["ga"]
    gs_ref[...] = pj["gs"]


def _proj(x2d, gain, cos, sina, sinb, w_in, seq):
    n = x2d.shape[0]
    tm = min(PROJ_ROWS, n)
    tab_rows = cos.shape[0]
    tab_tiles = tab_rows // tm
    assert n % tm == 0 and tab_rows % tm == 0

    def row_spec(w):
        return pl.BlockSpec((tm, w), lambda i: (i, 0))

    tab_spec = pl.BlockSpec((tm, LANES), lambda i: (i % tab_tiles, 0))
    widths = (QM_WIDTH, KV_WIDTH, KV_WIDTH, KV2_WIDTH, KV2_WIDTH,
              ATTN_WIDTH, SSM_WIDTH, SSM_WIDTH, D_MODEL, D_MODEL)
    dtypes = (BF16, F32, F32, BF16, BF16) + (F32,) * 5
    vmem = (2 * tm * D_MODEL * 4 + D_MODEL * IN_WIDTH * 2 + 3 * 2 * tm * LANES * 4
            + 3 * tm * IN_WIDTH * 4 + tm * D_MODEL * 8)
    return pl.pallas_call(
        _proj_kernel,
        grid=(n // tm,),
        in_specs=[row_spec(D_MODEL), _const_spec((1, D_MODEL)), tab_spec, tab_spec, tab_spec,
                  _const_spec((D_MODEL, IN_WIDTH))],
        out_specs=[row_spec(w) for w in widths],
        out_shape=[jax.ShapeDtypeStruct((n, w), d) for w, d in zip(widths, dtypes)],
        compiler_params=_params(vmem, 1),
        name=f"proj_{seq}",
    )(x2d, gain, cos, sina, sinb, w_in)


def _attn_kernel(sinks_ref, q_ref, k_ref, v_ref, sa_ref, o_ref, *, tq):
    t = pl.program_id(1)

    def rows(c):
        return slice(c * CHUNK, (c + 1) * CHUNK)

    def get_kv(c, kv):
        row0 = pl.multiple_of(t * tq + c * CHUNK, CHUNK)
        cols = slice(kv * LANES, (kv + 1) * LANES)
        return k_ref[0, pl.ds(row0, KEYS), cols], v_ref[0, pl.ds(row0, KEYS), cols]

    def emit(c, tile, o):
        cols = slice(tile * LANES, (tile + 1) * LANES)
        o_ref[0, rows(c), cols] = o * sa_ref[0, rows(c), cols]

    _run(_attn_steps(sinks_ref, range(tq // CHUNK),
                     lambda c, h: q_ref[0, rows(c), h * LANES:(h + 1) * LANES],
                     get_kv, lambda c: None, emit))


def _attn(sinks, qm, k2pad, v2pad, sa, seq):
    b, t, _ = qm.shape
    tq = min(ATTN_ROWS, t)
    assert t % tq == 0 and k2pad.shape[1] == t + WINDOW

    def row_spec(w):
        return pl.BlockSpec((1, tq, w), lambda i, j: (i, j, 0))

    kv_spec = pl.BlockSpec((1, t + WINDOW, KV2_WIDTH), lambda i, j: (i, 0, 0))
    vmem = (2 * tq * (QM_WIDTH * 2 + 2 * ATTN_WIDTH * 4) + 2 * 2 * (t + WINDOW) * KV2_WIDTH * 2
            + 8 * Q_PER_KV * CHUNK * 2 * LANES * 4)
    return pl.pallas_call(
        functools.partial(_attn_kernel, tq=tq),
        grid=(b, t // tq),
        in_specs=[pl.BlockSpec(memory_space=pltpu.SMEM), row_spec(QM_WIDTH), kv_spec, kv_spec,
                  row_spec(ATTN_WIDTH)],
        out_specs=row_spec(ATTN_WIDTH),
        out_shape=jax.ShapeDtypeStruct((b, t, ATTN_WIDTH), F32),
        compiler_params=_params(vmem, 2),
        name=f"attn_{seq}",
    )(sinks, qm, k2pad, v2pad, sa)


def _ssm_kernel(u_ref, sz_ref, h0r_ref, h0i_ref, wlag_ref, a8r_ref, a8i_ref, pr_ref, pi_ref,
                ck_ref, d_ref, wglu_ref, xs_ref, hr_ref, hi_ref, ubuf, cr_s, ci_s, hs, *, tt):
    @pl.when(pl.program_id(1) == 0)
    def _():
        _ssm_reset(ubuf, cr_s, ci_s, (h0r_ref, h0i_ref, pr_ref, pi_ref))

    u = u_ref[0]
    ubuf[LAGS:LAGS + tt, :] = u
    so = {}
    _run(_ssm_steps(ubuf, 0, tt, cr_s, ci_s, hs, wlag_ref, a8r_ref, a8i_ref, ck_ref, hr_ref, hi_ref, so))
    xs_ref[0] = _glu(so["y"], u, sz_ref[0], d_ref[...], wglu_ref)
    ubuf[0:LAGS, :] = ubuf[tt:tt + LAGS, :]


def _ssm(u, sz, h0r, h0i, consts, d_skip, w_glu, seq):
    b, t, _ = u.shape
    tt = min(SSM_ROWS, t)
    assert t % tt == 0 and tt % BF16_ROWS == 0
    wlag, a8r, a8i, pr, pi, ck = consts
    row_spec = pl.BlockSpec((1, tt, SSM_WIDTH), lambda i, j: (i, j, 0))
    st_spec = pl.BlockSpec((1, PAIRS, LANES), lambda i, j: (i, 0, 0))
    vmem = (3 * 2 * tt * SSM_WIDTH * 4 + wlag.size * 2 + ck.size * 2 + w_glu.size * 2
            + tt * N_STATE * 2 + (LAGS + 2) * tt * SSM_WIDTH * 4 + 4 * tt * SSM_WIDTH * 4
            + 4 * tt * PAIR_N * 4 + 16 * PAIRS * SUBLANES * LANES * 4)
    return pl.pallas_call(
        functools.partial(_ssm_kernel, tt=tt),
        grid=(b, t // tt),
        in_specs=[row_spec, row_spec, st_spec, st_spec,
                  _const_spec(wlag.shape), _const_spec(a8r.shape), _const_spec(a8i.shape),
                  _const_spec(pr.shape), _const_spec(pi.shape), _const_spec(ck.shape),
                  _const_spec((1, SSM_WIDTH)), _const_spec(w_glu.shape)],
        out_specs=[row_spec, st_spec, st_spec],
        out_shape=[jax.ShapeDtypeStruct((b, t, SSM_WIDTH), F32),
                   jax.ShapeDtypeStruct((b, PAIRS, LANES), F32),
                   jax.ShapeDtypeStruct((b, PAIRS, LANES), F32)],
        scratch_shapes=[pltpu.VMEM((tt + LAGS, SSM_WIDTH), F32),
                        pltpu.VMEM((PAIRS, SUBLANES, LANES), F32),
                        pltpu.VMEM((PAIRS, SUBLANES, LANES), F32),
                        pltpu.VMEM((tt, N_STATE), BF16)],
        compiler_params=_params(vmem, 2),
        name=f"ssm_{seq}",
    )(u, sz, h0r, h0i, wlag, a8r, a8i, pr, pi, ck, d_skip, w_glu)


def _out_kernel(xa_ref, xs_ref, ga_ref, gs_ref, x_ref, p_ref, woa_ref, wos_ref, wout_ref,
                wpg_ref, wpp_ref, fg_ref, y_ref):
    src = dict(xa=lambda: xa_ref[...], xs=lambda: xs_ref[...], ga=lambda: ga_ref[...],
               gs=lambda: gs_ref[...], x=lambda: x_ref[...], p=lambda: p_ref[...])

    def emit(y):
        y_ref[...] = y

    _run(_out_steps(src, woa_ref, wos_ref, wout_ref, wpg_ref, wpp_ref, fg_ref[...], emit))


def _out(xa, xs, ga, gs, x2d, p2d, woa, wos, wout, wpg, wpp, fgain, seq):
    n = x2d.shape[0]
    tm = min(OUT_ROWS, n)
    assert n % tm == 0

    def row_spec(w):
        return pl.BlockSpec((tm, w), lambda i: (i, 0))

    weights = (woa, wos, wout, wpg, wpp)
    vmem = (2 * tm * (2 * ATTN_WIDTH + 4 * D_MODEL + PLE_DIM) * 4 + sum(w.size for w in weights) * 2
            + 8 * tm * D_MODEL * 4)
    return pl.pallas_call(
        _out_kernel,
        grid=(n // tm,),
        in_specs=[row_spec(ATTN_WIDTH), row_spec(SSM_WIDTH), row_spec(D_MODEL), row_spec(D_MODEL),
                  row_spec(D_MODEL), row_spec(PLE_DIM)]
                 + [_const_spec(w.shape) for w in weights] + [_const_spec((1, D_MODEL))],
        out_specs=row_spec(D_MODEL),
        out_shape=jax.ShapeDtypeStruct((n, D_MODEL), F32),
        compiler_params=_params(vmem, 1),
        name=f"out_{seq}",
    )(xa, xs, ga, gs, x2d, p2d, woa, wos, wout, wpg, wpp, fgain)


def _both_halves(a):
    h0, h1 = a[..., :HEAD_DIM], a[..., HEAD_DIM:]
    return jnp.concatenate([h0, h0, h1, h1], axis=-1).astype(BF16)


def _layer_split(x, p, tabs, k_prefix, v_prefix, h0r, h0i, wts, consts, seq):
    b, t, _ = x.shape
    assert t <= WINDOW
    (gain, w_in, sinks, woa, d_skip, w_glu, wos, wout, wpg, wpp, fgain) = wts
    x2d = x.reshape(b * t, D_MODEL)
    qm, k, v, k2, v2, sa, u, sz, ga, gs = _proj(x2d, gain, *tabs, w_in, seq)
    k2pad = jnp.concatenate([_both_halves(k_prefix), k2.reshape(b, t, KV2_WIDTH)], axis=1)
    v2pad = jnp.concatenate([_both_halves(v_prefix), v2.reshape(b, t, KV2_WIDTH)], axis=1)
    xa = _attn(sinks, qm.reshape(b, t, QM_WIDTH), k2pad, v2pad, sa.reshape(b, t, ATTN_WIDTH), seq)
    xs, hr, hi = _ssm(u.reshape(b, t, SSM_WIDTH), sz.reshape(b, t, SSM_WIDTH), h0r, h0i,
                      consts, d_skip, w_glu, seq)
    y = _out(xa.reshape(b * t, ATTN_WIDTH), xs.reshape(b * t, SSM_WIDTH), ga, gs, x2d,
             p.reshape(b * t, PLE_DIM), woa, wos, wout, wpg, wpp, fgain, seq)
    k_new = jnp.concatenate([k_prefix[:, t:], k.reshape(b, t, KV_WIDTH)], axis=1)
    v_new = jnp.concatenate([v_prefix[:, t:], v.reshape(b, t, KV_WIDTH)], axis=1)
    return y.reshape(b, t, D_MODEL), k_new, v_new, hr, hi


def _ssm_constants(a_re, a_im, log_dt, b_re, b_im, c_re, c_im):
    dt = jnp.exp(log_dt.astype(F32))[:, None]
    lr = a_re.astype(F32).reshape(PAIRS, 1, 1, LANES)
    li = a_im.astype(F32).reshape(PAIRS, 1, 1, LANES)
    xr = (a_re.astype(F32) * dt).reshape(PAIRS, 1, 1, LANES)
    xi = (a_im.astype(F32) * dt).reshape(PAIRS, 1, 1, LANES)

    def apow(n):
        mag = jnp.exp(xr * n)
        return mag * jnp.cos(xi * n), mag * jnp.sin(xi * n)

    ar, ai = apow(1.0)
    nr, ni = ar - 1.0, ai
    den = lr * lr + li * li
    fr, fi = (nr * lr + ni * li) / den, (ni * lr - nr * li) / den

    n_slots = PAIR_K // SLOT
    qq, hi_ = np.arange(PAIRS)[:, None], np.arange(n_slots)[None, :]
    lag_tab = (PAIRS_PER_TILE * (hi_ // PAIRS_PER_TILE) + (hi_ % PAIRS_PER_TILE - qq) % PAIRS_PER_TILE)
    lag_tab = lag_tab.astype(np.float32)[:, :, None, None]
    same_group = (np.arange(SLOT)[:, None] // SSM_GROUP == np.arange(LANES)[None, :] // SSM_STATE)
    same_group = same_group.astype(np.float32)

    def b_rows(bm):
        t = jnp.transpose(bm.astype(F32).reshape(PAIRS, 2, SSM_STATE, SSM_GROUP), (0, 3, 1, 2))
        t = t.reshape(PAIRS, 1, 1, SSM_GROUP, LANES)
        t = jnp.broadcast_to(t, (PAIRS, 1, 2, SSM_GROUP, LANES))
        return t.reshape(PAIRS, 1, SLOT, LANES) * same_group

    br, bi = b_rows(b_re), b_rows(b_im)
    bbr, bbi = fr * br - fi * bi, fr * bi + fi * br
    er, ei = apow(lag_tab)
    wlag = jnp.concatenate([(er * bbr - ei * bbi).reshape(PAIRS, PAIR_K, LANES),
                            (er * bbi + ei * bbr).reshape(PAIRS, PAIR_K, LANES)], axis=-1).astype(BF16)

    a8r, a8i = (a.reshape(PAIRS, LANES) for a in apow(float(LAGS)))
    pwr, pwi = (a.reshape(PAIRS, LAGS, LANES)
                for a in apow(np.arange(1, LAGS + 1, dtype=np.float32)[None, :, None, None]))

    def c_cols(c):
        t = jnp.transpose(c.astype(F32).reshape(U_TILES, LANES // SSM_GROUP, SSM_GROUP, SSM_STATE),
                          (0, 3, 1, 2))
        return t.reshape(U_TILES, 1, 1, 1, SSM_STATE, LANES)

    cols_group = np.arange(LANES) // SSM_GROUP
    rows_group = 2 * np.arange(PAIRS_PER_TILE)[:, None] + np.arange(2)[None, :]
    c_mask = (rows_group[:, None, :, None, None] == cols_group[None, None, None, None, :])
    c_mask = c_mask.astype(np.float32)[None]
    ck = jnp.concatenate([c_cols(c_re) * c_mask, -c_cols(c_im) * c_mask], axis=2)
    ck = ck.reshape(U_TILES, PAIRS_PER_TILE * PAIR_N, LANES).astype(BF16)
    return wlag, a8r, a8i, pwr, pwi, ck


def _rope_tables(pos0, t, rows):
    half = ROT_DIM // 2
    d = np.arange(LANES) % HEAD_DIM
    inv = jnp.power(ROPE_THETA, -jnp.arange(half, dtype=F32) * 2.0 / ROT_DIM)
    pos = (pos0 + jnp.arange(t)).astype(F32)
    ang = pos[:, None] * inv[None, :]
    cos, sin = (jnp.tile(a, (1, LANES // half)) for a in (jnp.cos(ang), jnp.sin(ang)))
    cos_t = jnp.where((d < ROT_DIM)[None, :], cos, 1.0)
    sina = jnp.where(((d >= half) & (d < ROT_DIM))[None, :], sin, 0.0)
    sinb = jnp.where((d < half)[None, :], -sin, 0.0)
    reps = (max(rows // t, 1), 1)
    return tuple(jnp.tile(a, reps) for a in (cos_t, sina, sinb))


def kernel(x_prompt, x_sample, p_prompt, p_sample, cache_attn_k, cache_attn_v, state_ssm_re,
           state_ssm_im, norm_gain, w_in, attn_sinks, w_o_attn, ssm_a_re, ssm_a_im, ssm_log_dt,
           ssm_b_re, ssm_b_im, ssm_c_re, ssm_c_im, ssm_d, ssm_w_glu, w_o_ssm, w_out,
           w_ple_gate, w_ple_proj, final_norm_gain):
    assert norm_gain.shape[0] == 1, "single-layer model"
    bp, tp, _ = x_prompt.shape
    bs, ts, _ = x_sample.shape
    wts = (norm_gain[0].reshape(1, D_MODEL).astype(F32), w_in[0].astype(BF16),
           attn_sinks[0].astype(F32), w_o_attn[0].astype(BF16),
           ssm_d[0].reshape(1, SSM_WIDTH).astype(F32), ssm_w_glu[0].astype(BF16),
           w_o_ssm[0].astype(BF16), w_out[0].astype(BF16), w_ple_gate[0].astype(BF16),
           w_ple_proj[0].astype(BF16), final_norm_gain.reshape(1, D_MODEL).astype(F32))
    consts = _ssm_constants(ssm_a_re[0], ssm_a_im[0], ssm_log_dt[0], ssm_b_re[0], ssm_b_im[0],
                            ssm_c_re[0], ssm_c_im[0])

    y_p, k_p, v_p, hr_p, hi_p = _layer_fused(x_prompt, p_prompt[0], _rope_tables(0, tp, tp), wts, consts)

    ck = cache_attn_k[0].reshape(bs, WINDOW, KV_WIDTH).astype(F32)
    cv = cache_attn_v[0].reshape(bs, WINDOW, KV_WIDTH).astype(F32)
    h0r = state_ssm_re[0].reshape(bs, PAIRS, LANES).astype(F32)
    h0i = state_ssm_im[0].reshape(bs, PAIRS, LANES).astype(F32)
    tabs_s = _rope_tables(PAST_LEN, ts, min(PROJ_ROWS, bs * ts))
    y_s, k_s, v_s, hr_s, hi_s = _layer_split(x_sample, p_sample[0], tabs_s, ck, cv, h0r, h0i,
                                             wts, consts, "sample")

    def kv_out(a, b):
        return a.reshape(1, b, WINDOW, N_KV_HEADS, HEAD_DIM)

    def st_out(a, b):
        return a.reshape(1, b, SSM_GROUPS, SSM_STATE)

    return (y_p, y_s, kv_out(k_p[:, tp - WINDOW:], bp), kv_out(v_p[:, tp - WINDOW:], bp),
            st_out(hr_p, bp), st_out(hi_p, bp), kv_out(k_s, bs), kv_out(v_s, bs),
            st_out(hr_s, bs), st_out(hi_s, bs))
```

```python
import functools

import numpy as np
import jax
import jax.numpy as jnp
from jax import lax
from jax.experimental import pallas as pl
from jax.experimental.pallas import tpu as pltpu

F32 = jnp.float32
BF16 = jnp.bfloat16

LANES = 128
SUBLANES = 8
V7X_VMEM_BYTES = 64 * 1024 * 1024

D_MODEL = 1024
CHUNK = 64
WINDOW = 128
N_HEADS = 8
N_KV_HEADS = 2
HEAD_DIM = 64
Q_PER_KV = N_HEADS // N_KV_HEADS
LOG2E = 1.4426950408889634
Q_SCALE = HEAD_DIM ** -0.5 * LOG2E
ATTN_WIDTH = N_HEADS * HEAD_DIM
KV_WIDTH = N_KV_HEADS * HEAD_DIM
ROT_DIM = HEAD_DIM // 4
ROPE_THETA = 500000.0
SSM_WIDTH = D_MODEL // 2
SSM_GROUP = 16
SSM_GROUPS = SSM_WIDTH // SSM_GROUP
SSM_STATE = 64
PLE_DIM = 256
PAST_LEN = 1024
EPS = 1e-6

O_Q = 0
O_K = O_Q + ATTN_WIDTH
O_V = O_K + KV_WIDTH
O_ZA = O_V + KV_WIDTH
O_U = O_ZA + ATTN_WIDTH
O_ZS = O_U + SSM_WIDTH
O_GA = O_ZS + SSM_WIDTH
O_GS = O_GA + D_MODEL
IN_WIDTH = O_GS + D_MODEL

QM_WIDTH = N_HEADS * LANES
KV2_WIDTH = N_KV_HEADS * LANES
KEYS = WINDOW + CHUNK
LAGS = SUBLANES
PAIRS = SSM_GROUPS // 2
PAIR_K = 2 * LAGS * SSM_GROUP
PAIR_N = 2 * 2 * SSM_STATE
N_STATE = PAIRS * PAIR_N
U_TILES = SSM_WIDTH // LANES
PAIRS_PER_TILE = PAIRS // U_TILES
SLOT = 2 * SSM_GROUP
BF16_ROWS = 2 * SUBLANES
assert PAIRS_PER_TILE == 4 and LAGS == 2 * PAIRS_PER_TILE

LAYER_ROWS = 512
PROJ_ROWS = 256
ATTN_ROWS = 512
SSM_ROWS = 256
OUT_ROWS = 256


def _sigmoid(x):
    return 1.0 / (1.0 + jnp.exp2(x * (-LOG2E)))


def _const_spec(shape):
    zeros = (0,) * len(shape)
    return pl.BlockSpec(shape, lambda *_: zeros, pipeline_mode=pl.Buffered(1))


def _params(vmem_bytes, n_grid):
    return pltpu.CompilerParams(
        dimension_semantics=("arbitrary",) * n_grid,
        vmem_limit_bytes=min(int(vmem_bytes), V7X_VMEM_BYTES - 8 * 1024 * 1024),
    )


def _run(steps):
    for step in steps:
        step()


def _spread(main, other):
    merged, j = [], 0
    for i, step in enumerate(main):
        while j < len(other) and j * len(main) <= i * len(other):
            merged.append(other[j])
            j += 1
        merged.append(step)
    return merged + other[j:]


def _proj_steps(get_x, gain, get_tabs, w_ref, o, store=None):
    st = {}

    def norm():
        x = get_x()
        ms = jnp.mean(x * x, axis=-1, keepdims=True)
        st["xn"] = (x * lax.rsqrt(ms + EPS) * gain).astype(BF16)
        st["lo"] = lax.broadcasted_iota(jnp.int32, (x.shape[0], LANES), 1) < HEAD_DIM

    def seg(a, b):
        return jnp.dot(st["xn"], w_ref[:, a:b], preferred_element_type=F32)

    def rope(t):
        cos, sina, sinb = get_tabs()
        return (t * cos + pltpu.roll(t, ROT_DIM // 2, 1) * sina
                + pltpu.roll(t, LANES - ROT_DIM // 2, 1) * sinb)

    def both_halves(t):
        tr = pltpu.roll(t, HEAD_DIM, 1)
        return [jnp.where(st["lo"], t, tr).astype(BF16), jnp.where(st["lo"], tr, t).astype(BF16)]

    def done(name):
        if store is not None:
            store(name)

    def q():
        zq = seg(O_Q, O_K)
        o["qh"] = []
        for j in range(ATTN_WIDTH // LANES):
            qt = rope(zq[:, j * LANES:(j + 1) * LANES]) * Q_SCALE
            o["qh"] += [jnp.where(st["lo"], qt, 0.0).astype(BF16),
                        jnp.where(st["lo"], 0.0, qt).astype(BF16)]

    def kv():
        z = seg(O_K, O_ZA)
        o["k"] = rope(z[:, :KV_WIDTH])
        o["k2"] = both_halves(o["k"])
        done("k")
        o["v"] = z[:, KV_WIDTH:]
        o["v2"] = both_halves(o["v"])
        done("v")

    def za():
        z = seg(O_ZA, O_U)
        o["sa"] = z * _sigmoid(z)

    def u():
        o["u"] = seg(O_U, O_ZS)
        done("u")

    def zs():
        z = seg(O_ZS, O_GA)
        o["sz"] = z * _sigmoid(z)

    def ga():
        o["ga"] = _sigmoid(seg(O_GA, O_GS))

    def gs():
        o["gs"] = _sigmoid(seg(O_GS, IN_WIDTH))

    return [norm, q, kv, za, u, zs, ga, gs]


def _attn_steps(sinks_ref, chunks, get_q, get_kv, get_valid, emit):
    nt = (((1,), (1,)), ((), ()))
    units = [(c, kv) for c in chunks for kv in range(N_KV_HEADS)]
    n = len(units)
    st = {}

    def scores(c, kv):
        k2, v2 = get_kv(c, kv)
        qm = jnp.concatenate([get_q(c, kv * Q_PER_KV + h) for h in range(Q_PER_KV)], axis=0)
        s = lax.dot_general(qm, k2, nt, preferred_element_type=F32)
        valid = get_valid(c)
        if valid is not None:
            s = jnp.where(valid, s, -jnp.inf)
        return s, v2

    def softmax(s, kv):
        head_row = lax.broadcasted_iota(jnp.int32, (Q_PER_KV * CHUNK, 1), 0) // CHUNK
        sk = [sinks_ref[kv * Q_PER_KV + h] * LOG2E for h in range(Q_PER_KV)]
        sink = jnp.where(head_row == 0, sk[0],
                         jnp.where(head_row == 1, sk[1], jnp.where(head_row == 2, sk[2], sk[3])))
        m = jnp.maximum(jnp.max(s, axis=1, keepdims=True), sink)
        return jnp.exp2(s - m).astype(BF16), jnp.exp2(sink - m)

    def output(e, sink_term, v2, c, kv):
        lo_q = lax.broadcasted_iota(jnp.int32, (CHUNK, LANES), 1) < HEAD_DIM
        ones = jnp.ones((KEYS, LANES), BF16)
        pv = jnp.dot(e, jnp.concatenate([v2, ones], axis=1), preferred_element_type=F32)
        o = pv[:, :LANES] / (pv[:, LANES:] + sink_term)
        for j in range(Q_PER_KV // 2):
            even = o[2 * j * CHUNK:(2 * j + 1) * CHUNK]
            odd = o[(2 * j + 1) * CHUNK:(2 * j + 2) * CHUNK]
            emit(c, kv * (Q_PER_KV // 2) + j, jnp.where(lo_q, even, odd))

    def make(i):
        def step():
            if i < n:
                st[i] = scores(*units[i])
            if 0 <= i - 1 < n:
                s, v2 = st[i - 1]
                st[i - 1] = softmax(s, units[i - 1][1]) + (v2,)
            if 0 <= i - 2 < n:
                e, den, v2 = st.pop(i - 2)
                output(e, den, v2, *units[i - 2])
        return step

    return [make(i) for i in range(n + 2)]


def _ssm_reset(ubuf, cr_s, ci_s):
    ubuf[0:LAGS, :] = jnp.zeros((LAGS, SSM_WIDTH), F32)
    cr_s[...] = jnp.zeros(cr_s.shape, F32)
    ci_s[...] = jnp.zeros(ci_s.shape, F32)


def _ssm_steps(ubuf, row0, tt, cr_s, ci_s, hs, wlag_ref, a8r_ref, a8i_ref, ck_ref, hr_ref, hi_ref, o,
               seg=None):
    st = {"ys": []}

    def setup():
        if seg is not None:
            st["row_in_seg"] = lax.broadcasted_iota(jnp.int32, (tt, LANES), 0) % seg[0]
        slot = lax.broadcasted_iota(jnp.int32, (tt, LANES), 1) // SLOT
        st["to_low"] = [((slot + PAIRS_PER_TILE - s) % PAIRS_PER_TILE) < 2 for s in range(2)]
        st["same_parity"] = [((slot + sg) % 2) == 0 for sg in range(2)]

    def route(r):
        low = [jnp.where(st["to_low"][s], r[s], r[s + 2]) for s in range(2)]
        high = [jnp.where(st["to_low"][s], r[s + 2], r[s]) for s in range(2)]
        return [jnp.where(st["same_parity"][sg % 2], src[0], src[1])
                for sg, src in zip(range(PAIRS_PER_TILE), (low, low, high, high))]

    def lag_copies(k):
        def step():
            ub = ubuf[row0:row0 + LAGS + tt, k * LANES:(k + 1) * LANES]
            rolled = []
            for s in range(LAGS):
                us = ub[LAGS:] if s == 0 else pltpu.roll(ub, s, 0)[LAGS:]
                if seg is not None and s > 0:
                    us = jnp.where(st["row_in_seg"] >= s, us, 0.0)
                if s % PAIRS_PER_TILE:
                    us = pltpu.roll(us, SLOT * (s % PAIRS_PER_TILE), 1)
                rolled.append(us)
            st["halves"] = (route(rolled[:PAIRS_PER_TILE]), route(rolled[PAIRS_PER_TILE:]))
        return step

    def pair(k, sg):
        def step():
            q = k * PAIRS_PER_TILE + sg
            xl = jnp.concatenate([st["halves"][0][sg], st["halves"][1][sg]], axis=1).astype(BF16)
            w = jnp.dot(xl, wlag_ref[q], preferred_element_type=F32)
            ar, ai = a8r_ref[q:q + 1, :], a8i_ref[q:q + 1, :]
            if seg is None:
                cr, ci = cr_s[q], ci_s[q]
            else:
                seg_rows, h0r_ref, h0i_ref, pr_ref, pi_ref = seg
            for b2 in range(tt // BF16_ROWS):
                hrs, his = [], []
                for b in (2 * b2, 2 * b2 + 1):
                    if seg is not None and (b * SUBLANES) % seg_rows == 0:
                        n = b * SUBLANES // seg_rows
                        h0r, h0i = h0r_ref[n, q:q + 1, :], h0i_ref[n, q:q + 1, :]
                        cr = pr_ref[q] * h0r - pi_ref[q] * h0i
                        ci = pr_ref[q] * h0i + pi_ref[q] * h0r
                    blk = slice(b * SUBLANES, (b + 1) * SUBLANES)
                    hr = w[blk, :LANES] + cr
                    hi = w[blk, LANES:] + ci
                    cr = ar * hr - ai * hi
                    ci = ar * hi + ai * hr
                    hrs.append(hr)
                    his.append(hi)
                    if seg is not None and ((b + 1) * SUBLANES) % seg_rows == 0:
                        n = b * SUBLANES // seg_rows
                        hr_ref[n, q:q + 1, :] = hr[SUBLANES - 1:, :]
                        hi_ref[n, q:q + 1, :] = hi[SUBLANES - 1:, :]
                blk2 = slice(b2 * BF16_ROWS, (b2 + 1) * BF16_ROWS)
                hs[blk2, q * PAIR_N:q * PAIR_N + LANES] = jnp.concatenate(hrs, axis=0).astype(BF16)
                hs[blk2, q * PAIR_N + LANES:(q + 1) * PAIR_N] = jnp.concatenate(his, axis=0).astype(BF16)
            if seg is None:
                cr_s[q] = cr
                ci_s[q] = ci
                hr_ref[0, q:q + 1, :] = hr[SUBLANES - 1:, :]
                hi_ref[0, q:q + 1, :] = hi[SUBLANES - 1:, :]
        return step

    def c_proj(k):
        def step():
            cols = slice(k * PAIRS_PER_TILE * PAIR_N, (k + 1) * PAIRS_PER_TILE * PAIR_N)
            st["ys"].append(jnp.dot(hs[:, cols], ck_ref[k], preferred_element_type=F32))
            if k == U_TILES - 1:
                o["y"] = jnp.concatenate(st["ys"], axis=1)
        return step

    steps = [setup]
    for k in range(U_TILES):
        steps += [lag_copies(k)] + [pair(k, sg) for sg in range(PAIRS_PER_TILE)] + [c_proj(k)]
    return steps


def _glu(y, u, sz, d, wglu_ref):
    z = jax.nn.gelu(y + d * u)
    g = jnp.dot(z.astype(BF16), wglu_ref[...], preferred_element_type=F32)
    return z * _sigmoid(g) * sz


def _out_steps(src, woa_ref, wos_ref, wout_ref, wpg_ref, wpp_ref, fgain, emit):
    st = {}

    def mm(a, w_ref):
        return jnp.dot(a.astype(BF16), w_ref[...], preferred_element_type=F32)

    def branches():
        st["merged"] = src["ga"]() * mm(src["xa"](), woa_ref) + src["gs"]() * mm(src["xs"](), wos_ref)

    def residual():
        st["h"] = src["x"]() + mm(st.pop("merged"), wout_ref)

    def embed_gate():
        h = st.pop("h")
        st["h"] = h + _sigmoid(mm(h, wpg_ref)) * mm(src["p"](), wpp_ref)

    def norm():
        h = st.pop("h")
        ms = jnp.mean(h * h, axis=-1, keepdims=True)
        emit(h * lax.rsqrt(ms + EPS) * fgain)

    return [branches, residual, embed_gate, norm]


def _layer_kernel(sinks_ref, x_ref, p_ref, cos_ref, sina_ref, sinb_ref, gain_ref, w_in_ref,
                  wlag_ref, a8r_ref, a8i_ref, ck_ref, d_ref, wglu_ref,
                  woa_ref, wos_ref, wout_ref, wpg_ref, wpp_ref, fg_ref,
                  y_ref, k_ref, v_ref, hr_ref, hi_ref,
                  kbuf, vbuf, xa_s, ubuf, cr_s, ci_s, hs, *, tt):
    t = pl.program_id(1)
    half = tt // 2

    @pl.when(t == 0)
    def _():
        kbuf[0:WINDOW, :] = jnp.zeros((WINDOW, KV2_WIDTH), BF16)
        vbuf[0:WINDOW, :] = jnp.zeros((WINDOW, KV2_WIDTH), BF16)
        _ssm_reset(ubuf, cr_s, ci_s)

    pj, so = [{}, {}], [{}, {}]

    def rows(h):
        return slice(h * half, (h + 1) * half)

    def proj(h):
        def store(name):
            if name == "u":
                ubuf[LAGS + h * half:LAGS + (h + 1) * half, :] = pj[h]["u"]
                return
            last_ref, buf = (k_ref, kbuf) if name == "k" else (v_ref, vbuf)
            if h == 1:
                last_ref[0] = pj[h][name][half - WINDOW:]
            for j in range(N_KV_HEADS):
                buf[WINDOW + h * half:WINDOW + (h + 1) * half, j * LANES:(j + 1) * LANES] = pj[h][name + "2"][j]

        return _proj_steps(lambda: x_ref[0, rows(h)], gain_ref[...],
                           lambda: (cos_ref[rows(h)], sina_ref[rows(h)], sinb_ref[rows(h)]),
                           w_in_ref, pj[h], store)

    def mid(h):
        def get_q(c, head):
            r0 = c * CHUNK - h * half
            return pj[h]["qh"][head][r0:r0 + CHUNK]

        def get_kv(c, kv):
            krows, cols = slice(c * CHUNK, c * CHUNK + KEYS), slice(kv * LANES, (kv + 1) * LANES)
            return kbuf[krows, cols], vbuf[krows, cols]

        def get_valid(c):
            if c * CHUNK >= WINDOW:
                return None
            in_seq = c * CHUNK + lax.broadcasted_iota(jnp.int32, (1, KEYS), 1) >= WINDOW
            return jnp.logical_or(in_seq, t > 0)

        def emit(c, tile, o):
            r0, cols = c * CHUNK - h * half, slice(tile * LANES, (tile + 1) * LANES)
            xa_s[c * CHUNK:(c + 1) * CHUNK, cols] = (o * pj[h]["sa"][r0:r0 + CHUNK, cols]).astype(BF16)

        def glu():
            so[h]["xs"] = _glu(so[h].pop("y"), pj[h]["u"], pj[h]["sz"], d_ref[...], wglu_ref)

        chunks = range(h * half // CHUNK, (h + 1) * half // CHUNK)
        return (_attn_steps(sinks_ref, chunks, get_q, get_kv, get_valid, emit)
                + _ssm_steps(ubuf, h * half, half, cr_s, ci_s, hs, wlag_ref, a8r_ref, a8i_ref, ck_ref,
                             hr_ref, hi_ref, so[h])
                + [glu])

    def out(h):
        src = dict(xa=lambda: xa_s[rows(h)], xs=lambda: so[h]["xs"], ga=lambda: pj[h]["ga"],
                   gs=lambda: pj[h]["gs"], x=lambda: x_ref[0, rows(h)], p=lambda: p_ref[0, rows(h)])

        def emit(y):
            y_ref[0, rows(h)] = y

        return _out_steps(src, woa_ref, wos_ref, wout_ref, wpg_ref, wpp_ref, fg_ref[...], emit)

    _run(proj(0))
    _run(_spread(mid(0), proj(1)))
    _run(_spread(mid(1), out(0)))
    _run(out(1))

    kbuf[0:WINDOW, :] = kbuf[tt:tt + WINDOW, :]
    vbuf[0:WINDOW, :] = vbuf[tt:tt + WINDOW, :]
    ubuf[0:LAGS, :] = ubuf[tt:tt + LAGS, :]


def _layer_fused(x, p, tabs, wts, consts):
    b, t, _ = x.shape
    tt = min(LAYER_ROWS, t)
    assert t % tt == 0 and tt // 2 >= WINDOW and (tt // 2) % BF16_ROWS == 0 and tabs[0].shape[0] == t
    (gain, w_in, sinks, woa, d_skip, w_glu, wos, wout, wpg, wpp, fgain) = wts
    wlag, a8r, a8i, _, _, ck = consts

    def row_spec(w):
        return pl.BlockSpec((1, tt, w), lambda i, j: (i, j, 0))

    tab_spec = pl.BlockSpec((tt, LANES), lambda i, j: (j, 0))
    st_spec = pl.BlockSpec((1, PAIRS, LANES), lambda i, j: (i, 0, 0))
    win_spec = pl.BlockSpec((1, WINDOW, KV_WIDTH), lambda i, j: (i, 0, 0))
    consts_in = (gain, w_in, wlag, a8r, a8i, ck, d_skip, w_glu, woa, wos, wout, wpg, wpp, fgain)
    vmem = (sum(a.size * a.dtype.itemsize for a in consts_in)
            + 2 * tt * (2 * D_MODEL + PLE_DIM + 2 * KV_WIDTH + 3 * LANES) * 4
            + 2 * (WINDOW + tt) * KV2_WIDTH * 2 + tt * ATTN_WIDTH * 2 + (tt + LAGS) * SSM_WIDTH * 4
            + (tt // 2) * N_STATE * 2 + 3 * tt * IN_WIDTH * 4)
    y, k, v, hr, hi = pl.pallas_call(
        functools.partial(_layer_kernel, tt=tt),
        grid=(b, t // tt),
        in_specs=[pl.BlockSpec(memory_space=pltpu.SMEM), row_spec(D_MODEL), row_spec(PLE_DIM),
                  tab_spec, tab_spec, tab_spec]
                 + [_const_spec(a.shape) for a in consts_in],
        out_specs=[row_spec(D_MODEL), win_spec, win_spec, st_spec, st_spec],
        out_shape=[jax.ShapeDtypeStruct((b, t, D_MODEL), F32),
                   jax.ShapeDtypeStruct((b, WINDOW, KV_WIDTH), F32),
                   jax.ShapeDtypeStruct((b, WINDOW, KV_WIDTH), F32),
                   jax.ShapeDtypeStruct((b, PAIRS, LANES), F32),
                   jax.ShapeDtypeStruct((b, PAIRS, LANES), F32)],
        scratch_shapes=[pltpu.VMEM((WINDOW + tt, KV2_WIDTH), BF16),
                        pltpu.VMEM((WINDOW + tt, KV2_WIDTH), BF16),
                        pltpu.VMEM((tt, ATTN_WIDTH), BF16),
                        pltpu.VMEM((tt + LAGS, SSM_WIDTH), F32),
                        pltpu.VMEM((PAIRS, SUBLANES, LANES), F32),
                        pltpu.VMEM((PAIRS, SUBLANES, LANES), F32),
                        pltpu.VMEM((tt // 2, N_STATE), BF16)],
        compiler_params=_params(vmem, 2),
        name="layer_prompt",
    )(sinks, x, p, *tabs, *consts_in)
    return y, k, v, hr, hi


def _proj_kernel(x_ref, gain_ref, cos_ref, sina_ref, sinb_ref, w_ref,
                 q_ref, k_ref, v_ref, k2_ref, v2_ref, sa_ref, u_ref, sz_ref, ga_ref, gs_ref):
    pj = {}
    _run(_proj_steps(lambda: x_ref[...], gain_ref[...],
                     lambda: (cos_ref[...], sina_ref[...], sinb_ref[...]), w_ref, pj))
    for h in range(N_HEADS):
        q_ref[:, h * LANES:(h + 1) * LANES] = pj["qh"][h]
    for j in range(N_KV_HEADS):
        k2_ref[:, j * LANES:(j + 1) * LANES] = pj["k2"][j]
        v2_ref[:, j * LANES:(j + 1) * LANES] = pj["v2"][j]
    k_ref[...] = pj["k"]
    v_ref[...] = pj["v"]
    sa_ref[...] = pj["sa"]
    u_ref[...] = pj["u"]
    sz_ref[...] = pj["sz"]
    ga_ref[...] = pj["ga"]
    gs_ref[...] = pj["gs"]


def _proj(x2d, gain, cos, sina, sinb, w_in, seq):
    n = x2d.shape[0]
    tm = min(PROJ_ROWS, n)
    tab_rows = cos.shape[0]
    tab_tiles = tab_rows // tm
    assert n % tm == 0 and tab_rows % tm == 0

    def row_spec(w):
        return pl.BlockSpec((tm, w), lambda i: (i, 0))

    tab_spec = pl.BlockSpec((tm, LANES), lambda i: (i % tab_tiles, 0))
    widths = (QM_WIDTH, KV_WIDTH, KV_WIDTH, KV2_WIDTH, KV2_WIDTH,
              ATTN_WIDTH, SSM_WIDTH, SSM_WIDTH, D_MODEL, D_MODEL)
    dtypes = (BF16, F32, F32, BF16, BF16) + (F32,) * 5
    vmem = (2 * tm * D_MODEL * 4 + D_MODEL * IN_WIDTH * 2 + 3 * 2 * tm * LANES * 4
            + 3 * tm * IN_WIDTH * 4 + tm * D_MODEL * 8)
    return pl.pallas_call(
        _proj_kernel,
        grid=(n // tm,),
        in_specs=[row_spec(D_MODEL), _const_spec((1, D_MODEL)), tab_spec, tab_spec, tab_spec,
                  _const_spec((D_MODEL, IN_WIDTH))],
        out_specs=[row_spec(w) for w in widths],
        out_shape=[jax.ShapeDtypeStruct((n, w), d) for w, d in zip(widths, dtypes)],
        compiler_params=_params(vmem, 1),
        name=f"proj_{seq}",
    )(x2d, gain, cos, sina, sinb, w_in)


def _attn_kernel(sinks_ref, q_ref, k_ref, v_ref, sa_ref, o_ref, *, tq):
    t = pl.program_id(1)

    def rows(c):
        return slice(c * CHUNK, (c + 1) * CHUNK)

    def get_kv(c, kv):
        row0 = pl.multiple_of(t * tq + c * CHUNK, CHUNK)
        cols = slice(kv * LANES, (kv + 1) * LANES)
        return k_ref[0, pl.ds(row0, KEYS), cols], v_ref[0, pl.ds(row0, KEYS), cols]

    def emit(c, tile, o):
        cols = slice(tile * LANES, (tile + 1) * LANES)
        o_ref[0, rows(c), cols] = o * sa_ref[0, rows(c), cols]

    _run(_attn_steps(sinks_ref, range(tq // CHUNK),
                     lambda c, h: q_ref[0, rows(c), h * LANES:(h + 1) * LANES],
                     get_kv, lambda c: None, emit))


def _attn(sinks, qm, k2pad, v2pad, sa, seq):
    b, t, _ = qm.shape
    tq = min(ATTN_ROWS, t)
    assert t % tq == 0 and k2pad.shape[1] == t + WINDOW

    def row_spec(w):
        return pl.BlockSpec((1, tq, w), lambda i, j: (i, j, 0))

    kv_spec = pl.BlockSpec((1, t + WINDOW, KV2_WIDTH), lambda i, j: (i, 0, 0))
    vmem = (2 * tq * (QM_WIDTH * 2 + 2 * ATTN_WIDTH * 4) + 2 * 2 * (t + WINDOW) * KV2_WIDTH * 2
            + 8 * Q_PER_KV * CHUNK * 2 * LANES * 4)
    return pl.pallas_call(
        functools.partial(_attn_kernel, tq=tq),
        grid=(b, t // tq),
        in_specs=[pl.BlockSpec(memory_space=pltpu.SMEM), row_spec(QM_WIDTH), kv_spec, kv_spec,
                  row_spec(ATTN_WIDTH)],
        out_specs=row_spec(ATTN_WIDTH),
        out_shape=jax.ShapeDtypeStruct((b, t, ATTN_WIDTH), F32),
        compiler_params=_params(vmem, 2),
        name=f"attn_{seq}",
    )(sinks, qm, k2pad, v2pad, sa)


def _ssm_kernel(u_ref, sz_ref, h0r_ref, h0i_ref, wlag_ref, a8r_ref, a8i_ref, pr_ref, pi_ref,
                ck_ref, d_ref, wglu_ref, xs_ref, hr_ref, hi_ref, ubuf, hs, *, n, t):
    u = u_ref[...]
    ubuf[0:LAGS, :] = jnp.zeros((LAGS, SSM_WIDTH), F32)
    ubuf[LAGS:, :] = u
    so = {}
    _run(_ssm_steps(ubuf, 0, n * t, None, None, hs, wlag_ref, a8r_ref, a8i_ref, ck_ref, hr_ref, hi_ref, so,
                    seg=(t, h0r_ref, h0i_ref, pr_ref, pi_ref)))
    xs_ref[...] = _glu(so["y"], u, sz_ref[...], d_ref[...], wglu_ref)


def _ssm(u2d, sz2d, h0r, h0i, consts, d_skip, w_glu, seq):
    n, t = h0r.shape[0], u2d.shape[0] // h0r.shape[0]
    rows = n * t
    assert t % BF16_ROWS == 0
    wlag, a8r, a8i, pr, pi, ck = consts
    vmem = (3 * 2 * rows * SSM_WIDTH * 4 + wlag.size * 2 + ck.size * 2 + w_glu.size * 2
            + rows * N_STATE * 2 + (LAGS + 2) * rows * SSM_WIDTH * 4 + 4 * rows * SSM_WIDTH * 4
            + 4 * rows * PAIR_N * 4 + 8 * n * PAIRS * LANES * 4)
    operands = (u2d, sz2d, h0r, h0i, wlag, a8r, a8i, pr, pi, ck, d_skip, w_glu)
    return pl.pallas_call(
        functools.partial(_ssm_kernel, n=n, t=t),
        grid=(1,),
        in_specs=[_const_spec(a.shape) for a in operands],
        out_specs=[pl.BlockSpec((rows, SSM_WIDTH), lambda i: (0, 0)),
                   pl.BlockSpec(h0r.shape, lambda i: (0, 0, 0)), pl.BlockSpec(h0r.shape, lambda i: (0, 0, 0))],
        out_shape=[jax.ShapeDtypeStruct((rows, SSM_WIDTH), F32),
                   jax.ShapeDtypeStruct(h0r.shape, F32),
                   jax.ShapeDtypeStruct(h0r.shape, F32)],
        scratch_shapes=[pltpu.VMEM((rows + LAGS, SSM_WIDTH), F32),
                        pltpu.VMEM((rows, N_STATE), BF16)],
        compiler_params=_params(vmem, 1),
        name=f"ssm_{seq}",
    )(*operands)


def _out_kernel(xa_ref, xs_ref, ga_ref, gs_ref, x_ref, p_ref, woa_ref, wos_ref, wout_ref,
                wpg_ref, wpp_ref, fg_ref, y_ref):
    src = dict(xa=lambda: xa_ref[...], xs=lambda: xs_ref[...], ga=lambda: ga_ref[...],
               gs=lambda: gs_ref[...], x=lambda: x_ref[...], p=lambda: p_ref[...])

    def emit(y):
        y_ref[...] = y

    _run(_out_steps(src, woa_ref, wos_ref, wout_ref, wpg_ref, wpp_ref, fg_ref[...], emit))


def _out(xa, xs, ga, gs, x2d, p2d, woa, wos, wout, wpg, wpp, fgain, seq):
    n = x2d.shape[0]
    tm = min(OUT_ROWS, n)
    assert n % tm == 0

    def row_spec(w):
        return pl.BlockSpec((tm, w), lambda i: (i, 0))

    weights = (woa, wos, wout, wpg, wpp)
    vmem = (2 * tm * (2 * ATTN_WIDTH + 4 * D_MODEL + PLE_DIM) * 4 + sum(w.size for w in weights) * 2
            + 8 * tm * D_MODEL * 4)
    return pl.pallas_call(
        _out_kernel,
        grid=(n // tm,),
        in_specs=[row_spec(ATTN_WIDTH), row_spec(SSM_WIDTH), row_spec(D_MODEL), row_spec(D_MODEL),
                  row_spec(D_MODEL), row_spec(PLE_DIM)]
                 + [_const_spec(w.shape) for w in weights] + [_const_spec((1, D_MODEL))],
        out_specs=row_spec(D_MODEL),
        out_shape=jax.ShapeDtypeStruct((n, D_MODEL), F32),
        compiler_params=_params(vmem, 1),
        name=f"out_{seq}",
    )(xa, xs, ga, gs, x2d, p2d, woa, wos, wout, wpg, wpp, fgain)


def _both_halves(a):
    h0, h1 = a[..., :HEAD_DIM], a[..., HEAD_DIM:]
    return jnp.concatenate([h0, h0, h1, h1], axis=-1).astype(BF16)


def _layer_split(x, p, tabs, k_prefix, v_prefix, h0r, h0i, wts, consts, seq):
    b, t, _ = x.shape
    assert t <= WINDOW
    (gain, w_in, sinks, woa, d_skip, w_glu, wos, wout, wpg, wpp, fgain) = wts
    x2d = x.reshape(b * t, D_MODEL)
    qm, k, v, k2, v2, sa, u, sz, ga, gs = _proj(x2d, gain, *tabs, w_in, seq)
    k2pad = jnp.concatenate([_both_halves(k_prefix), k2.reshape(b, t, KV2_WIDTH)], axis=1)
    v2pad = jnp.concatenate([_both_halves(v_prefix), v2.reshape(b, t, KV2_WIDTH)], axis=1)
    xa = _attn(sinks, qm.reshape(b, t, QM_WIDTH), k2pad, v2pad, sa.reshape(b, t, ATTN_WIDTH), seq)
    xs, hr, hi = _ssm(u, sz, h0r, h0i, consts, d_skip, w_glu, seq)
    y = _out(xa.reshape(b * t, ATTN_WIDTH), xs, ga, gs, x2d,
             p.reshape(b * t, PLE_DIM), woa, wos, wout, wpg, wpp, fgain, seq)
    k_new = jnp.concatenate([k_prefix[:, t:], k.reshape(b, t, KV_WIDTH)], axis=1)
    v_new = jnp.concatenate([v_prefix[:, t:], v.reshape(b, t, KV_WIDTH)], axis=1)
    return y.reshape(b, t, D_MODEL), k_new, v_new, hr, hi


def _ssm_constants(a_re, a_im, log_dt, b_re, b_im, c_re, c_im):
    dt = jnp.exp(log_dt.astype(F32))[:, None]
    lr = a_re.astype(F32).reshape(PAIRS, 1, 1, LANES)
    li = a_im.astype(F32).reshape(PAIRS, 1, 1, LANES)
    xr = (a_re.astype(F32) * dt).reshape(PAIRS, 1, 1, LANES)
    xi = (a_im.astype(F32) * dt).reshape(PAIRS, 1, 1, LANES)

    def apow(n):
        mag = jnp.exp(xr * n)
        return mag * jnp.cos(xi * n), mag * jnp.sin(xi * n)

    ar, ai = apow(1.0)
    nr, ni = ar - 1.0, ai
    den = lr * lr + li * li
    fr, fi = (nr * lr + ni * li) / den, (ni * lr - nr * li) / den

    n_slots = PAIR_K // SLOT
    qq, hi_ = np.arange(PAIRS)[:, None], np.arange(n_slots)[None, :]
    lag_tab = (PAIRS_PER_TILE * (hi_ // PAIRS_PER_TILE) + (hi_ % PAIRS_PER_TILE - qq) % PAIRS_PER_TILE)
    lag_tab = lag_tab.astype(np.float32)[:, :, None, None]
    same_group = (np.arange(SLOT)[:, None] // SSM_GROUP == np.arange(LANES)[None, :] // SSM_STATE)
    same_group = same_group.astype(np.float32)

    def b_rows(bm):
        t = jnp.transpose(bm.astype(F32).reshape(PAIRS, 2, SSM_STATE, SSM_GROUP), (0, 3, 1, 2))
        t = t.reshape(PAIRS, 1, 1, SSM_GROUP, LANES)
        t = jnp.broadcast_to(t, (PAIRS, 1, 2, SSM_GROUP, LANES))
        return t.reshape(PAIRS, 1, SLOT, LANES) * same_group

    br, bi = b_rows(b_re), b_rows(b_im)
    bbr, bbi = fr * br - fi * bi, fr * bi + fi * br
    er, ei = apow(lag_tab)
    wlag = jnp.concatenate([(er * bbr - ei * bbi).reshape(PAIRS, PAIR_K, LANES),
                            (er * bbi + ei * bbr).reshape(PAIRS, PAIR_K, LANES)], axis=-1).astype(BF16)

    a8r, a8i = (a.reshape(PAIRS, LANES) for a in apow(float(LAGS)))
    pwr, pwi = (a.reshape(PAIRS, LAGS, LANES)
                for a in apow(np.arange(1, LAGS + 1, dtype=np.float32)[None, :, None, None]))

    def c_cols(c):
        t = jnp.transpose(c.astype(F32).reshape(U_TILES, LANES // SSM_GROUP, SSM_GROUP, SSM_STATE),
                          (0, 3, 1, 2))
        return t.reshape(U_TILES, 1, 1, 1, SSM_STATE, LANES)

    cols_group = np.arange(LANES) // SSM_GROUP
    rows_group = 2 * np.arange(PAIRS_PER_TILE)[:, None] + np.arange(2)[None, :]
    c_mask = (rows_group[:, None, :, None, None] == cols_group[None, None, None, None, :])
    c_mask = c_mask.astype(np.float32)[None]
    ck = jnp.concatenate([c_cols(c_re) * c_mask, -c_cols(c_im) * c_mask], axis=2)
    ck = ck.reshape(U_TILES, PAIRS_PER_TILE * PAIR_N, LANES).astype(BF16)
    return wlag, a8r, a8i, pwr, pwi, ck


def _rope_tables(pos0, t, rows):
    half = ROT_DIM // 2
    d = np.arange(LANES) % HEAD_DIM
    inv = jnp.power(ROPE_THETA, -jnp.arange(half, dtype=F32) * 2.0 / ROT_DIM)
    pos = (pos0 + jnp.arange(t)).astype(F32)
    ang = pos[:, None] * inv[None, :]
    cos, sin = (jnp.tile(a, (1, LANES // half)) for a in (jnp.cos(ang), jnp.sin(ang)))
    cos_t = jnp.where((d < ROT_DIM)[None, :], cos, 1.0)
    sina = jnp.where(((d >= half) & (d < ROT_DIM))[None, :], sin, 0.0)
    sinb = jnp.where((d < half)[None, :], -sin, 0.0)
    reps = (max(rows // t, 1), 1)
    return tuple(jnp.tile(a, reps) for a in (cos_t, sina, sinb))


def kernel(x_prompt, x_sample, p_prompt, p_sample, cache_attn_k, cache_attn_v, state_ssm_re,
           state_ssm_im, norm_gain, w_in, attn_sinks, w_o_attn, ssm_a_re, ssm_a_im, ssm_log_dt,
           ssm_b_re, ssm_b_im, ssm_c_re, ssm_c_im, ssm_d, ssm_w_glu, w_o_ssm, w_out,
           w_ple_gate, w_ple_proj, final_norm_gain):
    assert norm_gain.shape[0] == 1, "single-layer model"
    bp, tp, _ = x_prompt.shape
    bs, ts, _ = x_sample.shape
    wts = (norm_gain[0].reshape(1, D_MODEL).astype(F32), w_in[0].astype(BF16),
           attn_sinks[0].astype(F32), w_o_attn[0].astype(BF16),
           ssm_d[0].reshape(1, SSM_WIDTH).astype(F32), ssm_w_glu[0].astype(BF16),
           w_o_ssm[0].astype(BF16), w_out[0].astype(BF16), w_ple_gate[0].astype(BF16),
           w_ple_proj[0].astype(BF16), final_norm_gain.reshape(1, D_MODEL).astype(F32))
    consts = _ssm_constants(ssm_a_re[0], ssm_a_im[0], ssm_log_dt[0], ssm_b_re[0], ssm_b_im[0],
                            ssm_c_re[0], ssm_c_im[0])

    y_p, k_p, v_p, hr_p, hi_p = _layer_fused(x_prompt, p_prompt[0], _rope_tables(0, tp, tp), wts, consts)

    ck = cache_attn_k[0].reshape(bs, WINDOW, KV_WIDTH).astype(F32)
    cv = cache_attn_v[0].reshape(bs, WINDOW, KV_WIDTH).astype(F32)
    h0r = state_ssm_re[0].reshape(bs, PAIRS, LANES).astype(F32)
    h0i = state_ssm_im[0].reshape(bs, PAIRS, LANES).astype(F32)
    tabs_s = _rope_tables(PAST_LEN, ts, min(PROJ_ROWS, bs * ts))
    y_s, k_s, v_s, hr_s, hi_s = _layer_split(x_sample, p_sample[0], tabs_s, ck, cv, h0r, h0i,
                                             wts, consts, "sample")

    def kv_out(a, b):
        return a.reshape(1, b, WINDOW, N_KV_HEADS, HEAD_DIM)

    def st_out(a, b):
        return a.reshape(1, b, SSM_GROUPS, SSM_STATE)

    return (y_p, y_s, kv_out(k_p, bp), kv_out(v_p, bp),
            st_out(hr_p, bp), st_out(hi_p, bp), kv_out(k_s, bs), kv_out(v_s, bs),
            st_out(hr_s, bs), st_out(hi_s, bs))
```

```python
import functools

import numpy as np
import jax
import jax.numpy as jnp
from jax import lax
from jax.experimental import pallas as pl
from jax.experimental.pallas import tpu as pltpu

F32 = jnp.float32
BF16 = jnp.bfloat16

LANES = 128
SUBLANES = 8
V7X_VMEM_BYTES = 64 * 1024 * 1024

D_MODEL = 1024
CHUNK = 64
WINDOW = 128
N_HEADS = 8
N_KV_HEADS = 2
HEAD_DIM = 64
Q_PER_KV = N_HEADS // N_KV_HEADS
LOG2E = 1.4426950408889634
Q_SCALE = HEAD_DIM ** -0.5 * LOG2E
ATTN_WIDTH = N_HEADS * HEAD_DIM
KV_WIDTH = N_KV_HEADS * HEAD_DIM
ROT_DIM = HEAD_DIM // 4
ROPE_THETA = 500000.0
SSM_WIDTH = D_MODEL // 2
SSM_GROUP = 16
SSM_GROUPS = SSM_WIDTH // SSM_GROUP
SSM_STATE = 64
PLE_DIM = 256
PAST_LEN = 1024
EPS = 1e-6

O_Q = 0
O_K = O_Q + ATTN_WIDTH
O_V = O_K + KV_WIDTH
O_ZA = O_V + KV_WIDTH
O_U = O_ZA + ATTN_WIDTH
O_ZS = O_U + SSM_WIDTH
O_GA = O_ZS + SSM_WIDTH
O_GS = O_GA + D_MODEL
IN_WIDTH = O_GS + D_MODEL

QM_WIDTH = N_HEADS * LANES
KV2_WIDTH = N_KV_HEADS * LANES
KEYS = WINDOW + CHUNK
LAGS = SUBLANES
PAIRS = SSM_GROUPS // 2
PAIR_K = 2 * LAGS * SSM_GROUP
PAIR_N = 2 * 2 * SSM_STATE
N_STATE = PAIRS * PAIR_N
U_TILES = SSM_WIDTH // LANES
PAIRS_PER_TILE = PAIRS // U_TILES
SLOT = 2 * SSM_GROUP
BF16_ROWS = 2 * SUBLANES
assert PAIRS_PER_TILE == 4 and LAGS == 2 * PAIRS_PER_TILE

LAYER_ROWS = 512


def _sigmoid(x):
    return 1.0 / (1.0 + jnp.exp2(x * (-LOG2E)))


def _const_spec(shape):
    zeros = (0,) * len(shape)
    return pl.BlockSpec(shape, lambda *_: zeros, pipeline_mode=pl.Buffered(1))


def _params(vmem_bytes, n_grid):
    return pltpu.CompilerParams(
        dimension_semantics=("arbitrary",) * n_grid,
        vmem_limit_bytes=min(int(vmem_bytes), V7X_VMEM_BYTES - 8 * 1024 * 1024),
    )


def _run(steps):
    for step in steps:
        step()


def _spread(main, other):
    merged, j = [], 0
    for i, step in enumerate(main):
        while j < len(other) and j * len(main) <= i * len(other):
            merged.append(other[j])
            j += 1
        merged.append(step)
    return merged + other[j:]


def _proj_steps(get_x, gain, get_tabs, w_ref, o, store=None):
    st = {}

    def norm():
        x = get_x()
        ms = jnp.mean(x * x, axis=-1, keepdims=True)
        st["xn"] = (x * lax.rsqrt(ms + EPS) * gain).astype(BF16)
        st["lo"] = lax.broadcasted_iota(jnp.int32, (x.shape[0], LANES), 1) < HEAD_DIM

    def seg(a, b):
        return jnp.dot(st["xn"], w_ref[:, a:b], preferred_element_type=F32)

    def rope(t):
        cos, sina, sinb = get_tabs()
        return (t * cos + pltpu.roll(t, ROT_DIM // 2, 1) * sina
                + pltpu.roll(t, LANES - ROT_DIM // 2, 1) * sinb)

    def both_halves(t):
        tr = pltpu.roll(t, HEAD_DIM, 1)
        return [jnp.where(st["lo"], t, tr).astype(BF16), jnp.where(st["lo"], tr, t).astype(BF16)]

    def done(name):
        if store is not None:
            store(name)

    def q():
        zq = seg(O_Q, O_K)
        o["qh"] = []
        for j in range(ATTN_WIDTH // LANES):
            qt = rope(zq[:, j * LANES:(j + 1) * LANES]) * Q_SCALE
            o["qh"] += [jnp.where(st["lo"], qt, 0.0).astype(BF16),
                        jnp.where(st["lo"], 0.0, qt).astype(BF16)]

    def kv():
        z = seg(O_K, O_ZA)
        o["k"] = rope(z[:, :KV_WIDTH])
        o["k2"] = both_halves(o["k"])
        done("k")
        o["v"] = z[:, KV_WIDTH:]
        o["v2"] = both_halves(o["v"])
        done("v")

    def za():
        z = seg(O_ZA, O_U)
        o["sa"] = z * _sigmoid(z)

    def u():
        o["u"] = seg(O_U, O_ZS)
        done("u")

    def zs():
        z = seg(O_ZS, O_GA)
        o["sz"] = z * _sigmoid(z)

    def ga():
        o["ga"] = _sigmoid(seg(O_GA, O_GS))

    def gs():
        o["gs"] = _sigmoid(seg(O_GS, IN_WIDTH))

    return [norm, q, kv, za, u, zs, ga, gs]


def _attn_steps(sinks_ref, chunks, get_q, get_kv, get_valid, emit):
    nt = (((1,), (1,)), ((), ()))
    units = [(c, kv) for c in chunks for kv in range(N_KV_HEADS)]
    n = len(units)
    st = {}

    def scores(c, kv):
        k2, v2 = get_kv(c, kv)
        qm = jnp.concatenate([get_q(c, kv * Q_PER_KV + h) for h in range(Q_PER_KV)], axis=0)
        s = lax.dot_general(qm, k2, nt, preferred_element_type=F32)
        valid = get_valid(c)
        if valid is not None:
            s = jnp.where(valid, s, -jnp.inf)
        return s, v2

    def softmax(s, kv):
        head_row = lax.broadcasted_iota(jnp.int32, (Q_PER_KV * CHUNK, 1), 0) // CHUNK
        sk = [sinks_ref[kv * Q_PER_KV + h] * LOG2E for h in range(Q_PER_KV)]
        sink = jnp.where(head_row == 0, sk[0],
                         jnp.where(head_row == 1, sk[1], jnp.where(head_row == 2, sk[2], sk[3])))
        m = jnp.maximum(jnp.max(s, axis=1, keepdims=True), sink)
        return jnp.exp2(s - m).astype(BF16), jnp.exp2(sink - m)

    def output(e, sink_term, v2, c, kv):
        lo_q = lax.broadcasted_iota(jnp.int32, (CHUNK, LANES), 1) < HEAD_DIM
        ones = jnp.ones((KEYS, LANES), BF16)
        pv = jnp.dot(e, jnp.concatenate([v2, ones], axis=1), preferred_element_type=F32)
        o = pv[:, :LANES] / (pv[:, LANES:] + sink_term)
        for j in range(Q_PER_KV // 2):
            even = o[2 * j * CHUNK:(2 * j + 1) * CHUNK]
            odd = o[(2 * j + 1) * CHUNK:(2 * j + 2) * CHUNK]
            emit(c, kv * (Q_PER_KV // 2) + j, jnp.where(lo_q, even, odd))

    def make(i):
        def step():
            if i < n:
                st[i] = scores(*units[i])
            if 0 <= i - 1 < n:
                s, v2 = st[i - 1]
                st[i - 1] = softmax(s, units[i - 1][1]) + (v2,)
            if 0 <= i - 2 < n:
                e, den, v2 = st.pop(i - 2)
                output(e, den, v2, *units[i - 2])
        return step

    return [make(i) for i in range(n + 2)]


def _ssm_reset(ubuf, cr_s, ci_s):
    ubuf[0:LAGS, :] = jnp.zeros((LAGS, SSM_WIDTH), F32)
    cr_s[...] = jnp.zeros(cr_s.shape, F32)
    ci_s[...] = jnp.zeros(ci_s.shape, F32)


def _ssm_steps(ubuf, row0, tt, cr_s, ci_s, hs, wlag_ref, a8r_ref, a8i_ref, ck_ref, hr_ref, hi_ref, o,
               seg=None):
    st = {"ys": []}

    def setup():
        if seg is not None:
            st["row_in_seg"] = lax.broadcasted_iota(jnp.int32, (tt, LANES), 0) % seg[0]
        slot = lax.broadcasted_iota(jnp.int32, (tt, LANES), 1) // SLOT
        st["to_low"] = [((slot + PAIRS_PER_TILE - s) % PAIRS_PER_TILE) < 2 for s in range(2)]
        st["same_parity"] = [((slot + sg) % 2) == 0 for sg in range(2)]

    def route(r):
        low = [jnp.where(st["to_low"][s], r[s], r[s + 2]) for s in range(2)]
        high = [jnp.where(st["to_low"][s], r[s + 2], r[s]) for s in range(2)]
        return [jnp.where(st["same_parity"][sg % 2], src[0], src[1])
                for sg, src in zip(range(PAIRS_PER_TILE), (low, low, high, high))]

    def lag_copies(k):
        def step():
            ub = ubuf[row0:row0 + LAGS + tt, k * LANES:(k + 1) * LANES]
            rolled = []
            for s in range(LAGS):
                us = ub[LAGS:] if s == 0 else pltpu.roll(ub, s, 0)[LAGS:]
                if seg is not None and s > 0:
                    us = jnp.where(st["row_in_seg"] >= s, us, 0.0)
                if s % PAIRS_PER_TILE:
                    us = pltpu.roll(us, SLOT * (s % PAIRS_PER_TILE), 1)
                rolled.append(us)
            st["halves"] = (route(rolled[:PAIRS_PER_TILE]), route(rolled[PAIRS_PER_TILE:]))
        return step

    def pair(k, sg):
        def step():
            q = k * PAIRS_PER_TILE + sg
            xl = jnp.concatenate([st["halves"][0][sg], st["halves"][1][sg]], axis=1).astype(BF16)
            w = jnp.dot(xl, wlag_ref[q], preferred_element_type=F32)
            ar, ai = a8r_ref[q:q + 1, :], a8i_ref[q:q + 1, :]
            if seg is None:
                cr, ci = cr_s[q], ci_s[q]
            else:
                seg_rows, h0r_ref, h0i_ref, pr_ref, pi_ref = seg
            for b2 in range(tt // BF16_ROWS):
                hrs, his = [], []
                for b in (2 * b2, 2 * b2 + 1):
                    if seg is not None and (b * SUBLANES) % seg_rows == 0:
                        n = b * SUBLANES // seg_rows
                        h0r, h0i = h0r_ref[n, q:q + 1, :], h0i_ref[n, q:q + 1, :]
                        cr = pr_ref[q] * h0r - pi_ref[q] * h0i
                        ci = pr_ref[q] * h0i + pi_ref[q] * h0r
                    blk = slice(b * SUBLANES, (b + 1) * SUBLANES)
                    hr = w[blk, :LANES] + cr
                    hi = w[blk, LANES:] + ci
                    cr = ar * hr - ai * hi
                    ci = ar * hi + ai * hr
                    hrs.append(hr)
                    his.append(hi)
                    if seg is not None and ((b + 1) * SUBLANES) % seg_rows == 0:
                        n = b * SUBLANES // seg_rows
                        hr_ref[n, q:q + 1, :] = hr[SUBLANES - 1:, :]
                        hi_ref[n, q:q + 1, :] = hi[SUBLANES - 1:, :]
                blk2 = slice(b2 * BF16_ROWS, (b2 + 1) * BF16_ROWS)
                hs[blk2, q * PAIR_N:q * PAIR_N + LANES] = jnp.concatenate(hrs, axis=0).astype(BF16)
                hs[blk2, q * PAIR_N + LANES:(q + 1) * PAIR_N] = jnp.concatenate(his, axis=0).astype(BF16)
            if seg is None:
                cr_s[q] = cr
                ci_s[q] = ci
                hr_ref[0, q:q + 1, :] = hr[SUBLANES - 1:, :]
                hi_ref[0, q:q + 1, :] = hi[SUBLANES - 1:, :]
        return step

    def c_proj(k):
        def step():
            cols = slice(k * PAIRS_PER_TILE * PAIR_N, (k + 1) * PAIRS_PER_TILE * PAIR_N)
            st["ys"].append(jnp.dot(hs[:, cols], ck_ref[k], preferred_element_type=F32))
            if k == U_TILES - 1:
                o["y"] = jnp.concatenate(st["ys"], axis=1)
        return step

    steps = [setup]
    for k in range(U_TILES):
        steps += [lag_copies(k)] + [pair(k, sg) for sg in range(PAIRS_PER_TILE)] + [c_proj(k)]
    return steps


def _glu(y, u, sz, d, wglu_ref):
    z = jax.nn.gelu(y + d * u)
    g = jnp.dot(z.astype(BF16), wglu_ref[...], preferred_element_type=F32)
    return z * _sigmoid(g) * sz


def _out_steps(src, woa_ref, wos_ref, wout_ref, wpg_ref, wpp_ref, fgain, emit):
    st = {}

    def mm(a, w_ref):
        return jnp.dot(a.astype(BF16), w_ref[...], preferred_element_type=F32)

    def branches():
        st["merged"] = src["ga"]() * mm(src["xa"](), woa_ref) + src["gs"]() * mm(src["xs"](), wos_ref)

    def residual():
        st["h"] = src["x"]() + mm(st.pop("merged"), wout_ref)

    def embed_gate():
        h = st.pop("h")
        st["h"] = h + _sigmoid(mm(h, wpg_ref)) * mm(src["p"](), wpp_ref)

    def norm():
        h = st.pop("h")
        ms = jnp.mean(h * h, axis=-1, keepdims=True)
        emit(h * lax.rsqrt(ms + EPS) * fgain)

    return [branches, residual, embed_gate, norm]


def _layer_kernel(sinks_ref, x_ref, p_ref, cos_ref, sina_ref, sinb_ref, gain_ref, w_in_ref,
                  wlag_ref, a8r_ref, a8i_ref, ck_ref, d_ref, wglu_ref,
                  woa_ref, wos_ref, wout_ref, wpg_ref, wpp_ref, fg_ref,
                  y_ref, k_ref, v_ref, hr_ref, hi_ref,
                  kbuf, vbuf, xa_s, ubuf, cr_s, ci_s, hs, *, tt):
    t = pl.program_id(1)
    half = tt // 2

    @pl.when(t == 0)
    def _():
        kbuf[0:WINDOW, :] = jnp.zeros((WINDOW, KV2_WIDTH), BF16)
        vbuf[0:WINDOW, :] = jnp.zeros((WINDOW, KV2_WIDTH), BF16)
        _ssm_reset(ubuf, cr_s, ci_s)

    pj, so = [{}, {}], [{}, {}]

    def rows(h):
        return slice(h * half, (h + 1) * half)

    def proj(h):
        def store(name):
            if name == "u":
                ubuf[LAGS + h * half:LAGS + (h + 1) * half, :] = pj[h]["u"]
                return
            last_ref, buf = (k_ref, kbuf) if name == "k" else (v_ref, vbuf)
            if h == 1:
                last_ref[0] = pj[h][name][half - WINDOW:]
            for j in range(N_KV_HEADS):
                buf[WINDOW + h * half:WINDOW + (h + 1) * half, j * LANES:(j + 1) * LANES] = pj[h][name + "2"][j]

        return _proj_steps(lambda: x_ref[0, rows(h)], gain_ref[...],
                           lambda: (cos_ref[rows(h)], sina_ref[rows(h)], sinb_ref[rows(h)]),
                           w_in_ref, pj[h], store)

    def mid(h):
        def get_q(c, head):
            r0 = c * CHUNK - h * half
            return pj[h]["qh"][head][r0:r0 + CHUNK]

        def get_kv(c, kv):
            krows, cols = slice(c * CHUNK, c * CHUNK + KEYS), slice(kv * LANES, (kv + 1) * LANES)
            return kbuf[krows, cols], vbuf[krows, cols]

        def get_valid(c):
            if c * CHUNK >= WINDOW:
                return None
            in_seq = c * CHUNK + lax.broadcasted_iota(jnp.int32, (1, KEYS), 1) >= WINDOW
            return jnp.logical_or(in_seq, t > 0)

        def emit(c, tile, o):
            r0, cols = c * CHUNK - h * half, slice(tile * LANES, (tile + 1) * LANES)
            xa_s[c * CHUNK:(c + 1) * CHUNK, cols] = (o * pj[h]["sa"][r0:r0 + CHUNK, cols]).astype(BF16)

        def glu():
            so[h]["xs"] = _glu(so[h].pop("y"), pj[h]["u"], pj[h]["sz"], d_ref[...], wglu_ref)

        chunks = range(h * half // CHUNK, (h + 1) * half // CHUNK)
        return (_attn_steps(sinks_ref, chunks, get_q, get_kv, get_valid, emit)
                + _ssm_steps(ubuf, h * half, half, cr_s, ci_s, hs, wlag_ref, a8r_ref, a8i_ref, ck_ref,
                             hr_ref, hi_ref, so[h])
                + [glu])

    def out(h):
        src = dict(xa=lambda: xa_s[rows(h)], xs=lambda: so[h]["xs"], ga=lambda: pj[h]["ga"],
                   gs=lambda: pj[h]["gs"], x=lambda: x_ref[0, rows(h)], p=lambda: p_ref[0, rows(h)])

        def emit(y):
            y_ref[0, rows(h)] = y

        return _out_steps(src, woa_ref, wos_ref, wout_ref, wpg_ref, wpp_ref, fg_ref[...], emit)

    _run(proj(0))
    _run(_spread(mid(0), proj(1)))
    _run(_spread(mid(1), out(0)))
    _run(out(1))

    kbuf[0:WINDOW, :] = kbuf[tt:tt + WINDOW, :]
    vbuf[0:WINDOW, :] = vbuf[tt:tt + WINDOW, :]
    ubuf[0:LAGS, :] = ubuf[tt:tt + LAGS, :]


def _layer_fused(x, p, tabs, wts, consts):
    b, t, _ = x.shape
    tt = min(LAYER_ROWS, t)
    assert t % tt == 0 and tt // 2 >= WINDOW and (tt // 2) % BF16_ROWS == 0 and tabs[0].shape[0] == t
    (gain, w_in, sinks, woa, d_skip, w_glu, wos, wout, wpg, wpp, fgain) = wts
    wlag, a8r, a8i, _, _, ck = consts

    def row_spec(w):
        return pl.BlockSpec((1, tt, w), lambda i, j: (i, j, 0))

    tab_spec = pl.BlockSpec((tt, LANES), lambda i, j: (j, 0))
    st_spec = pl.BlockSpec((1, PAIRS, LANES), lambda i, j: (i, 0, 0))
    win_spec = pl.BlockSpec((1, WINDOW, KV_WIDTH), lambda i, j: (i, 0, 0))
    consts_in = (gain, w_in, wlag, a8r, a8i, ck, d_skip, w_glu, woa, wos, wout, wpg, wpp, fgain)
    vmem = (sum(a.size * a.dtype.itemsize for a in consts_in)
            + 2 * tt * (2 * D_MODEL + PLE_DIM + 2 * KV_WIDTH + 3 * LANES) * 4
            + 2 * (WINDOW + tt) * KV2_WIDTH * 2 + tt * ATTN_WIDTH * 2 + (tt + LAGS) * SSM_WIDTH * 4
            + (tt // 2) * N_STATE * 2 + 3 * tt * IN_WIDTH * 4)
    y, k, v, hr, hi = pl.pallas_call(
        functools.partial(_layer_kernel, tt=tt),
        grid=(b, t // tt),
        in_specs=[pl.BlockSpec(memory_space=pltpu.SMEM), row_spec(D_MODEL), row_spec(PLE_DIM),
                  tab_spec, tab_spec, tab_spec]
                 + [_const_spec(a.shape) for a in consts_in],
        out_specs=[row_spec(D_MODEL), win_spec, win_spec, st_spec, st_spec],
        out_shape=[jax.ShapeDtypeStruct((b, t, D_MODEL), F32),
                   jax.ShapeDtypeStruct((b, WINDOW, KV_WIDTH), F32),
                   jax.ShapeDtypeStruct((b, WINDOW, KV_WIDTH), F32),
                   jax.ShapeDtypeStruct((b, PAIRS, LANES), F32),
                   jax.ShapeDtypeStruct((b, PAIRS, LANES), F32)],
        scratch_shapes=[pltpu.VMEM((WINDOW + tt, KV2_WIDTH), BF16),
                        pltpu.VMEM((WINDOW + tt, KV2_WIDTH), BF16),
                        pltpu.VMEM((tt, ATTN_WIDTH), BF16),
                        pltpu.VMEM((tt + LAGS, SSM_WIDTH), F32),
                        pltpu.VMEM((PAIRS, SUBLANES, LANES), F32),
                        pltpu.VMEM((PAIRS, SUBLANES, LANES), F32),
                        pltpu.VMEM((tt // 2, N_STATE), BF16)],
        compiler_params=_params(vmem, 2),
        name="layer_prompt",
    )(sinks, x, p, *tabs, *consts_in)
    return y, k, v, hr, hi


def _sample_kernel(sinks_ref, x_ref, p_ref, cos_ref, sina_ref, sinb_ref, kpre_ref, vpre_ref,
                   h0r_ref, h0i_ref, gain_ref, w_in_ref, wlag_ref, a8r_ref, a8i_ref, pr_ref, pi_ref,
                   ck_ref, d_ref, wglu_ref, woa_ref, wos_ref, wout_ref, wpg_ref, wpp_ref, fg_ref,
                   y_ref, k_ref, v_ref, hr_ref, hi_ref,
                   kbuf, vbuf, xa_s, ubuf, hs, *, n, t):
    rows = n * t
    pj, so = {}, {}

    def store(name):
        if name == "u":
            ubuf[0:LAGS, :] = jnp.zeros((LAGS, SSM_WIDTH), F32)
            ubuf[LAGS:, :] = pj["u"]
            return
        full_ref, pre_ref, buf = (k_ref, kpre_ref, kbuf) if name == "k" else (v_ref, vpre_ref, vbuf)
        full_ref[...] = pj[name]
        for s in range(n):
            buf[s, 0:WINDOW, :] = pre_ref[s]
            for j in range(N_KV_HEADS):
                buf[s, WINDOW:WINDOW + t, j * LANES:(j + 1) * LANES] = pj[name + "2"][j][s * t:(s + 1) * t]

    def get_q(c, head):
        return pj["qh"][head][c * CHUNK:(c + 1) * CHUNK]

    def get_kv(c, kv):
        cols = slice(kv * LANES, (kv + 1) * LANES)
        return kbuf[c, :, cols], vbuf[c, :, cols]

    def emit(c, tile, o):
        r, cols = slice(c * CHUNK, (c + 1) * CHUNK), slice(tile * LANES, (tile + 1) * LANES)
        xa_s[r, cols] = (o * pj["sa"][r, cols]).astype(BF16)

    def glu():
        so["xs"] = _glu(so.pop("y"), pj["u"], pj["sz"], d_ref[...], wglu_ref)

    src = dict(xa=lambda: xa_s[...], xs=lambda: so["xs"], ga=lambda: pj["ga"], gs=lambda: pj["gs"],
               x=lambda: x_ref[...], p=lambda: p_ref[...])

    def emit_y(y):
        y_ref[...] = y

    _run(_proj_steps(lambda: x_ref[...], gain_ref[...],
                     lambda: (cos_ref[...], sina_ref[...], sinb_ref[...]), w_in_ref, pj, store))
    _run(_attn_steps(sinks_ref, range(n), get_q, get_kv, lambda c: None, emit))
    _run(_ssm_steps(ubuf, 0, rows, None, None, hs, wlag_ref, a8r_ref, a8i_ref, ck_ref, hr_ref, hi_ref, so,
                    seg=(t, h0r_ref, h0i_ref, pr_ref, pi_ref)) + [glu])
    _run(_out_steps(src, woa_ref, wos_ref, wout_ref, wpg_ref, wpp_ref, fg_ref[...], emit_y))


def _layer_sample(x, p, tabs, k_prefix, v_prefix, h0r, h0i, wts, consts):
    n, t, _ = x.shape
    assert t == CHUNK and tabs[0].shape[0] == n * t
    rows = n * t
    (gain, w_in, sinks, woa, d_skip, w_glu, wos, wout, wpg, wpp, fgain) = wts
    wlag, a8r, a8i, pr, pi, ck = consts
    operands = (x.reshape(rows, D_MODEL), p.reshape(rows, PLE_DIM), *tabs,
                _both_halves(k_prefix), _both_halves(v_prefix), h0r, h0i,
                gain, w_in, wlag, a8r, a8i, pr, pi, ck, d_skip, w_glu, woa, wos, wout, wpg, wpp, fgain)
    out_shapes = [(rows, D_MODEL), (rows, KV_WIDTH), (rows, KV_WIDTH), h0r.shape, h0r.shape]
    scratch = [((n, KEYS, KV2_WIDTH), BF16), ((n, KEYS, KV2_WIDTH), BF16), ((rows, ATTN_WIDTH), BF16),
               ((rows + LAGS, SSM_WIDTH), F32), ((rows, N_STATE), BF16)]
    vmem = (sum(a.size * a.dtype.itemsize for a in operands)
            + sum(int(np.prod(s)) * 4 for s in out_shapes)
            + sum(int(np.prod(s)) * np.dtype(d).itemsize for s, d in scratch)
            + 3 * rows * IN_WIDTH * 4)
    y, k, v, hr, hi = pl.pallas_call(
        functools.partial(_sample_kernel, n=n, t=t),
        grid=(1,),
        in_specs=[pl.BlockSpec(memory_space=pltpu.SMEM)] + [_const_spec(a.shape) for a in operands],
        out_specs=[pl.BlockSpec(s, lambda i, nd=len(s): (0,) * nd) for s in out_shapes],
        out_shape=[jax.ShapeDtypeStruct(s, F32) for s in out_shapes],
        scratch_shapes=[pltpu.VMEM(s, d) for s, d in scratch],
        compiler_params=_params(vmem, 1),
        name="layer_sample",
    )(sinks, *operands)
    k_new = jnp.concatenate([k_prefix[:, t:], k.reshape(n, t, KV_WIDTH)], axis=1)
    v_new = jnp.concatenate([v_prefix[:, t:], v.reshape(n, t, KV_WIDTH)], axis=1)
    return y.reshape(n, t, D_MODEL), k_new, v_new, hr, hi


def _both_halves(a):
    h0, h1 = a[..., :HEAD_DIM], a[..., HEAD_DIM:]
    return jnp.concatenate([h0, h0, h1, h1], axis=-1).astype(BF16)


def _ssm_constants(a_re, a_im, log_dt, b_re, b_im, c_re, c_im):
    dt = jnp.exp(log_dt.astype(F32))[:, None]
    lr = a_re.astype(F32).reshape(PAIRS, 1, 1, LANES)
    li = a_im.astype(F32).reshape(PAIRS, 1, 1, LANES)
    xr = (a_re.astype(F32) * dt).reshape(PAIRS, 1, 1, LANES)
    xi = (a_im.astype(F32) * dt).reshape(PAIRS, 1, 1, LANES)

    def apow(n):
        mag = jnp.exp(xr * n)
        return mag * jnp.cos(xi * n), mag * jnp.sin(xi * n)

    ar, ai = apow(1.0)
    nr, ni = ar - 1.0, ai
    den = lr * lr + li * li
    fr, fi = (nr * lr + ni * li) / den, (ni * lr - nr * li) / den

    n_slots = PAIR_K // SLOT
    qq, hi_ = np.arange(PAIRS)[:, None], np.arange(n_slots)[None, :]
    lag_tab = (PAIRS_PER_TILE * (hi_ // PAIRS_PER_TILE) + (hi_ % PAIRS_PER_TILE - qq) % PAIRS_PER_TILE)
    lag_tab = lag_tab.astype(np.float32)[:, :, None, None]
    same_group = (np.arange(SLOT)[:, None] // SSM_GROUP == np.arange(LANES)[None, :] // SSM_STATE)
    same_group = same_group.astype(np.float32)

    def b_rows(bm):
        t = jnp.transpose(bm.astype(F32).reshape(PAIRS, 2, SSM_STATE, SSM_GROUP), (0, 3, 1, 2))
        t = t.reshape(PAIRS, 1, 1, SSM_GROUP, LANES)
        t = jnp.broadcast_to(t, (PAIRS, 1, 2, SSM_GROUP, LANES))
        return t.reshape(PAIRS, 1, SLOT, LANES) * same_group

    br, bi = b_rows(b_re), b_rows(b_im)
    bbr, bbi = fr * br - fi * bi, fr * bi + fi * br
    er, ei = apow(lag_tab)
    wlag = jnp.concatenate([(er * bbr - ei * bbi).reshape(PAIRS, PAIR_K, LANES),
                            (er * bbi + ei * bbr).reshape(PAIRS, PAIR_K, LANES)], axis=-1).astype(BF16)

    a8r, a8i = (a.reshape(PAIRS, LANES) for a in apow(float(LAGS)))
    pwr, pwi = (a.reshape(PAIRS, LAGS, LANES)
                for a in apow(np.arange(1, LAGS + 1, dtype=np.float32)[None, :, None, None]))

    def c_cols(c):
        t = jnp.transpose(c.astype(F32).reshape(U_TILES, LANES // SSM_GROUP, SSM_GROUP, SSM_STATE),
                          (0, 3, 1, 2))
        return t.reshape(U_TILES, 1, 1, 1, SSM_STATE, LANES)

    cols_group = np.arange(LANES) // SSM_GROUP
    rows_group = 2 * np.arange(PAIRS_PER_TILE)[:, None] + np.arange(2)[None, :]
    c_mask = (rows_group[:, None, :, None, None] == cols_group[None, None, None, None, :])
    c_mask = c_mask.astype(np.float32)[None]
    ck = jnp.concatenate([c_cols(c_re) * c_mask, -c_cols(c_im) * c_mask], axis=2)
    ck = ck.reshape(U_TILES, PAIRS_PER_TILE * PAIR_N, LANES).astype(BF16)
    return wlag, a8r, a8i, pwr, pwi, ck


def _rope_tables(pos0, t, rows):
    half = ROT_DIM // 2
    d = np.arange(LANES) % HEAD_DIM
    inv = jnp.power(ROPE_THETA, -jnp.arange(half, dtype=F32) * 2.0 / ROT_DIM)
    pos = (pos0 + jnp.arange(t)).astype(F32)
    ang = pos[:, None] * inv[None, :]
    cos, sin = (jnp.tile(a, (1, LANES // half)) for a in (jnp.cos(ang), jnp.sin(ang)))
    cos_t = jnp.where((d < ROT_DIM)[None, :], cos, 1.0)
    sina = jnp.where(((d >= half) & (d < ROT_DIM))[None, :], sin, 0.0)
    sinb = jnp.where((d < half)[None, :], -sin, 0.0)
    reps = (max(rows // t, 1), 1)
    return tuple(jnp.tile(a, reps) for a in (cos_t, sina, sinb))


def kernel(x_prompt, x_sample, p_prompt, p_sample, cache_attn_k, cache_attn_v, state_ssm_re,
           state_ssm_im, norm_gain, w_in, attn_sinks, w_o_attn, ssm_a_re, ssm_a_im, ssm_log_dt,
           ssm_b_re, ssm_b_im, ssm_c_re, ssm_c_im, ssm_d, ssm_w_glu, w_o_ssm, w_out,
           w_ple_gate, w_ple_proj, final_norm_gain):
    assert norm_gain.shape[0] == 1, "single-layer model"
    bp, tp, _ = x_prompt.shape
    bs, ts, _ = x_sample.shape
    wts = (norm_gain[0].reshape(1, D_MODEL).astype(F32), w_in[0].astype(BF16),
           attn_sinks[0].astype(F32), w_o_attn[0].astype(BF16),
           ssm_d[0].reshape(1, SSM_WIDTH).astype(F32), ssm_w_glu[0].astype(BF16),
           w_o_ssm[0].astype(BF16), w_out[0].astype(BF16), w_ple_gate[0].astype(BF16),
           w_ple_proj[0].astype(BF16), final_norm_gain.reshape(1, D_MODEL).astype(F32))
    consts = _ssm_constants(ssm_a_re[0], ssm_a_im[0], ssm_log_dt[0], ssm_b_re[0], ssm_b_im[0],
                            ssm_c_re[0], ssm_c_im[0])

    y_p, k_p, v_p, hr_p, hi_p = _layer_fused(x_prompt, p_prompt[0], _rope_tables(0, tp, tp), wts, consts)

    ck = cache_attn_k[0].reshape(bs, WINDOW, KV_WIDTH).astype(F32)
    cv = cache_attn_v[0].reshape(bs, WINDOW, KV_WIDTH).astype(F32)
    h0r = state_ssm_re[0].reshape(bs, PAIRS, LANES).astype(F32)
    h0i = state_ssm_im[0].reshape(bs, PAIRS, LANES).astype(F32)
    tabs_s = _rope_tables(PAST_LEN, ts, bs * ts)
    y_s, k_s, v_s, hr_s, hi_s = _layer_sample(x_sample, p_sample[0], tabs_s, ck, cv, h0r, h0i, wts, consts)

    def kv_out(a, b):
        return a.reshape(1, b, WINDOW, N_KV_HEADS, HEAD_DIM)

    def st_out(a, b):
        return a.reshape(1, b, SSM_GROUPS, SSM_STATE)

    return (y_p, y_s, kv_out(k_p, bp), kv_out(v_p, bp),
            st_out(hr_p, bp), st_out(hi_p, bp), kv_out(k_s, bs), kv_out(v_s, bs),
            st_out(hr_s, bs), st_out(hi_s, bs))
```

```python
import functools

import numpy as np
import jax
import jax.numpy as jnp
from jax import lax
from jax.experimental import pallas as pl
from jax.experimental.pallas import tpu as pltpu

F32 = jnp.float32
BF16 = jnp.bfloat16

LANES = 128
SUBLANES = 8
V7X_VMEM_BYTES = 64 * 1024 * 1024

D_MODEL = 1024
CHUNK = 64
WINDOW = 128
N_HEADS = 8
N_KV_HEADS = 2
HEAD_DIM = 64
Q_PER_KV = N_HEADS // N_KV_HEADS
LOG2E = 1.4426950408889634
Q_SCALE = HEAD_DIM ** -0.5 * LOG2E
ATTN_WIDTH = N_HEADS * HEAD_DIM
KV_WIDTH = N_KV_HEADS * HEAD_DIM
ROT_DIM = HEAD_DIM // 4
ROPE_THETA = 500000.0
SSM_WIDTH = D_MODEL // 2
SSM_GROUP = 16
SSM_GROUPS = SSM_WIDTH // SSM_GROUP
SSM_STATE = 64
PLE_DIM = 256
PAST_LEN = 1024
EPS = 1e-6

O_Q = 0
O_K = O_Q + ATTN_WIDTH
O_V = O_K + KV_WIDTH
O_ZA = O_V + KV_WIDTH
O_U = O_ZA + ATTN_WIDTH
O_ZS = O_U + SSM_WIDTH
O_GA = O_ZS + SSM_WIDTH
O_GS = O_GA + D_MODEL
IN_WIDTH = O_GS + D_MODEL

QM_WIDTH = N_HEADS * LANES
KV2_WIDTH = N_KV_HEADS * LANES
KEYS = WINDOW + CHUNK
LAGS = SUBLANES
PAIRS = SSM_GROUPS // 2
PAIR_K = 2 * LAGS * SSM_GROUP
PAIR_N = 2 * 2 * SSM_STATE
N_STATE = PAIRS * PAIR_N
U_TILES = SSM_WIDTH // LANES
PAIRS_PER_TILE = PAIRS // U_TILES
SLOT = 2 * SSM_GROUP
BF16_ROWS = 2 * SUBLANES
assert PAIRS_PER_TILE == 4 and LAGS == 2 * PAIRS_PER_TILE

LAYER_ROWS = 512


def _sigmoid(x):
    return 1.0 / (1.0 + jnp.exp2(x * (-LOG2E)))


def _const_spec(shape):
    zeros = (0,) * len(shape)
    return pl.BlockSpec(shape, lambda *_: zeros, pipeline_mode=pl.Buffered(1))


def _params(vmem_bytes, n_grid):
    return pltpu.CompilerParams(
        dimension_semantics=("arbitrary",) * n_grid,
        vmem_limit_bytes=min(int(vmem_bytes), V7X_VMEM_BYTES - 8 * 1024 * 1024),
    )


def _run(steps):
    for step in steps:
        step()


def _spread(main, other):
    merged, j = [], 0
    for i, step in enumerate(main):
        while j < len(other) and j * len(main) <= i * len(other):
            merged.append(other[j])
            j += 1
        merged.append(step)
    return merged + other[j:]


def _proj_steps(get_x, gain, get_tabs, w_ref, o, store=None):
    st = {}

    def norm():
        x = get_x()
        ms = jnp.mean(x * x, axis=-1, keepdims=True)
        st["xn"] = (x * lax.rsqrt(ms + EPS) * gain).astype(BF16)
        st["lo"] = lax.broadcasted_iota(jnp.int32, (x.shape[0], LANES), 1) < HEAD_DIM

    def seg(a, b):
        return jnp.dot(st["xn"], w_ref[:, a:b], preferred_element_type=F32)

    def rope(t):
        cos, sina, sinb = get_tabs()
        return (t * cos + pltpu.roll(t, ROT_DIM // 2, 1) * sina
                + pltpu.roll(t, LANES - ROT_DIM // 2, 1) * sinb)

    def both_halves(t):
        tr = pltpu.roll(t, HEAD_DIM, 1)
        return [jnp.where(st["lo"], t, tr).astype(BF16), jnp.where(st["lo"], tr, t).astype(BF16)]

    def done(name):
        if store is not None:
            store(name)

    def q():
        zq = seg(O_Q, O_K)
        o["qh"] = []
        for j in range(ATTN_WIDTH // LANES):
            qt = rope(zq[:, j * LANES:(j + 1) * LANES]) * Q_SCALE
            o["qh"] += [jnp.where(st["lo"], qt, 0.0).astype(BF16),
                        jnp.where(st["lo"], 0.0, qt).astype(BF16)]

    def kv():
        z = seg(O_K, O_ZA)
        o["k"] = rope(z[:, :KV_WIDTH])
        o["k2"] = both_halves(o["k"])
        done("k")
        o["v"] = z[:, KV_WIDTH:]
        o["v2"] = both_halves(o["v"])
        done("v")

    def za():
        z = seg(O_ZA, O_U)
        o["sa"] = z * _sigmoid(z)

    def u():
        o["u"] = seg(O_U, O_ZS)
        done("u")

    def zs():
        z = seg(O_ZS, O_GA)
        o["sz"] = z * _sigmoid(z)

    def ga():
        o["ga"] = _sigmoid(seg(O_GA, O_GS))

    def gs():
        o["gs"] = _sigmoid(seg(O_GS, IN_WIDTH))

    return [norm, q, kv, za, u, zs, ga, gs]


def _attn_steps(sinks_ref, chunks, get_q, get_kv, get_valid, emit):
    nt = (((1,), (1,)), ((), ()))
    units = [(c, kv) for c in chunks for kv in range(N_KV_HEADS)]
    n = len(units)
    st = {}

    def scores(c, kv):
        k2, v2 = get_kv(c, kv)
        qm = jnp.concatenate([get_q(c, kv * Q_PER_KV + h) for h in range(Q_PER_KV)], axis=0)
        s = lax.dot_general(qm, k2, nt, preferred_element_type=F32)
        valid = get_valid(c)
        if valid is not None:
            s = jnp.where(valid, s, -jnp.inf)
        return s, v2

    def softmax(s, kv):
        head_row = lax.broadcasted_iota(jnp.int32, (Q_PER_KV * CHUNK, 1), 0) // CHUNK
        sk = [sinks_ref[kv * Q_PER_KV + h] * LOG2E for h in range(Q_PER_KV)]
        sink = jnp.where(head_row == 0, sk[0],
                         jnp.where(head_row == 1, sk[1], jnp.where(head_row == 2, sk[2], sk[3])))
        m = jnp.maximum(jnp.max(s, axis=1, keepdims=True), sink)
        return jnp.exp2(s - m).astype(BF16), jnp.exp2(sink - m)

    def output(e, sink_term, v2, c, kv):
        lo_q = lax.broadcasted_iota(jnp.int32, (CHUNK, LANES), 1) < HEAD_DIM
        ones = jnp.ones((KEYS, LANES), BF16)
        pv = jnp.dot(e, jnp.concatenate([v2, ones], axis=1), preferred_element_type=F32)
        o = pv[:, :LANES] / (pv[:, LANES:] + sink_term)
        for j in range(Q_PER_KV // 2):
            even = o[2 * j * CHUNK:(2 * j + 1) * CHUNK]
            odd = o[(2 * j + 1) * CHUNK:(2 * j + 2) * CHUNK]
            emit(c, kv * (Q_PER_KV // 2) + j, jnp.where(lo_q, even, odd))

    def make(i):
        def step():
            if i < n:
                st[i] = scores(*units[i])
            if 0 <= i - 1 < n:
                s, v2 = st[i - 1]
                st[i - 1] = softmax(s, units[i - 1][1]) + (v2,)
            if 0 <= i - 2 < n:
                e, den, v2 = st.pop(i - 2)
                output(e, den, v2, *units[i - 2])
        return step

    return [make(i) for i in range(n + 2)]


def _ssm_reset(ubuf, cr_s, ci_s):
    ubuf[0:LAGS, :] = jnp.zeros((LAGS, SSM_WIDTH), F32)
    cr_s[...] = jnp.zeros(cr_s.shape, F32)
    ci_s[...] = jnp.zeros(ci_s.shape, F32)


def _ssm_steps(ubuf, row0, tt, cr_s, ci_s, hs, wlag_ref, a8r_ref, a8i_ref, ck_ref, hr_ref, hi_ref, o,
               seg=None):
    st = {"ys": []}

    def setup():
        if seg is not None:
            st["row_in_seg"] = lax.broadcasted_iota(jnp.int32, (tt, LANES), 0) % seg[0]
        slot = lax.broadcasted_iota(jnp.int32, (tt, LANES), 1) // SLOT
        st["to_low"] = [((slot + PAIRS_PER_TILE - s) % PAIRS_PER_TILE) < 2 for s in range(2)]
        st["same_parity"] = [((slot + sg) % 2) == 0 for sg in range(2)]

    def route(r):
        low = [jnp.where(st["to_low"][s], r[s], r[s + 2]) for s in range(2)]
        high = [jnp.where(st["to_low"][s], r[s + 2], r[s]) for s in range(2)]
        return [jnp.where(st["same_parity"][sg % 2], src[0], src[1])
                for sg, src in zip(range(PAIRS_PER_TILE), (low, low, high, high))]

    def lag_copies(k):
        def step():
            ub = ubuf[row0:row0 + LAGS + tt, k * LANES:(k + 1) * LANES]
            rolled = []
            for s in range(LAGS):
                us = ub[LAGS:] if s == 0 else pltpu.roll(ub, s, 0)[LAGS:]
                if seg is not None and s > 0:
                    us = jnp.where(st["row_in_seg"] >= s, us, 0.0)
                if s % PAIRS_PER_TILE:
                    us = pltpu.roll(us, SLOT * (s % PAIRS_PER_TILE), 1)
                rolled.append(us)
            st["halves"] = (route(rolled[:PAIRS_PER_TILE]), route(rolled[PAIRS_PER_TILE:]))
        return step

    def pair(k, sg):
        def step():
            q = k * PAIRS_PER_TILE + sg
            xl = jnp.concatenate([st["halves"][0][sg], st["halves"][1][sg]], axis=1).astype(BF16)
            w = jnp.dot(xl, wlag_ref[q], preferred_element_type=F32)
            ar, ai = a8r_ref[q:q + 1, :], a8i_ref[q:q + 1, :]
            if seg is None:
                cr, ci = cr_s[q], ci_s[q]
            else:
                seg_rows, seq0, h0r_ref, h0i_ref, pr_ref, pi_ref = seg
            for b2 in range(tt // BF16_ROWS):
                hrs, his = [], []
                for b in (2 * b2, 2 * b2 + 1):
                    if seg is not None and (b * SUBLANES) % seg_rows == 0:
                        n = seq0 + b * SUBLANES // seg_rows
                        h0r, h0i = h0r_ref[n, q:q + 1, :], h0i_ref[n, q:q + 1, :]
                        cr = pr_ref[q] * h0r - pi_ref[q] * h0i
                        ci = pr_ref[q] * h0i + pi_ref[q] * h0r
                    blk = slice(b * SUBLANES, (b + 1) * SUBLANES)
                    hr = w[blk, :LANES] + cr
                    hi = w[blk, LANES:] + ci
                    cr = ar * hr - ai * hi
                    ci = ar * hi + ai * hr
                    hrs.append(hr)
                    his.append(hi)
                    if seg is not None and ((b + 1) * SUBLANES) % seg_rows == 0:
                        n = seq0 + b * SUBLANES // seg_rows
                        hr_ref[n, q:q + 1, :] = hr[SUBLANES - 1:, :]
                        hi_ref[n, q:q + 1, :] = hi[SUBLANES - 1:, :]
                blk2 = slice(b2 * BF16_ROWS, (b2 + 1) * BF16_ROWS)
                hs[blk2, q * PAIR_N:q * PAIR_N + LANES] = jnp.concatenate(hrs, axis=0).astype(BF16)
                hs[blk2, q * PAIR_N + LANES:(q + 1) * PAIR_N] = jnp.concatenate(his, axis=0).astype(BF16)
            if seg is None:
                cr_s[q] = cr
                ci_s[q] = ci
                hr_ref[0, q:q + 1, :] = hr[SUBLANES - 1:, :]
                hi_ref[0, q:q + 1, :] = hi[SUBLANES - 1:, :]
        return step

    def c_proj(k):
        def step():
            cols = slice(k * PAIRS_PER_TILE * PAIR_N, (k + 1) * PAIRS_PER_TILE * PAIR_N)
            st["ys"].append(jnp.dot(hs[0:tt, cols], ck_ref[k], preferred_element_type=F32))
            if k == U_TILES - 1:
                o["y"] = jnp.concatenate(st["ys"], axis=1)
        return step

    steps = [setup]
    for k in range(U_TILES):
        steps += [lag_copies(k)] + [pair(k, sg) for sg in range(PAIRS_PER_TILE)] + [c_proj(k)]
    return steps


def _glu(y, u, sz, d, wglu_ref):
    z = jax.nn.gelu(y + d * u)
    g = jnp.dot(z.astype(BF16), wglu_ref[...], preferred_element_type=F32)
    return z * _sigmoid(g) * sz


def _out_steps(src, woa_ref, wos_ref, wout_ref, wpg_ref, wpp_ref, fgain, emit):
    st = {}

    def mm(a, w_ref):
        return jnp.dot(a.astype(BF16), w_ref[...], preferred_element_type=F32)

    def branches():
        st["merged"] = src["ga"]() * mm(src["xa"](), woa_ref) + src["gs"]() * mm(src["xs"](), wos_ref)

    def residual():
        st["h"] = src["x"]() + mm(st.pop("merged"), wout_ref)

    def embed_gate():
        h = st.pop("h")
        st["h"] = h + _sigmoid(mm(h, wpg_ref)) * mm(src["p"](), wpp_ref)

    def norm():
        h = st.pop("h")
        ms = jnp.mean(h * h, axis=-1, keepdims=True)
        emit(h * lax.rsqrt(ms + EPS) * fgain)

    return [branches, residual, embed_gate, norm]


def _step(t, pre, sinks_ref, x_ref, p_ref, cos_ref, sina_ref, sinb_ref, weights,
          y_ref, k_ref, v_ref, hr_ref, hi_ref, kbuf, vbuf, xa_s, ubuf, cr_s, ci_s, hs, tt):
    (gain_ref, w_in_ref, wlag_ref, a8r_ref, a8i_ref, pr_ref, pi_ref, ck_ref, d_ref, wglu_ref,
     woa_ref, wos_ref, wout_ref, wpg_ref, wpp_ref, fg_ref) = weights
    stacked = pre is not None
    half = tt // 2

    if stacked:
        ubuf[0:LAGS, :] = jnp.zeros((LAGS, SSM_WIDTH), F32)
    else:
        @pl.when(t == 0)
        def _():
            kbuf[0:WINDOW, :] = jnp.zeros((WINDOW, KV2_WIDTH), BF16)
            vbuf[0:WINDOW, :] = jnp.zeros((WINDOW, KV2_WIDTH), BF16)
            _ssm_reset(ubuf, cr_s, ci_s)

    pj, so = [{}, {}], [{}, {}]

    def rows(h):
        return slice(h * half, (h + 1) * half)

    def block(ref, h):
        return ref[rows(h)] if stacked else ref[0, rows(h)]

    def proj(h):
        def store(name):
            if name == "u":
                ubuf[LAGS + h * half:LAGS + (h + 1) * half, :] = pj[h]["u"]
                return
            out_ref, buf = (k_ref, kbuf) if name == "k" else (v_ref, vbuf)
            tiles = pj[h][name + "2"]
            if stacked:
                out_ref[rows(h)] = pj[h][name]
                pre_ref = pre[0] if name == "k" else pre[1]
                for s in range(h * half // CHUNK, (h + 1) * half // CHUNK):
                    r0 = s * CHUNK - h * half
                    buf[s, 0:WINDOW, :] = pre_ref[s]
                    for j in range(N_KV_HEADS):
                        buf[s, WINDOW:KEYS, j * LANES:(j + 1) * LANES] = tiles[j][r0:r0 + CHUNK]
                return
            if h == 1:
                out_ref[0] = pj[h][name][half - WINDOW:]
            for j in range(N_KV_HEADS):
                buf[WINDOW + h * half:WINDOW + (h + 1) * half, j * LANES:(j + 1) * LANES] = tiles[j]

        return _proj_steps(lambda: block(x_ref, h), gain_ref[...],
                           lambda: (cos_ref[rows(h)], sina_ref[rows(h)], sinb_ref[rows(h)]),
                           w_in_ref, pj[h], store)

    def mid(h):
        def get_q(c, head):
            r0 = c * CHUNK - h * half
            return pj[h]["qh"][head][r0:r0 + CHUNK]

        def get_kv(c, kv):
            cols = slice(kv * LANES, (kv + 1) * LANES)
            if stacked:
                return kbuf[c, :, cols], vbuf[c, :, cols]
            krows = slice(c * CHUNK, c * CHUNK + KEYS)
            return kbuf[krows, cols], vbuf[krows, cols]

        def get_valid(c):
            if stacked or c * CHUNK >= WINDOW:
                return None
            in_seq = c * CHUNK + lax.broadcasted_iota(jnp.int32, (1, KEYS), 1) >= WINDOW
            return jnp.logical_or(in_seq, t > 0)

        def emit(c, tile, o):
            r0, cols = c * CHUNK - h * half, slice(tile * LANES, (tile + 1) * LANES)
            xa_s[c * CHUNK:(c + 1) * CHUNK, cols] = (o * pj[h]["sa"][r0:r0 + CHUNK, cols]).astype(BF16)

        def glu():
            so[h]["xs"] = _glu(so[h].pop("y"), pj[h]["u"], pj[h]["sz"], d_ref[...], wglu_ref)

        chunks = range(h * half // CHUNK, (h + 1) * half // CHUNK)
        seg = (CHUNK, chunks[0], pre[2], pre[3], pr_ref, pi_ref) if stacked else None
        return (_attn_steps(sinks_ref, chunks, get_q, get_kv, get_valid, emit)
                + _ssm_steps(ubuf, h * half, half, cr_s, ci_s, hs, wlag_ref, a8r_ref, a8i_ref, ck_ref,
                             hr_ref, hi_ref, so[h], seg)
                + [glu])

    def out(h):
        src = dict(xa=lambda: xa_s[rows(h)], xs=lambda: so[h]["xs"], ga=lambda: pj[h]["ga"],
                   gs=lambda: pj[h]["gs"], x=lambda: block(x_ref, h), p=lambda: block(p_ref, h))

        def emit(y):
            if stacked:
                y_ref[rows(h)] = y
            else:
                y_ref[0, rows(h)] = y

        return _out_steps(src, woa_ref, wos_ref, wout_ref, wpg_ref, wpp_ref, fg_ref[...], emit)

    _run(proj(0))
    _run(_spread(mid(0), proj(1)))
    _run(_spread(mid(1), out(0)))
    _run(out(1))

    if not stacked:
        kbuf[0:WINDOW, :] = kbuf[tt:tt + WINDOW, :]
        vbuf[0:WINDOW, :] = vbuf[tt:tt + WINDOW, :]
        ubuf[0:LAGS, :] = ubuf[tt:tt + LAGS, :]


N_ROW_IN, N_SAMPLE_PRE, N_WEIGHTS = 5, 4, 16


def _layer_kernel(sinks_ref, *refs, tt, tiles_per_seq):
    prompt_in, refs = refs[:N_ROW_IN], refs[N_ROW_IN:]
    sample_in, refs = refs[:N_ROW_IN], refs[N_ROW_IN:]
    sample_pre, refs = refs[:N_SAMPLE_PRE], refs[N_SAMPLE_PRE:]
    weights, refs = refs[:N_WEIGHTS], refs[N_WEIGHTS:]
    prompt_out, sample_out, refs = refs[:5], refs[5:10], refs[10:]
    kbuf, vbuf, kbuf_s, vbuf_s, xa_s, ubuf, cr_s, ci_s, hs = refs
    i = pl.program_id(0)

    @pl.when(i == 0)
    def _():
        _step(None, sample_pre, sinks_ref, *sample_in, weights, *sample_out,
              kbuf_s, vbuf_s, xa_s, ubuf, None, None, hs, tt)

    @pl.when(i > 0)
    def _():
        _step((i - 1) % tiles_per_seq, None, sinks_ref, *prompt_in, weights, *prompt_out,
              kbuf, vbuf, xa_s, ubuf, cr_s, ci_s, hs, tt)


def _layer(x, p, tabs, x_s, p_s, tabs_s, k_prefix, v_prefix, h0r, h0i, wts, consts):
    b, t, _ = x.shape
    n_s, t_s, _ = x_s.shape
    tt = min(LAYER_ROWS, t)
    rows_s = n_s * t_s
    assert t % tt == 0 and tt // 2 >= WINDOW and (tt // 2) % BF16_ROWS == 0 and tabs[0].shape[0] == t
    assert t_s == CHUNK and rows_s == tt and tabs_s[0].shape[0] == rows_s
    (gain, w_in, sinks, woa, d_skip, w_glu, wos, wout, wpg, wpp, fgain) = wts
    wlag, a8r, a8i, pr, pi, ck = consts
    tps = t // tt

    def tile(i):
        return jnp.maximum(i - 1, 0)

    def row_spec(w):
        return pl.BlockSpec((1, tt, w), lambda i: (tile(i) // tps, tile(i) % tps, 0))

    def seq_spec(rows, w):
        return pl.BlockSpec((1, rows, w), lambda i: (tile(i) // tps, 0, 0))

    tab_spec = pl.BlockSpec((tt, LANES), lambda i: (tile(i) % tps, 0))
    prompt_in = (x, p, *tabs)
    sample_in = (x_s.reshape(rows_s, D_MODEL), p_s.reshape(rows_s, PLE_DIM), *tabs_s,
                 _both_halves(k_prefix), _both_halves(v_prefix), h0r, h0i)
    weights = (gain, w_in, wlag, a8r, a8i, pr, pi, ck, d_skip, w_glu, woa, wos, wout, wpg, wpp, fgain)
    assert (len(prompt_in), len(sample_in), len(weights)) == (N_ROW_IN, N_ROW_IN + N_SAMPLE_PRE, N_WEIGHTS)
    sample_out = [(rows_s, D_MODEL), (rows_s, KV_WIDTH), (rows_s, KV_WIDTH), h0r.shape, h0r.shape]
    scratch = [((WINDOW + tt, KV2_WIDTH), BF16), ((WINDOW + tt, KV2_WIDTH), BF16),
               ((n_s, KEYS, KV2_WIDTH), BF16), ((n_s, KEYS, KV2_WIDTH), BF16),
               ((tt, ATTN_WIDTH), BF16), ((tt + LAGS, SSM_WIDTH), F32),
               ((PAIRS, SUBLANES, LANES), F32), ((PAIRS, SUBLANES, LANES), F32),
               ((tt // 2, N_STATE), BF16)]
    vmem = (sum(a.size * a.dtype.itemsize for a in sample_in + weights)
            + 2 * tt * (2 * D_MODEL + PLE_DIM + 3 * LANES) * 4
            + sum(int(np.prod(s)) * 4 for s in sample_out)
            + sum(int(np.prod(s)) * np.dtype(d).itemsize for s, d in scratch)
            + 3 * tt * IN_WIDTH * 4)
    outs = pl.pallas_call(
        functools.partial(_layer_kernel, tt=tt, tiles_per_seq=tps),
        grid=(b * tps + 1,),
        in_specs=[pl.BlockSpec(memory_space=pltpu.SMEM), row_spec(D_MODEL), row_spec(PLE_DIM),
                  tab_spec, tab_spec, tab_spec]
                 + [_const_spec(a.shape) for a in sample_in + weights],
        out_specs=[row_spec(D_MODEL), seq_spec(WINDOW, KV_WIDTH), seq_spec(WINDOW, KV_WIDTH),
                   seq_spec(PAIRS, LANES), seq_spec(PAIRS, LANES)] + [_const_spec(s) for s in sample_out],
        out_shape=[jax.ShapeDtypeStruct((b, t, D_MODEL), F32),
                   jax.ShapeDtypeStruct((b, WINDOW, KV_WIDTH), F32),
                   jax.ShapeDtypeStruct((b, WINDOW, KV_WIDTH), F32),
                   jax.ShapeDtypeStruct((b, PAIRS, LANES), F32),
                   jax.ShapeDtypeStruct((b, PAIRS, LANES), F32)]
                  + [jax.ShapeDtypeStruct(s, F32) for s in sample_out],
        scratch_shapes=[pltpu.VMEM(s, d) for s, d in scratch],
        compiler_params=_params(vmem, 1),
        name="layer",
    )(sinks, *prompt_in, *sample_in, *weights)
    y, k, v, hr, hi, y_s, k_s, v_s, hr_s, hi_s = outs
    k_new = jnp.concatenate([k_prefix[:, t_s:], k_s.reshape(n_s, t_s, KV_WIDTH)], axis=1)
    v_new = jnp.concatenate([v_prefix[:, t_s:], v_s.reshape(n_s, t_s, KV_WIDTH)], axis=1)
    return (y, k, v, hr, hi), (y_s.reshape(n_s, t_s, D_MODEL), k_new, v_new, hr_s, hi_s)


def _both_halves(a):
    h0, h1 = a[..., :HEAD_DIM], a[..., HEAD_DIM:]
    return jnp.concatenate([h0, h0, h1, h1], axis=-1).astype(BF16)


def _ssm_constants(a_re, a_im, log_dt, b_re, b_im, c_re, c_im):
    dt = jnp.exp(log_dt.astype(F32))[:, None]
    lr = a_re.astype(F32).reshape(PAIRS, 1, 1, LANES)
    li = a_im.astype(F32).reshape(PAIRS, 1, 1, LANES)
    xr = (a_re.astype(F32) * dt).reshape(PAIRS, 1, 1, LANES)
    xi = (a_im.astype(F32) * dt).reshape(PAIRS, 1, 1, LANES)

    def apow(n):
        mag = jnp.exp(xr * n)
        return mag * jnp.cos(xi * n), mag * jnp.sin(xi * n)

    ar, ai = apow(1.0)
    nr, ni = ar - 1.0, ai
    den = lr * lr + li * li
    fr, fi = (nr * lr + ni * li) / den, (ni * lr - nr * li) / den

    n_slots = PAIR_K // SLOT
    qq, hi_ = np.arange(PAIRS)[:, None], np.arange(n_slots)[None, :]
    lag_tab = (PAIRS_PER_TILE * (hi_ // PAIRS_PER_TILE) + (hi_ % PAIRS_PER_TILE - qq) % PAIRS_PER_TILE)
    lag_tab = lag_tab.astype(np.float32)[:, :, None, None]
    same_group = (np.arange(SLOT)[:, None] // SSM_GROUP == np.arange(LANES)[None, :] // SSM_STATE)
    same_group = same_group.astype(np.float32)

    def b_rows(bm):
        t = jnp.transpose(bm.astype(F32).reshape(PAIRS, 2, SSM_STATE, SSM_GROUP), (0, 3, 1, 2))
        t = t.reshape(PAIRS, 1, 1, SSM_GROUP, LANES)
        t = jnp.broadcast_to(t, (PAIRS, 1, 2, SSM_GROUP, LANES))
        return t.reshape(PAIRS, 1, SLOT, LANES) * same_group

    br, bi = b_rows(b_re), b_rows(b_im)
    bbr, bbi = fr * br - fi * bi, fr * bi + fi * br
    er, ei = apow(lag_tab)
    wlag = jnp.concatenate([(er * bbr - ei * bbi).reshape(PAIRS, PAIR_K, LANES),
                            (er * bbi + ei * bbr).reshape(PAIRS, PAIR_K, LANES)], axis=-1).astype(BF16)

    a8r, a8i = (a.reshape(PAIRS, LANES) for a in apow(float(LAGS)))
    pwr, pwi = (a.reshape(PAIRS, LAGS, LANES)
                for a in apow(np.arange(1, LAGS + 1, dtype=np.float32)[None, :, None, None]))

    def c_cols(c):
        t = jnp.transpose(c.astype(F32).reshape(U_TILES, LANES // SSM_GROUP, SSM_GROUP, SSM_STATE),
                          (0, 3, 1, 2))
        return t.reshape(U_TILES, 1, 1, 1, SSM_STATE, LANES)

    cols_group = np.arange(LANES) // SSM_GROUP
    rows_group = 2 * np.arange(PAIRS_PER_TILE)[:, None] + np.arange(2)[None, :]
    c_mask = (rows_group[:, None, :, None, None] == cols_group[None, None, None, None, :])
    c_mask = c_mask.astype(np.float32)[None]
    ck = jnp.concatenate([c_cols(c_re) * c_mask, -c_cols(c_im) * c_mask], axis=2)
    ck = ck.reshape(U_TILES, PAIRS_PER_TILE * PAIR_N, LANES).astype(BF16)
    return wlag, a8r, a8i, pwr, pwi, ck


def _rope_tables(pos0, t, rows):
    half = ROT_DIM // 2
    d = np.arange(LANES) % HEAD_DIM
    inv = jnp.power(ROPE_THETA, -jnp.arange(half, dtype=F32) * 2.0 / ROT_DIM)
    pos = (pos0 + jnp.arange(t)).astype(F32)
    ang = pos[:, None] * inv[None, :]
    cos, sin = (jnp.tile(a, (1, LANES // half)) for a in (jnp.cos(ang), jnp.sin(ang)))
    cos_t = jnp.where((d < ROT_DIM)[None, :], cos, 1.0)
    sina = jnp.where(((d >= half) & (d < ROT_DIM))[None, :], sin, 0.0)
    sinb = jnp.where((d < half)[None, :], -sin, 0.0)
    reps = (max(rows // t, 1), 1)
    return tuple(jnp.tile(a, reps) for a in (cos_t, sina, sinb))


def kernel(x_prompt, x_sample, p_prompt, p_sample, cache_attn_k, cache_attn_v, state_ssm_re,
           state_ssm_im, norm_gain, w_in, attn_sinks, w_o_attn, ssm_a_re, ssm_a_im, ssm_log_dt,
           ssm_b_re, ssm_b_im, ssm_c_re, ssm_c_im, ssm_d, ssm_w_glu, w_o_ssm, w_out,
           w_ple_gate, w_ple_proj, final_norm_gain):
    assert norm_gain.shape[0] == 1, "single-layer model"
    bp, tp, _ = x_prompt.shape
    bs, ts, _ = x_sample.shape
    wts = (norm_gain[0].reshape(1, D_MODEL).astype(F32), w_in[0].astype(BF16),
           attn_sinks[0].astype(F32), w_o_attn[0].astype(BF16),
           ssm_d[0].reshape(1, SSM_WIDTH).astype(F32), ssm_w_glu[0].astype(BF16),
           w_o_ssm[0].astype(BF16), w_out[0].astype(BF16), w_ple_gate[0].astype(BF16),
           w_ple_proj[0].astype(BF16), final_norm_gain.reshape(1, D_MODEL).astype(F32))
    consts = _ssm_constants(ssm_a_re[0], ssm_a_im[0], ssm_log_dt[0], ssm_b_re[0], ssm_b_im[0],
                            ssm_c_re[0], ssm_c_im[0])

    ck = cache_attn_k[0].reshape(bs, WINDOW, KV_WIDTH).astype(F32)
    cv = cache_attn_v[0].reshape(bs, WINDOW, KV_WIDTH).astype(F32)
    h0r = state_ssm_re[0].reshape(bs, PAIRS, LANES).astype(F32)
    h0i = state_ssm_im[0].reshape(bs, PAIRS, LANES).astype(F32)
    (y_p, k_p, v_p, hr_p, hi_p), (y_s, k_s, v_s, hr_s, hi_s) = _layer(
        x_prompt, p_prompt[0], _rope_tables(0, tp, tp),
        x_sample, p_sample[0], _rope_tables(PAST_LEN, ts, bs * ts), ck, cv, h0r, h0i, wts, consts)

    def kv_out(a, b):
        return a.reshape(1, b, WINDOW, N_KV_HEADS, HEAD_DIM)

    def st_out(a, b):
        return a.reshape(1, b, SSM_GROUPS, SSM_STATE)

    return (y_p, y_s, kv_out(k_p, bp), kv_out(v_p, bp),
            st_out(hr_p, bp), st_out(hi_p, bp), kv_out(k_s, bs), kv_out(v_s, bs),
            st_out(hr_s, bs), st_out(hi_s, bs))
```

```python
import functools

import numpy as np
import jax
import jax.numpy as jnp
from jax import lax
from jax.experimental import pallas as pl
from jax.experimental.pallas import tpu as pltpu

F32 = jnp.float32
BF16 = jnp.bfloat16

LANES = 128
SUBLANES = 8
MATMUL_COLS = 256
V7X_VMEM_BYTES = 64 * 1024 * 1024

D_MODEL = 1024
CHUNK = 64
WINDOW = 128
N_HEADS = 8
N_KV_HEADS = 2
HEAD_DIM = 64
Q_PER_KV = N_HEADS // N_KV_HEADS
LOG2E = 1.4426950408889634
Q_SCALE = HEAD_DIM ** -0.5 * LOG2E
ATTN_WIDTH = N_HEADS * HEAD_DIM
KV_WIDTH = N_KV_HEADS * HEAD_DIM
ROT_DIM = HEAD_DIM // 4
ROPE_THETA = 500000.0
SSM_WIDTH = D_MODEL // 2
SSM_GROUP = 16
SSM_GROUPS = SSM_WIDTH // SSM_GROUP
SSM_STATE = 64
PLE_DIM = 256
PAST_LEN = 1024
EPS = 1e-6

O_Q = 0
O_K = O_Q + ATTN_WIDTH
O_V = O_K + KV_WIDTH
O_ZA = O_V + KV_WIDTH
O_U = O_ZA + ATTN_WIDTH
O_ZS = O_U + SSM_WIDTH
O_GA = O_ZS + SSM_WIDTH
O_GS = O_GA + D_MODEL
IN_WIDTH = O_GS + D_MODEL

QM_WIDTH = N_HEADS * LANES
KV2_WIDTH = N_KV_HEADS * LANES
KEYS = WINDOW + CHUNK
LAGS = SUBLANES
PAIRS = SSM_GROUPS // 2
PAIR_K = 2 * LAGS * SSM_GROUP
PAIR_N = 2 * 2 * SSM_STATE
N_STATE = PAIRS * PAIR_N
U_TILES = SSM_WIDTH // LANES
PAIRS_PER_TILE = PAIRS // U_TILES
SLOT = 2 * SSM_GROUP
BF16_ROWS = 2 * SUBLANES
assert PAIRS_PER_TILE == 4 and LAGS == 2 * PAIRS_PER_TILE

LAYER_ROWS = 512


def _sigmoid(x):
    return 1.0 / (1.0 + jnp.exp2(x * (-LOG2E)))


def _const_spec(shape):
    zeros = (0,) * len(shape)
    return pl.BlockSpec(shape, lambda *_: zeros, pipeline_mode=pl.Buffered(1))


def _params(vmem_bytes, n_grid):
    return pltpu.CompilerParams(
        dimension_semantics=("arbitrary",) * n_grid,
        vmem_limit_bytes=min(int(vmem_bytes), V7X_VMEM_BYTES - 8 * 1024 * 1024),
    )


def _run(steps):
    for step in steps:
        step()


def _spread(main, other):
    merged, j = [], 0
    for i, step in enumerate(main):
        while j < len(other) and j * len(main) <= i * len(other):
            merged.append(other[j])
            j += 1
        merged.append(step)
    return merged + other[j:]


def _proj_steps(get_x, gain, get_tabs, w_ref, o, store=None):
    st = {}

    def norm():
        x = get_x()
        ms = jnp.mean(x * x, axis=-1, keepdims=True)
        st["xn"] = (x * lax.rsqrt(ms + EPS) * gain).astype(BF16)
        st["lo"] = lax.broadcasted_iota(jnp.int32, (x.shape[0], LANES), 1) < HEAD_DIM

    def seg(a, b):
        return jnp.dot(st["xn"], w_ref[:, a:b], preferred_element_type=F32)

    def rope(t):
        cos, sina, sinb = get_tabs()
        return (t * cos + pltpu.roll(t, ROT_DIM // 2, 1) * sina
                + pltpu.roll(t, LANES - ROT_DIM // 2, 1) * sinb)

    def both_halves(t):
        tr = pltpu.roll(t, HEAD_DIM, 1)
        return [jnp.where(st["lo"], t, tr).astype(BF16), jnp.where(st["lo"], tr, t).astype(BF16)]

    def done(name):
        if store is not None:
            store(name)

    def q(a):
        def step():
            zq = seg(a, a + MATMUL_COLS)
            for j in range(MATMUL_COLS // LANES):
                qt = rope(zq[:, j * LANES:(j + 1) * LANES]) * Q_SCALE
                o.setdefault("qh", []).extend([jnp.where(st["lo"], qt, 0.0).astype(BF16),
                                               jnp.where(st["lo"], 0.0, qt).astype(BF16)])
        return step

    def kv():
        z = seg(O_K, O_ZA)
        o["k"] = rope(z[:, :KV_WIDTH])
        o["k2"] = both_halves(o["k"])
        done("k")
        o["v"] = z[:, KV_WIDTH:]
        o["v2"] = both_halves(o["v"])
        done("v")

    def piece(name, a, last, act, notify):
        def step():
            st.setdefault(name, []).append(act(seg(a, a + MATMUL_COLS)))
            if last:
                o[name] = jnp.concatenate(st.pop(name), axis=1)
                if notify:
                    done(name)
        return step

    def pieces(name, a, b, act, notify=False):
        starts = range(a, b, MATMUL_COLS)
        return [piece(name, s, s == starts[-1], act, notify) for s in starts]

    def silu(z):
        return z * _sigmoid(z)

    return ([norm] + [q(a) for a in range(O_Q, O_K, MATMUL_COLS)] + [kv]
            + pieces("sa", O_ZA, O_U, silu) + pieces("u", O_U, O_ZS, lambda z: z, notify=True)
            + pieces("sz", O_ZS, O_GA, silu) + pieces("ga", O_GA, O_GS, _sigmoid)
            + pieces("gs", O_GS, IN_WIDTH, _sigmoid))


def _attn_steps(sinks_ref, chunks, get_q, get_kv, get_valid, emit):
    nt = (((1,), (1,)), ((), ()))
    units = [(c, kv) for c in chunks for kv in range(N_KV_HEADS)]
    n = len(units)
    st = {}

    def scores(c, kv):
        k2, v2 = get_kv(c, kv)
        qm = jnp.concatenate([get_q(c, kv * Q_PER_KV + h) for h in range(Q_PER_KV)], axis=0)
        s = lax.dot_general(qm, k2, nt, preferred_element_type=F32)
        valid = get_valid(c)
        if valid is not None:
            s = jnp.where(valid, s, -jnp.inf)
        return s, v2

    def softmax(s, kv):
        head_row = lax.broadcasted_iota(jnp.int32, (Q_PER_KV * CHUNK, 1), 0) // CHUNK
        sk = [sinks_ref[kv * Q_PER_KV + h] * LOG2E for h in range(Q_PER_KV)]
        sink = jnp.where(head_row == 0, sk[0],
                         jnp.where(head_row == 1, sk[1], jnp.where(head_row == 2, sk[2], sk[3])))
        m = jnp.maximum(jnp.max(s, axis=1, keepdims=True), sink)
        return jnp.exp2(s - m).astype(BF16), jnp.exp2(sink - m)

    def output(e, sink_term, v2, c, kv):
        lo_q = lax.broadcasted_iota(jnp.int32, (CHUNK, LANES), 1) < HEAD_DIM
        ones = jnp.ones((KEYS, LANES), BF16)
        pv = jnp.dot(e, jnp.concatenate([v2, ones], axis=1), preferred_element_type=F32)
        o = pv[:, :LANES] / (pv[:, LANES:] + sink_term)
        for j in range(Q_PER_KV // 2):
            even = o[2 * j * CHUNK:(2 * j + 1) * CHUNK]
            odd = o[(2 * j + 1) * CHUNK:(2 * j + 2) * CHUNK]
            emit(c, kv * (Q_PER_KV // 2) + j, jnp.where(lo_q, even, odd))

    def make(i):
        def step():
            if i < n:
                st[i] = scores(*units[i])
            if 0 <= i - 1 < n:
                s, v2 = st[i - 1]
                st[i - 1] = softmax(s, units[i - 1][1]) + (v2,)
            if 0 <= i - 2 < n:
                e, den, v2 = st.pop(i - 2)
                output(e, den, v2, *units[i - 2])
        return step

    return [make(i) for i in range(n + 2)]


def _ssm_reset(ubuf, cr_s, ci_s):
    ubuf[0:LAGS, :] = jnp.zeros((LAGS, SSM_WIDTH), F32)
    cr_s[...] = jnp.zeros(cr_s.shape, F32)
    ci_s[...] = jnp.zeros(ci_s.shape, F32)


def _ssm_steps(ubuf, row0, tt, cr_s, ci_s, hs, wlag_ref, a8r_ref, a8i_ref, ck_ref, hr_ref, hi_ref, o,
               seg=None):
    st = {"ys": []}

    def setup():
        if seg is not None:
            st["row_in_seg"] = lax.broadcasted_iota(jnp.int32, (tt, LANES), 0) % seg[0]
        slot = lax.broadcasted_iota(jnp.int32, (tt, LANES), 1) // SLOT
        st["to_low"] = [((slot + PAIRS_PER_TILE - s) % PAIRS_PER_TILE) < 2 for s in range(2)]
        st["same_parity"] = [((slot + sg) % 2) == 0 for sg in range(2)]

    def route(r):
        low = [jnp.where(st["to_low"][s], r[s], r[s + 2]) for s in range(2)]
        high = [jnp.where(st["to_low"][s], r[s + 2], r[s]) for s in range(2)]
        return [jnp.where(st["same_parity"][sg % 2], src[0], src[1])
                for sg, src in zip(range(PAIRS_PER_TILE), (low, low, high, high))]

    def lag_copies(k):
        def step():
            ub = ubuf[row0:row0 + LAGS + tt, k * LANES:(k + 1) * LANES]
            rolled = []
            for s in range(LAGS):
                us = ub[LAGS:] if s == 0 else pltpu.roll(ub, s, 0)[LAGS:]
                if seg is not None and s > 0:
                    us = jnp.where(st["row_in_seg"] >= s, us, 0.0)
                if s % PAIRS_PER_TILE:
                    us = pltpu.roll(us, SLOT * (s % PAIRS_PER_TILE), 1)
                rolled.append(us)
            lo4, hi4 = route(rolled[:PAIRS_PER_TILE]), route(rolled[PAIRS_PER_TILE:])
            st["xl", k] = [jnp.concatenate([lo4[sg], hi4[sg]], axis=1).astype(BF16)
                           for sg in range(PAIRS_PER_TILE)]
        return step

    def lag_dot(k, sg):
        def step():
            q = k * PAIRS_PER_TILE + sg
            st["w", q] = jnp.dot(st["xl", k][sg], wlag_ref[q], preferred_element_type=F32)
        return step

    def pair(k, sg):
        def step():
            q = k * PAIRS_PER_TILE + sg
            w = st.pop(("w", q))
            ar, ai = a8r_ref[q:q + 1, :], a8i_ref[q:q + 1, :]
            if seg is None:
                cr, ci = cr_s[q], ci_s[q]
            else:
                seg_rows, h0r_ref, h0i_ref, pr_ref, pi_ref = seg
            for b2 in range(tt // BF16_ROWS):
                hrs, his = [], []
                for b in (2 * b2, 2 * b2 + 1):
                    if seg is not None and (b * SUBLANES) % seg_rows == 0:
                        n = b * SUBLANES // seg_rows
                        h0r, h0i = h0r_ref[n, q:q + 1, :], h0i_ref[n, q:q + 1, :]
                        cr = pr_ref[q] * h0r - pi_ref[q] * h0i
                        ci = pr_ref[q] * h0i + pi_ref[q] * h0r
                    blk = slice(b * SUBLANES, (b + 1) * SUBLANES)
                    hr = w[blk, :LANES] + cr
                    hi = w[blk, LANES:] + ci
                    cr = ar * hr - ai * hi
                    ci = ar * hi + ai * hr
                    hrs.append(hr)
                    his.append(hi)
                    if seg is not None and ((b + 1) * SUBLANES) % seg_rows == 0:
                        n = b * SUBLANES // seg_rows
                        hr_ref[n, q:q + 1, :] = hr[SUBLANES - 1:, :]
                        hi_ref[n, q:q + 1, :] = hi[SUBLANES - 1:, :]
                blk2 = slice(b2 * BF16_ROWS, (b2 + 1) * BF16_ROWS)
                hs[blk2, q * PAIR_N:q * PAIR_N + LANES] = jnp.concatenate(hrs, axis=0).astype(BF16)
                hs[blk2, q * PAIR_N + LANES:(q + 1) * PAIR_N] = jnp.concatenate(his, axis=0).astype(BF16)
            if seg is None:
                cr_s[q] = cr
                ci_s[q] = ci
                hr_ref[0, q:q + 1, :] = hr[SUBLANES - 1:, :]
                hi_ref[0, q:q + 1, :] = hi[SUBLANES - 1:, :]
        return step

    def c_proj(k):
        def step():
            cols = slice(k * PAIRS_PER_TILE * PAIR_N, (k + 1) * PAIRS_PER_TILE * PAIR_N)
            st["ys"].append(jnp.dot(hs[:, cols], ck_ref[k], preferred_element_type=F32))
            if k == U_TILES - 1:
                o["y"] = jnp.concatenate(st["ys"], axis=1)
        return step

    pairs = range(PAIRS_PER_TILE)
    stages = (lambda k: [lag_copies(k)], lambda k: [lag_dot(k, sg) for sg in pairs],
              lambda k: [pair(k, sg) for sg in pairs], lambda k: [c_proj(k)])
    steps = [setup]
    for i in range(U_TILES + len(stages) - 1):
        for d, stage in enumerate(stages):
            if 0 <= i - d < U_TILES:
                steps += stage(i - d)
    return steps


def _glu(y, u, sz, d, wglu_ref):
    z = jax.nn.gelu(y + d * u)
    g = jnp.dot(z.astype(BF16), wglu_ref[...], preferred_element_type=F32)
    return z * _sigmoid(g) * sz


def _out_steps(src, woa_ref, wos_ref, wout_ref, wpg_ref, wpp_ref, fgain, emit):
    st = {}
    col_slices = [slice(c, c + MATMUL_COLS) for c in range(0, D_MODEL, MATMUL_COLS)]

    def mm(a16, w_ref, cols):
        return jnp.dot(a16, w_ref[:, cols], preferred_element_type=F32)

    def collect(name, part, cols):
        st.setdefault(name, []).append(part)
        if cols is col_slices[-1]:
            st[name] = jnp.concatenate(st[name], axis=1)

    def branches(cols):
        def step():
            if cols is col_slices[0]:
                st["xa"], st["xs"] = src["xa"]().astype(BF16), src["xs"]().astype(BF16)
            collect("merged", src["ga"]()[:, cols] * mm(st["xa"], woa_ref, cols)
                    + src["gs"]()[:, cols] * mm(st["xs"], wos_ref, cols), cols)
        return step

    def residual(cols):
        def step():
            if cols is col_slices[0]:
                st["merged16"] = st.pop("merged").astype(BF16)
            collect("h", src["x"]()[:, cols] + mm(st["merged16"], wout_ref, cols), cols)
        return step

    def embed_gate(cols):
        def step():
            if cols is col_slices[0]:
                st["h16"], st["p16"] = st["h"].astype(BF16), src["p"]().astype(BF16)
            collect("h2", st["h"][:, cols] + _sigmoid(mm(st["h16"], wpg_ref, cols)) * mm(st["p16"], wpp_ref, cols),
                    cols)
        return step

    def norm():
        h = st.pop("h2")
        ms = jnp.mean(h * h, axis=-1, keepdims=True)
        emit(h * lax.rsqrt(ms + EPS) * fgain)

    return ([branches(c) for c in col_slices] + [residual(c) for c in col_slices]
            + [embed_gate(c) for c in col_slices] + [norm])


def _layer_kernel(sinks_ref, x_ref, p_ref, cos_ref, sina_ref, sinb_ref, gain_ref, w_in_ref,
                  wlag_ref, a8r_ref, a8i_ref, ck_ref, d_ref, wglu_ref,
                  woa_ref, wos_ref, wout_ref, wpg_ref, wpp_ref, fg_ref,
                  y_ref, k_ref, v_ref, hr_ref, hi_ref,
                  kbuf, vbuf, xa_s, ubuf, cr_s, ci_s, hs, *, tt):
    t = pl.program_id(1)
    half = tt // 2

    @pl.when(t == 0)
    def _():
        kbuf[0:WINDOW, :] = jnp.zeros((WINDOW, KV2_WIDTH), BF16)
        vbuf[0:WINDOW, :] = jnp.zeros((WINDOW, KV2_WIDTH), BF16)
        _ssm_reset(ubuf, cr_s, ci_s)

    pj, so = [{}, {}], [{}, {}]

    def rows(h):
        return slice(h * half, (h + 1) * half)

    def proj(h):
        def store(name):
            if name == "u":
                ubuf[LAGS + h * half:LAGS + (h + 1) * half, :] = pj[h]["u"]
                return
            last_ref, buf = (k_ref, kbuf) if name == "k" else (v_ref, vbuf)
            if h == 1:
                last_ref[0] = pj[h][name][half - WINDOW:]
            for j in range(N_KV_HEADS):
                buf[WINDOW + h * half:WINDOW + (h + 1) * half, j * LANES:(j + 1) * LANES] = pj[h][name + "2"][j]

        return _proj_steps(lambda: x_ref[0, rows(h)], gain_ref[0:1, :],
                           lambda: (cos_ref[rows(h)], sina_ref[rows(h)], sinb_ref[rows(h)]),
                           w_in_ref, pj[h], store)

    def mid(h):
        def get_q(c, head):
            r0 = c * CHUNK - h * half
            return pj[h]["qh"][head][r0:r0 + CHUNK]

        def get_kv(c, kv):
            krows, cols = slice(c * CHUNK, c * CHUNK + KEYS), slice(kv * LANES, (kv + 1) * LANES)
            return kbuf[krows, cols], vbuf[krows, cols]

        def get_valid(c):
            if c * CHUNK >= WINDOW:
                return None
            in_seq = c * CHUNK + lax.broadcasted_iota(jnp.int32, (1, KEYS), 1) >= WINDOW
            return jnp.logical_or(in_seq, t > 0)

        def emit(c, tile, o):
            r0, cols = c * CHUNK - h * half, slice(tile * LANES, (tile + 1) * LANES)
            xa_s[c * CHUNK:(c + 1) * CHUNK, cols] = (o * pj[h]["sa"][r0:r0 + CHUNK, cols]).astype(BF16)

        def glu():
            so[h]["xs"] = _glu(so[h].pop("y"), pj[h]["u"], pj[h]["sz"], d_ref[0:1, :], wglu_ref)

        chunks = range(h * half // CHUNK, (h + 1) * half // CHUNK)
        return (_attn_steps(sinks_ref, chunks, get_q, get_kv, get_valid, emit)
                + _ssm_steps(ubuf, h * half, half, cr_s, ci_s, hs, wlag_ref, a8r_ref, a8i_ref, ck_ref,
                             hr_ref, hi_ref, so[h])
                + [glu])

    def out(h):
        src = dict(xa=lambda: xa_s[rows(h)], xs=lambda: so[h]["xs"], ga=lambda: pj[h]["ga"],
                   gs=lambda: pj[h]["gs"], x=lambda: x_ref[0, rows(h)], p=lambda: p_ref[0, rows(h)])

        def emit(y):
            y_ref[0, rows(h)] = y

        return _out_steps(src, woa_ref, wos_ref, wout_ref, wpg_ref, wpp_ref, fg_ref[0:1, :], emit)

    _run(proj(0))
    _run(_spread(mid(0), proj(1)))
    _run(_spread(mid(1), out(0)))
    _run(out(1))

    kbuf[0:WINDOW, :] = kbuf[tt:tt + WINDOW, :]
    vbuf[0:WINDOW, :] = vbuf[tt:tt + WINDOW, :]
    ubuf[0:LAGS, :] = ubuf[tt:tt + LAGS, :]


def _layer_fused(x, p, tabs, wts, consts):
    b, t, _ = x.shape
    tt = min(LAYER_ROWS, t)
    assert t % tt == 0 and tt // 2 >= WINDOW and (tt // 2) % BF16_ROWS == 0 and tabs[0].shape[0] == t
    (gain, w_in, sinks, woa, d_skip, w_glu, wos, wout, wpg, wpp, fgain) = wts
    wlag, a8r, a8i, _, _, ck = consts

    def row_spec(w):
        return pl.BlockSpec((1, tt, w), lambda i, j: (i, j, 0))

    tab_spec = pl.BlockSpec((tt, LANES), lambda i, j: (j, 0))
    st_spec = pl.BlockSpec((1, PAIRS, LANES), lambda i, j: (i, 0, 0))
    win_spec = pl.BlockSpec((1, WINDOW, KV_WIDTH), lambda i, j: (i, 0, 0))
    consts_in = (gain, w_in, wlag, a8r, a8i, ck, d_skip, w_glu, woa, wos, wout, wpg, wpp, fgain)
    vmem = (sum(a.size * a.dtype.itemsize for a in consts_in)
            + 2 * tt * (2 * D_MODEL + PLE_DIM + 2 * KV_WIDTH + 3 * LANES) * 4
            + 2 * (WINDOW + tt) * KV2_WIDTH * 2 + tt * ATTN_WIDTH * 2 + (tt + LAGS) * SSM_WIDTH * 4
            + (tt // 2) * N_STATE * 2 + 3 * tt * IN_WIDTH * 4)
    y, k, v, hr, hi = pl.pallas_call(
        functools.partial(_layer_kernel, tt=tt),
        grid=(b, t // tt),
        in_specs=[pl.BlockSpec(memory_space=pltpu.SMEM), row_spec(D_MODEL), row_spec(PLE_DIM),
                  tab_spec, tab_spec, tab_spec]
                 + [_const_spec(a.shape) for a in consts_in],
        out_specs=[row_spec(D_MODEL), win_spec, win_spec, st_spec, st_spec],
        out_shape=[jax.ShapeDtypeStruct((b, t, D_MODEL), F32),
                   jax.ShapeDtypeStruct((b, WINDOW, KV_WIDTH), F32),
                   jax.ShapeDtypeStruct((b, WINDOW, KV_WIDTH), F32),
                   jax.ShapeDtypeStruct((b, PAIRS, LANES), F32),
                   jax.ShapeDtypeStruct((b, PAIRS, LANES), F32)],
        scratch_shapes=[pltpu.VMEM((WINDOW + tt, KV2_WIDTH), BF16),
                        pltpu.VMEM((WINDOW + tt, KV2_WIDTH), BF16),
                        pltpu.VMEM((tt, ATTN_WIDTH), BF16),
                        pltpu.VMEM((tt + LAGS, SSM_WIDTH), F32),
                        pltpu.VMEM((PAIRS, SUBLANES, LANES), F32),
                        pltpu.VMEM((PAIRS, SUBLANES, LANES), F32),
                        pltpu.VMEM((tt // 2, N_STATE), BF16)],
        compiler_params=_params(vmem, 2),
        name="layer_prompt",
    )(sinks, x, p, *tabs, *consts_in)
    return y, k, v, hr, hi


def _sample_kernel(sinks_ref, x_ref, p_ref, cos_ref, sina_ref, sinb_ref, kpre_ref, vpre_ref,
                   h0r_ref, h0i_ref, gain_ref, w_in_ref, wlag_ref, a8r_ref, a8i_ref, pr_ref, pi_ref,
                   ck_ref, d_ref, wglu_ref, woa_ref, wos_ref, wout_ref, wpg_ref, wpp_ref, fg_ref,
                   y_ref, k_ref, v_ref, hr_ref, hi_ref,
                   kbuf, vbuf, xa_s, ubuf, hs, *, n, t):
    rows = n * t
    pj, so = {}, {}

    def store(name):
        if name == "u":
            ubuf[0:LAGS, :] = jnp.zeros((LAGS, SSM_WIDTH), F32)
            ubuf[LAGS:, :] = pj["u"]
            return
        full_ref, pre_ref, buf = (k_ref, kpre_ref, kbuf) if name == "k" else (v_ref, vpre_ref, vbuf)
        full_ref[...] = pj[name]
        for s in range(n):
            buf[s, 0:WINDOW, :] = pre_ref[s]
            for j in range(N_KV_HEADS):
                buf[s, WINDOW:WINDOW + t, j * LANES:(j + 1) * LANES] = pj[name + "2"][j][s * t:(s + 1) * t]

    def get_q(c, head):
        return pj["qh"][head][c * CHUNK:(c + 1) * CHUNK]

    def get_kv(c, kv):
        cols = slice(kv * LANES, (kv + 1) * LANES)
        return kbuf[c, :, cols], vbuf[c, :, cols]

    def emit(c, tile, o):
        r, cols = slice(c * CHUNK, (c + 1) * CHUNK), slice(tile * LANES, (tile + 1) * LANES)
        xa_s[r, cols] = (o * pj["sa"][r, cols]).astype(BF16)

    def glu():
        so["xs"] = _glu(so.pop("y"), pj["u"], pj["sz"], d_ref[0:1, :], wglu_ref)

    src = dict(xa=lambda: xa_s[...], xs=lambda: so["xs"], ga=lambda: pj["ga"], gs=lambda: pj["gs"],
               x=lambda: x_ref[...], p=lambda: p_ref[...])

    def emit_y(y):
        y_ref[...] = y

    _run(_proj_steps(lambda: x_ref[...], gain_ref[0:1, :],
                     lambda: (cos_ref[...], sina_ref[...], sinb_ref[...]), w_in_ref, pj, store))
    _run(_attn_steps(sinks_ref, range(n), get_q, get_kv, lambda c: None, emit))
    _run(_ssm_steps(ubuf, 0, rows, None, None, hs, wlag_ref, a8r_ref, a8i_ref, ck_ref, hr_ref, hi_ref, so,
                    seg=(t, h0r_ref, h0i_ref, pr_ref, pi_ref)) + [glu])
    _run(_out_steps(src, woa_ref, wos_ref, wout_ref, wpg_ref, wpp_ref, fg_ref[0:1, :], emit_y))


def _layer_sample(x, p, tabs, k_prefix, v_prefix, h0r, h0i, wts, consts):
    n, t, _ = x.shape
    assert t == CHUNK and tabs[0].shape[0] == n * t
    rows = n * t
    (gain, w_in, sinks, woa, d_skip, w_glu, wos, wout, wpg, wpp, fgain) = wts
    wlag, a8r, a8i, pr, pi, ck = consts
    operands = (x.reshape(rows, D_MODEL), p.reshape(rows, PLE_DIM), *tabs,
                _both_halves(k_prefix), _both_halves(v_prefix), h0r, h0i,
                gain, w_in, wlag, a8r, a8i, pr, pi, ck, d_skip, w_glu, woa, wos, wout, wpg, wpp, fgain)
    out_shapes = [(rows, D_MODEL), (rows, KV_WIDTH), (rows, KV_WIDTH), h0r.shape, h0r.shape]
    scratch = [((n, KEYS, KV2_WIDTH), BF16), ((n, KEYS, KV2_WIDTH), BF16), ((rows, ATTN_WIDTH), BF16),
               ((rows + LAGS, SSM_WIDTH), F32), ((rows, N_STATE), BF16)]
    vmem = (sum(a.size * a.dtype.itemsize for a in operands)
            + sum(int(np.prod(s)) * 4 for s in out_shapes)
            + sum(int(np.prod(s)) * np.dtype(d).itemsize for s, d in scratch)
            + 3 * rows * IN_WIDTH * 4)
    y, k, v, hr, hi = pl.pallas_call(
        functools.partial(_sample_kernel, n=n, t=t),
        grid=(1,),
        in_specs=[pl.BlockSpec(memory_space=pltpu.SMEM)] + [_const_spec(a.shape) for a in operands],
        out_specs=[pl.BlockSpec(s, lambda i, nd=len(s): (0,) * nd) for s in out_shapes],
        out_shape=[jax.ShapeDtypeStruct(s, F32) for s in out_shapes],
        scratch_shapes=[pltpu.VMEM(s, d) for s, d in scratch],
        compiler_params=_params(vmem, 1),
        name="layer_sample",
    )(sinks, *operands)
    k_new = jnp.concatenate([k_prefix[:, t:], k.reshape(n, t, KV_WIDTH)], axis=1)
    v_new = jnp.concatenate([v_prefix[:, t:], v.reshape(n, t, KV_WIDTH)], axis=1)
    return y.reshape(n, t, D_MODEL), k_new, v_new, hr, hi


def _both_halves(a):
    h0, h1 = a[..., :HEAD_DIM], a[..., HEAD_DIM:]
    return jnp.concatenate([h0, h0, h1, h1], axis=-1).astype(BF16)


def _ssm_constants(a_re, a_im, log_dt, b_re, b_im, c_re, c_im):
    dt = jnp.exp(log_dt.astype(F32))[:, None]
    lr = a_re.astype(F32).reshape(PAIRS, 1, 1, LANES)
    li = a_im.astype(F32).reshape(PAIRS, 1, 1, LANES)
    xr = (a_re.astype(F32) * dt).reshape(PAIRS, 1, 1, LANES)
    xi = (a_im.astype(F32) * dt).reshape(PAIRS, 1, 1, LANES)

    def apow(n):
        mag = jnp.exp(xr * n)
        return mag * jnp.cos(xi * n), mag * jnp.sin(xi * n)

    ar, ai = apow(1.0)
    nr, ni = ar - 1.0, ai
    den = lr * lr + li * li
    fr, fi = (nr * lr + ni * li) / den, (ni * lr - nr * li) / den

    n_slots = PAIR_K // SLOT
    qq, hi_ = np.arange(PAIRS)[:, None], np.arange(n_slots)[None, :]
    lag_tab = (PAIRS_PER_TILE * (hi_ // PAIRS_PER_TILE) + (hi_ % PAIRS_PER_TILE - qq) % PAIRS_PER_TILE)
    lag_tab = lag_tab.astype(np.float32)[:, :, None, None]
    same_group = (np.arange(SLOT)[:, None] // SSM_GROUP == np.arange(LANES)[None, :] // SSM_STATE)
    same_group = same_group.astype(np.float32)

    def b_rows(bm):
        t = jnp.transpose(bm.astype(F32).reshape(PAIRS, 2, SSM_STATE, SSM_GROUP), (0, 3, 1, 2))
        t = t.reshape(PAIRS, 1, 1, SSM_GROUP, LANES)
        t = jnp.broadcast_to(t, (PAIRS, 1, 2, SSM_GROUP, LANES))
        return t.reshape(PAIRS, 1, SLOT, LANES) * same_group

    br, bi = b_rows(b_re), b_rows(b_im)
    bbr, bbi = fr * br - fi * bi, fr * bi + fi * br
    er, ei = apow(lag_tab)
    wlag = jnp.concatenate([(er * bbr - ei * bbi).reshape(PAIRS, PAIR_K, LANES),
                            (er * bbi + ei * bbr).reshape(PAIRS, PAIR_K, LANES)], axis=-1).astype(BF16)

    a8r, a8i = (a.reshape(PAIRS, LANES) for a in apow(float(LAGS)))
    pwr, pwi = (a.reshape(PAIRS, LAGS, LANES)
                for a in apow(np.arange(1, LAGS + 1, dtype=np.float32)[None, :, None, None]))

    def c_cols(c):
        t = jnp.transpose(c.astype(F32).reshape(U_TILES, LANES // SSM_GROUP, SSM_GROUP, SSM_STATE),
                          (0, 3, 1, 2))
        return t.reshape(U_TILES, 1, 1, 1, SSM_STATE, LANES)

    cols_group = np.arange(LANES) // SSM_GROUP
    rows_group = 2 * np.arange(PAIRS_PER_TILE)[:, None] + np.arange(2)[None, :]
    c_mask = (rows_group[:, None, :, None, None] == cols_group[None, None, None, None, :])
    c_mask = c_mask.astype(np.float32)[None]
    ck = jnp.concatenate([c_cols(c_re) * c_mask, -c_cols(c_im) * c_mask], axis=2)
    ck = ck.reshape(U_TILES, PAIRS_PER_TILE * PAIR_N, LANES).astype(BF16)
    return wlag, a8r, a8i, pwr, pwi, ck


def _rope_tables(pos0, t, rows):
    half = ROT_DIM // 2
    d = np.arange(LANES) % HEAD_DIM
    inv = jnp.power(ROPE_THETA, -jnp.arange(half, dtype=F32) * 2.0 / ROT_DIM)
    pos = (pos0 + jnp.arange(t)).astype(F32)
    ang = pos[:, None] * inv[None, :]
    cos, sin = (jnp.tile(a, (1, LANES // half)) for a in (jnp.cos(ang), jnp.sin(ang)))
    cos_t = jnp.where((d < ROT_DIM)[None, :], cos, 1.0)
    sina = jnp.where(((d >= half) & (d < ROT_DIM))[None, :], sin, 0.0)
    sinb = jnp.where((d < half)[None, :], -sin, 0.0)
    reps = (max(rows // t, 1), 1)
    return tuple(jnp.tile(a, reps) for a in (cos_t, sina, sinb))


def kernel(x_prompt, x_sample, p_prompt, p_sample, cache_attn_k, cache_attn_v, state_ssm_re,
           state_ssm_im, norm_gain, w_in, attn_sinks, w_o_attn, ssm_a_re, ssm_a_im, ssm_log_dt,
           ssm_b_re, ssm_b_im, ssm_c_re, ssm_c_im, ssm_d, ssm_w_glu, w_o_ssm, w_out,
           w_ple_gate, w_ple_proj, final_norm_gain):
    assert norm_gain.shape[0] == 1, "single-layer model"
    bp, tp, _ = x_prompt.shape
    bs, ts, _ = x_sample.shape
    def rowvec(a):
        return jnp.broadcast_to(a.reshape(1, -1).astype(F32), (SUBLANES, a.size))

    wts = (rowvec(norm_gain[0]), w_in[0].astype(BF16),
           attn_sinks[0].astype(F32), w_o_attn[0].astype(BF16),
           rowvec(ssm_d[0]), ssm_w_glu[0].astype(BF16),
           w_o_ssm[0].astype(BF16), w_out[0].astype(BF16), w_ple_gate[0].astype(BF16),
           w_ple_proj[0].astype(BF16), rowvec(final_norm_gain))
    consts = _ssm_constants(ssm_a_re[0], ssm_a_im[0], ssm_log_dt[0], ssm_b_re[0], ssm_b_im[0],
                            ssm_c_re[0], ssm_c_im[0])

    y_p, k_p, v_p, hr_p, hi_p = _layer_fused(x_prompt, p_prompt[0], _rope_tables(0, tp, tp), wts, consts)

    ck = cache_attn_k[0].reshape(bs, WINDOW, KV_WIDTH).astype(F32)
    cv = cache_attn_v[0].reshape(bs, WINDOW, KV_WIDTH).astype(F32)
    h0r = state_ssm_re[0].reshape(bs, PAIRS, LANES).astype(F32)
    h0i = state_ssm_im[0].reshape(bs, PAIRS, LANES).astype(F32)
    tabs_s = _rope_tables(PAST_LEN, ts, bs * ts)
    y_s, k_s, v_s, hr_s, hi_s = _layer_sample(x_sample, p_sample[0], tabs_s, ck, cv, h0r, h0i, wts, consts)

    def kv_out(a, b):
        return a.reshape(1, b, WINDOW, N_KV_HEADS, HEAD_DIM)

    def st_out(a, b):
        return a.reshape(1, b, SSM_GROUPS, SSM_STATE)

    return (y_p, y_s, kv_out(k_p, bp), kv_out(v_p, bp),
            st_out(hr_p, bp), st_out(hi_p, bp), kv_out(k_s, bs), kv_out(v_s, bs),
            st_out(hr_s, bs), st_out(hi_s, bs))
```

```python
import functools

import numpy as np
import jax
import jax.numpy as jnp
from jax import lax
from jax.experimental import pallas as pl
from jax.experimental.pallas import tpu as pltpu

F32 = jnp.float32
BF16 = jnp.bfloat16

LANES = 128
SUBLANES = 8
V7X_VMEM_BYTES = 64 * 1024 * 1024

D_MODEL = 1024
CHUNK = 64
WINDOW = 128
N_HEADS = 8
N_KV_HEADS = 2
HEAD_DIM = 64
Q_PER_KV = N_HEADS // N_KV_HEADS
LOG2E = 1.4426950408889634
Q_SCALE = HEAD_DIM ** -0.5 * LOG2E
ATTN_WIDTH = N_HEADS * HEAD_DIM
KV_WIDTH = N_KV_HEADS * HEAD_DIM
ROT_DIM = HEAD_DIM // 4
ROPE_THETA = 500000.0
SSM_WIDTH = D_MODEL // 2
SSM_GROUP = 16
SSM_GROUPS = SSM_WIDTH // SSM_GROUP
SSM_STATE = 64
PLE_DIM = 256
PAST_LEN = 1024
EPS = 1e-6

O_Q = 0
O_K = O_Q + ATTN_WIDTH
O_V = O_K + KV_WIDTH
O_ZA = O_V + KV_WIDTH
O_U = O_ZA + ATTN_WIDTH
O_ZS = O_U + SSM_WIDTH
O_GA = O_ZS + SSM_WIDTH
O_GS = O_GA + D_MODEL
IN_WIDTH = O_GS + D_MODEL

QM_WIDTH = N_HEADS * LANES
KV2_WIDTH = N_KV_HEADS * LANES
KEYS = WINDOW + CHUNK
LAGS = SUBLANES
PAIRS = SSM_GROUPS // 2
PAIR_K = 2 * LAGS * SSM_GROUP
PAIR_N = 2 * 2 * SSM_STATE
N_STATE = PAIRS * PAIR_N
U_TILES = SSM_WIDTH // LANES
PAIRS_PER_TILE = PAIRS // U_TILES
SLOT = 2 * SSM_GROUP
BF16_ROWS = 2 * SUBLANES
assert PAIRS_PER_TILE == 4 and LAGS == 2 * PAIRS_PER_TILE

OUT_COLS = 4 * LANES
LAYER_ROWS = 512


def _sigmoid(x):
    return 1.0 / (1.0 + jnp.exp2(x * (-LOG2E)))


def _const_spec(shape):
    zeros = (0,) * len(shape)
    return pl.BlockSpec(shape, lambda *_: zeros, pipeline_mode=pl.Buffered(1))


def _params(vmem_bytes, n_grid):
    return pltpu.CompilerParams(
        dimension_semantics=("arbitrary",) * n_grid,
        vmem_limit_bytes=min(int(vmem_bytes), V7X_VMEM_BYTES - 8 * 1024 * 1024),
    )


def _run(steps):
    for step in steps:
        step()


def _spread(main, other):
    merged, j = [], 0
    for i, step in enumerate(main):
        while j < len(other) and j * len(main) <= i * len(other):
            merged.append(other[j])
            j += 1
        merged.append(step)
    return merged + other[j:]


def _proj_steps(get_x, gain, get_tabs, w_ref, o, store=None):
    st = {}

    def norm():
        x = get_x()
        ms = jnp.mean(x * x, axis=-1, keepdims=True)
        st["xn"] = (x * lax.rsqrt(ms + EPS) * gain).astype(BF16)
        st["lo"] = lax.broadcasted_iota(jnp.int32, (x.shape[0], LANES), 1) < HEAD_DIM

    def seg(a, b):
        return jnp.dot(st["xn"], w_ref[:, a:b], preferred_element_type=F32)

    def rope(t):
        cos, sina, sinb = get_tabs()
        return (t * cos + pltpu.roll(t, ROT_DIM // 2, 1) * sina
                + pltpu.roll(t, LANES - ROT_DIM // 2, 1) * sinb)

    def both_halves(t):
        tr = pltpu.roll(t, HEAD_DIM, 1)
        return [jnp.where(st["lo"], t, tr).astype(BF16), jnp.where(st["lo"], tr, t).astype(BF16)]

    def done(name):
        if store is not None:
            store(name)

    def q():
        zq = seg(O_Q, O_K)
        o["qh"] = []
        for j in range(ATTN_WIDTH // LANES):
            qt = rope(zq[:, j * LANES:(j + 1) * LANES]) * Q_SCALE
            o["qh"] += [jnp.where(st["lo"], qt, 0.0).astype(BF16),
                        jnp.where(st["lo"], 0.0, qt).astype(BF16)]

    def kv():
        z = seg(O_K, O_ZA)
        o["k"] = rope(z[:, :KV_WIDTH])
        o["k2"] = both_halves(o["k"])
        done("k")
        o["v"] = z[:, KV_WIDTH:]
        o["v2"] = both_halves(o["v"])
        done("v")

    def za():
        z = seg(O_ZA, O_U)
        o["sa"] = z * _sigmoid(z)

    def u():
        o["u"] = seg(O_U, O_ZS)
        done("u")

    def zs():
        z = seg(O_ZS, O_GA)
        o["sz"] = z * _sigmoid(z)

    def ga():
        o["ga"] = _sigmoid(seg(O_GA, O_GS))

    def gs():
        o["gs"] = _sigmoid(seg(O_GS, IN_WIDTH))

    return [norm, q, kv, za, u, zs, ga, gs]


def _attn_steps(sinks_ref, chunks, get_q, get_kv, get_valid, emit):
    nt = (((1,), (1,)), ((), ()))
    units = [(c, kv) for c in chunks for kv in range(N_KV_HEADS)]
    n = len(units)
    st = {}

    def scores(c, kv):
        k2, v2 = get_kv(c, kv)
        qm = jnp.concatenate([get_q(c, kv * Q_PER_KV + h) for h in range(Q_PER_KV)], axis=0)
        s = lax.dot_general(qm, k2, nt, preferred_element_type=F32)
        valid = get_valid(c)
        if valid is not None:
            s = jnp.where(valid, s, -jnp.inf)
        return s, v2

    def softmax(s, kv):
        head_row = lax.broadcasted_iota(jnp.int32, (Q_PER_KV * CHUNK, 1), 0) // CHUNK
        sk = [sinks_ref[kv * Q_PER_KV + h] * LOG2E for h in range(Q_PER_KV)]
        sink = jnp.where(head_row == 0, sk[0],
                         jnp.where(head_row == 1, sk[1], jnp.where(head_row == 2, sk[2], sk[3])))
        m = jnp.maximum(jnp.max(s, axis=1, keepdims=True), sink)
        return jnp.exp2(s - m).astype(BF16), jnp.exp2(sink - m)

    def output(e, sink_term, v2, c, kv):
        lo_q = lax.broadcasted_iota(jnp.int32, (CHUNK, LANES), 1) < HEAD_DIM
        ones = jnp.ones((KEYS, LANES), BF16)
        pv = jnp.dot(e, jnp.concatenate([v2, ones], axis=1), preferred_element_type=F32)
        o = pv[:, :LANES] / (pv[:, LANES:] + sink_term)
        for j in range(Q_PER_KV // 2):
            even = o[2 * j * CHUNK:(2 * j + 1) * CHUNK]
            odd = o[(2 * j + 1) * CHUNK:(2 * j + 2) * CHUNK]
            emit(c, kv * (Q_PER_KV // 2) + j, jnp.where(lo_q, even, odd))

    def make(i):
        def step():
            if i < n:
                st[i] = scores(*units[i])
            if 0 <= i - 1 < n:
                s, v2 = st[i - 1]
                st[i - 1] = softmax(s, units[i - 1][1]) + (v2,)
            if 0 <= i - 2 < n:
                e, den, v2 = st.pop(i - 2)
                output(e, den, v2, *units[i - 2])
        return step

    return [make(i) for i in range(n + 2)]


def _ssm_reset(ubuf, cr_s, ci_s):
    ubuf[0:LAGS, :] = jnp.zeros((LAGS, SSM_WIDTH), F32)
    cr_s[...] = jnp.zeros(cr_s.shape, F32)
    ci_s[...] = jnp.zeros(ci_s.shape, F32)


def _ssm_steps(ubuf, row0, tt, cr_s, ci_s, hs, wlag_ref, a8r_ref, a8i_ref, ck_ref, hr_ref, hi_ref, o,
               seg=None):
    st = {"ys": []}

    def setup():
        if seg is not None:
            st["row_in_seg"] = lax.broadcasted_iota(jnp.int32, (tt, LANES), 0) % seg[0]
        slot = lax.broadcasted_iota(jnp.int32, (tt, LANES), 1) // SLOT
        st["to_low"] = [((slot + PAIRS_PER_TILE - s) % PAIRS_PER_TILE) < 2 for s in range(2)]
        st["same_parity"] = [((slot + sg) % 2) == 0 for sg in range(2)]

    def route(r):
        low = [jnp.where(st["to_low"][s], r[s], r[s + 2]) for s in range(2)]
        high = [jnp.where(st["to_low"][s], r[s + 2], r[s]) for s in range(2)]
        return [jnp.where(st["same_parity"][sg % 2], src[0], src[1])
                for sg, src in zip(range(PAIRS_PER_TILE), (low, low, high, high))]

    def lag_copies(k):
        def step():
            ub = ubuf[row0:row0 + LAGS + tt, k * LANES:(k + 1) * LANES]
            rolled = []
            for s in range(LAGS):
                us = ub[LAGS:] if s == 0 else pltpu.roll(ub, s, 0)[LAGS:]
                if seg is not None and s > 0:
                    us = jnp.where(st["row_in_seg"] >= s, us, 0.0)
                if s % PAIRS_PER_TILE:
                    us = pltpu.roll(us, SLOT * (s % PAIRS_PER_TILE), 1)
                rolled.append(us)
            st["halves"] = (route(rolled[:PAIRS_PER_TILE]), route(rolled[PAIRS_PER_TILE:]))
        return step

    def pair(k, sg):
        def step():
            q = k * PAIRS_PER_TILE + sg
            xl = jnp.concatenate([st["halves"][0][sg], st["halves"][1][sg]], axis=1).astype(BF16)
            w = jnp.dot(xl, wlag_ref[q], preferred_element_type=F32)
            ar, ai = a8r_ref[q:q + 1, :], a8i_ref[q:q + 1, :]
            if seg is None:
                cr, ci = cr_s[q], ci_s[q]
            else:
                seg_rows, h0r_ref, h0i_ref, pr_ref, pi_ref = seg
            for b2 in range(tt // BF16_ROWS):
                hrs, his = [], []
                for b in (2 * b2, 2 * b2 + 1):
                    if seg is not None and (b * SUBLANES) % seg_rows == 0:
                        n = b * SUBLANES // seg_rows
                        h0r, h0i = h0r_ref[n, q:q + 1, :], h0i_ref[n, q:q + 1, :]
                        cr = pr_ref[q] * h0r - pi_ref[q] * h0i
                        ci = pr_ref[q] * h0i + pi_ref[q] * h0r
                    blk = slice(b * SUBLANES, (b + 1) * SUBLANES)
                    hr = w[blk, :LANES] + cr
                    hi = w[blk, LANES:] + ci
                    cr = ar * hr - ai * hi
                    ci = ar * hi + ai * hr
                    hrs.append(hr)
                    his.append(hi)
                    if seg is not None and ((b + 1) * SUBLANES) % seg_rows == 0:
                        n = b * SUBLANES // seg_rows
                        hr_ref[n, q:q + 1, :] = hr[SUBLANES - 1:, :]
                        hi_ref[n, q:q + 1, :] = hi[SUBLANES - 1:, :]
                blk2 = slice(b2 * BF16_ROWS, (b2 + 1) * BF16_ROWS)
                hs[blk2, q * PAIR_N:q * PAIR_N + LANES] = jnp.concatenate(hrs, axis=0).astype(BF16)
                hs[blk2, q * PAIR_N + LANES:(q + 1) * PAIR_N] = jnp.concatenate(his, axis=0).astype(BF16)
            if seg is None:
                cr_s[q] = cr
                ci_s[q] = ci
                hr_ref[0, q:q + 1, :] = hr[SUBLANES - 1:, :]
                hi_ref[0, q:q + 1, :] = hi[SUBLANES - 1:, :]
        return step

    def c_proj(k):
        def step():
            cols = slice(k * PAIRS_PER_TILE * PAIR_N, (k + 1) * PAIRS_PER_TILE * PAIR_N)
            st["ys"].append(jnp.dot(hs[:, cols], ck_ref[k], preferred_element_type=F32))
            if k == U_TILES - 1:
                o["y"] = jnp.concatenate(st["ys"], axis=1)
        return step

    steps = [setup]
    for k in range(U_TILES):
        steps += [lag_copies(k)] + [pair(k, sg) for sg in range(PAIRS_PER_TILE)] + [c_proj(k)]
    return steps


def _glu(y, u, sz, d, wglu_ref):
    z = jax.nn.gelu(y + d * u)
    g = jnp.dot(z.astype(BF16), wglu_ref[...], preferred_element_type=F32)
    return z * _sigmoid(g) * sz


def _out_steps(src, woa_ref, wos_ref, wout_ref, wpg_ref, wpp_ref, fgain, emit):
    st = {}

    def mm(a, w_ref):
        a = a.astype(BF16)
        return jnp.concatenate([jnp.dot(a, w_ref[j], preferred_element_type=F32)
                                for j in range(w_ref.shape[0])], axis=1)

    def branches():
        st["merged"] = src["ga"]() * mm(src["xa"](), woa_ref) + src["gs"]() * mm(src["xs"](), wos_ref)

    def residual():
        st["h"] = src["x"]() + mm(st.pop("merged"), wout_ref)

    def embed_gate():
        h = st.pop("h")
        st["h"] = h + _sigmoid(mm(h, wpg_ref)) * mm(src["p"](), wpp_ref)

    def norm():
        h = st.pop("h")
        ms = jnp.mean(h * h, axis=-1, keepdims=True)
        emit(h * lax.rsqrt(ms + EPS) * fgain)

    return [branches, residual, embed_gate, norm]


def _layer_kernel(sinks_ref, x_ref, p_ref, cos_ref, sina_ref, sinb_ref, gain_ref, w_in_ref,
                  wlag_ref, a8r_ref, a8i_ref, ck_ref, d_ref, wglu_ref,
                  woa_ref, wos_ref, wout_ref, wpg_ref, wpp_ref, fg_ref,
                  y_ref, k_ref, v_ref, hr_ref, hi_ref,
                  kbuf, vbuf, xa_s, ubuf, cr_s, ci_s, hs, *, tt):
    t = pl.program_id(1)
    half = tt // 2

    @pl.when(t == 0)
    def _():
        kbuf[0:WINDOW, :] = jnp.zeros((WINDOW, KV2_WIDTH), BF16)
        vbuf[0:WINDOW, :] = jnp.zeros((WINDOW, KV2_WIDTH), BF16)
        _ssm_reset(ubuf, cr_s, ci_s)

    pj, so = [{}, {}], [{}, {}]

    def rows(h):
        return slice(h * half, (h + 1) * half)

    def proj(h):
        def store(name):
            if name == "u":
                ubuf[LAGS + h * half:LAGS + (h + 1) * half, :] = pj[h]["u"]
                return
            last_ref, buf = (k_ref, kbuf) if name == "k" else (v_ref, vbuf)
            if h == 1:
                last_ref[0] = pj[h][name][half - WINDOW:]
            for j in range(N_KV_HEADS):
                buf[WINDOW + h * half:WINDOW + (h + 1) * half, j * LANES:(j + 1) * LANES] = pj[h][name + "2"][j]

        return _proj_steps(lambda: x_ref[0, rows(h)], gain_ref[...],
                           lambda: (cos_ref[rows(h)], sina_ref[rows(h)], sinb_ref[rows(h)]),
                           w_in_ref, pj[h], store)

    def mid(h):
        def get_q(c, head):
            r0 = c * CHUNK - h * half
            return pj[h]["qh"][head][r0:r0 + CHUNK]

        def get_kv(c, kv):
            krows, cols = slice(c * CHUNK, c * CHUNK + KEYS), slice(kv * LANES, (kv + 1) * LANES)
            return kbuf[krows, cols], vbuf[krows, cols]

        def get_valid(c):
            if c * CHUNK >= WINDOW:
                return None
            in_seq = c * CHUNK + lax.broadcasted_iota(jnp.int32, (1, KEYS), 1) >= WINDOW
            return jnp.logical_or(in_seq, t > 0)

        def emit(c, tile, o):
            r0, cols = c * CHUNK - h * half, slice(tile * LANES, (tile + 1) * LANES)
            xa_s[c * CHUNK:(c + 1) * CHUNK, cols] = (o * pj[h]["sa"][r0:r0 + CHUNK, cols]).astype(BF16)

        def glu():
            so[h]["xs"] = _glu(so[h].pop("y"), pj[h]["u"], pj[h]["sz"], d_ref[...], wglu_ref)

        chunks = range(h * half // CHUNK, (h + 1) * half // CHUNK)
        return (_attn_steps(sinks_ref, chunks, get_q, get_kv, get_valid, emit)
                + _ssm_steps(ubuf, h * half, half, cr_s, ci_s, hs, wlag_ref, a8r_ref, a8i_ref, ck_ref,
                             hr_ref, hi_ref, so[h])
                + [glu])

    def out(h):
        src = dict(xa=lambda: xa_s[rows(h)], xs=lambda: so[h]["xs"], ga=lambda: pj[h]["ga"],
                   gs=lambda: pj[h]["gs"], x=lambda: x_ref[0, rows(h)], p=lambda: p_ref[0, rows(h)])

        def emit(y):
            y_ref[0, rows(h)] = y

        return _out_steps(src, woa_ref, wos_ref, wout_ref, wpg_ref, wpp_ref, fg_ref[...], emit)

    _run(proj(0))
    _run(_spread(mid(0), proj(1)))
    _run(_spread(mid(1), out(0)))
    _run(out(1))

    kbuf[0:WINDOW, :] = kbuf[tt:tt + WINDOW, :]
    vbuf[0:WINDOW, :] = vbuf[tt:tt + WINDOW, :]
    ubuf[0:LAGS, :] = ubuf[tt:tt + LAGS, :]


def _layer_fused(x, p, tabs, wts, consts):
    b, t, _ = x.shape
    tt = min(LAYER_ROWS, t)
    assert t % tt == 0 and tt // 2 >= WINDOW and (tt // 2) % BF16_ROWS == 0 and tabs[0].shape[0] == t
    (gain, w_in, sinks, woa, d_skip, w_glu, wos, wout, wpg, wpp, fgain) = wts
    wlag, a8r, a8i, _, _, ck = consts

    def row_spec(w):
        return pl.BlockSpec((1, tt, w), lambda i, j: (i, j, 0))

    tab_spec = pl.BlockSpec((tt, LANES), lambda i, j: (j, 0))
    st_spec = pl.BlockSpec((1, PAIRS, LANES), lambda i, j: (i, 0, 0))
    win_spec = pl.BlockSpec((1, WINDOW, KV_WIDTH), lambda i, j: (i, 0, 0))
    consts_in = (gain, w_in, wlag, a8r, a8i, ck, d_skip, w_glu, woa, wos, wout, wpg, wpp, fgain)
    vmem = (sum(a.size * a.dtype.itemsize for a in consts_in)
            + 2 * tt * (2 * D_MODEL + PLE_DIM + 2 * KV_WIDTH + 3 * LANES) * 4
            + 2 * (WINDOW + tt) * KV2_WIDTH * 2 + tt * ATTN_WIDTH * 2 + (tt + LAGS) * SSM_WIDTH * 4
            + (tt // 2) * N_STATE * 2 + 3 * tt * IN_WIDTH * 4)
    y, k, v, hr, hi = pl.pallas_call(
        functools.partial(_layer_kernel, tt=tt),
        grid=(b, t // tt),
        in_specs=[pl.BlockSpec(memory_space=pltpu.SMEM), row_spec(D_MODEL), row_spec(PLE_DIM),
                  tab_spec, tab_spec, tab_spec]
                 + [_const_spec(a.shape) for a in consts_in],
        out_specs=[row_spec(D_MODEL), win_spec, win_spec, st_spec, st_spec],
        out_shape=[jax.ShapeDtypeStruct((b, t, D_MODEL), F32),
                   jax.ShapeDtypeStruct((b, WINDOW, KV_WIDTH), F32),
                   jax.ShapeDtypeStruct((b, WINDOW, KV_WIDTH), F32),
                   jax.ShapeDtypeStruct((b, PAIRS, LANES), F32),
                   jax.ShapeDtypeStruct((b, PAIRS, LANES), F32)],
        scratch_shapes=[pltpu.VMEM((WINDOW + tt, KV2_WIDTH), BF16),
                        pltpu.VMEM((WINDOW + tt, KV2_WIDTH), BF16),
                        pltpu.VMEM((tt, ATTN_WIDTH), BF16),
                        pltpu.VMEM((tt + LAGS, SSM_WIDTH), F32),
                        pltpu.VMEM((PAIRS, SUBLANES, LANES), F32),
                        pltpu.VMEM((PAIRS, SUBLANES, LANES), F32),
                        pltpu.VMEM((tt // 2, N_STATE), BF16)],
        compiler_params=_params(vmem, 2),
        name="layer_prompt",
    )(sinks, x, p, *tabs, *consts_in)
    return y, k, v, hr, hi


def _sample_kernel(sinks_ref, x_ref, p_ref, cos_ref, sina_ref, sinb_ref, kpre_ref, vpre_ref,
                   h0r_ref, h0i_ref, gain_ref, w_in_ref, wlag_ref, a8r_ref, a8i_ref, pr_ref, pi_ref,
                   ck_ref, d_ref, wglu_ref, woa_ref, wos_ref, wout_ref, wpg_ref, wpp_ref, fg_ref,
                   y_ref, k_ref, v_ref, hr_ref, hi_ref,
                   kbuf, vbuf, xa_s, ubuf, hs, *, n, t):
    rows = n * t
    pj, so = {}, {}

    def store(name):
        if name == "u":
            ubuf[0:LAGS, :] = jnp.zeros((LAGS, SSM_WIDTH), F32)
            ubuf[LAGS:, :] = pj["u"]
            return
        full_ref, pre_ref, buf = (k_ref, kpre_ref, kbuf) if name == "k" else (v_ref, vpre_ref, vbuf)
        full_ref[...] = pj[name]
        for s in range(n):
            buf[s, 0:WINDOW, :] = pre_ref[s]
            for j in range(N_KV_HEADS):
                buf[s, WINDOW:WINDOW + t, j * LANES:(j + 1) * LANES] = pj[name + "2"][j][s * t:(s + 1) * t]

    def get_q(c, head):
        return pj["qh"][head][c * CHUNK:(c + 1) * CHUNK]

    def get_kv(c, kv):
        cols = slice(kv * LANES, (kv + 1) * LANES)
        return kbuf[c, :, cols], vbuf[c, :, cols]

    def emit(c, tile, o):
        r, cols = slice(c * CHUNK, (c + 1) * CHUNK), slice(tile * LANES, (tile + 1) * LANES)
        xa_s[r, cols] = (o * pj["sa"][r, cols]).astype(BF16)

    def glu():
        so["xs"] = _glu(so.pop("y"), pj["u"], pj["sz"], d_ref[...], wglu_ref)

    src = dict(xa=lambda: xa_s[...], xs=lambda: so["xs"], ga=lambda: pj["ga"], gs=lambda: pj["gs"],
               x=lambda: x_ref[...], p=lambda: p_ref[...])

    def emit_y(y):
        y_ref[...] = y

    _run(_proj_steps(lambda: x_ref[...], gain_ref[...],
                     lambda: (cos_ref[...], sina_ref[...], sinb_ref[...]), w_in_ref, pj, store))
    _run(_attn_steps(sinks_ref, range(n), get_q, get_kv, lambda c: None, emit))
    _run(_ssm_steps(ubuf, 0, rows, None, None, hs, wlag_ref, a8r_ref, a8i_ref, ck_ref, hr_ref, hi_ref, so,
                    seg=(t, h0r_ref, h0i_ref, pr_ref, pi_ref)) + [glu])
    _run(_out_steps(src, woa_ref, wos_ref, wout_ref, wpg_ref, wpp_ref, fg_ref[...], emit_y))


def _layer_sample(x, p, tabs, k_prefix, v_prefix, h0r, h0i, wts, consts):
    n, t, _ = x.shape
    assert t == CHUNK and tabs[0].shape[0] == n * t
    rows = n * t
    (gain, w_in, sinks, woa, d_skip, w_glu, wos, wout, wpg, wpp, fgain) = wts
    wlag, a8r, a8i, pr, pi, ck = consts
    operands = (x.reshape(rows, D_MODEL), p.reshape(rows, PLE_DIM), *tabs,
                _both_halves(k_prefix), _both_halves(v_prefix), h0r, h0i,
                gain, w_in, wlag, a8r, a8i, pr, pi, ck, d_skip, w_glu, woa, wos, wout, wpg, wpp, fgain)
    out_shapes = [(rows, D_MODEL), (rows, KV_WIDTH), (rows, KV_WIDTH), h0r.shape, h0r.shape]
    scratch = [((n, KEYS, KV2_WIDTH), BF16), ((n, KEYS, KV2_WIDTH), BF16), ((rows, ATTN_WIDTH), BF16),
               ((rows + LAGS, SSM_WIDTH), F32), ((rows, N_STATE), BF16)]
    vmem = (sum(a.size * a.dtype.itemsize for a in operands)
            + sum(int(np.prod(s)) * 4 for s in out_shapes)
            + sum(int(np.prod(s)) * np.dtype(d).itemsize for s, d in scratch)
            + 3 * rows * IN_WIDTH * 4)
    y, k, v, hr, hi = pl.pallas_call(
        functools.partial(_sample_kernel, n=n, t=t),
        grid=(1,),
        in_specs=[pl.BlockSpec(memory_space=pltpu.SMEM)] + [_const_spec(a.shape) for a in operands],
        out_specs=[pl.BlockSpec(s, lambda i, nd=len(s): (0,) * nd) for s in out_shapes],
        out_shape=[jax.ShapeDtypeStruct(s, F32) for s in out_shapes],
        scratch_shapes=[pltpu.VMEM(s, d) for s, d in scratch],
        compiler_params=_params(vmem, 1),
        name="layer_sample",
    )(sinks, *operands)
    k_new = jnp.concatenate([k_prefix[:, t:], k.reshape(n, t, KV_WIDTH)], axis=1)
    v_new = jnp.concatenate([v_prefix[:, t:], v.reshape(n, t, KV_WIDTH)], axis=1)
    return y.reshape(n, t, D_MODEL), k_new, v_new, hr, hi


def _column_blocks(w):
    k, n = w.shape
    return w.reshape(k, n // OUT_COLS, OUT_COLS).transpose(1, 0, 2).astype(BF16)


def _both_halves(a):
    h0, h1 = a[..., :HEAD_DIM], a[..., HEAD_DIM:]
    return jnp.concatenate([h0, h0, h1, h1], axis=-1).astype(BF16)


def _ssm_constants(a_re, a_im, log_dt, b_re, b_im, c_re, c_im):
    dt = jnp.exp(log_dt.astype(F32))[:, None]
    lr = a_re.astype(F32).reshape(PAIRS, 1, 1, LANES)
    li = a_im.astype(F32).reshape(PAIRS, 1, 1, LANES)
    xr = (a_re.astype(F32) * dt).reshape(PAIRS, 1, 1, LANES)
    xi = (a_im.astype(F32) * dt).reshape(PAIRS, 1, 1, LANES)

    def apow(n):
        mag = jnp.exp(xr * n)
        return mag * jnp.cos(xi * n), mag * jnp.sin(xi * n)

    ar, ai = apow(1.0)
    nr, ni = ar - 1.0, ai
    den = lr * lr + li * li
    fr, fi = (nr * lr + ni * li) / den, (ni * lr - nr * li) / den

    n_slots = PAIR_K // SLOT
    qq, hi_ = np.arange(PAIRS)[:, None], np.arange(n_slots)[None, :]
    lag_tab = (PAIRS_PER_TILE * (hi_ // PAIRS_PER_TILE) + (hi_ % PAIRS_PER_TILE - qq) % PAIRS_PER_TILE)
    lag_tab = lag_tab.astype(np.float32)[:, :, None, None]
    same_group = (np.arange(SLOT)[:, None] // SSM_GROUP == np.arange(LANES)[None, :] // SSM_STATE)
    same_group = same_group.astype(np.float32)

    def b_rows(bm):
        t = jnp.transpose(bm.astype(F32).reshape(PAIRS, 2, SSM_STATE, SSM_GROUP), (0, 3, 1, 2))
        t = t.reshape(PAIRS, 1, 1, SSM_GROUP, LANES)
        t = jnp.broadcast_to(t, (PAIRS, 1, 2, SSM_GROUP, LANES))
        return t.reshape(PAIRS, 1, SLOT, LANES) * same_group

    br, bi = b_rows(b_re), b_rows(b_im)
    bbr, bbi = fr * br - fi * bi, fr * bi + fi * br
    er, ei = apow(lag_tab)
    wlag = jnp.concatenate([(er * bbr - ei * bbi).reshape(PAIRS, PAIR_K, LANES),
                            (er * bbi + ei * bbr).reshape(PAIRS, PAIR_K, LANES)], axis=-1).astype(BF16)

    a8r, a8i = (a.reshape(PAIRS, LANES) for a in apow(float(LAGS)))
    pwr, pwi = (a.reshape(PAIRS, LAGS, LANES)
                for a in apow(np.arange(1, LAGS + 1, dtype=np.float32)[None, :, None, None]))

    def c_cols(c):
        t = jnp.transpose(c.astype(F32).reshape(U_TILES, LANES // SSM_GROUP, SSM_GROUP, SSM_STATE),
                          (0, 3, 1, 2))
        return t.reshape(U_TILES, 1, 1, 1, SSM_STATE, LANES)

    cols_group = np.arange(LANES) // SSM_GROUP
    rows_group = 2 * np.arange(PAIRS_PER_TILE)[:, None] + np.arange(2)[None, :]
    c_mask = (rows_group[:, None, :, None, None] == cols_group[None, None, None, None, :])
    c_mask = c_mask.astype(np.float32)[None]
    ck = jnp.concatenate([c_cols(c_re) * c_mask, -c_cols(c_im) * c_mask], axis=2)
    ck = ck.reshape(U_TILES, PAIRS_PER_TILE * PAIR_N, LANES).astype(BF16)
    return wlag, a8r, a8i, pwr, pwi, ck


def _rope_tables(pos0, t, rows):
    half = ROT_DIM // 2
    d = np.arange(LANES) % HEAD_DIM
    inv = jnp.power(ROPE_THETA, -jnp.arange(half, dtype=F32) * 2.0 / ROT_DIM)
    pos = (pos0 + jnp.arange(t)).astype(F32)
    ang = pos[:, None] * inv[None, :]
    cos, sin = (jnp.tile(a, (1, LANES // half)) for a in (jnp.cos(ang), jnp.sin(ang)))
    cos_t = jnp.where((d < ROT_DIM)[None, :], cos, 1.0)
    sina = jnp.where(((d >= half) & (d < ROT_DIM))[None, :], sin, 0.0)
    sinb = jnp.where((d < half)[None, :], -sin, 0.0)
    reps = (max(rows // t, 1), 1)
    return tuple(jnp.tile(a, reps) for a in (cos_t, sina, sinb))


def kernel(x_prompt, x_sample, p_prompt, p_sample, cache_attn_k, cache_attn_v, state_ssm_re,
           state_ssm_im, norm_gain, w_in, attn_sinks, w_o_attn, ssm_a_re, ssm_a_im, ssm_log_dt,
           ssm_b_re, ssm_b_im, ssm_c_re, ssm_c_im, ssm_d, ssm_w_glu, w_o_ssm, w_out,
           w_ple_gate, w_ple_proj, final_norm_gain):
    assert norm_gain.shape[0] == 1, "single-layer model"
    bp, tp, _ = x_prompt.shape
    bs, ts, _ = x_sample.shape
    wts = (norm_gain[0].reshape(1, D_MODEL).astype(F32), w_in[0].astype(BF16),
           attn_sinks[0].astype(F32), _column_blocks(w_o_attn[0]),
           ssm_d[0].reshape(1, SSM_WIDTH).astype(F32), ssm_w_glu[0].astype(BF16),
           _column_blocks(w_o_ssm[0]), _column_blocks(w_out[0]), _column_blocks(w_ple_gate[0]),
           _column_blocks(w_ple_proj[0]), final_norm_gain.reshape(1, D_MODEL).astype(F32))
    consts = _ssm_constants(ssm_a_re[0], ssm_a_im[0], ssm_log_dt[0], ssm_b_re[0], ssm_b_im[0],
                            ssm_c_re[0], ssm_c_im[0])

    y_p, k_p, v_p, hr_p, hi_p = _layer_fused(x_prompt, p_prompt[0], _rope_tables(0, tp, tp), wts, consts)

    ck = cache_attn_k[0].reshape(bs, WINDOW, KV_WIDTH).astype(F32)
    cv = cache_attn_v[0].reshape(bs, WINDOW, KV_WIDTH).astype(F32)
    h0r = state_ssm_re[0].reshape(bs, PAIRS, LANES).astype(F32)
    h0i = state_ssm_im[0].reshape(bs, PAIRS, LANES).astype(F32)
    tabs_s = _rope_tables(PAST_LEN, ts, bs * ts)
    y_s, k_s, v_s, hr_s, hi_s = _layer_sample(x_sample, p_sample[0], tabs_s, ck, cv, h0r, h0i, wts, consts)

    def kv_out(a, b):
        return a.reshape(1, b, WINDOW, N_KV_HEADS, HEAD_DIM)

    def st_out(a, b):
        return a.reshape(1, b, SSM_GROUPS, SSM_STATE)

    return (y_p, y_s, kv_out(k_p, bp), kv_out(v_p, bp),
            st_out(hr_p, bp), st_out(hi_p, bp), kv_out(k_s, bs), kv_out(v_s, bs),
            st_out(hr_s, bs), st_out(hi_s, bs))
```

```python
import functools

import numpy as np
import jax
import jax.numpy as jnp
from jax import lax
from jax.experimental import pallas as pl
from jax.experimental.pallas import tpu as pltpu

F32 = jnp.float32
BF16 = jnp.bfloat16

LANES = 128
SUBLANES = 8
V7X_VMEM_BYTES = 64 * 1024 * 1024

D_MODEL = 1024
CHUNK = 64
WINDOW = 128
N_HEADS = 8
N_KV_HEADS = 2
HEAD_DIM = 64
Q_PER_KV = N_HEADS // N_KV_HEADS
LOG2E = 1.4426950408889634
Q_SCALE = HEAD_DIM ** -0.5 * LOG2E
ATTN_WIDTH = N_HEADS * HEAD_DIM
KV_WIDTH = N_KV_HEADS * HEAD_DIM
ROT_DIM = HEAD_DIM // 4
ROPE_THETA = 500000.0
SSM_WIDTH = D_MODEL // 2
SSM_GROUP = 16
SSM_GROUPS = SSM_WIDTH // SSM_GROUP
SSM_STATE = 64
PLE_DIM = 256
PAST_LEN = 1024
EPS = 1e-6

O_Q = 0
O_K = O_Q + ATTN_WIDTH
O_V = O_K + KV_WIDTH
O_ZA = O_V + KV_WIDTH
O_U = O_ZA + ATTN_WIDTH
O_ZS = O_U + SSM_WIDTH
O_GA = O_ZS + SSM_WIDTH
O_GS = O_GA + D_MODEL
IN_WIDTH = O_GS + D_MODEL

QM_WIDTH = N_HEADS * LANES
KV2_WIDTH = N_KV_HEADS * LANES
KEYS = WINDOW + CHUNK
LAGS = SUBLANES
PAIRS = SSM_GROUPS // 2
PAIR_K = 2 * LAGS * SSM_GROUP
PAIR_N = 2 * 2 * SSM_STATE
N_STATE = PAIRS * PAIR_N
U_TILES = SSM_WIDTH // LANES
PAIRS_PER_TILE = PAIRS // U_TILES
SLOT = 2 * SSM_GROUP
BF16_ROWS = 2 * SUBLANES
assert PAIRS_PER_TILE == 4 and LAGS == 2 * PAIRS_PER_TILE

OUT_COLS = 4 * LANES
OUT_BLOCKS = D_MODEL // OUT_COLS
N_OUT_WEIGHTS = 5
N_OUT_REFS = N_OUT_WEIGHTS * OUT_BLOCKS
LAYER_ROWS = 512


def _sigmoid(x):
    return 1.0 / (1.0 + jnp.exp2(x * (-LOG2E)))


def _const_spec(shape):
    zeros = (0,) * len(shape)
    return pl.BlockSpec(shape, lambda *_: zeros, pipeline_mode=pl.Buffered(1))


def _column_block_specs(ws):
    return [pl.BlockSpec((w.shape[0], OUT_COLS), lambda *_, j=j: (0, j), pipeline_mode=pl.Buffered(1))
            for w in ws for j in range(OUT_BLOCKS)]


def _column_block_operands(ws):
    return [w for w in ws for _ in range(OUT_BLOCKS)]


def _params(vmem_bytes, n_grid):
    return pltpu.CompilerParams(
        dimension_semantics=("arbitrary",) * n_grid,
        vmem_limit_bytes=min(int(vmem_bytes), V7X_VMEM_BYTES - 8 * 1024 * 1024),
    )


def _run(steps):
    for step in steps:
        step()


def _spread(main, other):
    merged, j = [], 0
    for i, step in enumerate(main):
        while j < len(other) and j * len(main) <= i * len(other):
            merged.append(other[j])
            j += 1
        merged.append(step)
    return merged + other[j:]


def _proj_steps(get_x, gain, get_tabs, w_ref, o, store=None):
    st = {}

    def norm():
        x = get_x()
        ms = jnp.mean(x * x, axis=-1, keepdims=True)
        st["xn"] = (x * lax.rsqrt(ms + EPS) * gain).astype(BF16)
        st["lo"] = lax.broadcasted_iota(jnp.int32, (x.shape[0], LANES), 1) < HEAD_DIM

    def seg(a, b):
        return jnp.dot(st["xn"], w_ref[:, a:b], preferred_element_type=F32)

    def rope(t):
        cos, sina, sinb = get_tabs()
        return (t * cos + pltpu.roll(t, ROT_DIM // 2, 1) * sina
                + pltpu.roll(t, LANES - ROT_DIM // 2, 1) * sinb)

    def both_halves(t):
        tr = pltpu.roll(t, HEAD_DIM, 1)
        return [jnp.where(st["lo"], t, tr).astype(BF16), jnp.where(st["lo"], tr, t).astype(BF16)]

    def done(name):
        if store is not None:
            store(name)

    def q():
        zq = seg(O_Q, O_K)
        o["qh"] = []
        for j in range(ATTN_WIDTH // LANES):
            qt = rope(zq[:, j * LANES:(j + 1) * LANES]) * Q_SCALE
            o["qh"] += [jnp.where(st["lo"], qt, 0.0).astype(BF16),
                        jnp.where(st["lo"], 0.0, qt).astype(BF16)]

    def kv():
        z = seg(O_K, O_ZA)
        o["k"] = rope(z[:, :KV_WIDTH])
        o["k2"] = both_halves(o["k"])
        done("k")
        o["v"] = z[:, KV_WIDTH:]
        o["v2"] = both_halves(o["v"])
        done("v")

    def za():
        z = seg(O_ZA, O_U)
        o["sa"] = z * _sigmoid(z)

    def u():
        o["u"] = seg(O_U, O_ZS)
        done("u")

    def zs():
        z = seg(O_ZS, O_GA)
        o["sz"] = z * _sigmoid(z)

    def ga():
        o["ga"] = _sigmoid(seg(O_GA, O_GS))

    def gs():
        o["gs"] = _sigmoid(seg(O_GS, IN_WIDTH))

    return [norm, q, kv, za, u, zs, ga, gs]


def _attn_steps(sinks_ref, chunks, get_q, get_kv, get_valid, emit):
    nt = (((1,), (1,)), ((), ()))
    units = [(c, kv) for c in chunks for kv in range(N_KV_HEADS)]
    n = len(units)
    st = {}

    def scores(c, kv):
        k2, v2 = get_kv(c, kv)
        qm = jnp.concatenate([get_q(c, kv * Q_PER_KV + h) for h in range(Q_PER_KV)], axis=0)
        s = lax.dot_general(qm, k2, nt, preferred_element_type=F32)
        valid = get_valid(c)
        if valid is not None:
            s = jnp.where(valid, s, -jnp.inf)
        return s, v2

    def softmax(s, kv):
        head_row = lax.broadcasted_iota(jnp.int32, (Q_PER_KV * CHUNK, 1), 0) // CHUNK
        sk = [sinks_ref[kv * Q_PER_KV + h] * LOG2E for h in range(Q_PER_KV)]
        sink = jnp.where(head_row == 0, sk[0],
                         jnp.where(head_row == 1, sk[1], jnp.where(head_row == 2, sk[2], sk[3])))
        m = jnp.maximum(jnp.max(s, axis=1, keepdims=True), sink)
        return jnp.exp2(s - m).astype(BF16), jnp.exp2(sink - m)

    def output(e, sink_term, v2, c, kv):
        lo_q = lax.broadcasted_iota(jnp.int32, (CHUNK, LANES), 1) < HEAD_DIM
        ones = jnp.ones((KEYS, LANES), BF16)
        pv = jnp.dot(e, jnp.concatenate([v2, ones], axis=1), preferred_element_type=F32)
        o = pv[:, :LANES] / (pv[:, LANES:] + sink_term)
        for j in range(Q_PER_KV // 2):
            even = o[2 * j * CHUNK:(2 * j + 1) * CHUNK]
            odd = o[(2 * j + 1) * CHUNK:(2 * j + 2) * CHUNK]
            emit(c, kv * (Q_PER_KV // 2) + j, jnp.where(lo_q, even, odd))

    def make(i):
        def step():
            if i < n:
                st[i] = scores(*units[i])
            if 0 <= i - 1 < n:
                s, v2 = st[i - 1]
                st[i - 1] = softmax(s, units[i - 1][1]) + (v2,)
            if 0 <= i - 2 < n:
                e, den, v2 = st.pop(i - 2)
                output(e, den, v2, *units[i - 2])
        return step

    return [make(i) for i in range(n + 2)]


def _ssm_reset(ubuf, cr_s, ci_s):
    ubuf[0:LAGS, :] = jnp.zeros((LAGS, SSM_WIDTH), F32)
    cr_s[...] = jnp.zeros(cr_s.shape, F32)
    ci_s[...] = jnp.zeros(ci_s.shape, F32)


def _ssm_steps(ubuf, row0, tt, cr_s, ci_s, hs, wlag_ref, a8r_ref, a8i_ref, ck_ref, hr_ref, hi_ref, o,
               seg=None):
    st = {"ys": []}

    def setup():
        if seg is not None:
            st["row_in_seg"] = lax.broadcasted_iota(jnp.int32, (tt, LANES), 0) % seg[0]
        slot = lax.broadcasted_iota(jnp.int32, (tt, LANES), 1) // SLOT
        st["to_low"] = [((slot + PAIRS_PER_TILE - s) % PAIRS_PER_TILE) < 2 for s in range(2)]
        st["same_parity"] = [((slot + sg) % 2) == 0 for sg in range(2)]

    def route(r):
        low = [jnp.where(st["to_low"][s], r[s], r[s + 2]) for s in range(2)]
        high = [jnp.where(st["to_low"][s], r[s + 2], r[s]) for s in range(2)]
        return [jnp.where(st["same_parity"][sg % 2], src[0], src[1])
                for sg, src in zip(range(PAIRS_PER_TILE), (low, low, high, high))]

    def lag_copies(k):
        def step():
            ub = ubuf[row0:row0 + LAGS + tt, k * LANES:(k + 1) * LANES]
            rolled = []
            for s in range(LAGS):
                us = ub[LAGS:] if s == 0 else pltpu.roll(ub, s, 0)[LAGS:]
                if seg is not None and s > 0:
                    us = jnp.where(st["row_in_seg"] >= s, us, 0.0)
                if s % PAIRS_PER_TILE:
                    us = pltpu.roll(us, SLOT * (s % PAIRS_PER_TILE), 1)
                rolled.append(us)
            st["halves"] = (route(rolled[:PAIRS_PER_TILE]), route(rolled[PAIRS_PER_TILE:]))
        return step

    def pair(k, sg):
        def step():
            q = k * PAIRS_PER_TILE + sg
            xl = jnp.concatenate([st["halves"][0][sg], st["halves"][1][sg]], axis=1).astype(BF16)
            w = jnp.dot(xl, wlag_ref[q], preferred_element_type=F32)
            ar, ai = a8r_ref[q:q + 1, :], a8i_ref[q:q + 1, :]
            if seg is None:
                cr, ci = cr_s[q], ci_s[q]
            else:
                seg_rows, h0r_ref, h0i_ref, pr_ref, pi_ref = seg
            for b2 in range(tt // BF16_ROWS):
                hrs, his = [], []
                for b in (2 * b2, 2 * b2 + 1):
                    if seg is not None and (b * SUBLANES) % seg_rows == 0:
                        n = b * SUBLANES // seg_rows
                        h0r, h0i = h0r_ref[n, q:q + 1, :], h0i_ref[n, q:q + 1, :]
                        cr = pr_ref[q] * h0r - pi_ref[q] * h0i
                        ci = pr_ref[q] * h0i + pi_ref[q] * h0r
                    blk = slice(b * SUBLANES, (b + 1) * SUBLANES)
                    hr = w[blk, :LANES] + cr
                    hi = w[blk, LANES:] + ci
                    cr = ar * hr - ai * hi
                    ci = ar * hi + ai * hr
                    hrs.append(hr)
                    his.append(hi)
                    if seg is not None and ((b + 1) * SUBLANES) % seg_rows == 0:
                        n = b * SUBLANES // seg_rows
                        hr_ref[n, q:q + 1, :] = hr[SUBLANES - 1:, :]
                        hi_ref[n, q:q + 1, :] = hi[SUBLANES - 1:, :]
                blk2 = slice(b2 * BF16_ROWS, (b2 + 1) * BF16_ROWS)
                hs[blk2, q * PAIR_N:q * PAIR_N + LANES] = jnp.concatenate(hrs, axis=0).astype(BF16)
                hs[blk2, q * PAIR_N + LANES:(q + 1) * PAIR_N] = jnp.concatenate(his, axis=0).astype(BF16)
            if seg is None:
                cr_s[q] = cr
                ci_s[q] = ci
                hr_ref[0, q:q + 1, :] = hr[SUBLANES - 1:, :]
                hi_ref[0, q:q + 1, :] = hi[SUBLANES - 1:, :]
        return step

    def c_proj(k):
        def step():
            cols = slice(k * PAIRS_PER_TILE * PAIR_N, (k + 1) * PAIRS_PER_TILE * PAIR_N)
            st["ys"].append(jnp.dot(hs[:, cols], ck_ref[k], preferred_element_type=F32))
            if k == U_TILES - 1:
                o["y"] = jnp.concatenate(st["ys"], axis=1)
        return step

    steps = [setup]
    for k in range(U_TILES):
        steps += [lag_copies(k)] + [pair(k, sg) for sg in range(PAIRS_PER_TILE)] + [c_proj(k)]
    return steps


def _glu(y, u, sz, d, wglu_ref):
    z = jax.nn.gelu(y + d * u)
    g = jnp.dot(z.astype(BF16), wglu_ref[...], preferred_element_type=F32)
    return z * _sigmoid(g) * sz


def _out_steps(src, w_refs, fgain, emit):
    st = {}
    woa, wos, wout, wpg, wpp = (w_refs[i * OUT_BLOCKS:(i + 1) * OUT_BLOCKS] for i in range(N_OUT_WEIGHTS))

    def mm(a, blocks):
        a = a.astype(BF16)
        return jnp.concatenate([jnp.dot(a, w_ref[...], preferred_element_type=F32) for w_ref in blocks], axis=1)

    def branches():
        st["merged"] = src["ga"]() * mm(src["xa"](), woa) + src["gs"]() * mm(src["xs"](), wos)

    def residual():
        st["h"] = src["x"]() + mm(st.pop("merged"), wout)

    def embed_gate():
        h = st.pop("h")
        st["h"] = h + _sigmoid(mm(h, wpg)) * mm(src["p"](), wpp)

    def norm():
        h = st.pop("h")
        ms = jnp.mean(h * h, axis=-1, keepdims=True)
        emit(h * lax.rsqrt(ms + EPS) * fgain)

    return [branches, residual, embed_gate, norm]


def _layer_kernel(sinks_ref, x_ref, p_ref, cos_ref, sina_ref, sinb_ref, gain_ref, w_in_ref,
                  wlag_ref, a8r_ref, a8i_ref, ck_ref, d_ref, wglu_ref, fg_ref, *rest, tt):
    out_w, (y_ref, k_ref, v_ref, hr_ref, hi_ref,
            kbuf, vbuf, xa_s, ubuf, cr_s, ci_s, hs) = rest[:N_OUT_REFS], rest[N_OUT_REFS:]
    t = pl.program_id(1)
    half = tt // 2

    @pl.when(t == 0)
    def _():
        kbuf[0:WINDOW, :] = jnp.zeros((WINDOW, KV2_WIDTH), BF16)
        vbuf[0:WINDOW, :] = jnp.zeros((WINDOW, KV2_WIDTH), BF16)
        _ssm_reset(ubuf, cr_s, ci_s)

    pj, so = [{}, {}], [{}, {}]

    def rows(h):
        return slice(h * half, (h + 1) * half)

    def proj(h):
        def store(name):
            if name == "u":
                ubuf[LAGS + h * half:LAGS + (h + 1) * half, :] = pj[h]["u"]
                return
            last_ref, buf = (k_ref, kbuf) if name == "k" else (v_ref, vbuf)
            if h == 1:
                last_ref[0] = pj[h][name][half - WINDOW:]
            for j in range(N_KV_HEADS):
                buf[WINDOW + h * half:WINDOW + (h + 1) * half, j * LANES:(j + 1) * LANES] = pj[h][name + "2"][j]

        return _proj_steps(lambda: x_ref[0, rows(h)], gain_ref[...],
                           lambda: (cos_ref[rows(h)], sina_ref[rows(h)], sinb_ref[rows(h)]),
                           w_in_ref, pj[h], store)

    def mid(h):
        def get_q(c, head):
            r0 = c * CHUNK - h * half
            return pj[h]["qh"][head][r0:r0 + CHUNK]

        def get_kv(c, kv):
            krows, cols = slice(c * CHUNK, c * CHUNK + KEYS), slice(kv * LANES, (kv + 1) * LANES)
            return kbuf[krows, cols], vbuf[krows, cols]

        def get_valid(c):
            if c * CHUNK >= WINDOW:
                return None
            in_seq = c * CHUNK + lax.broadcasted_iota(jnp.int32, (1, KEYS), 1) >= WINDOW
            return jnp.logical_or(in_seq, t > 0)

        def emit(c, tile, o):
            r0, cols = c * CHUNK - h * half, slice(tile * LANES, (tile + 1) * LANES)
            xa_s[c * CHUNK:(c + 1) * CHUNK, cols] = (o * pj[h]["sa"][r0:r0 + CHUNK, cols]).astype(BF16)

        def glu():
            so[h]["xs"] = _glu(so[h].pop("y"), pj[h]["u"], pj[h]["sz"], d_ref[...], wglu_ref)

        chunks = range(h * half // CHUNK, (h + 1) * half // CHUNK)
        return (_attn_steps(sinks_ref, chunks, get_q, get_kv, get_valid, emit)
                + _ssm_steps(ubuf, h * half, half, cr_s, ci_s, hs, wlag_ref, a8r_ref, a8i_ref, ck_ref,
                             hr_ref, hi_ref, so[h])
                + [glu])

    def out(h):
        src = dict(xa=lambda: xa_s[rows(h)], xs=lambda: so[h]["xs"], ga=lambda: pj[h]["ga"],
                   gs=lambda: pj[h]["gs"], x=lambda: x_ref[0, rows(h)], p=lambda: p_ref[0, rows(h)])

        def emit(y):
            y_ref[0, rows(h)] = y

        return _out_steps(src, out_w, fg_ref[...], emit)

    _run(proj(0))
    _run(_spread(mid(0), proj(1)))
    _run(_spread(mid(1), out(0)))
    _run(out(1))

    kbuf[0:WINDOW, :] = kbuf[tt:tt + WINDOW, :]
    vbuf[0:WINDOW, :] = vbuf[tt:tt + WINDOW, :]
    ubuf[0:LAGS, :] = ubuf[tt:tt + LAGS, :]


def _layer_fused(x, p, tabs, wts, consts):
    b, t, _ = x.shape
    tt = min(LAYER_ROWS, t)
    assert t % tt == 0 and tt // 2 >= WINDOW and (tt // 2) % BF16_ROWS == 0 and tabs[0].shape[0] == t
    (gain, w_in, sinks, woa, d_skip, w_glu, wos, wout, wpg, wpp, fgain) = wts
    wlag, a8r, a8i, _, _, ck = consts

    def row_spec(w):
        return pl.BlockSpec((1, tt, w), lambda i, j: (i, j, 0))

    tab_spec = pl.BlockSpec((tt, LANES), lambda i, j: (j, 0))
    st_spec = pl.BlockSpec((1, PAIRS, LANES), lambda i, j: (i, 0, 0))
    win_spec = pl.BlockSpec((1, WINDOW, KV_WIDTH), lambda i, j: (i, 0, 0))
    consts_in = (gain, w_in, wlag, a8r, a8i, ck, d_skip, w_glu, fgain)
    out_w = (woa, wos, wout, wpg, wpp)
    vmem = (sum(a.size * a.dtype.itemsize for a in consts_in + out_w)
            + 2 * tt * (2 * D_MODEL + PLE_DIM + 2 * KV_WIDTH + 3 * LANES) * 4
            + 2 * (WINDOW + tt) * KV2_WIDTH * 2 + tt * ATTN_WIDTH * 2 + (tt + LAGS) * SSM_WIDTH * 4
            + (tt // 2) * N_STATE * 2 + 3 * tt * IN_WIDTH * 4)
    y, k, v, hr, hi = pl.pallas_call(
        functools.partial(_layer_kernel, tt=tt),
        grid=(b, t // tt),
        in_specs=[pl.BlockSpec(memory_space=pltpu.SMEM), row_spec(D_MODEL), row_spec(PLE_DIM),
                  tab_spec, tab_spec, tab_spec]
                 + [_const_spec(a.shape) for a in consts_in] + _column_block_specs(out_w),
        out_specs=[row_spec(D_MODEL), win_spec, win_spec, st_spec, st_spec],
        out_shape=[jax.ShapeDtypeStruct((b, t, D_MODEL), F32),
                   jax.ShapeDtypeStruct((b, WINDOW, KV_WIDTH), F32),
                   jax.ShapeDtypeStruct((b, WINDOW, KV_WIDTH), F32),
                   jax.ShapeDtypeStruct((b, PAIRS, LANES), F32),
                   jax.ShapeDtypeStruct((b, PAIRS, LANES), F32)],
        scratch_shapes=[pltpu.VMEM((WINDOW + tt, KV2_WIDTH), BF16),
                        pltpu.VMEM((WINDOW + tt, KV2_WIDTH), BF16),
                        pltpu.VMEM((tt, ATTN_WIDTH), BF16),
                        pltpu.VMEM((tt + LAGS, SSM_WIDTH), F32),
                        pltpu.VMEM((PAIRS, SUBLANES, LANES), F32),
                        pltpu.VMEM((PAIRS, SUBLANES, LANES), F32),
                        pltpu.VMEM((tt // 2, N_STATE), BF16)],
        compiler_params=_params(vmem, 2),
        name="layer_prompt",
    )(sinks, x, p, *tabs, *consts_in, *_column_block_operands(out_w))
    return y, k, v, hr, hi


def _sample_kernel(sinks_ref, x_ref, p_ref, cos_ref, sina_ref, sinb_ref, kpre_ref, vpre_ref,
                   h0r_ref, h0i_ref, gain_ref, w_in_ref, wlag_ref, a8r_ref, a8i_ref, pr_ref, pi_ref,
                   ck_ref, d_ref, wglu_ref, fg_ref, *rest, n, t):
    out_w, (y_ref, k_ref, v_ref, hr_ref, hi_ref,
            kbuf, vbuf, xa_s, ubuf, hs) = rest[:N_OUT_REFS], rest[N_OUT_REFS:]
    rows = n * t
    pj, so = {}, {}

    def store(name):
        if name == "u":
            ubuf[0:LAGS, :] = jnp.zeros((LAGS, SSM_WIDTH), F32)
            ubuf[LAGS:, :] = pj["u"]
            return
        full_ref, pre_ref, buf = (k_ref, kpre_ref, kbuf) if name == "k" else (v_ref, vpre_ref, vbuf)
        full_ref[...] = pj[name]
        for s in range(n):
            buf[s, 0:WINDOW, :] = pre_ref[s]
            for j in range(N_KV_HEADS):
                buf[s, WINDOW:WINDOW + t, j * LANES:(j + 1) * LANES] = pj[name + "2"][j][s * t:(s + 1) * t]

    def get_q(c, head):
        return pj["qh"][head][c * CHUNK:(c + 1) * CHUNK]

    def get_kv(c, kv):
        cols = slice(kv * LANES, (kv + 1) * LANES)
        return kbuf[c, :, cols], vbuf[c, :, cols]

    def emit(c, tile, o):
        r, cols = slice(c * CHUNK, (c + 1) * CHUNK), slice(tile * LANES, (tile + 1) * LANES)
        xa_s[r, cols] = (o * pj["sa"][r, cols]).astype(BF16)

    def glu():
        so["xs"] = _glu(so.pop("y"), pj["u"], pj["sz"], d_ref[...], wglu_ref)

    src = dict(xa=lambda: xa_s[...], xs=lambda: so["xs"], ga=lambda: pj["ga"], gs=lambda: pj["gs"],
               x=lambda: x_ref[...], p=lambda: p_ref[...])

    def emit_y(y):
        y_ref[...] = y

    _run(_proj_steps(lambda: x_ref[...], gain_ref[...],
                     lambda: (cos_ref[...], sina_ref[...], sinb_ref[...]), w_in_ref, pj, store))
    _run(_attn_steps(sinks_ref, range(n), get_q, get_kv, lambda c: None, emit))
    _run(_ssm_steps(ubuf, 0, rows, None, None, hs, wlag_ref, a8r_ref, a8i_ref, ck_ref, hr_ref, hi_ref, so,
                    seg=(t, h0r_ref, h0i_ref, pr_ref, pi_ref)) + [glu])
    _run(_out_steps(src, out_w, fg_ref[...], emit_y))


def _layer_sample(x, p, tabs, k_prefix, v_prefix, h0r, h0i, wts, consts):
    n, t, _ = x.shape
    assert t == CHUNK and tabs[0].shape[0] == n * t
    rows = n * t
    (gain, w_in, sinks, woa, d_skip, w_glu, wos, wout, wpg, wpp, fgain) = wts
    wlag, a8r, a8i, pr, pi, ck = consts
    operands = (x.reshape(rows, D_MODEL), p.reshape(rows, PLE_DIM), *tabs,
                _both_halves(k_prefix), _both_halves(v_prefix), h0r, h0i,
                gain, w_in, wlag, a8r, a8i, pr, pi, ck, d_skip, w_glu, fgain)
    out_w = (woa, wos, wout, wpg, wpp)
    out_shapes = [(rows, D_MODEL), (rows, KV_WIDTH), (rows, KV_WIDTH), h0r.shape, h0r.shape]
    scratch = [((n, KEYS, KV2_WIDTH), BF16), ((n, KEYS, KV2_WIDTH), BF16), ((rows, ATTN_WIDTH), BF16),
               ((rows + LAGS, SSM_WIDTH), F32), ((rows, N_STATE), BF16)]
    vmem = (sum(a.size * a.dtype.itemsize for a in operands + out_w)
            + sum(int(np.prod(s)) * 4 for s in out_shapes)
            + sum(int(np.prod(s)) * np.dtype(d).itemsize for s, d in scratch)
            + 3 * rows * IN_WIDTH * 4)
    y, k, v, hr, hi = pl.pallas_call(
        functools.partial(_sample_kernel, n=n, t=t),
        grid=(1,),
        in_specs=([pl.BlockSpec(memory_space=pltpu.SMEM)] + [_const_spec(a.shape) for a in operands]
                  + _column_block_specs(out_w)),
        out_specs=[pl.BlockSpec(s, lambda i, nd=len(s): (0,) * nd) for s in out_shapes],
        out_shape=[jax.ShapeDtypeStruct(s, F32) for s in out_shapes],
        scratch_shapes=[pltpu.VMEM(s, d) for s, d in scratch],
        compiler_params=_params(vmem, 1),
        name="layer_sample",
    )(sinks, *operands, *_column_block_operands(out_w))
    k_new = jnp.concatenate([k_prefix[:, t:], k.reshape(n, t, KV_WIDTH)], axis=1)
    v_new = jnp.concatenate([v_prefix[:, t:], v.reshape(n, t, KV_WIDTH)], axis=1)
    return y.reshape(n, t, D_MODEL), k_new, v_new, hr, hi


def _both_halves(a):
    h0, h1 = a[..., :HEAD_DIM], a[..., HEAD_DIM:]
    return jnp.concatenate([h0, h0, h1, h1], axis=-1).astype(BF16)


def _ssm_constants(a_re, a_im, log_dt, b_re, b_im, c_re, c_im):
    dt = jnp.exp(log_dt.astype(F32))[:, None]
    lr = a_re.astype(F32).reshape(PAIRS, 1, 1, LANES)
    li = a_im.astype(F32).reshape(PAIRS, 1, 1, LANES)
    xr = (a_re.astype(F32) * dt).reshape(PAIRS, 1, 1, LANES)
    xi = (a_im.astype(F32) * dt).reshape(PAIRS, 1, 1, LANES)

    def apow(n):
        mag = jnp.exp(xr * n)
        return mag * jnp.cos(xi * n), mag * jnp.sin(xi * n)

    ar, ai = apow(1.0)
    nr, ni = ar - 1.0, ai
    den = lr * lr + li * li
    fr, fi = (nr * lr + ni * li) / den, (ni * lr - nr * li) / den

    n_slots = PAIR_K // SLOT
    qq, hi_ = np.arange(PAIRS)[:, None], np.arange(n_slots)[None, :]
    lag_tab = (PAIRS_PER_TILE * (hi_ // PAIRS_PER_TILE) + (hi_ % PAIRS_PER_TILE - qq) % PAIRS_PER_TILE)
    lag_tab = lag_tab.astype(np.float32)[:, :, None, None]
    same_group = (np.arange(SLOT)[:, None] // SSM_GROUP == np.arange(LANES)[None, :] // SSM_STATE)
    same_group = same_group.astype(np.float32)

    def b_rows(bm):
        t = jnp.transpose(bm.astype(F32).reshape(PAIRS, 2, SSM_STATE, SSM_GROUP), (0, 3, 1, 2))
        t = t.reshape(PAIRS, 1, 1, SSM_GROUP, LANES)
        t = jnp.broadcast_to(t, (PAIRS, 1, 2, SSM_GROUP, LANES))
        return t.reshape(PAIRS, 1, SLOT, LANES) * same_group

    br, bi = b_rows(b_re), b_rows(b_im)
    bbr, bbi = fr * br - fi * bi, fr * bi + fi * br
    er, ei = apow(lag_tab)
    wlag = jnp.concatenate([(er * bbr - ei * bbi).reshape(PAIRS, PAIR_K, LANES),
                            (er * bbi + ei * bbr).reshape(PAIRS, PAIR_K, LANES)], axis=-1).astype(BF16)

    a8r, a8i = (a.reshape(PAIRS, LANES) for a in apow(float(LAGS)))
    pwr, pwi = (a.reshape(PAIRS, LAGS, LANES)
                for a in apow(np.arange(1, LAGS + 1, dtype=np.float32)[None, :, None, None]))

    def c_cols(c):
        t = jnp.transpose(c.astype(F32).reshape(U_TILES, LANES // SSM_GROUP, SSM_GROUP, SSM_STATE),
                          (0, 3, 1, 2))
        return t.reshape(U_TILES, 1, 1, 1, SSM_STATE, LANES)

    cols_group = np.arange(LANES) // SSM_GROUP
    rows_group = 2 * np.arange(PAIRS_PER_TILE)[:, None] + np.arange(2)[None, :]
    c_mask = (rows_group[:, None, :, None, None] == cols_group[None, None, None, None, :])
    c_mask = c_mask.astype(np.float32)[None]
    ck = jnp.concatenate([c_cols(c_re) * c_mask, -c_cols(c_im) * c_mask], axis=2)
    ck = ck.reshape(U_TILES, PAIRS_PER_TILE * PAIR_N, LANES).astype(BF16)
    return wlag, a8r, a8i, pwr, pwi, ck


def _rope_tables(pos0, t, rows):
    half = ROT_DIM // 2
    d = np.arange(LANES) % HEAD_DIM
    inv = jnp.power(ROPE_THETA, -jnp.arange(half, dtype=F32) * 2.0 / ROT_DIM)
    pos = (pos0 + jnp.arange(t)).astype(F32)
    ang = pos[:, None] * inv[None, :]
    cos, sin = (jnp.tile(a, (1, LANES // half)) for a in (jnp.cos(ang), jnp.sin(ang)))
    cos_t = jnp.where((d < ROT_DIM)[None, :], cos, 1.0)
    sina = jnp.where(((d >= half) & (d < ROT_DIM))[None, :], sin, 0.0)
    sinb = jnp.where((d < half)[None, :], -sin, 0.0)
    reps = (max(rows // t, 1), 1)
    return tuple(jnp.tile(a, reps) for a in (cos_t, sina, sinb))


def kernel(x_prompt, x_sample, p_prompt, p_sample, cache_attn_k, cache_attn_v, state_ssm_re,
           state_ssm_im, norm_gain, w_in, attn_sinks, w_o_attn, ssm_a_re, ssm_a_im, ssm_log_dt,
           ssm_b_re, ssm_b_im, ssm_c_re, ssm_c_im, ssm_d, ssm_w_glu, w_o_ssm, w_out,
           w_ple_gate, w_ple_proj, final_norm_gain):
    assert norm_gain.shape[0] == 1, "single-layer model"
    bp, tp, _ = x_prompt.shape
    bs, ts, _ = x_sample.shape
    wts = (norm_gain[0].reshape(1, D_MODEL).astype(F32), w_in[0].astype(BF16),
           attn_sinks[0].astype(F32), w_o_attn[0].astype(BF16),
           ssm_d[0].reshape(1, SSM_WIDTH).astype(F32), ssm_w_glu[0].astype(BF16),
           w_o_ssm[0].astype(BF16), w_out[0].astype(BF16), w_ple_gate[0].astype(BF16),
           w_ple_proj[0].astype(BF16), final_norm_gain.reshape(1, D_MODEL).astype(F32))
    consts = _ssm_constants(ssm_a_re[0], ssm_a_im[0], ssm_log_dt[0], ssm_b_re[0], ssm_b_im[0],
                            ssm_c_re[0], ssm_c_im[0])

    y_p, k_p, v_p, hr_p, hi_p = _layer_fused(x_prompt, p_prompt[0], _rope_tables(0, tp, tp), wts, consts)

    ck = cache_attn_k[0].reshape(bs, WINDOW, KV_WIDTH).astype(F32)
    cv = cache_attn_v[0].reshape(bs, WINDOW, KV_WIDTH).astype(F32)
    h0r = state_ssm_re[0].reshape(bs, PAIRS, LANES).astype(F32)
    h0i = state_ssm_im[0].reshape(bs, PAIRS, LANES).astype(F32)
    tabs_s = _rope_tables(PAST_LEN, ts, bs * ts)
    y_s, k_s, v_s, hr_s, hi_s = _layer_sample(x_sample, p_sample[0], tabs_s, ck, cv, h0r, h0i, wts, consts)

    def kv_out(a, b):
        return a.reshape(1, b, WINDOW, N_KV_HEADS, HEAD_DIM)

    def st_out(a, b):
        return a.reshape(1, b, SSM_GROUPS, SSM_STATE)

    return (y_p, y_s, kv_out(k_p, bp), kv_out(v_p, bp),
            st_out(hr_p, bp), st_out(hi_p, bp), kv_out(k_s, bs), kv_out(v_s, bs),
            st_out(hr_s, bs), st_out(hi_s, bs))
```

```python
import functools

import numpy as np
import jax
import jax.numpy as jnp
from jax import lax
from jax.experimental import pallas as pl
from jax.experimental.pallas import tpu as pltpu

F32 = jnp.float32
BF16 = jnp.bfloat16

LANES = 128
SUBLANES = 8
V7X_VMEM_BYTES = 64 * 1024 * 1024

D_MODEL = 1024
CHUNK = 64
WINDOW = 128
N_HEADS = 8
N_KV_HEADS = 2
HEAD_DIM = 64
Q_PER_KV = N_HEADS // N_KV_HEADS
LOG2E = 1.4426950408889634
Q_SCALE = HEAD_DIM ** -0.5 * LOG2E
ATTN_WIDTH = N_HEADS * HEAD_DIM
KV_WIDTH = N_KV_HEADS * HEAD_DIM
ROT_DIM = HEAD_DIM // 4
ROPE_THETA = 500000.0
SSM_WIDTH = D_MODEL // 2
SSM_GROUP = 16
SSM_GROUPS = SSM_WIDTH // SSM_GROUP
SSM_STATE = 64
PLE_DIM = 256
PAST_LEN = 1024
EPS = 1e-6

O_Q = 0
O_K = O_Q + ATTN_WIDTH
O_V = O_K + KV_WIDTH
O_ZA = O_V + KV_WIDTH
O_U = O_ZA + ATTN_WIDTH
O_ZS = O_U + SSM_WIDTH
O_GA = O_ZS + SSM_WIDTH
O_GS = O_GA + D_MODEL
IN_WIDTH = O_GS + D_MODEL

QM_WIDTH = N_HEADS * LANES
KV2_WIDTH = N_KV_HEADS * LANES
KEYS = WINDOW + CHUNK
LAGS = SUBLANES
PAIRS = SSM_GROUPS // 2
PAIR_K = 2 * LAGS * SSM_GROUP
PAIR_N = 2 * 2 * SSM_STATE
N_STATE = PAIRS * PAIR_N
U_TILES = SSM_WIDTH // LANES
PAIRS_PER_TILE = PAIRS // U_TILES
SLOT = 2 * SSM_GROUP
BF16_ROWS = 2 * SUBLANES
assert PAIRS_PER_TILE == 4 and LAGS == 2 * PAIRS_PER_TILE

OUT_COLS = 4 * LANES
OUT_BLOCKS = D_MODEL // OUT_COLS
N_OUT_WEIGHTS = 5
N_OUT_REFS = N_OUT_WEIGHTS * OUT_BLOCKS
LAYER_ROWS = 512


def _sigmoid(x):
    return 1.0 / (1.0 + jnp.exp2(x * (-LOG2E)))


def _const_spec(shape):
    zeros = (0,) * len(shape)
    return pl.BlockSpec(shape, lambda *_: zeros, pipeline_mode=pl.Buffered(1))


def _column_blocks(w):
    return jnp.stack([w[:, j * OUT_COLS:(j + 1) * OUT_COLS] for j in range(OUT_BLOCKS)]).astype(BF16)


def _column_block_specs(ws):
    return [pl.BlockSpec((None, w.shape[1], OUT_COLS), lambda *_, j=j: (j, 0, 0), pipeline_mode=pl.Buffered(1))
            for w in ws for j in range(OUT_BLOCKS)]


def _column_block_operands(ws):
    return [w for w in ws for _ in range(OUT_BLOCKS)]


def _params(vmem_bytes, n_grid):
    return pltpu.CompilerParams(
        dimension_semantics=("arbitrary",) * n_grid,
        vmem_limit_bytes=min(int(vmem_bytes), V7X_VMEM_BYTES - 8 * 1024 * 1024),
    )


def _run(steps):
    for step in steps:
        step()


def _spread(main, other):
    merged, j = [], 0
    for i, step in enumerate(main):
        while j < len(other) and j * len(main) <= i * len(other):
            merged.append(other[j])
            j += 1
        merged.append(step)
    return merged + other[j:]


def _proj_steps(get_x, gain, get_tabs, w_ref, o, store=None):
    st = {}

    def norm():
        x = get_x()
        ms = jnp.mean(x * x, axis=-1, keepdims=True)
        st["xn"] = (x * lax.rsqrt(ms + EPS) * gain).astype(BF16)
        st["lo"] = lax.broadcasted_iota(jnp.int32, (x.shape[0], LANES), 1) < HEAD_DIM

    def seg(a, b):
        return jnp.dot(st["xn"], w_ref[:, a:b], preferred_element_type=F32)

    def rope(t):
        cos, sina, sinb = get_tabs()
        return (t * cos + pltpu.roll(t, ROT_DIM // 2, 1) * sina
                + pltpu.roll(t, LANES - ROT_DIM // 2, 1) * sinb)

    def both_halves(t):
        tr = pltpu.roll(t, HEAD_DIM, 1)
        return [jnp.where(st["lo"], t, tr).astype(BF16), jnp.where(st["lo"], tr, t).astype(BF16)]

    def done(name):
        if store is not None:
            store(name)

    def q():
        zq = seg(O_Q, O_K)
        o["qh"] = []
        for j in range(ATTN_WIDTH // LANES):
            qt = rope(zq[:, j * LANES:(j + 1) * LANES]) * Q_SCALE
            o["qh"] += [jnp.where(st["lo"], qt, 0.0).astype(BF16),
                        jnp.where(st["lo"], 0.0, qt).astype(BF16)]

    def kv():
        z = seg(O_K, O_ZA)
        o["k"] = rope(z[:, :KV_WIDTH])
        o["k2"] = both_halves(o["k"])
        done("k")
        o["v"] = z[:, KV_WIDTH:]
        o["v2"] = both_halves(o["v"])
        done("v")

    def za():
        z = seg(O_ZA, O_U)
        o["sa"] = z * _sigmoid(z)

    def u():
        o["u"] = seg(O_U, O_ZS)
        done("u")

    def zs():
        z = seg(O_ZS, O_GA)
        o["sz"] = z * _sigmoid(z)

    def ga():
        o["ga"] = _sigmoid(seg(O_GA, O_GS))

    def gs():
        o["gs"] = _sigmoid(seg(O_GS, IN_WIDTH))

    return [norm, q, kv, za, u, zs, ga, gs]


def _attn_steps(sinks_ref, chunks, get_q, get_kv, get_valid, emit):
    nt = (((1,), (1,)), ((), ()))
    units = [(c, kv) for c in chunks for kv in range(N_KV_HEADS)]
    n = len(units)
    st = {}

    def scores(c, kv):
        k2, v2 = get_kv(c, kv)
        qm = jnp.concatenate([get_q(c, kv * Q_PER_KV + h) for h in range(Q_PER_KV)], axis=0)
        s = lax.dot_general(qm, k2, nt, preferred_element_type=F32)
        valid = get_valid(c)
        if valid is not None:
            s = jnp.where(valid, s, -jnp.inf)
        return s, v2

    def softmax(s, kv):
        head_row = lax.broadcasted_iota(jnp.int32, (Q_PER_KV * CHUNK, 1), 0) // CHUNK
        sk = [sinks_ref[kv * Q_PER_KV + h] * LOG2E for h in range(Q_PER_KV)]
        sink = jnp.where(head_row == 0, sk[0],
                         jnp.where(head_row == 1, sk[1], jnp.where(head_row == 2, sk[2], sk[3])))
        m = jnp.maximum(jnp.max(s, axis=1, keepdims=True), sink)
        return jnp.exp2(s - m).astype(BF16), jnp.exp2(sink - m)

    def output(e, sink_term, v2, c, kv):
        lo_q = lax.broadcasted_iota(jnp.int32, (CHUNK, LANES), 1) < HEAD_DIM
        ones = jnp.ones((KEYS, LANES), BF16)
        pv = jnp.dot(e, jnp.concatenate([v2, ones], axis=1), preferred_element_type=F32)
        o = pv[:, :LANES] / (pv[:, LANES:] + sink_term)
        for j in range(Q_PER_KV // 2):
            even = o[2 * j * CHUNK:(2 * j + 1) * CHUNK]
            odd = o[(2 * j + 1) * CHUNK:(2 * j + 2) * CHUNK]
            emit(c, kv * (Q_PER_KV // 2) + j, jnp.where(lo_q, even, odd))

    def make(i):
        def step():
            if i < n:
                st[i] = scores(*units[i])
            if 0 <= i - 1 < n:
                s, v2 = st[i - 1]
                st[i - 1] = softmax(s, units[i - 1][1]) + (v2,)
            if 0 <= i - 2 < n:
                e, den, v2 = st.pop(i - 2)
                output(e, den, v2, *units[i - 2])
        return step

    return [make(i) for i in range(n + 2)]


def _ssm_reset(ubuf, cr_s, ci_s):
    ubuf[0:LAGS, :] = jnp.zeros((LAGS, SSM_WIDTH), F32)
    cr_s[...] = jnp.zeros(cr_s.shape, F32)
    ci_s[...] = jnp.zeros(ci_s.shape, F32)


def _ssm_steps(ubuf, row0, tt, cr_s, ci_s, hs, wlag_ref, a8r_ref, a8i_ref, ck_ref, hr_ref, hi_ref, o,
               seg=None):
    st = {"ys": []}

    def setup():
        if seg is not None:
            st["row_in_seg"] = lax.broadcasted_iota(jnp.int32, (tt, LANES), 0) % seg[0]
        slot = lax.broadcasted_iota(jnp.int32, (tt, LANES), 1) // SLOT
        st["to_low"] = [((slot + PAIRS_PER_TILE - s) % PAIRS_PER_TILE) < 2 for s in range(2)]
        st["same_parity"] = [((slot + sg) % 2) == 0 for sg in range(2)]

    def route(r):
        low = [jnp.where(st["to_low"][s], r[s], r[s + 2]) for s in range(2)]
        high = [jnp.where(st["to_low"][s], r[s + 2], r[s]) for s in range(2)]
        return [jnp.where(st["same_parity"][sg % 2], src[0], src[1])
                for sg, src in zip(range(PAIRS_PER_TILE), (low, low, high, high))]

    def lag_copies(k):
        def step():
            ub = ubuf[row0:row0 + LAGS + tt, k * LANES:(k + 1) * LANES]
            rolled = []
            for s in range(LAGS):
                us = ub[LAGS:] if s == 0 else pltpu.roll(ub, s, 0)[LAGS:]
                if seg is not None and s > 0:
                    us = jnp.where(st["row_in_seg"] >= s, us, 0.0)
                if s % PAIRS_PER_TILE:
                    us = pltpu.roll(us, SLOT * (s % PAIRS_PER_TILE), 1)
                rolled.append(us)
            st["halves"] = (route(rolled[:PAIRS_PER_TILE]), route(rolled[PAIRS_PER_TILE:]))
        return step

    def pair(k, sg):
        def step():
            q = k * PAIRS_PER_TILE + sg
            xl = jnp.concatenate([st["halves"][0][sg], st["halves"][1][sg]], axis=1).astype(BF16)
            w = jnp.dot(xl, wlag_ref[q], preferred_element_type=F32)
            ar, ai = a8r_ref[q:q + 1, :], a8i_ref[q:q + 1, :]
            if seg is None:
                cr, ci = cr_s[q], ci_s[q]
            else:
                seg_rows, h0r_ref, h0i_ref, pr_ref, pi_ref = seg
            for b2 in range(tt // BF16_ROWS):
                hrs, his = [], []
                for b in (2 * b2, 2 * b2 + 1):
                    if seg is not None and (b * SUBLANES) % seg_rows == 0:
                        n = b * SUBLANES // seg_rows
                        h0r, h0i = h0r_ref[n, q:q + 1, :], h0i_ref[n, q:q + 1, :]
                        cr = pr_ref[q] * h0r - pi_ref[q] * h0i
                        ci = pr_ref[q] * h0i + pi_ref[q] * h0r
                    blk = slice(b * SUBLANES, (b + 1) * SUBLANES)
                    hr = w[blk, :LANES] + cr
                    hi = w[blk, LANES:] + ci
                    cr = ar * hr - ai * hi
                    ci = ar * hi + ai * hr
                    hrs.append(hr)
                    his.append(hi)
                    if seg is not None and ((b + 1) * SUBLANES) % seg_rows == 0:
                        n = b * SUBLANES // seg_rows
                        hr_ref[n, q:q + 1, :] = hr[SUBLANES - 1:, :]
                        hi_ref[n, q:q + 1, :] = hi[SUBLANES - 1:, :]
                blk2 = slice(b2 * BF16_ROWS, (b2 + 1) * BF16_ROWS)
                hs[blk2, q * PAIR_N:q * PAIR_N + LANES] = jnp.concatenate(hrs, axis=0).astype(BF16)
                hs[blk2, q * PAIR_N + LANES:(q + 1) * PAIR_N] = jnp.concatenate(his, axis=0).astype(BF16)
            if seg is None:
                cr_s[q] = cr
                ci_s[q] = ci
                hr_ref[0, q:q + 1, :] = hr[SUBLANES - 1:, :]
                hi_ref[0, q:q + 1, :] = hi[SUBLANES - 1:, :]
        return step

    def c_proj(k):
        def step():
            cols = slice(k * PAIRS_PER_TILE * PAIR_N, (k + 1) * PAIRS_PER_TILE * PAIR_N)
            st["ys"].append(jnp.dot(hs[:, cols], ck_ref[k], preferred_element_type=F32))
            if k == U_TILES - 1:
                o["y"] = jnp.concatenate(st["ys"], axis=1)
        return step

    steps = [setup]
    for k in range(U_TILES):
        steps += [lag_copies(k)] + [pair(k, sg) for sg in range(PAIRS_PER_TILE)] + [c_proj(k)]
    return steps


def _glu(y, u, sz, d, wglu_ref):
    z = jax.nn.gelu(y + d * u)
    g = jnp.dot(z.astype(BF16), wglu_ref[...], preferred_element_type=F32)
    return z * _sigmoid(g) * sz


def _out_steps(src, w_refs, fgain, emit):
    st = {}
    woa, wos, wout, wpg, wpp = (w_refs[i * OUT_BLOCKS:(i + 1) * OUT_BLOCKS] for i in range(N_OUT_WEIGHTS))

    def mm(a, blocks):
        a = a.astype(BF16)
        return jnp.concatenate([jnp.dot(a, w_ref[...], preferred_element_type=F32) for w_ref in blocks], axis=1)

    def branches():
        st["merged"] = src["ga"]() * mm(src["xa"](), woa) + src["gs"]() * mm(src["xs"](), wos)

    def residual():
        st["h"] = src["x"]() + mm(st.pop("merged"), wout)

    def embed_gate():
        h = st.pop("h")
        st["h"] = h + _sigmoid(mm(h, wpg)) * mm(src["p"](), wpp)

    def norm():
        h = st.pop("h")
        ms = jnp.mean(h * h, axis=-1, keepdims=True)
        emit(h * lax.rsqrt(ms + EPS) * fgain)

    return [branches, residual, embed_gate, norm]


def _layer_kernel(sinks_ref, x_ref, p_ref, cos_ref, sina_ref, sinb_ref, gain_ref, w_in_ref,
                  wlag_ref, a8r_ref, a8i_ref, ck_ref, d_ref, wglu_ref, fg_ref, *rest, tt):
    out_w, (y_ref, k_ref, v_ref, hr_ref, hi_ref,
            kbuf, vbuf, xa_s, ubuf, cr_s, ci_s, hs) = rest[:N_OUT_REFS], rest[N_OUT_REFS:]
    t = pl.program_id(1)
    half = tt // 2

    @pl.when(t == 0)
    def _():
        kbuf[0:WINDOW, :] = jnp.zeros((WINDOW, KV2_WIDTH), BF16)
        vbuf[0:WINDOW, :] = jnp.zeros((WINDOW, KV2_WIDTH), BF16)
        _ssm_reset(ubuf, cr_s, ci_s)

    pj, so = [{}, {}], [{}, {}]

    def rows(h):
        return slice(h * half, (h + 1) * half)

    def proj(h):
        def store(name):
            if name == "u":
                ubuf[LAGS + h * half:LAGS + (h + 1) * half, :] = pj[h]["u"]
                return
            last_ref, buf = (k_ref, kbuf) if name == "k" else (v_ref, vbuf)
            if h == 1:
                last_ref[0] = pj[h][name][half - WINDOW:]
            for j in range(N_KV_HEADS):
                buf[WINDOW + h * half:WINDOW + (h + 1) * half, j * LANES:(j + 1) * LANES] = pj[h][name + "2"][j]

        return _proj_steps(lambda: x_ref[0, rows(h)], gain_ref[...],
                           lambda: (cos_ref[rows(h)], sina_ref[rows(h)], sinb_ref[rows(h)]),
                           w_in_ref, pj[h], store)

    def mid(h):
        def get_q(c, head):
            r0 = c * CHUNK - h * half
            return pj[h]["qh"][head][r0:r0 + CHUNK]

        def get_kv(c, kv):
            krows, cols = slice(c * CHUNK, c * CHUNK + KEYS), slice(kv * LANES, (kv + 1) * LANES)
            return kbuf[krows, cols], vbuf[krows, cols]

        def get_valid(c):
            if c * CHUNK >= WINDOW:
                return None
            in_seq = c * CHUNK + lax.broadcasted_iota(jnp.int32, (1, KEYS), 1) >= WINDOW
            return jnp.logical_or(in_seq, t > 0)

        def emit(c, tile, o):
            r0, cols = c * CHUNK - h * half, slice(tile * LANES, (tile + 1) * LANES)
            xa_s[c * CHUNK:(c + 1) * CHUNK, cols] = (o * pj[h]["sa"][r0:r0 + CHUNK, cols]).astype(BF16)

        def glu():
            so[h]["xs"] = _glu(so[h].pop("y"), pj[h]["u"], pj[h]["sz"], d_ref[...], wglu_ref)

        chunks = range(h * half // CHUNK, (h + 1) * half // CHUNK)
        return (_attn_steps(sinks_ref, chunks, get_q, get_kv, get_valid, emit)
                + _ssm_steps(ubuf, h * half, half, cr_s, ci_s, hs, wlag_ref, a8r_ref, a8i_ref, ck_ref,
                             hr_ref, hi_ref, so[h])
                + [glu])

    def out(h):
        src = dict(xa=lambda: xa_s[rows(h)], xs=lambda: so[h]["xs"], ga=lambda: pj[h]["ga"],
                   gs=lambda: pj[h]["gs"], x=lambda: x_ref[0, rows(h)], p=lambda: p_ref[0, rows(h)])

        def emit(y):
            y_ref[0, rows(h)] = y

        return _out_steps(src, out_w, fg_ref[...], emit)

    _run(proj(0))
    _run(_spread(mid(0), proj(1)))
    _run(_spread(mid(1), out(0)))
    _run(out(1))

    kbuf[0:WINDOW, :] = kbuf[tt:tt + WINDOW, :]
    vbuf[0:WINDOW, :] = vbuf[tt:tt + WINDOW, :]
    ubuf[0:LAGS, :] = ubuf[tt:tt + LAGS, :]


def _layer_fused(x, p, tabs, wts, consts):
    b, t, _ = x.shape
    tt = min(LAYER_ROWS, t)
    assert t % tt == 0 and tt // 2 >= WINDOW and (tt // 2) % BF16_ROWS == 0 and tabs[0].shape[0] == t
    (gain, w_in, sinks, woa, d_skip, w_glu, wos, wout, wpg, wpp, fgain) = wts
    wlag, a8r, a8i, _, _, ck = consts

    def row_spec(w):
        return pl.BlockSpec((1, tt, w), lambda i, j: (i, j, 0))

    tab_spec = pl.BlockSpec((tt, LANES), lambda i, j: (j, 0))
    st_spec = pl.BlockSpec((1, PAIRS, LANES), lambda i, j: (i, 0, 0))
    win_spec = pl.BlockSpec((1, WINDOW, KV_WIDTH), lambda i, j: (i, 0, 0))
    consts_in = (gain, w_in, wlag, a8r, a8i, ck, d_skip, w_glu, fgain)
    out_w = (woa, wos, wout, wpg, wpp)
    vmem = (sum(a.size * a.dtype.itemsize for a in consts_in + out_w)
            + 2 * tt * (2 * D_MODEL + PLE_DIM + 2 * KV_WIDTH + 3 * LANES) * 4
            + 2 * (WINDOW + tt) * KV2_WIDTH * 2 + tt * ATTN_WIDTH * 2 + (tt + LAGS) * SSM_WIDTH * 4
            + (tt // 2) * N_STATE * 2 + 3 * tt * IN_WIDTH * 4)
    y, k, v, hr, hi = pl.pallas_call(
        functools.partial(_layer_kernel, tt=tt),
        grid=(b, t // tt),
        in_specs=[pl.BlockSpec(memory_space=pltpu.SMEM), row_spec(D_MODEL), row_spec(PLE_DIM),
                  tab_spec, tab_spec, tab_spec]
                 + [_const_spec(a.shape) for a in consts_in] + _column_block_specs(out_w),
        out_specs=[row_spec(D_MODEL), win_spec, win_spec, st_spec, st_spec],
        out_shape=[jax.ShapeDtypeStruct((b, t, D_MODEL), F32),
                   jax.ShapeDtypeStruct((b, WINDOW, KV_WIDTH), F32),
                   jax.ShapeDtypeStruct((b, WINDOW, KV_WIDTH), F32),
                   jax.ShapeDtypeStruct((b, PAIRS, LANES), F32),
                   jax.ShapeDtypeStruct((b, PAIRS, LANES), F32)],
        scratch_shapes=[pltpu.VMEM((WINDOW + tt, KV2_WIDTH), BF16),
                        pltpu.VMEM((WINDOW + tt, KV2_WIDTH), BF16),
                        pltpu.VMEM((tt, ATTN_WIDTH), BF16),
                        pltpu.VMEM((tt + LAGS, SSM_WIDTH), F32),
                        pltpu.VMEM((PAIRS, SUBLANES, LANES), F32),
                        pltpu.VMEM((PAIRS, SUBLANES, LANES), F32),
                        pltpu.VMEM((tt // 2, N_STATE), BF16)],
        compiler_params=_params(vmem, 2),
        name="layer_prompt",
    )(sinks, x, p, *tabs, *consts_in, *_column_block_operands(out_w))
    return y, k, v, hr, hi


def _sample_kernel(sinks_ref, x_ref, p_ref, cos_ref, sina_ref, sinb_ref, kpre_ref, vpre_ref,
                   h0r_ref, h0i_ref, gain_ref, w_in_ref, wlag_ref, a8r_ref, a8i_ref, pr_ref, pi_ref,
                   ck_ref, d_ref, wglu_ref, fg_ref, *rest, n, t):
    out_w, (y_ref, k_ref, v_ref, hr_ref, hi_ref,
            kbuf, vbuf, xa_s, ubuf, hs) = rest[:N_OUT_REFS], rest[N_OUT_REFS:]
    rows = n * t
    pj, so = {}, {}

    def store(name):
        if name == "u":
            ubuf[0:LAGS, :] = jnp.zeros((LAGS, SSM_WIDTH), F32)
            ubuf[LAGS:, :] = pj["u"]
            return
        full_ref, pre_ref, buf = (k_ref, kpre_ref, kbuf) if name == "k" else (v_ref, vpre_ref, vbuf)
        full_ref[...] = pj[name]
        for s in range(n):
            buf[s, 0:WINDOW, :] = pre_ref[s]
            for j in range(N_KV_HEADS):
                buf[s, WINDOW:WINDOW + t, j * LANES:(j + 1) * LANES] = pj[name + "2"][j][s * t:(s + 1) * t]

    def get_q(c, head):
        return pj["qh"][head][c * CHUNK:(c + 1) * CHUNK]

    def get_kv(c, kv):
        cols = slice(kv * LANES, (kv + 1) * LANES)
        return kbuf[c, :, cols], vbuf[c, :, cols]

    def emit(c, tile, o):
        r, cols = slice(c * CHUNK, (c + 1) * CHUNK), slice(tile * LANES, (tile + 1) * LANES)
        xa_s[r, cols] = (o * pj["sa"][r, cols]).astype(BF16)

    def glu():
        so["xs"] = _glu(so.pop("y"), pj["u"], pj["sz"], d_ref[...], wglu_ref)

    src = dict(xa=lambda: xa_s[...], xs=lambda: so["xs"], ga=lambda: pj["ga"], gs=lambda: pj["gs"],
               x=lambda: x_ref[...], p=lambda: p_ref[...])

    def emit_y(y):
        y_ref[...] = y

    _run(_proj_steps(lambda: x_ref[...], gain_ref[...],
                     lambda: (cos_ref[...], sina_ref[...], sinb_ref[...]), w_in_ref, pj, store))
    _run(_attn_steps(sinks_ref, range(n), get_q, get_kv, lambda c: None, emit))
    _run(_ssm_steps(ubuf, 0, rows, None, None, hs, wlag_ref, a8r_ref, a8i_ref, ck_ref, hr_ref, hi_ref, so,
                    seg=(t, h0r_ref, h0i_ref, pr_ref, pi_ref)) + [glu])
    _run(_out_steps(src, out_w, fg_ref[...], emit_y))


def _layer_sample(x, p, tabs, k_prefix, v_prefix, h0r, h0i, wts, consts):
    n, t, _ = x.shape
    assert t == CHUNK and tabs[0].shape[0] == n * t
    rows = n * t
    (gain, w_in, sinks, woa, d_skip, w_glu, wos, wout, wpg, wpp, fgain) = wts
    wlag, a8r, a8i, pr, pi, ck = consts
    operands = (x.reshape(rows, D_MODEL), p.reshape(rows, PLE_DIM), *tabs,
                _both_halves(k_prefix), _both_halves(v_prefix), h0r, h0i,
                gain, w_in, wlag, a8r, a8i, pr, pi, ck, d_skip, w_glu, fgain)
    out_w = (woa, wos, wout, wpg, wpp)
    out_shapes = [(rows, D_MODEL), (rows, KV_WIDTH), (rows, KV_WIDTH), h0r.shape, h0r.shape]
    scratch = [((n, KEYS, KV2_WIDTH), BF16), ((n, KEYS, KV2_WIDTH), BF16), ((rows, ATTN_WIDTH), BF16),
               ((rows + LAGS, SSM_WIDTH), F32), ((rows, N_STATE), BF16)]
    vmem = (sum(a.size * a.dtype.itemsize for a in operands + out_w)
            + sum(int(np.prod(s)) * 4 for s in out_shapes)
            + sum(int(np.prod(s)) * np.dtype(d).itemsize for s, d in scratch)
            + 3 * rows * IN_WIDTH * 4)
    y, k, v, hr, hi = pl.pallas_call(
        functools.partial(_sample_kernel, n=n, t=t),
        grid=(1,),
        in_specs=([pl.BlockSpec(memory_space=pltpu.SMEM)] + [_const_spec(a.shape) for a in operands]
                  + _column_block_specs(out_w)),
        out_specs=[pl.BlockSpec(s, lambda i, nd=len(s): (0,) * nd) for s in out_shapes],
        out_shape=[jax.ShapeDtypeStruct(s, F32) for s in out_shapes],
        scratch_shapes=[pltpu.VMEM(s, d) for s, d in scratch],
        compiler_params=_params(vmem, 1),
        name="layer_sample",
    )(sinks, *operands, *_column_block_operands(out_w))
    k_new = jnp.concatenate([k_prefix[:, t:], k.reshape(n, t, KV_WIDTH)], axis=1)
    v_new = jnp.concatenate([v_prefix[:, t:], v.reshape(n, t, KV_WIDTH)], axis=1)
    return y.reshape(n, t, D_MODEL), k_new, v_new, hr, hi


def _both_halves(a):
    h0, h1 = a[..., :HEAD_DIM], a[..., HEAD_DIM:]
    return jnp.concatenate([h0, h0, h1, h1], axis=-1).astype(BF16)


def _ssm_constants(a_re, a_im, log_dt, b_re, b_im, c_re, c_im):
    dt = jnp.exp(log_dt.astype(F32))[:, None]
    lr = a_re.astype(F32).reshape(PAIRS, 1, 1, LANES)
    li = a_im.astype(F32).reshape(PAIRS, 1, 1, LANES)
    xr = (a_re.astype(F32) * dt).reshape(PAIRS, 1, 1, LANES)
    xi = (a_im.astype(F32) * dt).reshape(PAIRS, 1, 1, LANES)

    def apow(n):
        mag = jnp.exp(xr * n)
        return mag * jnp.cos(xi * n), mag * jnp.sin(xi * n)

    ar, ai = apow(1.0)
    nr, ni = ar - 1.0, ai
    den = lr * lr + li * li
    fr, fi = (nr * lr + ni * li) / den, (ni * lr - nr * li) / den

    n_slots = PAIR_K // SLOT
    qq, hi_ = np.arange(PAIRS)[:, None], np.arange(n_slots)[None, :]
    lag_tab = (PAIRS_PER_TILE * (hi_ // PAIRS_PER_TILE) + (hi_ % PAIRS_PER_TILE - qq) % PAIRS_PER_TILE)
    lag_tab = lag_tab.astype(np.float32)[:, :, None, None]
    same_group = (np.arange(SLOT)[:, None] // SSM_GROUP == np.arange(LANES)[None, :] // SSM_STATE)
    same_group = same_group.astype(np.float32)

    def b_rows(bm):
        t = jnp.transpose(bm.astype(F32).reshape(PAIRS, 2, SSM_STATE, SSM_GROUP), (0, 3, 1, 2))
        t = t.reshape(PAIRS, 1, 1, SSM_GROUP, LANES)
        t = jnp.broadcast_to(t, (PAIRS, 1, 2, SSM_GROUP, LANES))
        return t.reshape(PAIRS, 1, SLOT, LANES) * same_group

    br, bi = b_rows(b_re), b_rows(b_im)
    bbr, bbi = fr * br - fi * bi, fr * bi + fi * br
    er, ei = apow(lag_tab)
    wlag = jnp.concatenate([(er * bbr - ei * bbi).reshape(PAIRS, PAIR_K, LANES),
                            (er * bbi + ei * bbr).reshape(PAIRS, PAIR_K, LANES)], axis=-1).astype(BF16)

    a8r, a8i = (a.reshape(PAIRS, LANES) for a in apow(float(LAGS)))
    pwr, pwi = (a.reshape(PAIRS, LAGS, LANES)
                for a in apow(np.arange(1, LAGS + 1, dtype=np.float32)[None, :, None, None]))

    def c_cols(c):
        t = jnp.transpose(c.astype(F32).reshape(U_TILES, LANES // SSM_GROUP, SSM_GROUP, SSM_STATE),
                          (0, 3, 1, 2))
        return t.reshape(U_TILES, 1, 1, 1, SSM_STATE, LANES)

    cols_group = np.arange(LANES) // SSM_GROUP
    rows_group = 2 * np.arange(PAIRS_PER_TILE)[:, None] + np.arange(2)[None, :]
    c_mask = (rows_group[:, None, :, None, None] == cols_group[None, None, None, None, :])
    c_mask = c_mask.astype(np.float32)[None]
    ck = jnp.concatenate([c_cols(c_re) * c_mask, -c_cols(c_im) * c_mask], axis=2)
    ck = ck.reshape(U_TILES, PAIRS_PER_TILE * PAIR_N, LANES).astype(BF16)
    return wlag, a8r, a8i, pwr, pwi, ck


def _rope_tables(pos0, t, rows):
    half = ROT_DIM // 2
    d = np.arange(LANES) % HEAD_DIM
    inv = jnp.power(ROPE_THETA, -jnp.arange(half, dtype=F32) * 2.0 / ROT_DIM)
    pos = (pos0 + jnp.arange(t)).astype(F32)
    ang = pos[:, None] * inv[None, :]
    cos, sin = (jnp.tile(a, (1, LANES // half)) for a in (jnp.cos(ang), jnp.sin(ang)))
    cos_t = jnp.where((d < ROT_DIM)[None, :], cos, 1.0)
    sina = jnp.where(((d >= half) & (d < ROT_DIM))[None, :], sin, 0.0)
    sinb = jnp.where((d < half)[None, :], -sin, 0.0)
    reps = (max(rows // t, 1), 1)
    return tuple(jnp.tile(a, reps) for a in (cos_t, sina, sinb))


def kernel(x_prompt, x_sample, p_prompt, p_sample, cache_attn_k, cache_attn_v, state_ssm_re,
           state_ssm_im, norm_gain, w_in, attn_sinks, w_o_attn, ssm_a_re, ssm_a_im, ssm_log_dt,
           ssm_b_re, ssm_b_im, ssm_c_re, ssm_c_im, ssm_d, ssm_w_glu, w_o_ssm, w_out,
           w_ple_gate, w_ple_proj, final_norm_gain):
    assert norm_gain.shape[0] == 1, "single-layer model"
    bp, tp, _ = x_prompt.shape
    bs, ts, _ = x_sample.shape
    wts = (norm_gain[0].reshape(1, D_MODEL).astype(F32), w_in[0].astype(BF16),
           attn_sinks[0].astype(F32), _column_blocks(w_o_attn[0]),
           ssm_d[0].reshape(1, SSM_WIDTH).astype(F32), ssm_w_glu[0].astype(BF16),
           _column_blocks(w_o_ssm[0]), _column_blocks(w_out[0]), _column_blocks(w_ple_gate[0]),
           _column_blocks(w_ple_proj[0]), final_norm_gain.reshape(1, D_MODEL).astype(F32))
    consts = _ssm_constants(ssm_a_re[0], ssm_a_im[0], ssm_log_dt[0], ssm_b_re[0], ssm_b_im[0],
                            ssm_c_re[0], ssm_c_im[0])

    y_p, k_p, v_p, hr_p, hi_p = _layer_fused(x_prompt, p_prompt[0], _rope_tables(0, tp, tp), wts, consts)

    ck = cache_attn_k[0].reshape(bs, WINDOW, KV_WIDTH).astype(F32)
    cv = cache_attn_v[0].reshape(bs, WINDOW, KV_WIDTH).astype(F32)
    h0r = state_ssm_re[0].reshape(bs, PAIRS, LANES).astype(F32)
    h0i = state_ssm_im[0].reshape(bs, PAIRS, LANES).astype(F32)
    tabs_s = _rope_tables(PAST_LEN, ts, bs * ts)
    y_s, k_s, v_s, hr_s, hi_s = _layer_sample(x_sample, p_sample[0], tabs_s, ck, cv, h0r, h0i, wts, consts)

    def kv_out(a, b):
        return a.reshape(1, b, WINDOW, N_KV_HEADS, HEAD_DIM)

    def st_out(a, b):
        return a.reshape(1, b, SSM_GROUPS, SSM_STATE)

    return (y_p, y_s, kv_out(k_p, bp), kv_out(v_p, bp),
            st_out(hr_p, bp), st_out(hi_p, bp), kv_out(k_s, bs), kv_out(v_s, bs),
            st_out(hr_s, bs), st_out(hi_s, bs))
```

```python
import functools

import numpy as np
import jax
import jax.numpy as jnp
from jax import lax
from jax.experimental import pallas as pl
from jax.experimental.pallas import tpu as pltpu

F32 = jnp.float32
BF16 = jnp.bfloat16

LANES = 128
SUBLANES = 8
V7X_VMEM_BYTES = 64 * 1024 * 1024

D_MODEL = 1024
CHUNK = 64
WINDOW = 128
N_HEADS = 8
N_KV_HEADS = 2
HEAD_DIM = 64
Q_PER_KV = N_HEADS // N_KV_HEADS
LOG2E = 1.4426950408889634
Q_SCALE = HEAD_DIM ** -0.5 * LOG2E
ATTN_WIDTH = N_HEADS * HEAD_DIM
KV_WIDTH = N_KV_HEADS * HEAD_DIM
ROT_DIM = HEAD_DIM // 4
ROPE_THETA = 500000.0
SSM_WIDTH = D_MODEL // 2
SSM_GROUP = 16
SSM_GROUPS = SSM_WIDTH // SSM_GROUP
SSM_STATE = 64
PLE_DIM = 256
PAST_LEN = 1024
EPS = 1e-6

O_Q = 0
O_K = O_Q + ATTN_WIDTH
O_V = O_K + KV_WIDTH
O_ZA = O_V + KV_WIDTH
O_U = O_ZA + ATTN_WIDTH
O_ZS = O_U + SSM_WIDTH
O_GA = O_ZS + SSM_WIDTH
O_GS = O_GA + D_MODEL
IN_WIDTH = O_GS + D_MODEL

QM_WIDTH = N_HEADS * LANES
KV2_WIDTH = N_KV_HEADS * LANES
KEYS = WINDOW + CHUNK
LAGS = SUBLANES
PAIRS = SSM_GROUPS // 2
PAIR_K = 2 * LAGS * SSM_GROUP
PAIR_N = 2 * 2 * SSM_STATE
N_STATE = PAIRS * PAIR_N
U_TILES = SSM_WIDTH // LANES
PAIRS_PER_TILE = PAIRS // U_TILES
SLOT = 2 * SSM_GROUP
BF16_ROWS = 2 * SUBLANES
assert PAIRS_PER_TILE == 4 and LAGS == 2 * PAIRS_PER_TILE

LAYER_ROWS = 512


def _sigmoid(x):
    return 1.0 / (1.0 + jnp.exp2(x * (-LOG2E)))


def _const_spec(shape):
    zeros = (0,) * len(shape)
    return pl.BlockSpec(shape, lambda *_: zeros, pipeline_mode=pl.Buffered(1))


def _params(vmem_bytes, n_grid):
    return pltpu.CompilerParams(
        dimension_semantics=("arbitrary",) * n_grid,
        vmem_limit_bytes=min(int(vmem_bytes), V7X_VMEM_BYTES - 8 * 1024 * 1024),
    )


def _run(steps):
    for step in steps:
        step()


def _spread(main, other):
    merged, j = [], 0
    for i, step in enumerate(main):
        while j < len(other) and j * len(main) <= i * len(other):
            merged.append(other[j])
            j += 1
        merged.append(step)
    return merged + other[j:]


def _proj_steps(get_x, gain, get_tabs, w_ref, o, store=None):
    st = {}

    def norm():
        x = get_x()
        ms = jnp.mean(x * x, axis=-1, keepdims=True)
        st["xn"] = (x * lax.rsqrt(ms + EPS) * gain).astype(BF16)
        st["lo"] = lax.broadcasted_iota(jnp.int32, (x.shape[0], LANES), 1) < HEAD_DIM

    def seg(a, b):
        return jnp.dot(st["xn"], w_ref[:, a:b], preferred_element_type=F32)

    def rope(t):
        cos, sina, sinb = get_tabs()
        return (t * cos + pltpu.roll(t, ROT_DIM // 2, 1) * sina
                + pltpu.roll(t, LANES - ROT_DIM // 2, 1) * sinb)

    def both_halves(t):
        tr = pltpu.roll(t, HEAD_DIM, 1)
        return [jnp.where(st["lo"], t, tr).astype(BF16), jnp.where(st["lo"], tr, t).astype(BF16)]

    def done(name):
        if store is not None:
            store(name)

    def q():
        zq = seg(O_Q, O_K)
        o["qh"] = []
        for j in range(ATTN_WIDTH // LANES):
            qt = rope(zq[:, j * LANES:(j + 1) * LANES]) * Q_SCALE
            o["qh"] += [jnp.where(st["lo"], qt, 0.0).astype(BF16),
                        jnp.where(st["lo"], 0.0, qt).astype(BF16)]

    def kv():
        z = seg(O_K, O_ZA)
        o["k"] = rope(z[:, :KV_WIDTH])
        o["k2"] = both_halves(o["k"])
        done("k")
        o["v"] = z[:, KV_WIDTH:]
        o["v2"] = both_halves(o["v"])
        done("v")

    def za():
        z = seg(O_ZA, O_U)
        o["sa"] = z * _sigmoid(z)

    def u():
        o["u"] = seg(O_U, O_ZS)
        done("u")

    def zs():
        z = seg(O_ZS, O_GA)
        o["sz"] = z * _sigmoid(z)

    def ga():
        o["ga"] = _sigmoid(seg(O_GA, O_GS))

    def gs():
        o["gs"] = _sigmoid(seg(O_GS, IN_WIDTH))

    return [norm, q, kv, za, u, zs, ga, gs]


def _attn_steps(sinks_ref, chunks, get_q, get_kv, get_valid, emit):
    nt = (((1,), (1,)), ((), ()))
    units = [(c, kv) for c in chunks for kv in range(N_KV_HEADS)]
    n = len(units)
    st = {}

    def scores(c, kv):
        k2, v2 = get_kv(c, kv)
        qm = jnp.concatenate([get_q(c, kv * Q_PER_KV + h) for h in range(Q_PER_KV)], axis=0)
        s = lax.dot_general(qm, k2, nt, preferred_element_type=F32)
        valid = get_valid(c)
        if valid is not None:
            s = jnp.where(valid, s, -jnp.inf)
        return s, v2

    def softmax(s, kv):
        head_row = lax.broadcasted_iota(jnp.int32, (Q_PER_KV * CHUNK, 1), 0) // CHUNK
        sk = [sinks_ref[kv * Q_PER_KV + h] * LOG2E for h in range(Q_PER_KV)]
        sink = jnp.where(head_row == 0, sk[0],
                         jnp.where(head_row == 1, sk[1], jnp.where(head_row == 2, sk[2], sk[3])))
        m = jnp.maximum(jnp.max(s, axis=1, keepdims=True), sink)
        return jnp.exp2(s - m).astype(BF16), jnp.exp2(sink - m)

    def output(e, sink_term, v2, c, kv):
        lo_q = lax.broadcasted_iota(jnp.int32, (CHUNK, LANES), 1) < HEAD_DIM
        ones = jnp.ones((KEYS, LANES), BF16)
        pv = jnp.dot(e, jnp.concatenate([v2, ones], axis=1), preferred_element_type=F32)
        o = pv[:, :LANES] / (pv[:, LANES:] + sink_term)
        for j in range(Q_PER_KV // 2):
            even = o[2 * j * CHUNK:(2 * j + 1) * CHUNK]
            odd = o[(2 * j + 1) * CHUNK:(2 * j + 2) * CHUNK]
            emit(c, kv * (Q_PER_KV // 2) + j, jnp.where(lo_q, even, odd))

    def make(i):
        def step():
            if i < n:
                st[i] = scores(*units[i])
            if 0 <= i - 1 < n:
                s, v2 = st[i - 1]
                st[i - 1] = softmax(s, units[i - 1][1]) + (v2,)
            if 0 <= i - 2 < n:
                e, den, v2 = st.pop(i - 2)
                output(e, den, v2, *units[i - 2])
        return step

    return [make(i) for i in range(n + 2)]


def _ssm_reset(ubuf, cr_s, ci_s):
    ubuf[0:LAGS, :] = jnp.zeros((LAGS, SSM_WIDTH), F32)
    cr_s[...] = jnp.zeros(cr_s.shape, F32)
    ci_s[...] = jnp.zeros(ci_s.shape, F32)


def _ssm_steps(ubuf, row0, tt, cr_s, ci_s, hs, wlag_ref, a8r_ref, a8i_ref, ck_ref, hr_ref, hi_ref, o,
               seg=None):
    st = {"ys": []}

    def setup():
        if seg is not None:
            st["row_in_seg"] = lax.broadcasted_iota(jnp.int32, (tt, LANES), 0) % seg[0]
        slot = lax.broadcasted_iota(jnp.int32, (tt, LANES), 1) // SLOT
        st["to_low"] = [((slot + PAIRS_PER_TILE - s) % PAIRS_PER_TILE) < 2 for s in range(2)]
        st["same_parity"] = [((slot + sg) % 2) == 0 for sg in range(2)]

    def route(r):
        low = [jnp.where(st["to_low"][s], r[s], r[s + 2]) for s in range(2)]
        high = [jnp.where(st["to_low"][s], r[s + 2], r[s]) for s in range(2)]
        return [jnp.where(st["same_parity"][sg % 2], src[0], src[1])
                for sg, src in zip(range(PAIRS_PER_TILE), (low, low, high, high))]

    def lag_copies(k):
        def step():
            ub = ubuf[row0:row0 + LAGS + tt, k * LANES:(k + 1) * LANES]
            rolled = []
            for s in range(LAGS):
                us = ub[LAGS:] if s == 0 else pltpu.roll(ub, s, 0)[LAGS:]
                if seg is not None and s > 0:
                    us = jnp.where(st["row_in_seg"] >= s, us, 0.0)
                if s % PAIRS_PER_TILE:
                    us = pltpu.roll(us, SLOT * (s % PAIRS_PER_TILE), 1)
                rolled.append(us)
            st["halves"] = (route(rolled[:PAIRS_PER_TILE]), route(rolled[PAIRS_PER_TILE:]))
        return step

    def pair(k, sg):
        def step():
            q = k * PAIRS_PER_TILE + sg
            xl = jnp.concatenate([st["halves"][0][sg], st["halves"][1][sg]], axis=1).astype(BF16)
            w = jnp.dot(xl, wlag_ref[q], preferred_element_type=F32)
            ar, ai = a8r_ref[q:q + 1, :], a8i_ref[q:q + 1, :]
            if seg is None:
                cr, ci = cr_s[q], ci_s[q]
            else:
                seg_rows, h0r_ref, h0i_ref, pr_ref, pi_ref = seg
            for b2 in range(tt // BF16_ROWS):
                hrs, his = [], []
                for b in (2 * b2, 2 * b2 + 1):
                    if seg is not None and (b * SUBLANES) % seg_rows == 0:
                        n = b * SUBLANES // seg_rows
                        h0r, h0i = h0r_ref[n, q:q + 1, :], h0i_ref[n, q:q + 1, :]
                        cr = pr_ref[q] * h0r - pi_ref[q] * h0i
                        ci = pr_ref[q] * h0i + pi_ref[q] * h0r
                    blk = slice(b * SUBLANES, (b + 1) * SUBLANES)
                    hr = w[blk, :LANES] + cr
                    hi = w[blk, LANES:] + ci
                    cr = ar * hr - ai * hi
                    ci = ar * hi + ai * hr
                    hrs.append(hr)
                    his.append(hi)
                    if seg is not None and ((b + 1) * SUBLANES) % seg_rows == 0:
                        n = b * SUBLANES // seg_rows
                        hr_ref[n, q:q + 1, :] = hr[SUBLANES - 1:, :]
                        hi_ref[n, q:q + 1, :] = hi[SUBLANES - 1:, :]
                blk2 = slice(b2 * BF16_ROWS, (b2 + 1) * BF16_ROWS)
                hs[blk2, q * PAIR_N:q * PAIR_N + LANES] = jnp.concatenate(hrs, axis=0).astype(BF16)
                hs[blk2, q * PAIR_N + LANES:(q + 1) * PAIR_N] = jnp.concatenate(his, axis=0).astype(BF16)
            if seg is None:
                cr_s[q] = cr
                ci_s[q] = ci
                hr_ref[0, q:q + 1, :] = hr[SUBLANES - 1:, :]
                hi_ref[0, q:q + 1, :] = hi[SUBLANES - 1:, :]
        return step

    def c_proj(k):
        def step():
            cols = slice(k * PAIRS_PER_TILE * PAIR_N, (k + 1) * PAIRS_PER_TILE * PAIR_N)
            st["ys"].append(jnp.dot(hs[:, cols], ck_ref[k], preferred_element_type=F32))
            if k == U_TILES - 1:
                o["y"] = jnp.concatenate(st["ys"], axis=1)
        return step

    steps = [setup]
    for k in range(U_TILES):
        steps += [lag_copies(k)] + [pair(k, sg) for sg in range(PAIRS_PER_TILE)] + [c_proj(k)]
    return steps


def _glu(y, u, sz, d, wglu_ref):
    z = jax.nn.gelu(y + d * u)
    g = jnp.dot(z.astype(BF16), wglu_ref[...], preferred_element_type=F32)
    return z * _sigmoid(g) * sz


def _out_steps(src, woa_ref, wos_ref, wout_ref, wpg_ref, wpp_ref, fgain, emit):
    st = {}

    def mm(a, w_ref):
        return jnp.dot(a.astype(BF16), w_ref[...], preferred_element_type=F32)

    def branches():
        st["merged"] = src["ga"]() * mm(src["xa"](), woa_ref) + src["gs"]() * mm(src["xs"](), wos_ref)

    def residual():
        st["h"] = src["x"]() + mm(st.pop("merged"), wout_ref)

    def embed_gate():
        h = st.pop("h")
        st["h"] = h + _sigmoid(mm(h, wpg_ref)) * mm(src["p"](), wpp_ref)

    def norm():
        h = st.pop("h")
        ms = jnp.mean(h * h, axis=-1, keepdims=True)
        emit(h * lax.rsqrt(ms + EPS) * fgain)

    return [branches, residual, embed_gate, norm]


def _layer_kernel(sinks_ref, x_ref, p_ref, cos_ref, sina_ref, sinb_ref, gain_ref, w_in_ref,
                  wlag_ref, a8r_ref, a8i_ref, ck_ref, d_ref, wglu_ref,
                  woa_ref, wos_ref, wout_ref, wpg_ref, wpp_ref, fg_ref,
                  y_ref, k_ref, v_ref, hr_ref, hi_ref,
                  kbuf, vbuf, xa_s, ubuf, cr_s, ci_s, hs, *, tt):
    t = pl.program_id(1)
    half = tt // 2

    @pl.when(t == 0)
    def _():
        kbuf[0:WINDOW, :] = jnp.zeros((WINDOW, KV2_WIDTH), BF16)
        vbuf[0:WINDOW, :] = jnp.zeros((WINDOW, KV2_WIDTH), BF16)
        _ssm_reset(ubuf, cr_s, ci_s)

    pj, so = [{}, {}], [{}, {}]

    def rows(h):
        return slice(h * half, (h + 1) * half)

    def proj(h):
        def store(name):
            if name == "u":
                ubuf[LAGS + h * half:LAGS + (h + 1) * half, :] = pj[h]["u"]
                return
            last_ref, buf = (k_ref, kbuf) if name == "k" else (v_ref, vbuf)
            if h == 1:
                last_ref[0] = pj[h][name][half - WINDOW:]
            for j in range(N_KV_HEADS):
                buf[WINDOW + h * half:WINDOW + (h + 1) * half, j * LANES:(j + 1) * LANES] = pj[h][name + "2"][j]

        return _proj_steps(lambda: x_ref[0, rows(h)], gain_ref[...],
                           lambda: (cos_ref[rows(h)], sina_ref[rows(h)], sinb_ref[rows(h)]),
                           w_in_ref, pj[h], store)

    def mid(h):
        def get_q(c, head):
            r0 = c * CHUNK - h * half
            return pj[h]["qh"][head][r0:r0 + CHUNK]

        def get_kv(c, kv):
            krows, cols = slice(c * CHUNK, c * CHUNK + KEYS), slice(kv * LANES, (kv + 1) * LANES)
            return kbuf[krows, cols], vbuf[krows, cols]

        def get_valid(c):
            if c * CHUNK >= WINDOW:
                return None
            in_seq = c * CHUNK + lax.broadcasted_iota(jnp.int32, (1, KEYS), 1) >= WINDOW
            return jnp.logical_or(in_seq, t > 0)

        def emit(c, tile, o):
            r0, cols = c * CHUNK - h * half, slice(tile * LANES, (tile + 1) * LANES)
            xa_s[c * CHUNK:(c + 1) * CHUNK, cols] = (o * pj[h]["sa"][r0:r0 + CHUNK, cols]).astype(BF16)

        def glu():
            so[h]["xs"] = _glu(so[h].pop("y"), pj[h]["u"], pj[h]["sz"], d_ref[...], wglu_ref)

        chunks = range(h * half // CHUNK, (h + 1) * half // CHUNK)
        return (_attn_steps(sinks_ref, chunks, get_q, get_kv, get_valid, emit)
                + _ssm_steps(ubuf, h * half, half, cr_s, ci_s, hs, wlag_ref, a8r_ref, a8i_ref, ck_ref,
                             hr_ref, hi_ref, so[h])
                + [glu])

    def out(h):
        src = dict(xa=lambda: xa_s[rows(h)], xs=lambda: so[h]["xs"], ga=lambda: pj[h]["ga"],
                   gs=lambda: pj[h]["gs"], x=lambda: x_ref[0, rows(h)], p=lambda: p_ref[0, rows(h)])

        def emit(y):
            y_ref[0, rows(h)] = y

        return _out_steps(src, woa_ref, wos_ref, wout_ref, wpg_ref, wpp_ref, fg_ref[...], emit)

    _run(proj(0))
    _run(_spread(mid(0), proj(1)))
    _run(_spread(mid(1), out(0)))
    _run(out(1))

    kbuf[0:WINDOW, :] = kbuf[tt:tt + WINDOW, :]
    vbuf[0:WINDOW, :] = vbuf[tt:tt + WINDOW, :]
    ubuf[0:LAGS, :] = ubuf[tt:tt + LAGS, :]


def _layer_fused(x, p, tabs, wts, consts):
    b, t, _ = x.shape
    tt = min(LAYER_ROWS, t)
    assert t % tt == 0 and tt // 2 >= WINDOW and (tt // 2) % BF16_ROWS == 0 and tabs[0].shape[0] == t
    (gain, w_in, sinks, woa, d_skip, w_glu, wos, wout, wpg, wpp, fgain) = wts
    wlag, a8r, a8i, _, _, ck = consts

    def row_spec(w):
        return pl.BlockSpec((1, tt, w), lambda i, j: (i, j, 0))

    tab_spec = pl.BlockSpec((tt, LANES), lambda i, j: (j, 0))
    st_spec = pl.BlockSpec((1, PAIRS, LANES), lambda i, j: (i, 0, 0))
    win_spec = pl.BlockSpec((1, WINDOW, KV_WIDTH), lambda i, j: (i, 0, 0))
    consts_in = (gain, w_in, wlag, a8r, a8i, ck, d_skip, w_glu, woa, wos, wout, wpg, wpp, fgain)
    vmem = (sum(a.size * a.dtype.itemsize for a in consts_in)
            + 2 * tt * (2 * D_MODEL + PLE_DIM + 2 * KV_WIDTH + 3 * LANES) * 4
            + 2 * (WINDOW + tt) * KV2_WIDTH * 2 + tt * ATTN_WIDTH * 2 + (tt + LAGS) * SSM_WIDTH * 4
            + (tt // 2) * N_STATE * 2 + 3 * tt * IN_WIDTH * 4)
    y, k, v, hr, hi = pl.pallas_call(
        functools.partial(_layer_kernel, tt=tt),
        grid=(b, t // tt),
        in_specs=[pl.BlockSpec(memory_space=pltpu.SMEM), row_spec(D_MODEL), row_spec(PLE_DIM),
                  tab_spec, tab_spec, tab_spec]
                 + [_const_spec(a.shape) for a in consts_in],
        out_specs=[row_spec(D_MODEL), win_spec, win_spec, st_spec, st_spec],
        out_shape=[jax.ShapeDtypeStruct((b, t, D_MODEL), F32),
                   jax.ShapeDtypeStruct((b, WINDOW, KV_WIDTH), F32),
                   jax.ShapeDtypeStruct((b, WINDOW, KV_WIDTH), F32),
                   jax.ShapeDtypeStruct((b, PAIRS, LANES), F32),
                   jax.ShapeDtypeStruct((b, PAIRS, LANES), F32)],
        scratch_shapes=[pltpu.VMEM((WINDOW + tt, KV2_WIDTH), BF16),
                        pltpu.VMEM((WINDOW + tt, KV2_WIDTH), BF16),
                        pltpu.VMEM((tt, ATTN_WIDTH), BF16),
                        pltpu.VMEM((tt + LAGS, SSM_WIDTH), F32),
                        pltpu.VMEM((PAIRS, SUBLANES, LANES), F32),
                        pltpu.VMEM((PAIRS, SUBLANES, LANES), F32),
                        pltpu.VMEM((tt // 2, N_STATE), BF16)],
        compiler_params=_params(vmem, 2),
        name="layer_prompt",
    )(sinks, x, p, *tabs, *consts_in)
    return y, k, v, hr, hi


def _sample_kernel(sinks_ref, x_ref, p_ref, cos_ref, sina_ref, sinb_ref, kpre_ref, vpre_ref,
                   h0r_ref, h0i_ref, gain_ref, w_in_ref, a8r_ref, a8i_ref, pr_ref, pi_ref, d_ref, fg_ref,
                   wlag_hbm, ck_hbm, wglu_hbm, woa_hbm, wos_hbm, wout_hbm, wpg_hbm, wpp_hbm,
                   y_ref, k_ref, v_ref, hr_ref, hi_ref,
                   kbuf, vbuf, xa_s, ubuf, hs,
                   wlag_ref, ck_ref, wglu_ref, woa_ref, wos_ref, wout_ref, wpg_ref, wpp_ref, sems, *, n, t):
    rows = n * t
    pj, so = {}, {}

    mid_w = [(wlag_hbm, wlag_ref), (ck_hbm, ck_ref), (wglu_hbm, wglu_ref)]
    out_w = [(woa_hbm, woa_ref), (wos_hbm, wos_ref), (wout_hbm, wout_ref), (wpg_hbm, wpg_ref),
             (wpp_hbm, wpp_ref)]
    copies = [pltpu.make_async_copy(hbm, vmem, sems.at[i]) for i, (hbm, vmem) in enumerate(mid_w + out_w)]
    for copy in copies:
        copy.start()

    def arrived(group):
        def step():
            for copy in group:
                copy.wait()
        return step

    def store(name):
        if name == "u":
            ubuf[0:LAGS, :] = jnp.zeros((LAGS, SSM_WIDTH), F32)
            ubuf[LAGS:, :] = pj["u"]
            return
        full_ref, pre_ref, buf = (k_ref, kpre_ref, kbuf) if name == "k" else (v_ref, vpre_ref, vbuf)
        full_ref[...] = pj[name]
        for s in range(n):
            buf[s, 0:WINDOW, :] = pre_ref[s]
            for j in range(N_KV_HEADS):
                buf[s, WINDOW:WINDOW + t, j * LANES:(j + 1) * LANES] = pj[name + "2"][j][s * t:(s + 1) * t]

    def get_q(c, head):
        return pj["qh"][head][c * CHUNK:(c + 1) * CHUNK]

    def get_kv(c, kv):
        cols = slice(kv * LANES, (kv + 1) * LANES)
        return kbuf[c, :, cols], vbuf[c, :, cols]

    def emit(c, tile, o):
        r, cols = slice(c * CHUNK, (c + 1) * CHUNK), slice(tile * LANES, (tile + 1) * LANES)
        xa_s[r, cols] = (o * pj["sa"][r, cols]).astype(BF16)

    def glu():
        so["xs"] = _glu(so.pop("y"), pj["u"], pj["sz"], d_ref[...], wglu_ref)

    src = dict(xa=lambda: xa_s[...], xs=lambda: so["xs"], ga=lambda: pj["ga"], gs=lambda: pj["gs"],
               x=lambda: x_ref[...], p=lambda: p_ref[...])

    def emit_y(y):
        y_ref[...] = y

    _run(_proj_steps(lambda: x_ref[...], gain_ref[...],
                     lambda: (cos_ref[...], sina_ref[...], sinb_ref[...]), w_in_ref, pj, store))
    _run(_attn_steps(sinks_ref, range(n), get_q, get_kv, lambda c: None, emit))
    _run([arrived(copies[:len(mid_w)])]
         + _ssm_steps(ubuf, 0, rows, None, None, hs, wlag_ref, a8r_ref, a8i_ref, ck_ref, hr_ref, hi_ref, so,
                      seg=(t, h0r_ref, h0i_ref, pr_ref, pi_ref)) + [glu])
    _run([arrived(copies[len(mid_w):])]
         + _out_steps(src, woa_ref, wos_ref, wout_ref, wpg_ref, wpp_ref, fg_ref[...], emit_y))


def _layer_sample(x, p, tabs, k_prefix, v_prefix, h0r, h0i, wts, consts):
    n, t, _ = x.shape
    assert t == CHUNK and tabs[0].shape[0] == n * t
    rows = n * t
    (gain, w_in, sinks, woa, d_skip, w_glu, wos, wout, wpg, wpp, fgain) = wts
    wlag, a8r, a8i, pr, pi, ck = consts
    operands = (x.reshape(rows, D_MODEL), p.reshape(rows, PLE_DIM), *tabs,
                _both_halves(k_prefix), _both_halves(v_prefix), h0r, h0i,
                gain, w_in, a8r, a8i, pr, pi, d_skip, fgain)
    later = (wlag, ck, w_glu, woa, wos, wout, wpg, wpp)
    out_shapes = [(rows, D_MODEL), (rows, KV_WIDTH), (rows, KV_WIDTH), h0r.shape, h0r.shape]
    scratch = [((n, KEYS, KV2_WIDTH), BF16), ((n, KEYS, KV2_WIDTH), BF16), ((rows, ATTN_WIDTH), BF16),
               ((rows + LAGS, SSM_WIDTH), F32), ((rows, N_STATE), BF16)] + [(w.shape, w.dtype) for w in later]
    vmem = (sum(a.size * a.dtype.itemsize for a in operands)
            + sum(int(np.prod(s)) * 4 for s in out_shapes)
            + sum(int(np.prod(s)) * np.dtype(d).itemsize for s, d in scratch)
            + 3 * rows * IN_WIDTH * 4)
    y, k, v, hr, hi = pl.pallas_call(
        functools.partial(_sample_kernel, n=n, t=t),
        grid=(1,),
        in_specs=([pl.BlockSpec(memory_space=pltpu.SMEM)] + [_const_spec(a.shape) for a in operands]
                  + [pl.BlockSpec(memory_space=pltpu.HBM)] * len(later)),
        out_specs=[pl.BlockSpec(s, lambda i, nd=len(s): (0,) * nd) for s in out_shapes],
        out_shape=[jax.ShapeDtypeStruct(s, F32) for s in out_shapes],
        scratch_shapes=([pltpu.VMEM(s, d) for s, d in scratch]
                        + [pltpu.SemaphoreType.DMA((len(later),))]),
        compiler_params=_params(vmem, 1),
        name="layer_sample",
    )(sinks, *operands, *later)
    k_new = jnp.concatenate([k_prefix[:, t:], k.reshape(n, t, KV_WIDTH)], axis=1)
    v_new = jnp.concatenate([v_prefix[:, t:], v.reshape(n, t, KV_WIDTH)], axis=1)
    return y.reshape(n, t, D_MODEL), k_new, v_new, hr, hi


def _both_halves(a):
    h0, h1 = a[..., :HEAD_DIM], a[..., HEAD_DIM:]
    return jnp.concatenate([h0, h0, h1, h1], axis=-1).astype(BF16)


def _ssm_constants(a_re, a_im, log_dt, b_re, b_im, c_re, c_im):
    dt = jnp.exp(log_dt.astype(F32))[:, None]
    lr = a_re.astype(F32).reshape(PAIRS, 1, 1, LANES)
    li = a_im.astype(F32).reshape(PAIRS, 1, 1, LANES)
    xr = (a_re.astype(F32) * dt).reshape(PAIRS, 1, 1, LANES)
    xi = (a_im.astype(F32) * dt).reshape(PAIRS, 1, 1, LANES)

    def apow(n):
        mag = jnp.exp(xr * n)
        return mag * jnp.cos(xi * n), mag * jnp.sin(xi * n)

    ar, ai = apow(1.0)
    nr, ni = ar - 1.0, ai
    den = lr * lr + li * li
    fr, fi = (nr * lr + ni * li) / den, (ni * lr - nr * li) / den

    n_slots = PAIR_K // SLOT
    qq, hi_ = np.arange(PAIRS)[:, None], np.arange(n_slots)[None, :]
    lag_tab = (PAIRS_PER_TILE * (hi_ // PAIRS_PER_TILE) + (hi_ % PAIRS_PER_TILE - qq) % PAIRS_PER_TILE)
    lag_tab = lag_tab.astype(np.float32)[:, :, None, None]
    same_group = (np.arange(SLOT)[:, None] // SSM_GROUP == np.arange(LANES)[None, :] // SSM_STATE)
    same_group = same_group.astype(np.float32)

    def b_rows(bm):
        t = jnp.transpose(bm.astype(F32).reshape(PAIRS, 2, SSM_STATE, SSM_GROUP), (0, 3, 1, 2))
        t = t.reshape(PAIRS, 1, 1, SSM_GROUP, LANES)
        t = jnp.broadcast_to(t, (PAIRS, 1, 2, SSM_GROUP, LANES))
        return t.reshape(PAIRS, 1, SLOT, LANES) * same_group

    br, bi = b_rows(b_re), b_rows(b_im)
    bbr, bbi = fr * br - fi * bi, fr * bi + fi * br
    er, ei = apow(lag_tab)
    wlag = jnp.concatenate([(er * bbr - ei * bbi).reshape(PAIRS, PAIR_K, LANES),
                            (er * bbi + ei * bbr).reshape(PAIRS, PAIR_K, LANES)], axis=-1).astype(BF16)

    a8r, a8i = (a.reshape(PAIRS, LANES) for a in apow(float(LAGS)))
    pwr, pwi = (a.reshape(PAIRS, LAGS, LANES)
                for a in apow(np.arange(1, LAGS + 1, dtype=np.float32)[None, :, None, None]))

    def c_cols(c):
        t = jnp.transpose(c.astype(F32).reshape(U_TILES, LANES // SSM_GROUP, SSM_GROUP, SSM_STATE),
                          (0, 3, 1, 2))
        return t.reshape(U_TILES, 1, 1, 1, SSM_STATE, LANES)

    cols_group = np.arange(LANES) // SSM_GROUP
    rows_group = 2 * np.arange(PAIRS_PER_TILE)[:, None] + np.arange(2)[None, :]
    c_mask = (rows_group[:, None, :, None, None] == cols_group[None, None, None, None, :])
    c_mask = c_mask.astype(np.float32)[None]
    ck = jnp.concatenate([c_cols(c_re) * c_mask, -c_cols(c_im) * c_mask], axis=2)
    ck = ck.reshape(U_TILES, PAIRS_PER_TILE * PAIR_N, LANES).astype(BF16)
    return wlag, a8r, a8i, pwr, pwi, ck


def _rope_tables(pos0, t, rows):
    half = ROT_DIM // 2
    d = np.arange(LANES) % HEAD_DIM
    inv = jnp.power(ROPE_THETA, -jnp.arange(half, dtype=F32) * 2.0 / ROT_DIM)
    pos = (pos0 + jnp.arange(t)).astype(F32)
    ang = pos[:, None] * inv[None, :]
    cos, sin = (jnp.tile(a, (1, LANES // half)) for a in (jnp.cos(ang), jnp.sin(ang)))
    cos_t = jnp.where((d < ROT_DIM)[None, :], cos, 1.0)
    sina = jnp.where(((d >= half) & (d < ROT_DIM))[None, :], sin, 0.0)
    sinb = jnp.where((d < half)[None, :], -sin, 0.0)
    reps = (max(rows // t, 1), 1)
    return tuple(jnp.tile(a, reps) for a in (cos_t, sina, sinb))


def kernel(x_prompt, x_sample, p_prompt, p_sample, cache_attn_k, cache_attn_v, state_ssm_re,
           state_ssm_im, norm_gain, w_in, attn_sinks, w_o_attn, ssm_a_re, ssm_a_im, ssm_log_dt,
           ssm_b_re, ssm_b_im, ssm_c_re, ssm_c_im, ssm_d, ssm_w_glu, w_o_ssm, w_out,
           w_ple_gate, w_ple_proj, final_norm_gain):
    assert norm_gain.shape[0] == 1, "single-layer model"
    bp, tp, _ = x_prompt.shape
    bs, ts, _ = x_sample.shape
    wts = (norm_gain[0].reshape(1, D_MODEL).astype(F32), w_in[0].astype(BF16),
           attn_sinks[0].astype(F32), w_o_attn[0].astype(BF16),
           ssm_d[0].reshape(1, SSM_WIDTH).astype(F32), ssm_w_glu[0].astype(BF16),
           w_o_ssm[0].astype(BF16), w_out[0].astype(BF16), w_ple_gate[0].astype(BF16),
           w_ple_proj[0].astype(BF16), final_norm_gain.reshape(1, D_MODEL).astype(F32))
    consts = _ssm_constants(ssm_a_re[0], ssm_a_im[0], ssm_log_dt[0], ssm_b_re[0], ssm_b_im[0],
                            ssm_c_re[0], ssm_c_im[0])

    y_p, k_p, v_p, hr_p, hi_p = _layer_fused(x_prompt, p_prompt[0], _rope_tables(0, tp, tp), wts, consts)

    ck = cache_attn_k[0].reshape(bs, WINDOW, KV_WIDTH).astype(F32)
    cv = cache_attn_v[0].reshape(bs, WINDOW, KV_WIDTH).astype(F32)
    h0r = state_ssm_re[0].reshape(bs, PAIRS, LANES).astype(F32)
    h0i = state_ssm_im[0].reshape(bs, PAIRS, LANES).astype(F32)
    tabs_s = _rope_tables(PAST_LEN, ts, bs * ts)
    y_s, k_s, v_s, hr_s, hi_s = _layer_sample(x_sample, p_sample[0], tabs_s, ck, cv, h0r, h0i, wts, consts)

    def kv_out(a, b):
        return a.reshape(1, b, WINDOW, N_KV_HEADS, HEAD_DIM)

    def st_out(a, b):
        return a.reshape(1, b, SSM_GROUPS, SSM_STATE)

    return (y_p, y_s, kv_out(k_p, bp), kv_out(v_p, bp),
            st_out(hr_p, bp), st_out(hi_p, bp), kv_out(k_s, bs), kv_out(v_s, bs),
            st_out(hr_s, bs), st_out(hi_s, bs))
```

```python
import functools

import numpy as np
import jax
import jax.numpy as jnp
from jax import lax
from jax.experimental import pallas as pl
from jax.experimental.pallas import tpu as pltpu

F32 = jnp.float32
BF16 = jnp.bfloat16

LANES = 128
SUBLANES = 8
V7X_VMEM_BYTES = 64 * 1024 * 1024

D_MODEL = 1024
CHUNK = 64
WINDOW = 128
N_HEADS = 8
N_KV_HEADS = 2
HEAD_DIM = 64
Q_PER_KV = N_HEADS // N_KV_HEADS
LOG2E = 1.4426950408889634
Q_SCALE = HEAD_DIM ** -0.5 * LOG2E
ATTN_WIDTH = N_HEADS * HEAD_DIM
KV_WIDTH = N_KV_HEADS * HEAD_DIM
ROT_DIM = HEAD_DIM // 4
ROPE_THETA = 500000.0
SSM_WIDTH = D_MODEL // 2
SSM_GROUP = 16
SSM_GROUPS = SSM_WIDTH // SSM_GROUP
SSM_STATE = 64
PLE_DIM = 256
PAST_LEN = 1024
EPS = 1e-6

O_Q = 0
O_K = O_Q + ATTN_WIDTH
O_V = O_K + KV_WIDTH
O_ZA = O_V + KV_WIDTH
O_U = O_ZA + ATTN_WIDTH
O_ZS = O_U + SSM_WIDTH
O_GA = O_ZS + SSM_WIDTH
O_GS = O_GA + D_MODEL
IN_WIDTH = O_GS + D_MODEL

QM_WIDTH = N_HEADS * LANES
KV2_WIDTH = N_KV_HEADS * LANES
KEYS = WINDOW + CHUNK
LAGS = SUBLANES
PAIRS = SSM_GROUPS // 2
PAIR_K = 2 * LAGS * SSM_GROUP
PAIR_N = 2 * 2 * SSM_STATE
N_STATE = PAIRS * PAIR_N
U_TILES = SSM_WIDTH // LANES
PAIRS_PER_TILE = PAIRS // U_TILES
SLOT = 2 * SSM_GROUP
BF16_ROWS = 2 * SUBLANES
assert PAIRS_PER_TILE == 4 and LAGS == 2 * PAIRS_PER_TILE

LAYER_ROWS = 512
W_CHUNK = 64


def _sigmoid(x):
    return 1.0 / (1.0 + jnp.exp2(x * (-LOG2E)))


def _const_spec(shape):
    zeros = (0,) * len(shape)
    return pl.BlockSpec(shape, lambda *_: zeros, pipeline_mode=pl.Buffered(1))


def _params(vmem_bytes, n_grid):
    return pltpu.CompilerParams(
        dimension_semantics=("arbitrary",) * n_grid,
        vmem_limit_bytes=min(int(vmem_bytes), V7X_VMEM_BYTES - 8 * 1024 * 1024),
    )


def _run(steps):
    for step in steps:
        step()


def _spread(main, other):
    merged, j = [], 0
    for i, step in enumerate(main):
        while j < len(other) and j * len(main) <= i * len(other):
            merged.append(other[j])
            j += 1
        merged.append(step)
    return merged + other[j:]


def _proj_steps(get_x, gain, get_tabs, w_ref, o, store=None):
    st = {}

    def norm():
        x = get_x()
        ms = jnp.mean(x * x, axis=-1, keepdims=True)
        st["xn"] = (x * lax.rsqrt(ms + EPS) * gain).astype(BF16)
        st["lo"] = lax.broadcasted_iota(jnp.int32, (x.shape[0], LANES), 1) < HEAD_DIM

    def seg(a, b):
        return jnp.dot(st["xn"], w_ref[:, a:b], preferred_element_type=F32)

    def rope(t):
        cos, sina, sinb = get_tabs()
        return (t * cos + pltpu.roll(t, ROT_DIM // 2, 1) * sina
                + pltpu.roll(t, LANES - ROT_DIM // 2, 1) * sinb)

    def both_halves(t):
        tr = pltpu.roll(t, HEAD_DIM, 1)
        return [jnp.where(st["lo"], t, tr).astype(BF16), jnp.where(st["lo"], tr, t).astype(BF16)]

    def done(name):
        if store is not None:
            store(name)

    def q():
        zq = seg(O_Q, O_K)
        o["qh"] = []
        for j in range(ATTN_WIDTH // LANES):
            qt = rope(zq[:, j * LANES:(j + 1) * LANES]) * Q_SCALE
            o["qh"] += [jnp.where(st["lo"], qt, 0.0).astype(BF16),
                        jnp.where(st["lo"], 0.0, qt).astype(BF16)]

    def kv():
        z = seg(O_K, O_ZA)
        o["k"] = rope(z[:, :KV_WIDTH])
        o["k2"] = both_halves(o["k"])
        done("k")
        o["v"] = z[:, KV_WIDTH:]
        o["v2"] = both_halves(o["v"])
        done("v")

    def za():
        z = seg(O_ZA, O_U)
        o["sa"] = z * _sigmoid(z)

    def u():
        o["u"] = seg(O_U, O_ZS)
        done("u")

    def zs():
        z = seg(O_ZS, O_GA)
        o["sz"] = z * _sigmoid(z)

    def ga():
        o["ga"] = _sigmoid(seg(O_GA, O_GS))

    def gs():
        o["gs"] = _sigmoid(seg(O_GS, IN_WIDTH))

    return [norm, q, kv, za, u, zs, ga, gs]


def _attn_steps(sinks_ref, chunks, get_q, get_kv, get_valid, emit):
    nt = (((1,), (1,)), ((), ()))
    units = [(c, kv) for c in chunks for kv in range(N_KV_HEADS)]
    n = len(units)
    st = {}

    def scores(c, kv):
        k2, v2 = get_kv(c, kv)
        qm = jnp.concatenate([get_q(c, kv * Q_PER_KV + h) for h in range(Q_PER_KV)], axis=0)
        s = lax.dot_general(qm, k2, nt, preferred_element_type=F32)
        valid = get_valid(c)
        if valid is not None:
            s = jnp.where(valid, s, -jnp.inf)
        return s, v2

    def softmax(s, kv):
        head_row = lax.broadcasted_iota(jnp.int32, (Q_PER_KV * CHUNK, 1), 0) // CHUNK
        sk = [sinks_ref[kv * Q_PER_KV + h] * LOG2E for h in range(Q_PER_KV)]
        sink = jnp.where(head_row == 0, sk[0],
                         jnp.where(head_row == 1, sk[1], jnp.where(head_row == 2, sk[2], sk[3])))
        m = jnp.maximum(jnp.max(s, axis=1, keepdims=True), sink)
        return jnp.exp2(s - m).astype(BF16), jnp.exp2(sink - m)

    def output(e, sink_term, v2, c, kv):
        lo_q = lax.broadcasted_iota(jnp.int32, (CHUNK, LANES), 1) < HEAD_DIM
        ones = jnp.ones((KEYS, LANES), BF16)
        pv = jnp.dot(e, jnp.concatenate([v2, ones], axis=1), preferred_element_type=F32)
        o = pv[:, :LANES] / (pv[:, LANES:] + sink_term)
        for j in range(Q_PER_KV // 2):
            even = o[2 * j * CHUNK:(2 * j + 1) * CHUNK]
            odd = o[(2 * j + 1) * CHUNK:(2 * j + 2) * CHUNK]
            emit(c, kv * (Q_PER_KV // 2) + j, jnp.where(lo_q, even, odd))

    def make(i):
        def step():
            if i < n:
                st[i] = scores(*units[i])
            if 0 <= i - 1 < n:
                s, v2 = st[i - 1]
                st[i - 1] = softmax(s, units[i - 1][1]) + (v2,)
            if 0 <= i - 2 < n:
                e, den, v2 = st.pop(i - 2)
                output(e, den, v2, *units[i - 2])
        return step

    return [make(i) for i in range(n + 2)]


def _ssm_reset(ubuf, cr_s, ci_s):
    ubuf[0:LAGS, :] = jnp.zeros((LAGS, SSM_WIDTH), F32)
    cr_s[...] = jnp.zeros(cr_s.shape, F32)
    ci_s[...] = jnp.zeros(ci_s.shape, F32)


def _ssm_steps(ubuf, row0, tt, cr_s, ci_s, hs, wlag_ref, a8r_ref, a8i_ref, ck_ref, hr_ref, hi_ref, o,
               seg=None):
    st = {"ys": []}

    def setup():
        if seg is not None:
            st["row_in_seg"] = lax.broadcasted_iota(jnp.int32, (tt, LANES), 0) % seg[0]
        slot = lax.broadcasted_iota(jnp.int32, (tt, LANES), 1) // SLOT
        st["to_low"] = [((slot + PAIRS_PER_TILE - s) % PAIRS_PER_TILE) < 2 for s in range(2)]
        st["same_parity"] = [((slot + sg) % 2) == 0 for sg in range(2)]

    def route(r):
        low = [jnp.where(st["to_low"][s], r[s], r[s + 2]) for s in range(2)]
        high = [jnp.where(st["to_low"][s], r[s + 2], r[s]) for s in range(2)]
        return [jnp.where(st["same_parity"][sg % 2], src[0], src[1])
                for sg, src in zip(range(PAIRS_PER_TILE), (low, low, high, high))]

    def lag_copies(k):
        def step():
            ub = ubuf[row0:row0 + LAGS + tt, k * LANES:(k + 1) * LANES]
            rolled = []
            for s in range(LAGS):
                us = ub[LAGS:] if s == 0 else pltpu.roll(ub, s, 0)[LAGS:]
                if seg is not None and s > 0:
                    us = jnp.where(st["row_in_seg"] >= s, us, 0.0)
                if s % PAIRS_PER_TILE:
                    us = pltpu.roll(us, SLOT * (s % PAIRS_PER_TILE), 1)
                rolled.append(us)
            st["halves"] = (route(rolled[:PAIRS_PER_TILE]), route(rolled[PAIRS_PER_TILE:]))
        return step

    def pair(k, sg):
        def step():
            q = k * PAIRS_PER_TILE + sg
            xl = jnp.concatenate([st["halves"][0][sg], st["halves"][1][sg]], axis=1).astype(BF16)
            w = jnp.dot(xl, wlag_ref[q], preferred_element_type=F32)
            ar, ai = a8r_ref[q:q + 1, :], a8i_ref[q:q + 1, :]
            if seg is None:
                cr, ci = cr_s[q], ci_s[q]
            else:
                seg_rows, h0r_ref, h0i_ref, pr_ref, pi_ref = seg
            for b2 in range(tt // BF16_ROWS):
                hrs, his = [], []
                for b in (2 * b2, 2 * b2 + 1):
                    if seg is not None and (b * SUBLANES) % seg_rows == 0:
                        n = b * SUBLANES // seg_rows
                        h0r, h0i = h0r_ref[n, q:q + 1, :], h0i_ref[n, q:q + 1, :]
                        cr = pr_ref[q] * h0r - pi_ref[q] * h0i
                        ci = pr_ref[q] * h0i + pi_ref[q] * h0r
                    blk = slice(b * SUBLANES, (b + 1) * SUBLANES)
                    hr = w[blk, :LANES] + cr
                    hi = w[blk, LANES:] + ci
                    cr = ar * hr - ai * hi
                    ci = ar * hi + ai * hr
                    hrs.append(hr)
                    his.append(hi)
                    if seg is not None and ((b + 1) * SUBLANES) % seg_rows == 0:
                        n = b * SUBLANES // seg_rows
                        hr_ref[n, q:q + 1, :] = hr[SUBLANES - 1:, :]
                        hi_ref[n, q:q + 1, :] = hi[SUBLANES - 1:, :]
                blk2 = slice(b2 * BF16_ROWS, (b2 + 1) * BF16_ROWS)
                hs[blk2, q * PAIR_N:q * PAIR_N + LANES] = jnp.concatenate(hrs, axis=0).astype(BF16)
                hs[blk2, q * PAIR_N + LANES:(q + 1) * PAIR_N] = jnp.concatenate(his, axis=0).astype(BF16)
            if seg is None:
                cr_s[q] = cr
                ci_s[q] = ci
                hr_ref[0, q:q + 1, :] = hr[SUBLANES - 1:, :]
                hi_ref[0, q:q + 1, :] = hi[SUBLANES - 1:, :]
        return step

    def c_proj(k):
        def step():
            cols = slice(k * PAIRS_PER_TILE * PAIR_N, (k + 1) * PAIRS_PER_TILE * PAIR_N)
            st["ys"].append(jnp.dot(hs[:, cols], ck_ref[k], preferred_element_type=F32))
            if k == U_TILES - 1:
                o["y"] = jnp.concatenate(st["ys"], axis=1)
        return step

    steps = [setup]
    for k in range(U_TILES):
        steps += [lag_copies(k)] + [pair(k, sg) for sg in range(PAIRS_PER_TILE)] + [c_proj(k)]
    return steps


def _glu(y, u, sz, d, wglu_ref):
    z = jax.nn.gelu(y + d * u)
    g = jnp.dot(z.astype(BF16), wglu_ref[...], preferred_element_type=F32)
    return z * _sigmoid(g) * sz


def _out_steps(src, woa_ref, wos_ref, wout_ref, wpg_ref, wpp_ref, fgain, emit):
    st = {}

    def mm(a, w_ref):
        return jnp.dot(a.astype(BF16), w_ref[...], preferred_element_type=F32)

    def branches():
        st["merged"] = src["ga"]() * mm(src["xa"](), woa_ref) + src["gs"]() * mm(src["xs"](), wos_ref)

    def residual():
        st["h"] = src["x"]() + mm(st.pop("merged"), wout_ref)

    def embed_gate():
        h = st.pop("h")
        st["h"] = h + _sigmoid(mm(h, wpg_ref)) * mm(src["p"](), wpp_ref)

    def norm():
        h = st.pop("h")
        ms = jnp.mean(h * h, axis=-1, keepdims=True)
        emit(h * lax.rsqrt(ms + EPS) * fgain)

    return [branches, residual, embed_gate, norm]


def _layer_kernel(sinks_ref, x_ref, p_ref, cos_ref, sina_ref, sinb_ref, gain_ref, w_in_hbm,
                  wlag_ref, a8r_ref, a8i_ref, ck_ref, d_ref, wglu_ref,
                  woa_ref, wos_ref, wout_ref, wpg_ref, wpp_ref, fg_ref,
                  y_ref, k_ref, v_ref, hr_ref, hi_ref, w_bf16_hbm,
                  kbuf, vbuf, xa_s, ubuf, cr_s, ci_s, hs, w_in_ref, stage, sems, *, tt):
    t = pl.program_id(1)
    half = tt // 2
    first = jnp.logical_and(pl.program_id(0) == 0, t == 0)
    n_chunks = D_MODEL // W_CHUNK
    export = pltpu.make_async_copy(w_in_ref, w_bf16_hbm, sems.at[2])

    def chunk(c):
        return pltpu.make_async_copy(w_in_hbm.at[pl.ds(c * W_CHUNK, W_CHUNK)], stage.at[c % 2], sems.at[c % 2])

    @pl.when(first)
    def _():
        chunk(0).start()
        for c in range(n_chunks):
            if c + 1 < n_chunks:
                chunk(c + 1).start()
            chunk(c).wait()
            w_in_ref[c * W_CHUNK:(c + 1) * W_CHUNK, :] = stage[c % 2].astype(BF16)
        export.start()

    @pl.when(t == 0)
    def _():
        kbuf[0:WINDOW, :] = jnp.zeros((WINDOW, KV2_WIDTH), BF16)
        vbuf[0:WINDOW, :] = jnp.zeros((WINDOW, KV2_WIDTH), BF16)
        _ssm_reset(ubuf, cr_s, ci_s)

    pj, so = [{}, {}], [{}, {}]

    def rows(h):
        return slice(h * half, (h + 1) * half)

    def proj(h):
        def store(name):
            if name == "u":
                ubuf[LAGS + h * half:LAGS + (h + 1) * half, :] = pj[h]["u"]
                return
            last_ref, buf = (k_ref, kbuf) if name == "k" else (v_ref, vbuf)
            if h == 1:
                last_ref[0] = pj[h][name][half - WINDOW:]
            for j in range(N_KV_HEADS):
                buf[WINDOW + h * half:WINDOW + (h + 1) * half, j * LANES:(j + 1) * LANES] = pj[h][name + "2"][j]

        return _proj_steps(lambda: x_ref[0, rows(h)], gain_ref[...],
                           lambda: (cos_ref[rows(h)], sina_ref[rows(h)], sinb_ref[rows(h)]),
                           w_in_ref, pj[h], store)

    def mid(h):
        def get_q(c, head):
            r0 = c * CHUNK - h * half
            return pj[h]["qh"][head][r0:r0 + CHUNK]

        def get_kv(c, kv):
            krows, cols = slice(c * CHUNK, c * CHUNK + KEYS), slice(kv * LANES, (kv + 1) * LANES)
            return kbuf[krows, cols], vbuf[krows, cols]

        def get_valid(c):
            if c * CHUNK >= WINDOW:
                return None
            in_seq = c * CHUNK + lax.broadcasted_iota(jnp.int32, (1, KEYS), 1) >= WINDOW
            return jnp.logical_or(in_seq, t > 0)

        def emit(c, tile, o):
            r0, cols = c * CHUNK - h * half, slice(tile * LANES, (tile + 1) * LANES)
            xa_s[c * CHUNK:(c + 1) * CHUNK, cols] = (o * pj[h]["sa"][r0:r0 + CHUNK, cols]).astype(BF16)

        def glu():
            so[h]["xs"] = _glu(so[h].pop("y"), pj[h]["u"], pj[h]["sz"], d_ref[...], wglu_ref)

        chunks = range(h * half // CHUNK, (h + 1) * half // CHUNK)
        return (_attn_steps(sinks_ref, chunks, get_q, get_kv, get_valid, emit)
                + _ssm_steps(ubuf, h * half, half, cr_s, ci_s, hs, wlag_ref, a8r_ref, a8i_ref, ck_ref,
                             hr_ref, hi_ref, so[h])
                + [glu])

    def out(h):
        src = dict(xa=lambda: xa_s[rows(h)], xs=lambda: so[h]["xs"], ga=lambda: pj[h]["ga"],
                   gs=lambda: pj[h]["gs"], x=lambda: x_ref[0, rows(h)], p=lambda: p_ref[0, rows(h)])

        def emit(y):
            y_ref[0, rows(h)] = y

        return _out_steps(src, woa_ref, wos_ref, wout_ref, wpg_ref, wpp_ref, fg_ref[...], emit)

    _run(proj(0))
    _run(_spread(mid(0), proj(1)))
    _run(_spread(mid(1), out(0)))
    _run(out(1))

    kbuf[0:WINDOW, :] = kbuf[tt:tt + WINDOW, :]
    vbuf[0:WINDOW, :] = vbuf[tt:tt + WINDOW, :]
    ubuf[0:LAGS, :] = ubuf[tt:tt + LAGS, :]

    @pl.when(first)
    def _():
        export.wait()


def _layer_fused(x, p, tabs, wts, consts):
    b, t, _ = x.shape
    tt = min(LAYER_ROWS, t)
    assert t % tt == 0 and tt // 2 >= WINDOW and (tt // 2) % BF16_ROWS == 0 and tabs[0].shape[0] == t
    (gain, w_in, sinks, woa, d_skip, w_glu, wos, wout, wpg, wpp, fgain) = wts
    wlag, a8r, a8i, _, _, ck = consts

    def row_spec(w):
        return pl.BlockSpec((1, tt, w), lambda i, j: (i, j, 0))

    tab_spec = pl.BlockSpec((tt, LANES), lambda i, j: (j, 0))
    st_spec = pl.BlockSpec((1, PAIRS, LANES), lambda i, j: (i, 0, 0))
    win_spec = pl.BlockSpec((1, WINDOW, KV_WIDTH), lambda i, j: (i, 0, 0))
    consts_in = (gain, w_in, wlag, a8r, a8i, ck, d_skip, w_glu, woa, wos, wout, wpg, wpp, fgain)
    hbm_spec = pl.BlockSpec(memory_space=pltpu.HBM)
    vmem = (sum(a.size * a.dtype.itemsize for a in consts_in if a is not w_in)
            + w_in.size * 2 + 2 * W_CHUNK * IN_WIDTH * 4
            + 2 * tt * (2 * D_MODEL + PLE_DIM + 2 * KV_WIDTH + 3 * LANES) * 4
            + 2 * (WINDOW + tt) * KV2_WIDTH * 2 + tt * ATTN_WIDTH * 2 + (tt + LAGS) * SSM_WIDTH * 4
            + (tt // 2) * N_STATE * 2 + 3 * tt * IN_WIDTH * 4)
    y, k, v, hr, hi, w_bf16 = pl.pallas_call(
        functools.partial(_layer_kernel, tt=tt),
        grid=(b, t // tt),
        in_specs=[pl.BlockSpec(memory_space=pltpu.SMEM), row_spec(D_MODEL), row_spec(PLE_DIM),
                  tab_spec, tab_spec, tab_spec]
                 + [hbm_spec if a is w_in else _const_spec(a.shape) for a in consts_in],
        out_specs=[row_spec(D_MODEL), win_spec, win_spec, st_spec, st_spec, hbm_spec],
        out_shape=[jax.ShapeDtypeStruct((b, t, D_MODEL), F32),
                   jax.ShapeDtypeStruct((b, WINDOW, KV_WIDTH), F32),
                   jax.ShapeDtypeStruct((b, WINDOW, KV_WIDTH), F32),
                   jax.ShapeDtypeStruct((b, PAIRS, LANES), F32),
                   jax.ShapeDtypeStruct((b, PAIRS, LANES), F32),
                   jax.ShapeDtypeStruct(w_in.shape, BF16)],
        scratch_shapes=[pltpu.VMEM((WINDOW + tt, KV2_WIDTH), BF16),
                        pltpu.VMEM((WINDOW + tt, KV2_WIDTH), BF16),
                        pltpu.VMEM((tt, ATTN_WIDTH), BF16),
                        pltpu.VMEM((tt + LAGS, SSM_WIDTH), F32),
                        pltpu.VMEM((PAIRS, SUBLANES, LANES), F32),
                        pltpu.VMEM((PAIRS, SUBLANES, LANES), F32),
                        pltpu.VMEM((tt // 2, N_STATE), BF16),
                        pltpu.VMEM(w_in.shape, BF16),
                        pltpu.VMEM((2, W_CHUNK, IN_WIDTH), F32),
                        pltpu.SemaphoreType.DMA((3,))],
        compiler_params=_params(vmem, 2),
        name="layer_prompt",
    )(sinks, x, p, *tabs, *consts_in)
    return y, k, v, hr, hi, w_bf16


def _sample_kernel(sinks_ref, x_ref, p_ref, cos_ref, sina_ref, sinb_ref, kpre_ref, vpre_ref,
                   h0r_ref, h0i_ref, gain_ref, w_in_ref, wlag_ref, a8r_ref, a8i_ref, pr_ref, pi_ref,
                   ck_ref, d_ref, wglu_ref, woa_ref, wos_ref, wout_ref, wpg_ref, wpp_ref, fg_ref,
                   y_ref, k_ref, v_ref, hr_ref, hi_ref,
                   kbuf, vbuf, xa_s, ubuf, hs, *, n, t):
    rows = n * t
    pj, so = {}, {}

    def store(name):
        if name == "u":
            ubuf[0:LAGS, :] = jnp.zeros((LAGS, SSM_WIDTH), F32)
            ubuf[LAGS:, :] = pj["u"]
            return
        full_ref, pre_ref, buf = (k_ref, kpre_ref, kbuf) if name == "k" else (v_ref, vpre_ref, vbuf)
        full_ref[...] = pj[name]
        for s in range(n):
            buf[s, 0:WINDOW, :] = pre_ref[s]
            for j in range(N_KV_HEADS):
                buf[s, WINDOW:WINDOW + t, j * LANES:(j + 1) * LANES] = pj[name + "2"][j][s * t:(s + 1) * t]

    def get_q(c, head):
        return pj["qh"][head][c * CHUNK:(c + 1) * CHUNK]

    def get_kv(c, kv):
        cols = slice(kv * LANES, (kv + 1) * LANES)
        return kbuf[c, :, cols], vbuf[c, :, cols]

    def emit(c, tile, o):
        r, cols = slice(c * CHUNK, (c + 1) * CHUNK), slice(tile * LANES, (tile + 1) * LANES)
        xa_s[r, cols] = (o * pj["sa"][r, cols]).astype(BF16)

    def glu():
        so["xs"] = _glu(so.pop("y"), pj["u"], pj["sz"], d_ref[...], wglu_ref)

    src = dict(xa=lambda: xa_s[...], xs=lambda: so["xs"], ga=lambda: pj["ga"], gs=lambda: pj["gs"],
               x=lambda: x_ref[...], p=lambda: p_ref[...])

    def emit_y(y):
        y_ref[...] = y

    _run(_proj_steps(lambda: x_ref[...], gain_ref[...],
                     lambda: (cos_ref[...], sina_ref[...], sinb_ref[...]), w_in_ref, pj, store))
    _run(_attn_steps(sinks_ref, range(n), get_q, get_kv, lambda c: None, emit))
    _run(_ssm_steps(ubuf, 0, rows, None, None, hs, wlag_ref, a8r_ref, a8i_ref, ck_ref, hr_ref, hi_ref, so,
                    seg=(t, h0r_ref, h0i_ref, pr_ref, pi_ref)) + [glu])
    _run(_out_steps(src, woa_ref, wos_ref, wout_ref, wpg_ref, wpp_ref, fg_ref[...], emit_y))


def _layer_sample(x, p, tabs, k_prefix, v_prefix, h0r, h0i, wts, consts):
    n, t, _ = x.shape
    assert t == CHUNK and tabs[0].shape[0] == n * t
    rows = n * t
    (gain, w_in, sinks, woa, d_skip, w_glu, wos, wout, wpg, wpp, fgain) = wts
    wlag, a8r, a8i, pr, pi, ck = consts
    operands = (x.reshape(rows, D_MODEL), p.reshape(rows, PLE_DIM), *tabs,
                _both_halves(k_prefix), _both_halves(v_prefix), h0r, h0i,
                gain, w_in, wlag, a8r, a8i, pr, pi, ck, d_skip, w_glu, woa, wos, wout, wpg, wpp, fgain)
    out_shapes = [(rows, D_MODEL), (rows, KV_WIDTH), (rows, KV_WIDTH), h0r.shape, h0r.shape]
    scratch = [((n, KEYS, KV2_WIDTH), BF16), ((n, KEYS, KV2_WIDTH), BF16), ((rows, ATTN_WIDTH), BF16),
               ((rows + LAGS, SSM_WIDTH), F32), ((rows, N_STATE), BF16)]
    vmem = (sum(a.size * a.dtype.itemsize for a in operands)
            + sum(int(np.prod(s)) * 4 for s in out_shapes)
            + sum(int(np.prod(s)) * np.dtype(d).itemsize for s, d in scratch)
            + 3 * rows * IN_WIDTH * 4)
    y, k, v, hr, hi = pl.pallas_call(
        functools.partial(_sample_kernel, n=n, t=t),
        grid=(1,),
        in_specs=[pl.BlockSpec(memory_space=pltpu.SMEM)] + [_const_spec(a.shape) for a in operands],
        out_specs=[pl.BlockSpec(s, lambda i, nd=len(s): (0,) * nd) for s in out_shapes],
        out_shape=[jax.ShapeDtypeStruct(s, F32) for s in out_shapes],
        scratch_shapes=[pltpu.VMEM(s, d) for s, d in scratch],
        compiler_params=_params(vmem, 1),
        name="layer_sample",
    )(sinks, *operands)
    k_new = jnp.concatenate([k_prefix[:, t:], k.reshape(n, t, KV_WIDTH)], axis=1)
    v_new = jnp.concatenate([v_prefix[:, t:], v.reshape(n, t, KV_WIDTH)], axis=1)
    return y.reshape(n, t, D_MODEL), k_new, v_new, hr, hi


def _both_halves(a):
    h0, h1 = a[..., :HEAD_DIM], a[..., HEAD_DIM:]
    return jnp.concatenate([h0, h0, h1, h1], axis=-1).astype(BF16)


def _ssm_constants(a_re, a_im, log_dt, b_re, b_im, c_re, c_im):
    dt = jnp.exp(log_dt.astype(F32))[:, None]
    lr = a_re.astype(F32).reshape(PAIRS, 1, 1, LANES)
    li = a_im.astype(F32).reshape(PAIRS, 1, 1, LANES)
    xr = (a_re.astype(F32) * dt).reshape(PAIRS, 1, 1, LANES)
    xi = (a_im.astype(F32) * dt).reshape(PAIRS, 1, 1, LANES)

    def apow(n):
        mag = jnp.exp(xr * n)
        return mag * jnp.cos(xi * n), mag * jnp.sin(xi * n)

    ar, ai = apow(1.0)
    nr, ni = ar - 1.0, ai
    den = lr * lr + li * li
    fr, fi = (nr * lr + ni * li) / den, (ni * lr - nr * li) / den

    n_slots = PAIR_K // SLOT
    qq, hi_ = np.arange(PAIRS)[:, None], np.arange(n_slots)[None, :]
    lag_tab = (PAIRS_PER_TILE * (hi_ // PAIRS_PER_TILE) + (hi_ % PAIRS_PER_TILE - qq) % PAIRS_PER_TILE)
    lag_tab = lag_tab.astype(np.float32)[:, :, None, None]
    same_group = (np.arange(SLOT)[:, None] // SSM_GROUP == np.arange(LANES)[None, :] // SSM_STATE)
    same_group = same_group.astype(np.float32)

    def b_rows(bm):
        t = jnp.transpose(bm.astype(F32).reshape(PAIRS, 2, SSM_STATE, SSM_GROUP), (0, 3, 1, 2))
        t = t.reshape(PAIRS, 1, 1, SSM_GROUP, LANES)
        t = jnp.broadcast_to(t, (PAIRS, 1, 2, SSM_GROUP, LANES))
        return t.reshape(PAIRS, 1, SLOT, LANES) * same_group

    br, bi = b_rows(b_re), b_rows(b_im)
    bbr, bbi = fr * br - fi * bi, fr * bi + fi * br
    er, ei = apow(lag_tab)
    wlag = jnp.concatenate([(er * bbr - ei * bbi).reshape(PAIRS, PAIR_K, LANES),
                            (er * bbi + ei * bbr).reshape(PAIRS, PAIR_K, LANES)], axis=-1).astype(BF16)

    a8r, a8i = (a.reshape(PAIRS, LANES) for a in apow(float(LAGS)))
    pwr, pwi = (a.reshape(PAIRS, LAGS, LANES)
                for a in apow(np.arange(1, LAGS + 1, dtype=np.float32)[None, :, None, None]))

    def c_cols(c):
        t = jnp.transpose(c.astype(F32).reshape(U_TILES, LANES // SSM_GROUP, SSM_GROUP, SSM_STATE),
                          (0, 3, 1, 2))
        return t.reshape(U_TILES, 1, 1, 1, SSM_STATE, LANES)

    cols_group = np.arange(LANES) // SSM_GROUP
    rows_group = 2 * np.arange(PAIRS_PER_TILE)[:, None] + np.arange(2)[None, :]
    c_mask = (rows_group[:, None, :, None, None] == cols_group[None, None, None, None, :])
    c_mask = c_mask.astype(np.float32)[None]
    ck = jnp.concatenate([c_cols(c_re) * c_mask, -c_cols(c_im) * c_mask], axis=2)
    ck = ck.reshape(U_TILES, PAIRS_PER_TILE * PAIR_N, LANES).astype(BF16)
    return wlag, a8r, a8i, pwr, pwi, ck


def _rope_tables(pos0, t, rows):
    half = ROT_DIM // 2
    d = np.arange(LANES) % HEAD_DIM
    inv = jnp.power(ROPE_THETA, -jnp.arange(half, dtype=F32) * 2.0 / ROT_DIM)
    pos = (pos0 + jnp.arange(t)).astype(F32)
    ang = pos[:, None] * inv[None, :]
    cos, sin = (jnp.tile(a, (1, LANES // half)) for a in (jnp.cos(ang), jnp.sin(ang)))
    cos_t = jnp.where((d < ROT_DIM)[None, :], cos, 1.0)
    sina = jnp.where(((d >= half) & (d < ROT_DIM))[None, :], sin, 0.0)
    sinb = jnp.where((d < half)[None, :], -sin, 0.0)
    reps = (max(rows // t, 1), 1)
    return tuple(jnp.tile(a, reps) for a in (cos_t, sina, sinb))


def kernel(x_prompt, x_sample, p_prompt, p_sample, cache_attn_k, cache_attn_v, state_ssm_re,
           state_ssm_im, norm_gain, w_in, attn_sinks, w_o_attn, ssm_a_re, ssm_a_im, ssm_log_dt,
           ssm_b_re, ssm_b_im, ssm_c_re, ssm_c_im, ssm_d, ssm_w_glu, w_o_ssm, w_out,
           w_ple_gate, w_ple_proj, final_norm_gain):
    assert norm_gain.shape[0] == 1, "single-layer model"
    bp, tp, _ = x_prompt.shape
    bs, ts, _ = x_sample.shape
    wts = (norm_gain[0].reshape(1, D_MODEL).astype(F32), w_in[0].astype(F32),
           attn_sinks[0].astype(F32), w_o_attn[0].astype(BF16),
           ssm_d[0].reshape(1, SSM_WIDTH).astype(F32), ssm_w_glu[0].astype(BF16),
           w_o_ssm[0].astype(BF16), w_out[0].astype(BF16), w_ple_gate[0].astype(BF16),
           w_ple_proj[0].astype(BF16), final_norm_gain.reshape(1, D_MODEL).astype(F32))
    consts = _ssm_constants(ssm_a_re[0], ssm_a_im[0], ssm_log_dt[0], ssm_b_re[0], ssm_b_im[0],
                            ssm_c_re[0], ssm_c_im[0])

    y_p, k_p, v_p, hr_p, hi_p, w_in_bf16 = _layer_fused(x_prompt, p_prompt[0], _rope_tables(0, tp, tp),
                                                        wts, consts)
    wts = wts[:1] + (w_in_bf16,) + wts[2:]

    ck = cache_attn_k[0].reshape(bs, WINDOW, KV_WIDTH).astype(F32)
    cv = cache_attn_v[0].reshape(bs, WINDOW, KV_WIDTH).astype(F32)
    h0r = state_ssm_re[0].reshape(bs, PAIRS, LANES).astype(F32)
    h0i = state_ssm_im[0].reshape(bs, PAIRS, LANES).astype(F32)
    tabs_s = _rope_tables(PAST_LEN, ts, bs * ts)
    y_s, k_s, v_s, hr_s, hi_s = _layer_sample(x_sample, p_sample[0], tabs_s, ck, cv, h0r, h0i, wts, consts)

    def kv_out(a, b):
        return a.reshape(1, b, WINDOW, N_KV_HEADS, HEAD_DIM)

    def st_out(a, b):
        return a.reshape(1, b, SSM_GROUPS, SSM_STATE)

    return (y_p, y_s, kv_out(k_p, bp), kv_out(v_p, bp),
            st_out(hr_p, bp), st_out(hi_p, bp), kv_out(k_s, bs), kv_out(v_s, bs),
            st_out(hr_s, bs), st_out(hi_s, bs))
```

```python
import functools

import numpy as np
import jax
import jax.numpy as jnp
from jax import lax
from jax.experimental import pallas as pl
from jax.experimental.pallas import tpu as pltpu

F32 = jnp.float32
BF16 = jnp.bfloat16

LANES = 128
SUBLANES = 8
V7X_VMEM_BYTES = 64 * 1024 * 1024

D_MODEL = 1024
CHUNK = 64
WINDOW = 128
N_HEADS = 8
N_KV_HEADS = 2
HEAD_DIM = 64
Q_PER_KV = N_HEADS // N_KV_HEADS
LOG2E = 1.4426950408889634
Q_SCALE = HEAD_DIM ** -0.5 * LOG2E
ATTN_WIDTH = N_HEADS * HEAD_DIM
KV_WIDTH = N_KV_HEADS * HEAD_DIM
ROT_DIM = HEAD_DIM // 4
ROPE_THETA = 500000.0
SSM_WIDTH = D_MODEL // 2
SSM_GROUP = 16
SSM_GROUPS = SSM_WIDTH // SSM_GROUP
SSM_STATE = 64
PLE_DIM = 256
PAST_LEN = 1024
EPS = 1e-6

O_Q = 0
O_K = O_Q + ATTN_WIDTH
O_V = O_K + KV_WIDTH
O_ZA = O_V + KV_WIDTH
O_U = O_ZA + ATTN_WIDTH
O_ZS = O_U + SSM_WIDTH
O_GA = O_ZS + SSM_WIDTH
O_GS = O_GA + D_MODEL
IN_WIDTH = O_GS + D_MODEL

QM_WIDTH = N_HEADS * LANES
KV2_WIDTH = N_KV_HEADS * LANES
KEYS = WINDOW + CHUNK
LAGS = SUBLANES
PAIRS = SSM_GROUPS // 2
PAIR_K = 2 * LAGS * SSM_GROUP
PAIR_N = 2 * 2 * SSM_STATE
N_STATE = PAIRS * PAIR_N
U_TILES = SSM_WIDTH // LANES
PAIRS_PER_TILE = PAIRS // U_TILES
SLOT = 2 * SSM_GROUP
BF16_ROWS = 2 * SUBLANES
assert PAIRS_PER_TILE == 4 and LAGS == 2 * PAIRS_PER_TILE

LAYER_ROWS = 512


def _sigmoid(x):
    return 1.0 / (1.0 + jnp.exp2(x * (-LOG2E)))


def _const_spec(shape):
    zeros = (0,) * len(shape)
    return pl.BlockSpec(shape, lambda *_: zeros, pipeline_mode=pl.Buffered(1))


def _params(vmem_bytes, n_grid):
    return pltpu.CompilerParams(
        dimension_semantics=("arbitrary",) * n_grid,
        vmem_limit_bytes=min(int(vmem_bytes), V7X_VMEM_BYTES - 8 * 1024 * 1024),
    )


def _run(steps):
    for step in steps:
        step()


def _spread(main, other):
    merged, j = [], 0
    for i, step in enumerate(main):
        while j < len(other) and j * len(main) <= i * len(other):
            merged.append(other[j])
            j += 1
        merged.append(step)
    return merged + other[j:]


def _proj_steps(get_x, gain, get_tabs, w_ref, o, store=None):
    st = {}

    def norm():
        x = get_x()
        ms = jnp.mean(x * x, axis=-1, keepdims=True)
        st["xn"] = (x * lax.rsqrt(ms + EPS) * gain).astype(BF16)
        st["lo"] = lax.broadcasted_iota(jnp.int32, (x.shape[0], LANES), 1) < HEAD_DIM

    def seg(a, b):
        return jnp.dot(st["xn"], w_ref[:, a:b], preferred_element_type=F32)

    def rope(t):
        cos, sina, sinb = get_tabs()
        return (t * cos + pltpu.roll(t, ROT_DIM // 2, 1) * sina
                + pltpu.roll(t, LANES - ROT_DIM // 2, 1) * sinb)

    def both_halves(t):
        tr = pltpu.roll(t, HEAD_DIM, 1)
        return [jnp.where(st["lo"], t, tr).astype(BF16), jnp.where(st["lo"], tr, t).astype(BF16)]

    def done(name):
        if store is not None:
            store(name)

    def q():
        zq = seg(O_Q, O_K)
        o["qh"] = []
        for j in range(ATTN_WIDTH // LANES):
            qt = rope(zq[:, j * LANES:(j + 1) * LANES]) * Q_SCALE
            o["qh"] += [jnp.where(st["lo"], qt, 0.0).astype(BF16),
                        jnp.where(st["lo"], 0.0, qt).astype(BF16)]

    def kv():
        z = seg(O_K, O_ZA)
        o["k"] = rope(z[:, :KV_WIDTH])
        o["k2"] = both_halves(o["k"])
        done("k")
        o["v"] = z[:, KV_WIDTH:]
        o["v2"] = both_halves(o["v"])
        done("v")

    def za():
        z = seg(O_ZA, O_U)
        o["sa"] = z * _sigmoid(z)

    def u():
        o["u"] = seg(O_U, O_ZS)
        done("u")

    def zs():
        z = seg(O_ZS, O_GA)
        o["sz"] = z * _sigmoid(z)

    def ga():
        o["ga"] = _sigmoid(seg(O_GA, O_GS))

    def gs():
        o["gs"] = _sigmoid(seg(O_GS, IN_WIDTH))

    return [norm, q, kv, za, u, zs, ga, gs]


def _attn_steps(sinks_ref, chunks, get_q, get_kv, get_valid, emit):
    nt = (((1,), (1,)), ((), ()))
    units = [(c, kv) for c in chunks for kv in range(N_KV_HEADS)]
    n = len(units)
    st = {}

    def scores(c, kv):
        k2, v2 = get_kv(c, kv)
        qm = jnp.concatenate([get_q(c, kv * Q_PER_KV + h) for h in range(Q_PER_KV)], axis=0)
        s = lax.dot_general(qm, k2, nt, preferred_element_type=F32)
        valid = get_valid(c)
        if valid is not None:
            s = jnp.where(valid, s, -jnp.inf)
        return s, v2

    def softmax(s, kv):
        head_row = lax.broadcasted_iota(jnp.int32, (Q_PER_KV * CHUNK, 1), 0) // CHUNK
        sk = [sinks_ref[kv * Q_PER_KV + h] * LOG2E for h in range(Q_PER_KV)]
        sink = jnp.where(head_row == 0, sk[0],
                         jnp.where(head_row == 1, sk[1], jnp.where(head_row == 2, sk[2], sk[3])))
        m = jnp.maximum(jnp.max(s, axis=1, keepdims=True), sink)
        return jnp.exp2(s - m).astype(BF16), jnp.exp2(sink - m)

    def output(e, sink_term, v2, c, kv):
        lo_q = lax.broadcasted_iota(jnp.int32, (CHUNK, LANES), 1) < HEAD_DIM
        ones = jnp.ones((KEYS, LANES), BF16)
        pv = jnp.dot(e, jnp.concatenate([v2, ones], axis=1), preferred_element_type=F32)
        o = pv[:, :LANES] / (pv[:, LANES:] + sink_term)
        for j in range(Q_PER_KV // 2):
            even = o[2 * j * CHUNK:(2 * j + 1) * CHUNK]
            odd = o[(2 * j + 1) * CHUNK:(2 * j + 2) * CHUNK]
            emit(c, kv * (Q_PER_KV // 2) + j, jnp.where(lo_q, even, odd))

    def make(i):
        def step():
            if i < n:
                st[i] = scores(*units[i])
            if 0 <= i - 1 < n:
                s, v2 = st[i - 1]
                st[i - 1] = softmax(s, units[i - 1][1]) + (v2,)
            if 0 <= i - 2 < n:
                e, den, v2 = st.pop(i - 2)
                output(e, den, v2, *units[i - 2])
        return step

    return [make(i) for i in range(n + 2)]


def _ssm_reset(ubuf, cr_s, ci_s):
    ubuf[0:LAGS, :] = jnp.zeros((LAGS, SSM_WIDTH), F32)
    cr_s[...] = jnp.zeros(cr_s.shape, F32)
    ci_s[...] = jnp.zeros(ci_s.shape, F32)


def _ssm_steps(ubuf, row0, tt, cr_s, ci_s, hs, wlag_ref, a8r_ref, a8i_ref, ck_ref, hr_ref, hi_ref, o,
               seg=None):
    st = {"ys": []}

    def setup():
        if seg is not None:
            st["row_in_seg"] = lax.broadcasted_iota(jnp.int32, (tt, LANES), 0) % seg[0]
        slot = lax.broadcasted_iota(jnp.int32, (tt, LANES), 1) // SLOT
        st["to_low"] = [((slot + PAIRS_PER_TILE - s) % PAIRS_PER_TILE) < 2 for s in range(2)]
        st["same_parity"] = [((slot + sg) % 2) == 0 for sg in range(2)]

    def route(r):
        low = [jnp.where(st["to_low"][s], r[s], r[s + 2]) for s in range(2)]
        high = [jnp.where(st["to_low"][s], r[s + 2], r[s]) for s in range(2)]
        return [jnp.where(st["same_parity"][sg % 2], src[0], src[1])
                for sg, src in zip(range(PAIRS_PER_TILE), (low, low, high, high))]

    def lag_copies(k):
        def step():
            ub = ubuf[row0:row0 + LAGS + tt, k * LANES:(k + 1) * LANES]
            rolled = []
            for s in range(LAGS):
                us = ub[LAGS:] if s == 0 else pltpu.roll(ub, s, 0)[LAGS:]
                if seg is not None and s > 0:
                    us = jnp.where(st["row_in_seg"] >= s, us, 0.0)
                if s % PAIRS_PER_TILE:
                    us = pltpu.roll(us, SLOT * (s % PAIRS_PER_TILE), 1)
                rolled.append(us)
            st["halves"] = (route(rolled[:PAIRS_PER_TILE]), route(rolled[PAIRS_PER_TILE:]))
        return step

    def pair(k, sg):
        def step():
            q = k * PAIRS_PER_TILE + sg
            xl = jnp.concatenate([st["halves"][0][sg], st["halves"][1][sg]], axis=1).astype(BF16)
            w = jnp.dot(xl, wlag_ref[q], preferred_element_type=F32)
            ar, ai = a8r_ref[q:q + 1, :], a8i_ref[q:q + 1, :]
            if seg is None:
                cr, ci = cr_s[q], ci_s[q]
            else:
                seg_rows, h0r_ref, h0i_ref, pr_ref, pi_ref = seg
            for b2 in range(tt // BF16_ROWS):
                hrs, his = [], []
                for b in (2 * b2, 2 * b2 + 1):
                    if seg is not None and (b * SUBLANES) % seg_rows == 0:
                        n = b * SUBLANES // seg_rows
                        h0r, h0i = h0r_ref[n, q:q + 1, :], h0i_ref[n, q:q + 1, :]
                        cr = pr_ref[q] * h0r - pi_ref[q] * h0i
                        ci = pr_ref[q] * h0i + pi_ref[q] * h0r
                    blk = slice(b * SUBLANES, (b + 1) * SUBLANES)
                    hr = w[blk, :LANES] + cr
                    hi = w[blk, LANES:] + ci
                    cr = ar * hr - ai * hi
                    ci = ar * hi + ai * hr
                    hrs.append(hr)
                    his.append(hi)
                    if seg is not None and ((b + 1) * SUBLANES) % seg_rows == 0:
                        n = b * SUBLANES // seg_rows
                        hr_ref[n, q:q + 1, :] = hr[SUBLANES - 1:, :]
                        hi_ref[n, q:q + 1, :] = hi[SUBLANES - 1:, :]
                blk2 = slice(b2 * BF16_ROWS, (b2 + 1) * BF16_ROWS)
                hs[blk2, q * PAIR_N:q * PAIR_N + LANES] = jnp.concatenate(hrs, axis=0).astype(BF16)
                hs[blk2, q * PAIR_N + LANES:(q + 1) * PAIR_N] = jnp.concatenate(his, axis=0).astype(BF16)
            if seg is None:
                cr_s[q] = cr
                ci_s[q] = ci
                hr_ref[0, q:q + 1, :] = hr[SUBLANES - 1:, :]
                hi_ref[0, q:q + 1, :] = hi[SUBLANES - 1:, :]
        return step

    def c_proj(k):
        def step():
            cols = slice(k * PAIRS_PER_TILE * PAIR_N, (k + 1) * PAIRS_PER_TILE * PAIR_N)
            st["ys"].append(jnp.dot(hs[:, cols], ck_ref[k], preferred_element_type=F32))
            if k == U_TILES - 1:
                o["y"] = jnp.concatenate(st["ys"], axis=1)
        return step

    steps = [setup]
    for k in range(U_TILES):
        steps += [lag_copies(k)] + [pair(k, sg) for sg in range(PAIRS_PER_TILE)] + [c_proj(k)]
    return steps


def _glu(y, u, sz, d, wglu_ref):
    z = jax.nn.gelu(y + d * u)
    g = jnp.dot(z.astype(BF16), wglu_ref[...], preferred_element_type=F32)
    return z * _sigmoid(g) * sz


def _out_steps(src, woa_ref, wos_ref, wout_ref, wpg_ref, wpp_ref, fgain, emit):
    st = {}

    def mm(a, w_ref):
        return jnp.dot(a.astype(BF16), w_ref[...], preferred_element_type=F32)

    def branches():
        st["merged"] = src["ga"]() * mm(src["xa"](), woa_ref) + src["gs"]() * mm(src["xs"](), wos_ref)

    def residual():
        st["h"] = src["x"]() + mm(st.pop("merged"), wout_ref)

    def embed_gate():
        h = st.pop("h")
        st["h"] = h + _sigmoid(mm(h, wpg_ref)) * mm(src["p"](), wpp_ref)

    def norm():
        h = st.pop("h")
        ms = jnp.mean(h * h, axis=-1, keepdims=True)
        emit(h * lax.rsqrt(ms + EPS) * fgain)

    return [branches, residual, embed_gate, norm]


def _layer_kernel(sinks_ref, x_ref, p_ref, cos_ref, sina_ref, sinb_ref, gain_ref, w_in_ref,
                  wlag_ref, a8r_ref, a8i_ref, ck_ref, d_ref, wglu_ref,
                  woa_ref, wos_ref, wout_ref, wpg_ref, wpp_ref, fg_ref,
                  y_ref, k_ref, v_ref, hr_ref, hi_ref,
                  kbuf, vbuf, xa_s, ubuf, cr_s, ci_s, hs, *, tt):
    t = pl.program_id(1)
    half = tt // 2

    @pl.when(t == 0)
    def _():
        kbuf[0:WINDOW, :] = jnp.zeros((WINDOW, KV2_WIDTH), BF16)
        vbuf[0:WINDOW, :] = jnp.zeros((WINDOW, KV2_WIDTH), BF16)
        _ssm_reset(ubuf, cr_s, ci_s)

    pj, so = [{}, {}], [{}, {}]

    def rows(h):
        return slice(h * half, (h + 1) * half)

    def proj(h):
        def store(name):
            if name == "u":
                ubuf[LAGS + h * half:LAGS + (h + 1) * half, :] = pj[h]["u"]
                return
            last_ref, buf = (k_ref, kbuf) if name == "k" else (v_ref, vbuf)
            if h == 1:
                last_ref[0] = pj[h][name][half - WINDOW:]
            for j in range(N_KV_HEADS):
                buf[WINDOW + h * half:WINDOW + (h + 1) * half, j * LANES:(j + 1) * LANES] = pj[h][name + "2"][j]

        return _proj_steps(lambda: x_ref[0, rows(h)], gain_ref[...],
                           lambda: tuple(tab[pl.ds(pl.multiple_of(t * tt + h * half, half), half)]
                                          for tab in (cos_ref, sina_ref, sinb_ref)),
                           w_in_ref, pj[h], store)

    def mid(h):
        def get_q(c, head):
            r0 = c * CHUNK - h * half
            return pj[h]["qh"][head][r0:r0 + CHUNK]

        def get_kv(c, kv):
            krows, cols = slice(c * CHUNK, c * CHUNK + KEYS), slice(kv * LANES, (kv + 1) * LANES)
            return kbuf[krows, cols], vbuf[krows, cols]

        def get_valid(c):
            if c * CHUNK >= WINDOW:
                return None
            in_seq = c * CHUNK + lax.broadcasted_iota(jnp.int32, (1, KEYS), 1) >= WINDOW
            return jnp.logical_or(in_seq, t > 0)

        def emit(c, tile, o):
            r0, cols = c * CHUNK - h * half, slice(tile * LANES, (tile + 1) * LANES)
            xa_s[c * CHUNK:(c + 1) * CHUNK, cols] = (o * pj[h]["sa"][r0:r0 + CHUNK, cols]).astype(BF16)

        def glu():
            so[h]["xs"] = _glu(so[h].pop("y"), pj[h]["u"], pj[h]["sz"], d_ref[...], wglu_ref)

        chunks = range(h * half // CHUNK, (h + 1) * half // CHUNK)
        return (_attn_steps(sinks_ref, chunks, get_q, get_kv, get_valid, emit)
                + _ssm_steps(ubuf, h * half, half, cr_s, ci_s, hs, wlag_ref, a8r_ref, a8i_ref, ck_ref,
                             hr_ref, hi_ref, so[h])
                + [glu])

    def out(h):
        src = dict(xa=lambda: xa_s[rows(h)], xs=lambda: so[h]["xs"], ga=lambda: pj[h]["ga"],
                   gs=lambda: pj[h]["gs"], x=lambda: x_ref[0, rows(h)], p=lambda: p_ref[0, rows(h)])

        def emit(y):
            y_ref[0, rows(h)] = y

        return _out_steps(src, woa_ref, wos_ref, wout_ref, wpg_ref, wpp_ref, fg_ref[...], emit)

    _run(proj(0))
    _run(_spread(mid(0), proj(1)))
    _run(_spread(mid(1), out(0)))
    _run(out(1))

    kbuf[0:WINDOW, :] = kbuf[tt:tt + WINDOW, :]
    vbuf[0:WINDOW, :] = vbuf[tt:tt + WINDOW, :]
    ubuf[0:LAGS, :] = ubuf[tt:tt + LAGS, :]


def _layer_fused(x, p, tabs, wts, consts):
    b, t, _ = x.shape
    tt = min(LAYER_ROWS, t)
    assert t % tt == 0 and tt // 2 >= WINDOW and (tt // 2) % BF16_ROWS == 0 and tabs[0].shape[0] == t
    (gain, w_in, sinks, woa, d_skip, w_glu, wos, wout, wpg, wpp, fgain) = wts
    wlag, a8r, a8i, _, _, ck = consts

    def row_spec(w):
        return pl.BlockSpec((1, tt, w), lambda i, j: (i, j, 0))

    tab_spec = _const_spec((t, LANES))
    st_spec = pl.BlockSpec((1, PAIRS, LANES), lambda i, j: (i, 0, 0))
    win_spec = pl.BlockSpec((1, WINDOW, KV_WIDTH), lambda i, j: (i, 0, 0))
    consts_in = (gain, w_in, wlag, a8r, a8i, ck, d_skip, w_glu, woa, wos, wout, wpg, wpp, fgain)
    vmem = (sum(a.size * a.dtype.itemsize for a in consts_in)
            + 2 * tt * (2 * D_MODEL + PLE_DIM + 2 * KV_WIDTH) * 4 + 3 * t * LANES * 4
            + 2 * (WINDOW + tt) * KV2_WIDTH * 2 + tt * ATTN_WIDTH * 2 + (tt + LAGS) * SSM_WIDTH * 4
            + (tt // 2) * N_STATE * 2 + 3 * tt * IN_WIDTH * 4)
    y, k, v, hr, hi = pl.pallas_call(
        functools.partial(_layer_kernel, tt=tt),
        grid=(b, t // tt),
        in_specs=[pl.BlockSpec(memory_space=pltpu.SMEM), row_spec(D_MODEL), row_spec(PLE_DIM),
                  tab_spec, tab_spec, tab_spec]
                 + [_const_spec(a.shape) for a in consts_in],
        out_specs=[row_spec(D_MODEL), win_spec, win_spec, st_spec, st_spec],
        out_shape=[jax.ShapeDtypeStruct((b, t, D_MODEL), F32),
                   jax.ShapeDtypeStruct((b, WINDOW, KV_WIDTH), F32),
                   jax.ShapeDtypeStruct((b, WINDOW, KV_WIDTH), F32),
                   jax.ShapeDtypeStruct((b, PAIRS, LANES), F32),
                   jax.ShapeDtypeStruct((b, PAIRS, LANES), F32)],
        scratch_shapes=[pltpu.VMEM((WINDOW + tt, KV2_WIDTH), BF16),
                        pltpu.VMEM((WINDOW + tt, KV2_WIDTH), BF16),
                        pltpu.VMEM((tt, ATTN_WIDTH), BF16),
                        pltpu.VMEM((tt + LAGS, SSM_WIDTH), F32),
                        pltpu.VMEM((PAIRS, SUBLANES, LANES), F32),
                        pltpu.VMEM((PAIRS, SUBLANES, LANES), F32),
                        pltpu.VMEM((tt // 2, N_STATE), BF16)],
        compiler_params=_params(vmem, 2),
        name="layer_prompt",
    )(sinks, x, p, *tabs, *consts_in)
    return y, k, v, hr, hi


def _sample_kernel(sinks_ref, x_ref, p_ref, cos_ref, sina_ref, sinb_ref, kpre_ref, vpre_ref,
                   h0r_ref, h0i_ref, gain_ref, w_in_ref, wlag_ref, a8r_ref, a8i_ref, pr_ref, pi_ref,
                   ck_ref, d_ref, wglu_ref, woa_ref, wos_ref, wout_ref, wpg_ref, wpp_ref, fg_ref,
                   y_ref, k_ref, v_ref, hr_ref, hi_ref,
                   kbuf, vbuf, xa_s, ubuf, hs, *, n, t):
    rows = n * t
    pj, so = {}, {}

    def store(name):
        if name == "u":
            ubuf[0:LAGS, :] = jnp.zeros((LAGS, SSM_WIDTH), F32)
            ubuf[LAGS:, :] = pj["u"]
            return
        full_ref, pre_ref, buf = (k_ref, kpre_ref, kbuf) if name == "k" else (v_ref, vpre_ref, vbuf)
        full_ref[...] = pj[name]
        for s in range(n):
            buf[s, 0:WINDOW, :] = pre_ref[s]
            for j in range(N_KV_HEADS):
                buf[s, WINDOW:WINDOW + t, j * LANES:(j + 1) * LANES] = pj[name + "2"][j][s * t:(s + 1) * t]

    def get_q(c, head):
        return pj["qh"][head][c * CHUNK:(c + 1) * CHUNK]

    def get_kv(c, kv):
        cols = slice(kv * LANES, (kv + 1) * LANES)
        return kbuf[c, :, cols], vbuf[c, :, cols]

    def emit(c, tile, o):
        r, cols = slice(c * CHUNK, (c + 1) * CHUNK), slice(tile * LANES, (tile + 1) * LANES)
        xa_s[r, cols] = (o * pj["sa"][r, cols]).astype(BF16)

    def glu():
        so["xs"] = _glu(so.pop("y"), pj["u"], pj["sz"], d_ref[...], wglu_ref)

    src = dict(xa=lambda: xa_s[...], xs=lambda: so["xs"], ga=lambda: pj["ga"], gs=lambda: pj["gs"],
               x=lambda: x_ref[...], p=lambda: p_ref[...])

    def emit_y(y):
        y_ref[...] = y

    _run(_proj_steps(lambda: x_ref[...], gain_ref[...],
                     lambda: (cos_ref[...], sina_ref[...], sinb_ref[...]), w_in_ref, pj, store))
    _run(_attn_steps(sinks_ref, range(n), get_q, get_kv, lambda c: None, emit))
    _run(_ssm_steps(ubuf, 0, rows, None, None, hs, wlag_ref, a8r_ref, a8i_ref, ck_ref, hr_ref, hi_ref, so,
                    seg=(t, h0r_ref, h0i_ref, pr_ref, pi_ref)) + [glu])
    _run(_out_steps(src, woa_ref, wos_ref, wout_ref, wpg_ref, wpp_ref, fg_ref[...], emit_y))


def _layer_sample(x, p, tabs, k_prefix, v_prefix, h0r, h0i, wts, consts):
    n, t, _ = x.shape
    assert t == CHUNK and tabs[0].shape[0] == n * t
    rows = n * t
    (gain, w_in, sinks, woa, d_skip, w_glu, wos, wout, wpg, wpp, fgain) = wts
    wlag, a8r, a8i, pr, pi, ck = consts
    operands = (x.reshape(rows, D_MODEL), p.reshape(rows, PLE_DIM), *tabs,
                _both_halves(k_prefix), _both_halves(v_prefix), h0r, h0i,
                gain, w_in, wlag, a8r, a8i, pr, pi, ck, d_skip, w_glu, woa, wos, wout, wpg, wpp, fgain)
    out_shapes = [(rows, D_MODEL), (rows, KV_WIDTH), (rows, KV_WIDTH), h0r.shape, h0r.shape]
    scratch = [((n, KEYS, KV2_WIDTH), BF16), ((n, KEYS, KV2_WIDTH), BF16), ((rows, ATTN_WIDTH), BF16),
               ((rows + LAGS, SSM_WIDTH), F32), ((rows, N_STATE), BF16)]
    vmem = (sum(a.size * a.dtype.itemsize for a in operands)
            + sum(int(np.prod(s)) * 4 for s in out_shapes)
            + sum(int(np.prod(s)) * np.dtype(d).itemsize for s, d in scratch)
            + 3 * rows * IN_WIDTH * 4)
    y, k, v, hr, hi = pl.pallas_call(
        functools.partial(_sample_kernel, n=n, t=t),
        grid=(1,),
        in_specs=[pl.BlockSpec(memory_space=pltpu.SMEM)] + [_const_spec(a.shape) for a in operands],
        out_specs=[pl.BlockSpec(s, lambda i, nd=len(s): (0,) * nd) for s in out_shapes],
        out_shape=[jax.ShapeDtypeStruct(s, F32) for s in out_shapes],
        scratch_shapes=[pltpu.VMEM(s, d) for s, d in scratch],
        compiler_params=_params(vmem, 1),
        name="layer_sample",
    )(sinks, *operands)
    k_new = jnp.concatenate([k_prefix[:, t:], k.reshape(n, t, KV_WIDTH)], axis=1)
    v_new = jnp.concatenate([v_prefix[:, t:], v.reshape(n, t, KV_WIDTH)], axis=1)
    return y.reshape(n, t, D_MODEL), k_new, v_new, hr, hi


def _both_halves(a):
    h0, h1 = a[..., :HEAD_DIM], a[..., HEAD_DIM:]
    return jnp.concatenate([h0, h0, h1, h1], axis=-1).astype(BF16)


def _ssm_constants(a_re, a_im, log_dt, b_re, b_im, c_re, c_im):
    dt = jnp.exp(log_dt.astype(F32))[:, None]
    lr = a_re.astype(F32).reshape(PAIRS, 1, 1, LANES)
    li = a_im.astype(F32).reshape(PAIRS, 1, 1, LANES)
    xr = (a_re.astype(F32) * dt).reshape(PAIRS, 1, 1, LANES)
    xi = (a_im.astype(F32) * dt).reshape(PAIRS, 1, 1, LANES)

    def apow(n):
        mag = jnp.exp(xr * n)
        return mag * jnp.cos(xi * n), mag * jnp.sin(xi * n)

    ar, ai = apow(1.0)
    nr, ni = ar - 1.0, ai
    den = lr * lr + li * li
    fr, fi = (nr * lr + ni * li) / den, (ni * lr - nr * li) / den

    n_slots = PAIR_K // SLOT
    qq, hi_ = np.arange(PAIRS)[:, None], np.arange(n_slots)[None, :]
    lag_tab = (PAIRS_PER_TILE * (hi_ // PAIRS_PER_TILE) + (hi_ % PAIRS_PER_TILE - qq) % PAIRS_PER_TILE)
    lag_tab = lag_tab.astype(np.float32)[:, :, None, None]
    same_group = (np.arange(SLOT)[:, None] // SSM_GROUP == np.arange(LANES)[None, :] // SSM_STATE)
    same_group = same_group.astype(np.float32)

    def b_rows(bm):
        t = jnp.transpose(bm.astype(F32).reshape(PAIRS, 2, SSM_STATE, SSM_GROUP), (0, 3, 1, 2))
        t = t.reshape(PAIRS, 1, 1, SSM_GROUP, LANES)
        t = jnp.broadcast_to(t, (PAIRS, 1, 2, SSM_GROUP, LANES))
        return t.reshape(PAIRS, 1, SLOT, LANES) * same_group

    br, bi = b_rows(b_re), b_rows(b_im)
    bbr, bbi = fr * br - fi * bi, fr * bi + fi * br
    er, ei = apow(lag_tab)
    wlag = jnp.concatenate([(er * bbr - ei * bbi).reshape(PAIRS, PAIR_K, LANES),
                            (er * bbi + ei * bbr).reshape(PAIRS, PAIR_K, LANES)], axis=-1).astype(BF16)

    a8r, a8i = (a.reshape(PAIRS, LANES) for a in apow(float(LAGS)))
    pwr, pwi = (a.reshape(PAIRS, LAGS, LANES)
                for a in apow(np.arange(1, LAGS + 1, dtype=np.float32)[None, :, None, None]))

    def c_cols(c):
        t = jnp.transpose(c.astype(F32).reshape(U_TILES, LANES // SSM_GROUP, SSM_GROUP, SSM_STATE),
                          (0, 3, 1, 2))
        return t.reshape(U_TILES, 1, 1, 1, SSM_STATE, LANES)

    cols_group = np.arange(LANES) // SSM_GROUP
    rows_group = 2 * np.arange(PAIRS_PER_TILE)[:, None] + np.arange(2)[None, :]
    c_mask = (rows_group[:, None, :, None, None] == cols_group[None, None, None, None, :])
    c_mask = c_mask.astype(np.float32)[None]
    ck = jnp.concatenate([c_cols(c_re) * c_mask, -c_cols(c_im) * c_mask], axis=2)
    ck = ck.reshape(U_TILES, PAIRS_PER_TILE * PAIR_N, LANES).astype(BF16)
    return wlag, a8r, a8i, pwr, pwi, ck


def _rope_tables(pos0, t, rows):
    half = ROT_DIM // 2
    d = np.arange(LANES) % HEAD_DIM
    inv = jnp.power(ROPE_THETA, -jnp.arange(half, dtype=F32) * 2.0 / ROT_DIM)
    pos = (pos0 + jnp.arange(t)).astype(F32)
    ang = pos[:, None] * inv[None, :]
    cos, sin = (jnp.tile(a, (1, LANES // half)) for a in (jnp.cos(ang), jnp.sin(ang)))
    cos_t = jnp.where((d < ROT_DIM)[None, :], cos, 1.0)
    sina = jnp.where(((d >= half) & (d < ROT_DIM))[None, :], sin, 0.0)
    sinb = jnp.where((d < half)[None, :], -sin, 0.0)
    reps = (max(rows // t, 1), 1)
    return tuple(jnp.tile(a, reps) for a in (cos_t, sina, sinb))


def kernel(x_prompt, x_sample, p_prompt, p_sample, cache_attn_k, cache_attn_v, state_ssm_re,
           state_ssm_im, norm_gain, w_in, attn_sinks, w_o_attn, ssm_a_re, ssm_a_im, ssm_log_dt,
           ssm_b_re, ssm_b_im, ssm_c_re, ssm_c_im, ssm_d, ssm_w_glu, w_o_ssm, w_out,
           w_ple_gate, w_ple_proj, final_norm_gain):
    assert norm_gain.shape[0] == 1, "single-layer model"
    bp, tp, _ = x_prompt.shape
    bs, ts, _ = x_sample.shape
    wts = (norm_gain[0].reshape(1, D_MODEL).astype(F32), w_in[0].astype(BF16),
           attn_sinks[0].astype(F32), w_o_attn[0].astype(BF16),
           ssm_d[0].reshape(1, SSM_WIDTH).astype(F32), ssm_w_glu[0].astype(BF16),
           w_o_ssm[0].astype(BF16), w_out[0].astype(BF16), w_ple_gate[0].astype(BF16),
           w_ple_proj[0].astype(BF16), final_norm_gain.reshape(1, D_MODEL).astype(F32))
    consts = _ssm_constants(ssm_a_re[0], ssm_a_im[0], ssm_log_dt[0], ssm_b_re[0], ssm_b_im[0],
                            ssm_c_re[0], ssm_c_im[0])

    y_p, k_p, v_p, hr_p, hi_p = _layer_fused(x_prompt, p_prompt[0], _rope_tables(0, tp, tp), wts, consts)

    ck = cache_attn_k[0].reshape(bs, WINDOW, KV_WIDTH).astype(F32)
    cv = cache_attn_v[0].reshape(bs, WINDOW, KV_WIDTH).astype(F32)
    h0r = state_ssm_re[0].reshape(bs, PAIRS, LANES).astype(F32)
    h0i = state_ssm_im[0].reshape(bs, PAIRS, LANES).astype(F32)
    tabs_s = _rope_tables(PAST_LEN, ts, bs * ts)
    y_s, k_s, v_s, hr_s, hi_s = _layer_sample(x_sample, p_sample[0], tabs_s, ck, cv, h0r, h0i, wts, consts)

    def kv_out(a, b):
        return a.reshape(1, b, WINDOW, N_KV_HEADS, HEAD_DIM)

    def st_out(a, b):
        return a.reshape(1, b, SSM_GROUPS, SSM_STATE)

    return (y_p, y_s, kv_out(k_p, bp), kv_out(v_p, bp),
            st_out(hr_p, bp), st_out(hi_p, bp), kv_out(k_s, bs), kv_out(v_s, bs),
            st_out(hr_s, bs), st_out(hi_s, bs))
```

```python
import functools

import numpy as np
import jax
import jax.numpy as jnp
from jax import lax
from jax.experimental import pallas as pl
from jax.experimental.pallas import tpu as pltpu

F32 = jnp.float32
BF16 = jnp.bfloat16

LANES = 128
SUBLANES = 8
V7X_VMEM_BYTES = 64 * 1024 * 1024

D_MODEL = 1024
CHUNK = 64
WINDOW = 128
N_HEADS = 8
N_KV_HEADS = 2
HEAD_DIM = 64
Q_PER_KV = N_HEADS // N_KV_HEADS
LOG2E = 1.4426950408889634
Q_SCALE = HEAD_DIM ** -0.5 * LOG2E
ATTN_WIDTH = N_HEADS * HEAD_DIM
KV_WIDTH = N_KV_HEADS * HEAD_DIM
ROT_DIM = HEAD_DIM // 4
ROPE_THETA = 500000.0
SSM_WIDTH = D_MODEL // 2
SSM_GROUP = 16
SSM_GROUPS = SSM_WIDTH // SSM_GROUP
SSM_STATE = 64
PLE_DIM = 256
PAST_LEN = 1024
EPS = 1e-6

O_Q = 0
O_K = O_Q + ATTN_WIDTH
O_V = O_K + KV_WIDTH
O_ZA = O_V + KV_WIDTH
O_U = O_ZA + ATTN_WIDTH
O_ZS = O_U + SSM_WIDTH
O_GA = O_ZS + SSM_WIDTH
O_GS = O_GA + D_MODEL
IN_WIDTH = O_GS + D_MODEL

QM_WIDTH = N_HEADS * LANES
KV2_WIDTH = N_KV_HEADS * LANES
KEYS = WINDOW + CHUNK
LAGS = SUBLANES
PAIRS = SSM_GROUPS // 2
PAIR_K = 2 * LAGS * SSM_GROUP
PAIR_N = 2 * 2 * SSM_STATE
N_STATE = PAIRS * PAIR_N
U_TILES = SSM_WIDTH // LANES
PAIRS_PER_TILE = PAIRS // U_TILES
SLOT = 2 * SSM_GROUP
BF16_ROWS = 2 * SUBLANES
assert PAIRS_PER_TILE == 4 and LAGS == 2 * PAIRS_PER_TILE

LAYER_ROWS = 512


def _sigmoid(x):
    return 1.0 / (1.0 + jnp.exp2(x * (-LOG2E)))


def _const_spec(shape):
    zeros = (0,) * len(shape)
    return pl.BlockSpec(shape, lambda *_: zeros, pipeline_mode=pl.Buffered(1))


def _params(vmem_bytes, n_grid):
    return pltpu.CompilerParams(
        dimension_semantics=("arbitrary",) * n_grid,
        vmem_limit_bytes=min(int(vmem_bytes), V7X_VMEM_BYTES - 8 * 1024 * 1024),
    )


def _run(steps):
    for step in steps:
        step()


def _spread(main, other):
    merged, j = [], 0
    for i, step in enumerate(main):
        while j < len(other) and j * len(main) <= i * len(other):
            merged.append(other[j])
            j += 1
        merged.append(step)
    return merged + other[j:]


def _proj_steps(get_x, gain, get_tabs, w_ref, o, store=None):
    st = {}

    def norm():
        x = get_x()
        ms = jnp.mean(x * x, axis=-1, keepdims=True)
        st["xn"] = (x * lax.rsqrt(ms + EPS) * gain).astype(BF16)
        st["lo"] = lax.broadcasted_iota(jnp.int32, (x.shape[0], LANES), 1) < HEAD_DIM

    def seg(a, b):
        return jnp.dot(st["xn"], w_ref[:, a:b], preferred_element_type=F32)

    def rope(t):
        cos, sina, sinb = get_tabs()
        return (t * cos + pltpu.roll(t, ROT_DIM // 2, 1) * sina
                + pltpu.roll(t, LANES - ROT_DIM // 2, 1) * sinb)

    def both_halves(t):
        tr = pltpu.roll(t, HEAD_DIM, 1)
        return [jnp.where(st["lo"], t, tr).astype(BF16), jnp.where(st["lo"], tr, t).astype(BF16)]

    def done(name):
        if store is not None:
            store(name)

    def q():
        zq = seg(O_Q, O_K)
        o["qh"] = []
        for j in range(ATTN_WIDTH // LANES):
            qt = rope(zq[:, j * LANES:(j + 1) * LANES]) * Q_SCALE
            o["qh"] += [jnp.where(st["lo"], qt, 0.0).astype(BF16),
                        jnp.where(st["lo"], 0.0, qt).astype(BF16)]

    def kv():
        z = seg(O_K, O_ZA)
        o["k"] = rope(z[:, :KV_WIDTH])
        o["k2"] = both_halves(o["k"])
        done("k")
        o["v"] = z[:, KV_WIDTH:]
        o["v2"] = both_halves(o["v"])
        done("v")

    def za():
        z = seg(O_ZA, O_U)
        o["sa"] = z * _sigmoid(z)

    def u():
        o["u"] = seg(O_U, O_ZS)
        done("u")

    def zs():
        z = seg(O_ZS, O_GA)
        o["sz"] = z * _sigmoid(z)

    def ga():
        o["ga"] = _sigmoid(seg(O_GA, O_GS))

    def gs():
        o["gs"] = _sigmoid(seg(O_GS, IN_WIDTH))

    return [norm, q, kv, za, u, zs, ga, gs]


def _attn_steps(sinks_ref, chunks, get_q, get_kv, get_valid, emit):
    nt = (((1,), (1,)), ((), ()))
    units = [(c, kv) for c in chunks for kv in range(N_KV_HEADS)]
    n = len(units)
    st = {}

    def scores(c, kv):
        k2, v2 = get_kv(c, kv)
        qm = jnp.concatenate([get_q(c, kv * Q_PER_KV + h) for h in range(Q_PER_KV)], axis=0)
        s = lax.dot_general(qm, k2, nt, preferred_element_type=F32)
        valid = get_valid(c)
        if valid is not None:
            s = jnp.where(valid, s, -jnp.inf)
        return s, v2

    def softmax(s, kv):
        head_row = lax.broadcasted_iota(jnp.int32, (Q_PER_KV * CHUNK, 1), 0) // CHUNK
        sk = [sinks_ref[kv * Q_PER_KV + h] * LOG2E for h in range(Q_PER_KV)]
        sink = jnp.where(head_row == 0, sk[0],
                         jnp.where(head_row == 1, sk[1], jnp.where(head_row == 2, sk[2], sk[3])))
        m = jnp.maximum(jnp.max(s, axis=1, keepdims=True), sink)
        return jnp.exp2(s - m).astype(BF16), jnp.exp2(sink - m)

    def output(e, sink_term, v2, c, kv):
        lo_q = lax.broadcasted_iota(jnp.int32, (CHUNK, LANES), 1) < HEAD_DIM
        ones = jnp.ones((KEYS, LANES), BF16)
        pv = jnp.dot(e, jnp.concatenate([v2, ones], axis=1), preferred_element_type=F32)
        o = pv[:, :LANES] / (pv[:, LANES:] + sink_term)
        for j in range(Q_PER_KV // 2):
            even = o[2 * j * CHUNK:(2 * j + 1) * CHUNK]
            odd = o[(2 * j + 1) * CHUNK:(2 * j + 2) * CHUNK]
            emit(c, kv * (Q_PER_KV // 2) + j, jnp.where(lo_q, even, odd))

    def make(i):
        def step():
            if i < n:
                st[i] = scores(*units[i])
            if 0 <= i - 1 < n:
                s, v2 = st[i - 1]
                st[i - 1] = softmax(s, units[i - 1][1]) + (v2,)
            if 0 <= i - 2 < n:
                e, den, v2 = st.pop(i - 2)
                output(e, den, v2, *units[i - 2])
        return step

    return [make(i) for i in range(n + 2)]


def _ssm_reset(ubuf, cr_s, ci_s):
    ubuf[0:LAGS, :] = jnp.zeros((LAGS, SSM_WIDTH), F32)
    cr_s[...] = jnp.zeros(cr_s.shape, F32)
    ci_s[...] = jnp.zeros(ci_s.shape, F32)


def _ssm_steps(ubuf, row0, tt, cr_s, ci_s, hs, wlag_ref, a8r_ref, a8i_ref, ck_ref, hr_ref, hi_ref, o,
               seg=None):
    st = {"ys": []}

    def setup():
        if seg is not None:
            st["row_in_seg"] = lax.broadcasted_iota(jnp.int32, (tt, LANES), 0) % seg[0]
        slot = lax.broadcasted_iota(jnp.int32, (tt, LANES), 1) // SLOT
        st["to_low"] = [((slot + PAIRS_PER_TILE - s) % PAIRS_PER_TILE) < 2 for s in range(2)]
        st["same_parity"] = [((slot + sg) % 2) == 0 for sg in range(2)]

    def route(r):
        low = [jnp.where(st["to_low"][s], r[s], r[s + 2]) for s in range(2)]
        high = [jnp.where(st["to_low"][s], r[s + 2], r[s]) for s in range(2)]
        return [jnp.where(st["same_parity"][sg % 2], src[0], src[1])
                for sg, src in zip(range(PAIRS_PER_TILE), (low, low, high, high))]

    def lag_copies(k):
        def step():
            ub = ubuf[row0:row0 + LAGS + tt, k * LANES:(k + 1) * LANES]
            rolled = []
            for s in range(LAGS):
                us = ub[LAGS:] if s == 0 else pltpu.roll(ub, s, 0)[LAGS:]
                if seg is not None and s > 0:
                    us = jnp.where(st["row_in_seg"] >= s, us, 0.0)
                if s % PAIRS_PER_TILE:
                    us = pltpu.roll(us, SLOT * (s % PAIRS_PER_TILE), 1)
                rolled.append(us)
            st["halves"] = (route(rolled[:PAIRS_PER_TILE]), route(rolled[PAIRS_PER_TILE:]))
        return step

    def pair(k, sg):
        def step():
            q = k * PAIRS_PER_TILE + sg
            xl = jnp.concatenate([st["halves"][0][sg], st["halves"][1][sg]], axis=1).astype(BF16)
            w = jnp.dot(xl, wlag_ref[q], preferred_element_type=F32)
            ar, ai = a8r_ref[q:q + 1, :], a8i_ref[q:q + 1, :]
            if seg is None:
                cr, ci = cr_s[q], ci_s[q]
            else:
                seg_rows, h0r_ref, h0i_ref, pr_ref, pi_ref = seg
            for b2 in range(tt // BF16_ROWS):
                hrs, his = [], []
                for b in (2 * b2, 2 * b2 + 1):
                    if seg is not None and (b * SUBLANES) % seg_rows == 0:
                        n = b * SUBLANES // seg_rows
                        h0r, h0i = h0r_ref[n, q:q + 1, :], h0i_ref[n, q:q + 1, :]
                        cr = pr_ref[q] * h0r - pi_ref[q] * h0i
                        ci = pr_ref[q] * h0i + pi_ref[q] * h0r
                    blk = slice(b * SUBLANES, (b + 1) * SUBLANES)
                    hr = w[blk, :LANES] + cr
                    hi = w[blk, LANES:] + ci
                    cr = ar * hr - ai * hi
                    ci = ar * hi + ai * hr
                    hrs.append(hr)
                    his.append(hi)
                    if seg is not None and ((b + 1) * SUBLANES) % seg_rows == 0:
                        n = b * SUBLANES // seg_rows
                        hr_ref[n, q:q + 1, :] = hr[SUBLANES - 1:, :]
                        hi_ref[n, q:q + 1, :] = hi[SUBLANES - 1:, :]
                blk2 = slice(b2 * BF16_ROWS, (b2 + 1) * BF16_ROWS)
                hs[blk2, q * PAIR_N:q * PAIR_N + LANES] = jnp.concatenate(hrs, axis=0).astype(BF16)
                hs[blk2, q * PAIR_N + LANES:(q + 1) * PAIR_N] = jnp.concatenate(his, axis=0).astype(BF16)
            if seg is None:
                cr_s[q] = cr
                ci_s[q] = ci
                hr_ref[0, q:q + 1, :] = hr[SUBLANES - 1:, :]
                hi_ref[0, q:q + 1, :] = hi[SUBLANES - 1:, :]
        return step

    def c_proj(k):
        def step():
            cols = slice(k * PAIRS_PER_TILE * PAIR_N, (k + 1) * PAIRS_PER_TILE * PAIR_N)
            st["ys"].append(jnp.dot(hs[:, cols], ck_ref[k], preferred_element_type=F32))
            if k == U_TILES - 1:
                o["y"] = jnp.concatenate(st["ys"], axis=1)
        return step

    steps = [setup]
    for k in range(U_TILES):
        steps += [lag_copies(k)] + [pair(k, sg) for sg in range(PAIRS_PER_TILE)] + [c_proj(k)]
    return steps


def _glu(y, u, sz, d, wglu_ref):
    z = jax.nn.gelu(y + d * u)
    g = jnp.dot(z.astype(BF16), wglu_ref[...], preferred_element_type=F32)
    return z * _sigmoid(g) * sz


def _out_steps(src, woa_ref, wos_ref, wout_ref, wpg_ref, wpp_ref, fgain, emit):
    st = {}

    def mm(a, w_ref):
        return jnp.dot(a.astype(BF16), w_ref[...], preferred_element_type=F32)

    def branches():
        st["merged"] = src["ga"]() * mm(src["xa"](), woa_ref) + src["gs"]() * mm(src["xs"](), wos_ref)

    def residual():
        st["h"] = src["x"]() + mm(st.pop("merged"), wout_ref)

    def embed_gate():
        h = st.pop("h")
        st["h"] = h + _sigmoid(mm(h, wpg_ref)) * mm(src["p"](), wpp_ref)

    def norm():
        h = st.pop("h")
        ms = jnp.mean(h * h, axis=-1, keepdims=True)
        emit(h * lax.rsqrt(ms + EPS) * fgain)

    return [branches, residual, embed_gate, norm]


def _layer_kernel(sinks_ref, x_ref, p_ref, cos_ref, sina_ref, sinb_ref, gain_ref, w_in_ref,
                  wlag_ref, a8r_ref, a8i_ref, ck_ref, d_ref, wglu_ref,
                  woa_ref, wos_ref, wout_ref, wpg_ref, wpp_ref, fg_ref,
                  y_ref, k_ref, v_ref, hr_ref, hi_ref,
                  kbuf, vbuf, xa_s, ubuf, cr_s, ci_s, hs, *, tt):
    t = pl.program_id(1)
    half = tt // 2

    @pl.when(t == 0)
    def _():
        kbuf[0:WINDOW, :] = jnp.zeros((WINDOW, KV2_WIDTH), BF16)
        vbuf[0:WINDOW, :] = jnp.zeros((WINDOW, KV2_WIDTH), BF16)
        _ssm_reset(ubuf, cr_s, ci_s)

    pj, so = [{}, {}], [{}, {}]

    def rows(h):
        return slice(h * half, (h + 1) * half)

    def proj(h):
        def store(name):
            if name == "u":
                ubuf[LAGS + h * half:LAGS + (h + 1) * half, :] = pj[h]["u"]
                return
            last_ref, buf = (k_ref, kbuf) if name == "k" else (v_ref, vbuf)
            if h == 1:
                last_ref[0] = pj[h][name][half - WINDOW:]
            for j in range(N_KV_HEADS):
                buf[WINDOW + h * half:WINDOW + (h + 1) * half, j * LANES:(j + 1) * LANES] = pj[h][name + "2"][j]

        return _proj_steps(lambda: x_ref[0, rows(h)], gain_ref[...],
                           lambda: (cos_ref[rows(h)], sina_ref[rows(h)], sinb_ref[rows(h)]),
                           w_in_ref, pj[h], store)

    def mid(h):
        def get_q(c, head):
            r0 = c * CHUNK - h * half
            return pj[h]["qh"][head][r0:r0 + CHUNK]

        def get_kv(c, kv):
            krows, cols = slice(c * CHUNK, c * CHUNK + KEYS), slice(kv * LANES, (kv + 1) * LANES)
            return kbuf[krows, cols], vbuf[krows, cols]

        def get_valid(c):
            if c * CHUNK >= WINDOW:
                return None
            in_seq = c * CHUNK + lax.broadcasted_iota(jnp.int32, (1, KEYS), 1) >= WINDOW
            return jnp.logical_or(in_seq, t > 0)

        def emit(c, tile, o):
            r0, cols = c * CHUNK - h * half, slice(tile * LANES, (tile + 1) * LANES)
            xa_s[c * CHUNK:(c + 1) * CHUNK, cols] = (o * pj[h]["sa"][r0:r0 + CHUNK, cols]).astype(BF16)

        def glu():
            so[h]["xs"] = _glu(so[h].pop("y"), pj[h]["u"], pj[h]["sz"], d_ref[...], wglu_ref)

        chunks = range(h * half // CHUNK, (h + 1) * half // CHUNK)
        return _spread(_attn_steps(sinks_ref, chunks, get_q, get_kv, get_valid, emit),
                       _ssm_steps(ubuf, h * half, half, cr_s, ci_s, hs, wlag_ref, a8r_ref, a8i_ref, ck_ref,
                                  hr_ref, hi_ref, so[h])
                       + [glu])

    def out(h):
        src = dict(xa=lambda: xa_s[rows(h)], xs=lambda: so[h]["xs"], ga=lambda: pj[h]["ga"],
                   gs=lambda: pj[h]["gs"], x=lambda: x_ref[0, rows(h)], p=lambda: p_ref[0, rows(h)])

        def emit(y):
            y_ref[0, rows(h)] = y

        return _out_steps(src, woa_ref, wos_ref, wout_ref, wpg_ref, wpp_ref, fg_ref[...], emit)

    _run(proj(0))
    _run(_spread(mid(0), proj(1)))
    _run(_spread(mid(1), out(0)))
    _run(out(1))

    kbuf[0:WINDOW, :] = kbuf[tt:tt + WINDOW, :]
    vbuf[0:WINDOW, :] = vbuf[tt:tt + WINDOW, :]
    ubuf[0:LAGS, :] = ubuf[tt:tt + LAGS, :]


def _layer_fused(x, p, tabs, wts, consts):
    b, t, _ = x.shape
    tt = min(LAYER_ROWS, t)
    assert t % tt == 0 and tt // 2 >= WINDOW and (tt // 2) % BF16_ROWS == 0 and tabs[0].shape[0] == t
    (gain, w_in, sinks, woa, d_skip, w_glu, wos, wout, wpg, wpp, fgain) = wts
    wlag, a8r, a8i, _, _, ck = consts

    def row_spec(w):
        return pl.BlockSpec((1, tt, w), lambda i, j: (i, j, 0))

    tab_spec = pl.BlockSpec((tt, LANES), lambda i, j: (j, 0))
    st_spec = pl.BlockSpec((1, PAIRS, LANES), lambda i, j: (i, 0, 0))
    win_spec = pl.BlockSpec((1, WINDOW, KV_WIDTH), lambda i, j: (i, 0, 0))
    consts_in = (gain, w_in, wlag, a8r, a8i, ck, d_skip, w_glu, woa, wos, wout, wpg, wpp, fgain)
    vmem = (sum(a.size * a.dtype.itemsize for a in consts_in)
            + 2 * tt * (2 * D_MODEL + PLE_DIM + 2 * KV_WIDTH + 3 * LANES) * 4
            + 2 * (WINDOW + tt) * KV2_WIDTH * 2 + tt * ATTN_WIDTH * 2 + (tt + LAGS) * SSM_WIDTH * 4
            + (tt // 2) * N_STATE * 2 + 3 * tt * IN_WIDTH * 4)
    y, k, v, hr, hi = pl.pallas_call(
        functools.partial(_layer_kernel, tt=tt),
        grid=(b, t // tt),
        in_specs=[pl.BlockSpec(memory_space=pltpu.SMEM), row_spec(D_MODEL), row_spec(PLE_DIM),
                  tab_spec, tab_spec, tab_spec]
                 + [_const_spec(a.shape) for a in consts_in],
        out_specs=[row_spec(D_MODEL), win_spec, win_spec, st_spec, st_spec],
        out_shape=[jax.ShapeDtypeStruct((b, t, D_MODEL), F32),
                   jax.ShapeDtypeStruct((b, WINDOW, KV_WIDTH), F32),
                   jax.ShapeDtypeStruct((b, WINDOW, KV_WIDTH), F32),
                   jax.ShapeDtypeStruct((b, PAIRS, LANES), F32),
                   jax.ShapeDtypeStruct((b, PAIRS, LANES), F32)],
        scratch_shapes=[pltpu.VMEM((WINDOW + tt, KV2_WIDTH), BF16),
                        pltpu.VMEM((WINDOW + tt, KV2_WIDTH), BF16),
                        pltpu.VMEM((tt, ATTN_WIDTH), BF16),
                        pltpu.VMEM((tt + LAGS, SSM_WIDTH), F32),
                        pltpu.VMEM((PAIRS, SUBLANES, LANES), F32),
                        pltpu.VMEM((PAIRS, SUBLANES, LANES), F32),
                        pltpu.VMEM((tt // 2, N_STATE), BF16)],
        compiler_params=_params(vmem, 2),
        name="layer_prompt",
    )(sinks, x, p, *tabs, *consts_in)
    return y, k, v, hr, hi


def _sample_kernel(sinks_ref, x_ref, p_ref, cos_ref, sina_ref, sinb_ref, kpre_ref, vpre_ref,
                   h0r_ref, h0i_ref, gain_ref, w_in_ref, wlag_ref, a8r_ref, a8i_ref, pr_ref, pi_ref,
                   ck_ref, d_ref, wglu_ref, woa_ref, wos_ref, wout_ref, wpg_ref, wpp_ref, fg_ref,
                   y_ref, k_ref, v_ref, hr_ref, hi_ref,
                   kbuf, vbuf, xa_s, ubuf, hs, *, n, t):
    rows = n * t
    pj, so = {}, {}

    def store(name):
        if name == "u":
            ubuf[0:LAGS, :] = jnp.zeros((LAGS, SSM_WIDTH), F32)
            ubuf[LAGS:, :] = pj["u"]
            return
        full_ref, pre_ref, buf = (k_ref, kpre_ref, kbuf) if name == "k" else (v_ref, vpre_ref, vbuf)
        full_ref[...] = pj[name]
        for s in range(n):
            buf[s, 0:WINDOW, :] = pre_ref[s]
            for j in range(N_KV_HEADS):
                buf[s, WINDOW:WINDOW + t, j * LANES:(j + 1) * LANES] = pj[name + "2"][j][s * t:(s + 1) * t]

    def get_q(c, head):
        return pj["qh"][head][c * CHUNK:(c + 1) * CHUNK]

    def get_kv(c, kv):
        cols = slice(kv * LANES, (kv + 1) * LANES)
        return kbuf[c, :, cols], vbuf[c, :, cols]

    def emit(c, tile, o):
        r, cols = slice(c * CHUNK, (c + 1) * CHUNK), slice(tile * LANES, (tile + 1) * LANES)
        xa_s[r, cols] = (o * pj["sa"][r, cols]).astype(BF16)

    def glu():
        so["xs"] = _glu(so.pop("y"), pj["u"], pj["sz"], d_ref[...], wglu_ref)

    src = dict(xa=lambda: xa_s[...], xs=lambda: so["xs"], ga=lambda: pj["ga"], gs=lambda: pj["gs"],
               x=lambda: x_ref[...], p=lambda: p_ref[...])

    def emit_y(y):
        y_ref[...] = y

    _run(_proj_steps(lambda: x_ref[...], gain_ref[...],
                     lambda: (cos_ref[...], sina_ref[...], sinb_ref[...]), w_in_ref, pj, store))
    _run(_spread(_attn_steps(sinks_ref, range(n), get_q, get_kv, lambda c: None, emit),
                 _ssm_steps(ubuf, 0, rows, None, None, hs, wlag_ref, a8r_ref, a8i_ref, ck_ref, hr_ref, hi_ref, so,
                            seg=(t, h0r_ref, h0i_ref, pr_ref, pi_ref)) + [glu]))
    _run(_out_steps(src, woa_ref, wos_ref, wout_ref, wpg_ref, wpp_ref, fg_ref[...], emit_y))


def _layer_sample(x, p, tabs, k_prefix, v_prefix, h0r, h0i, wts, consts):
    n, t, _ = x.shape
    assert t == CHUNK and tabs[0].shape[0] == n * t
    rows = n * t
    (gain, w_in, sinks, woa, d_skip, w_glu, wos, wout, wpg, wpp, fgain) = wts
    wlag, a8r, a8i, pr, pi, ck = consts
    operands = (x.reshape(rows, D_MODEL), p.reshape(rows, PLE_DIM), *tabs,
                _both_halves(k_prefix), _both_halves(v_prefix), h0r, h0i,
                gain, w_in, wlag, a8r, a8i, pr, pi, ck, d_skip, w_glu, woa, wos, wout, wpg, wpp, fgain)
    out_shapes = [(rows, D_MODEL), (rows, KV_WIDTH), (rows, KV_WIDTH), h0r.shape, h0r.shape]
    scratch = [((n, KEYS, KV2_WIDTH), BF16), ((n, KEYS, KV2_WIDTH), BF16), ((rows, ATTN_WIDTH), BF16),
               ((rows + LAGS, SSM_WIDTH), F32), ((rows, N_STATE), BF16)]
    vmem = (sum(a.size * a.dtype.itemsize for a in operands)
            + sum(int(np.prod(s)) * 4 for s in out_shapes)
            + sum(int(np.prod(s)) * np.dtype(d).itemsize for s, d in scratch)
            + 3 * rows * IN_WIDTH * 4)
    y, k, v, hr, hi = pl.pallas_call(
        functools.partial(_sample_kernel, n=n, t=t),
        grid=(1,),
        in_specs=[pl.BlockSpec(memory_space=pltpu.SMEM)] + [_const_spec(a.shape) for a in operands],
        out_specs=[pl.BlockSpec(s, lambda i, nd=len(s): (0,) * nd) for s in out_shapes],
        out_shape=[jax.ShapeDtypeStruct(s, F32) for s in out_shapes],
        scratch_shapes=[pltpu.VMEM(s, d) for s, d in scratch],
        compiler_params=_params(vmem, 1),
        name="layer_sample",
    )(sinks, *operands)
    k_new = jnp.concatenate([k_prefix[:, t:], k.reshape(n, t, KV_WIDTH)], axis=1)
    v_new = jnp.concatenate([v_prefix[:, t:], v.reshape(n, t, KV_WIDTH)], axis=1)
    return y.reshape(n, t, D_MODEL), k_new, v_new, hr, hi


def _both_halves(a):
    h0, h1 = a[..., :HEAD_DIM], a[..., HEAD_DIM:]
    return jnp.concatenate([h0, h0, h1, h1], axis=-1).astype(BF16)


def _ssm_constants(a_re, a_im, log_dt, b_re, b_im, c_re, c_im):
    dt = jnp.exp(log_dt.astype(F32))[:, None]
    lr = a_re.astype(F32).reshape(PAIRS, 1, 1, LANES)
    li = a_im.astype(F32).reshape(PAIRS, 1, 1, LANES)
    xr = (a_re.astype(F32) * dt).reshape(PAIRS, 1, 1, LANES)
    xi = (a_im.astype(F32) * dt).reshape(PAIRS, 1, 1, LANES)

    def apow(n):
        mag = jnp.exp(xr * n)
        return mag * jnp.cos(xi * n), mag * jnp.sin(xi * n)

    ar, ai = apow(1.0)
    nr, ni = ar - 1.0, ai
    den = lr * lr + li * li
    fr, fi = (nr * lr + ni * li) / den, (ni * lr - nr * li) / den

    n_slots = PAIR_K // SLOT
    qq, hi_ = np.arange(PAIRS)[:, None], np.arange(n_slots)[None, :]
    lag_tab = (PAIRS_PER_TILE * (hi_ // PAIRS_PER_TILE) + (hi_ % PAIRS_PER_TILE - qq) % PAIRS_PER_TILE)
    lag_tab = lag_tab.astype(np.float32)[:, :, None, None]
    same_group = (np.arange(SLOT)[:, None] // SSM_GROUP == np.arange(LANES)[None, :] // SSM_STATE)
    same_group = same_group.astype(np.float32)

    def b_rows(bm):
        t = jnp.transpose(bm.astype(F32).reshape(PAIRS, 2, SSM_STATE, SSM_GROUP), (0, 3, 1, 2))
        t = t.reshape(PAIRS, 1, 1, SSM_GROUP, LANES)
        t = jnp.broadcast_to(t, (PAIRS, 1, 2, SSM_GROUP, LANES))
        return t.reshape(PAIRS, 1, SLOT, LANES) * same_group

    br, bi = b_rows(b_re), b_rows(b_im)
    bbr, bbi = fr * br - fi * bi, fr * bi + fi * br
    er, ei = apow(lag_tab)
    wlag = jnp.concatenate([(er * bbr - ei * bbi).reshape(PAIRS, PAIR_K, LANES),
                            (er * bbi + ei * bbr).reshape(PAIRS, PAIR_K, LANES)], axis=-1).astype(BF16)

    a8r, a8i = (a.reshape(PAIRS, LANES) for a in apow(float(LAGS)))
    pwr, pwi = (a.reshape(PAIRS, LAGS, LANES)
                for a in apow(np.arange(1, LAGS + 1, dtype=np.float32)[None, :, None, None]))

    def c_cols(c):
        t = jnp.transpose(c.astype(F32).reshape(U_TILES, LANES // SSM_GROUP, SSM_GROUP, SSM_STATE),
                          (0, 3, 1, 2))
        return t.reshape(U_TILES, 1, 1, 1, SSM_STATE, LANES)

    cols_group = np.arange(LANES) // SSM_GROUP
    rows_group = 2 * np.arange(PAIRS_PER_TILE)[:, None] + np.arange(2)[None, :]
    c_mask = (rows_group[:, None, :, None, None] == cols_group[None, None, None, None, :])
    c_mask = c_mask.astype(np.float32)[None]
    ck = jnp.concatenate([c_cols(c_re) * c_mask, -c_cols(c_im) * c_mask], axis=2)
    ck = ck.reshape(U_TILES, PAIRS_PER_TILE * PAIR_N, LANES).astype(BF16)
    return wlag, a8r, a8i, pwr, pwi, ck


def _rope_tables(pos0, t, rows):
    half = ROT_DIM // 2
    d = np.arange(LANES) % HEAD_DIM
    inv = jnp.power(ROPE_THETA, -jnp.arange(half, dtype=F32) * 2.0 / ROT_DIM)
    pos = (pos0 + jnp.arange(t)).astype(F32)
    ang = pos[:, None] * inv[None, :]
    cos, sin = (jnp.tile(a, (1, LANES // half)) for a in (jnp.cos(ang), jnp.sin(ang)))
    cos_t = jnp.where((d < ROT_DIM)[None, :], cos, 1.0)
    sina = jnp.where(((d >= half) & (d < ROT_DIM))[None, :], sin, 0.0)
    sinb = jnp.where((d < half)[None, :], -sin, 0.0)
    reps = (max(rows // t, 1), 1)
    return tuple(jnp.tile(a, reps) for a in (cos_t, sina, sinb))


def kernel(x_prompt, x_sample, p_prompt, p_sample, cache_attn_k, cache_attn_v, state_ssm_re,
           state_ssm_im, norm_gain, w_in, attn_sinks, w_o_attn, ssm_a_re, ssm_a_im, ssm_log_dt,
           ssm_b_re, ssm_b_im, ssm_c_re, ssm_c_im, ssm_d, ssm_w_glu, w_o_ssm, w_out,
           w_ple_gate, w_ple_proj, final_norm_gain):
    assert norm_gain.shape[0] == 1, "single-layer model"
    bp, tp, _ = x_prompt.shape
    bs, ts, _ = x_sample.shape
    wts = (norm_gain[0].reshape(1, D_MODEL).astype(F32), w_in[0].astype(BF16),
           attn_sinks[0].astype(F32), w_o_attn[0].astype(BF16),
           ssm_d[0].reshape(1, SSM_WIDTH).astype(F32), ssm_w_glu[0].astype(BF16),
           w_o_ssm[0].astype(BF16), w_out[0].astype(BF16), w_ple_gate[0].astype(BF16),
           w_ple_proj[0].astype(BF16), final_norm_gain.reshape(1, D_MODEL).astype(F32))
    consts = _ssm_constants(ssm_a_re[0], ssm_a_im[0], ssm_log_dt[0], ssm_b_re[0], ssm_b_im[0],
                            ssm_c_re[0], ssm_c_im[0])

    y_p, k_p, v_p, hr_p, hi_p = _layer_fused(x_prompt, p_prompt[0], _rope_tables(0, tp, tp), wts, consts)

    ck = cache_attn_k[0].reshape(bs, WINDOW, KV_WIDTH).astype(F32)
    cv = cache_attn_v[0].reshape(bs, WINDOW, KV_WIDTH).astype(F32)
    h0r = state_ssm_re[0].reshape(bs, PAIRS, LANES).astype(F32)
    h0i = state_ssm_im[0].reshape(bs, PAIRS, LANES).astype(F32)
    tabs_s = _rope_tables(PAST_LEN, ts, bs * ts)
    y_s, k_s, v_s, hr_s, hi_s = _layer_sample(x_sample, p_sample[0], tabs_s, ck, cv, h0r, h0i, wts, consts)

    def kv_out(a, b):
        return a.reshape(1, b, WINDOW, N_KV_HEADS, HEAD_DIM)

    def st_out(a, b):
        return a.reshape(1, b, SSM_GROUPS, SSM_STATE)

    return (y_p, y_s, kv_out(k_p, bp), kv_out(v_p, bp),
            st_out(hr_p, bp), st_out(hi_p, bp), kv_out(k_s, bs), kv_out(v_s, bs),
            st_out(hr_s, bs), st_out(hi_s, bs))
```

```python
import functools

import numpy as np
import jax
import jax.numpy as jnp
from jax import lax
from jax.experimental import pallas as pl
from jax.experimental.pallas import tpu as pltpu

F32 = jnp.float32
BF16 = jnp.bfloat16

LANES = 128
SUBLANES = 8
V7X_VMEM_BYTES = 64 * 1024 * 1024

D_MODEL = 1024
CHUNK = 64
WINDOW = 128
N_HEADS = 8
N_KV_HEADS = 2
HEAD_DIM = 64
Q_PER_KV = N_HEADS // N_KV_HEADS
LOG2E = 1.4426950408889634
Q_SCALE = HEAD_DIM ** -0.5 * LOG2E
ATTN_WIDTH = N_HEADS * HEAD_DIM
KV_WIDTH = N_KV_HEADS * HEAD_DIM
ROT_DIM = HEAD_DIM // 4
ROPE_THETA = 500000.0
SSM_WIDTH = D_MODEL // 2
SSM_GROUP = 16
SSM_GROUPS = SSM_WIDTH // SSM_GROUP
SSM_STATE = 64
PLE_DIM = 256
PAST_LEN = 1024
EPS = 1e-6

O_Q = 0
O_K = O_Q + ATTN_WIDTH
O_V = O_K + KV_WIDTH
O_ZA = O_V + KV_WIDTH
O_U = O_ZA + ATTN_WIDTH
O_ZS = O_U + SSM_WIDTH
O_GA = O_ZS + SSM_WIDTH
O_GS = O_GA + D_MODEL
IN_WIDTH = O_GS + D_MODEL

QM_WIDTH = N_HEADS * LANES
KV2_WIDTH = N_KV_HEADS * LANES
KEYS = WINDOW + CHUNK
LAGS = SUBLANES
PAIRS = SSM_GROUPS // 2
PAIR_K = 2 * LAGS * SSM_GROUP
PAIR_N = 2 * 2 * SSM_STATE
N_STATE = PAIRS * PAIR_N
U_TILES = SSM_WIDTH // LANES
PAIRS_PER_TILE = PAIRS // U_TILES
SLOT = 2 * SSM_GROUP
BF16_ROWS = 2 * SUBLANES
assert PAIRS_PER_TILE == 4 and LAGS == 2 * PAIRS_PER_TILE

LAYER_ROWS = 512


def _sigmoid(x):
    return 1.0 / (1.0 + jnp.exp2(x * (-LOG2E)))


def _const_spec(shape):
    zeros = (0,) * len(shape)
    return pl.BlockSpec(shape, lambda *_: zeros, pipeline_mode=pl.Buffered(1))


def _params(vmem_bytes, n_grid):
    return pltpu.CompilerParams(
        dimension_semantics=("arbitrary",) * n_grid,
        vmem_limit_bytes=min(int(vmem_bytes), V7X_VMEM_BYTES - 8 * 1024 * 1024),
    )


def _run(steps):
    for step in steps:
        step()


def _spread(main, other):
    merged, j = [], 0
    for i, step in enumerate(main):
        while j < len(other) and j * len(main) <= i * len(other):
            merged.append(other[j])
            j += 1
        merged.append(step)
    return merged + other[j:]


def _proj_steps(get_x, gain, get_tabs, w_ref, o, store=None):
    st = {}

    def norm():
        x = get_x()
        ms = jnp.mean(x * x, axis=-1, keepdims=True)
        st["xn"] = (x * lax.rsqrt(ms + EPS) * gain).astype(BF16)
        st["lo"] = lax.broadcasted_iota(jnp.int32, (x.shape[0], LANES), 1) < HEAD_DIM

    def seg(a, b):
        return jnp.dot(st["xn"], w_ref[:, a:b], preferred_element_type=F32)

    def rope(t):
        cos, sina, sinb = get_tabs()
        return (t * cos + pltpu.roll(t, ROT_DIM // 2, 1) * sina
                + pltpu.roll(t, LANES - ROT_DIM // 2, 1) * sinb)

    def both_halves(t):
        tr = pltpu.roll(t, HEAD_DIM, 1)
        return [jnp.where(st["lo"], t, tr).astype(BF16), jnp.where(st["lo"], tr, t).astype(BF16)]

    def done(name):
        if store is not None:
            store(name)

    def q():
        zq = seg(O_Q, O_K)
        o["qh"] = []
        for j in range(ATTN_WIDTH // LANES):
            qt = rope(zq[:, j * LANES:(j + 1) * LANES]) * Q_SCALE
            o["qh"] += [jnp.where(st["lo"], qt, 0.0).astype(BF16),
                        jnp.where(st["lo"], 0.0, qt).astype(BF16)]

    def kv():
        z = seg(O_K, O_ZA)
        o["k"] = rope(z[:, :KV_WIDTH])
        o["k2"] = both_halves(o["k"])
        done("k")
        o["v"] = z[:, KV_WIDTH:]
        o["v2"] = both_halves(o["v"])
        done("v")

    def za():
        z = seg(O_ZA, O_U)
        o["sa"] = z * _sigmoid(z)

    def u():
        o["u"] = seg(O_U, O_ZS)
        done("u")

    def zs():
        z = seg(O_ZS, O_GA)
        o["sz"] = z * _sigmoid(z)

    def ga():
        o["ga"] = _sigmoid(seg(O_GA, O_GS))

    def gs():
        o["gs"] = _sigmoid(seg(O_GS, IN_WIDTH))

    return [norm, q, kv, za, u, zs, ga, gs]


def _attn_steps(sinks_ref, chunks, get_q, get_kv, get_valid, emit):
    nt = (((1,), (1,)), ((), ()))
    units = [(c, kv) for c in chunks for kv in range(N_KV_HEADS)]
    n = len(units)
    st = {}

    def scores(c, kv):
        k2, v2 = get_kv(c, kv)
        qm = jnp.concatenate([get_q(c, kv * Q_PER_KV + h) for h in range(Q_PER_KV)], axis=0)
        s = lax.dot_general(qm, k2, nt, preferred_element_type=F32)
        valid = get_valid(c)
        if valid is not None:
            s = jnp.where(valid, s, -jnp.inf)
        return s, v2

    def softmax(s, kv):
        head_row = lax.broadcasted_iota(jnp.int32, (Q_PER_KV * CHUNK, 1), 0) // CHUNK
        sk = [sinks_ref[kv * Q_PER_KV + h] * LOG2E for h in range(Q_PER_KV)]
        sink = jnp.where(head_row == 0, sk[0],
                         jnp.where(head_row == 1, sk[1], jnp.where(head_row == 2, sk[2], sk[3])))
        m = jnp.maximum(jnp.max(s, axis=1, keepdims=True), sink)
        return jnp.exp2(s - m).astype(BF16), jnp.exp2(sink - m)

    def output(e, sink_term, v2, c, kv):
        lo_q = lax.broadcasted_iota(jnp.int32, (CHUNK, LANES), 1) < HEAD_DIM
        ones = jnp.ones((KEYS, LANES), BF16)
        pv = jnp.dot(e, jnp.concatenate([v2, ones], axis=1), preferred_element_type=F32)
        o = pv[:, :LANES] / (pv[:, LANES:] + sink_term)
        for j in range(Q_PER_KV // 2):
            even = o[2 * j * CHUNK:(2 * j + 1) * CHUNK]
            odd = o[(2 * j + 1) * CHUNK:(2 * j + 2) * CHUNK]
            emit(c, kv * (Q_PER_KV // 2) + j, jnp.where(lo_q, even, odd))

    def make(i):
        def step():
            if i < n:
                st[i] = scores(*units[i])
            if 0 <= i - 1 < n:
                s, v2 = st[i - 1]
                st[i - 1] = softmax(s, units[i - 1][1]) + (v2,)
            if 0 <= i - 2 < n:
                e, den, v2 = st.pop(i - 2)
                output(e, den, v2, *units[i - 2])
        return step

    return [make(i) for i in range(n + 2)]


def _ssm_reset(ubuf, cr_s, ci_s):
    ubuf[0:LAGS, :] = jnp.zeros((LAGS, SSM_WIDTH), F32)
    cr_s[...] = jnp.zeros(cr_s.shape, F32)
    ci_s[...] = jnp.zeros(ci_s.shape, F32)


def _ssm_steps(ubuf, row0, tt, cr_s, ci_s, hs, wlag_ref, a8r_ref, a8i_ref, ck_ref, hr_ref, hi_ref, o,
               seg=None):
    st = {"ys": []}

    def setup():
        if seg is not None:
            st["row_in_seg"] = lax.broadcasted_iota(jnp.int32, (tt, LANES), 0) % seg[0]
        slot = lax.broadcasted_iota(jnp.int32, (tt, LANES), 1) // SLOT
        st["to_low"] = [((slot + PAIRS_PER_TILE - s) % PAIRS_PER_TILE) < 2 for s in range(2)]
        st["same_parity"] = [((slot + sg) % 2) == 0 for sg in range(2)]

    def route(r):
        low = [jnp.where(st["to_low"][s], r[s], r[s + 2]) for s in range(2)]
        high = [jnp.where(st["to_low"][s], r[s + 2], r[s]) for s in range(2)]
        return [jnp.where(st["same_parity"][sg % 2], src[0], src[1])
                for sg, src in zip(range(PAIRS_PER_TILE), (low, low, high, high))]

    def lag_copies(k):
        def step():
            ub = ubuf[row0:row0 + LAGS + tt, k * LANES:(k + 1) * LANES]
            rolled = []
            for s in range(LAGS):
                us = ub[LAGS:] if s == 0 else pltpu.roll(ub, s, 0)[LAGS:]
                if seg is not None and s > 0:
                    us = jnp.where(st["row_in_seg"] >= s, us, 0.0)
                if s % PAIRS_PER_TILE:
                    us = pltpu.roll(us, SLOT * (s % PAIRS_PER_TILE), 1)
                rolled.append(us)
            st["halves"] = (route(rolled[:PAIRS_PER_TILE]), route(rolled[PAIRS_PER_TILE:]))
        return step

    def pair(k, sg):
        def step():
            q = k * PAIRS_PER_TILE + sg
            xl = jnp.concatenate([st["halves"][0][sg], st["halves"][1][sg]], axis=1).astype(BF16)
            w = jnp.dot(xl, wlag_ref[q], preferred_element_type=F32)
            ar, ai = a8r_ref[q:q + 1, :], a8i_ref[q:q + 1, :]
            if seg is None:
                cr, ci = cr_s[q], ci_s[q]
            else:
                seg_rows, h0r_ref, h0i_ref, pr_ref, pi_ref = seg
            for b2 in range(tt // BF16_ROWS):
                hrs, his = [], []
                for b in (2 * b2, 2 * b2 + 1):
                    if seg is not None and (b * SUBLANES) % seg_rows == 0:
                        n = b * SUBLANES // seg_rows
                        h0r, h0i = h0r_ref[n, q:q + 1, :], h0i_ref[n, q:q + 1, :]
                        cr = pr_ref[q] * h0r - pi_ref[q] * h0i
                        ci = pr_ref[q] * h0i + pi_ref[q] * h0r
                    blk = slice(b * SUBLANES, (b + 1) * SUBLANES)
                    hr = w[blk, :LANES] + cr
                    hi = w[blk, LANES:] + ci
                    cr = ar * hr - ai * hi
                    ci = ar * hi + ai * hr
                    hrs.append(hr)
                    his.append(hi)
                    if seg is not None and ((b + 1) * SUBLANES) % seg_rows == 0:
                        n = b * SUBLANES // seg_rows
                        hr_ref[n, q:q + 1, :] = hr[SUBLANES - 1:, :]
                        hi_ref[n, q:q + 1, :] = hi[SUBLANES - 1:, :]
                blk2 = slice(b2 * BF16_ROWS, (b2 + 1) * BF16_ROWS)
                hs[blk2, q * PAIR_N:q * PAIR_N + LANES] = jnp.concatenate(hrs, axis=0).astype(BF16)
                hs[blk2, q * PAIR_N + LANES:(q + 1) * PAIR_N] = jnp.concatenate(his, axis=0).astype(BF16)
            if seg is None:
                cr_s[q] = cr
                ci_s[q] = ci
                hr_ref[0, q:q + 1, :] = hr[SUBLANES - 1:, :]
                hi_ref[0, q:q + 1, :] = hi[SUBLANES - 1:, :]
        return step

    def c_proj(k):
        def step():
            cols = slice(k * PAIRS_PER_TILE * PAIR_N, (k + 1) * PAIRS_PER_TILE * PAIR_N)
            st["ys"].append(jnp.dot(hs[:, cols], ck_ref[k], preferred_element_type=F32))
            if k == U_TILES - 1:
                o["y"] = jnp.concatenate(st["ys"], axis=1)
        return step

    steps = [setup]
    for k in range(U_TILES):
        steps += [lag_copies(k)] + [pair(k, sg) for sg in range(PAIRS_PER_TILE)] + [c_proj(k)]
    return steps


def _glu(y, u, sz, d, wglu_ref):
    z = jax.nn.gelu(y + d * u)
    g = jnp.dot(z.astype(BF16), wglu_ref[...], preferred_element_type=F32)
    return z * _sigmoid(g) * sz


def _out_steps(src, woa_ref, wos_ref, wout_ref, wpg_ref, wpp_ref, fgain, emit):
    st = {}

    def mm(a, w_ref):
        return jnp.dot(a.astype(BF16), w_ref[...], preferred_element_type=F32)

    def branches():
        st["merged"] = src["ga"]() * mm(src["xa"](), woa_ref) + src["gs"]() * mm(src["xs"](), wos_ref)

    def residual():
        st["h"] = src["x"]() + mm(st.pop("merged"), wout_ref)

    def embed_gate():
        h = st.pop("h")
        st["h"] = h + _sigmoid(mm(h, wpg_ref)) * mm(src["p"](), wpp_ref)

    def norm():
        h = st.pop("h")
        ms = jnp.mean(h * h, axis=-1, keepdims=True)
        emit(h * lax.rsqrt(ms + EPS) * fgain)

    return [branches, residual, embed_gate, norm]


def _layer_kernel(sinks_ref, x_ref, p_ref, cos_ref, sina_ref, sinb_ref, gain_ref, w_in_ref,
                  wlag_ref, a8r_ref, a8i_ref, ck_ref, d_ref, wglu_ref,
                  woa_ref, wos_ref, wout_ref, wpg_ref, wpp_ref, fg_ref,
                  y_ref, k_ref, v_ref, hr_ref, hi_ref,
                  kbuf, vbuf, xa_s, ubuf, cr_s, ci_s, hs, *, tt):
    t = pl.program_id(1)
    half = tt // 2

    @pl.when(t == 0)
    def _():
        kbuf[0:WINDOW, :] = jnp.zeros((WINDOW, KV2_WIDTH), BF16)
        vbuf[0:WINDOW, :] = jnp.zeros((WINDOW, KV2_WIDTH), BF16)
        _ssm_reset(ubuf, cr_s, ci_s)

    pj, so = [{}, {}], [{}, {}]

    def rows(h):
        return slice(h * half, (h + 1) * half)

    def proj(h):
        def store(name):
            if name == "u":
                ubuf[LAGS + h * half:LAGS + (h + 1) * half, :] = pj[h]["u"]
                return
            last_ref, buf = (k_ref, kbuf) if name == "k" else (v_ref, vbuf)
            if h == 1:
                last_ref[0] = pj[h][name][half - WINDOW:]
            for j in range(N_KV_HEADS):
                buf[WINDOW + h * half:WINDOW + (h + 1) * half, j * LANES:(j + 1) * LANES] = pj[h][name + "2"][j]

        return _proj_steps(lambda: x_ref[0, rows(h)], gain_ref[...],
                           lambda: (cos_ref[rows(h)], sina_ref[rows(h)], sinb_ref[rows(h)]),
                           w_in_ref, pj[h], store)

    def mid(h):
        def get_q(c, head):
            r0 = c * CHUNK - h * half
            return pj[h]["qh"][head][r0:r0 + CHUNK]

        def get_kv(c, kv):
            krows, cols = slice(c * CHUNK, c * CHUNK + KEYS), slice(kv * LANES, (kv + 1) * LANES)
            return kbuf[krows, cols], vbuf[krows, cols]

        def get_valid(c):
            if c * CHUNK >= WINDOW:
                return None
            in_seq = c * CHUNK + lax.broadcasted_iota(jnp.int32, (1, KEYS), 1) >= WINDOW
            return jnp.logical_or(in_seq, t > 0)

        def emit(c, tile, o):
            r0, cols = c * CHUNK - h * half, slice(tile * LANES, (tile + 1) * LANES)
            xa_s[c * CHUNK:(c + 1) * CHUNK, cols] = (o * pj[h]["sa"][r0:r0 + CHUNK, cols]).astype(BF16)

        def glu():
            so[h]["xs"] = _glu(so[h].pop("y"), pj[h]["u"], pj[h]["sz"], d_ref[...], wglu_ref)

        chunks = range(h * half // CHUNK, (h + 1) * half // CHUNK)
        return _spread(_attn_steps(sinks_ref, chunks, get_q, get_kv, get_valid, emit),
                       _ssm_steps(ubuf, h * half, half, cr_s, ci_s, hs, wlag_ref, a8r_ref, a8i_ref, ck_ref,
                                  hr_ref, hi_ref, so[h])
                       + [glu])

    def out(h):
        src = dict(xa=lambda: xa_s[rows(h)], xs=lambda: so[h]["xs"], ga=lambda: pj[h]["ga"],
                   gs=lambda: pj[h]["gs"], x=lambda: x_ref[0, rows(h)], p=lambda: p_ref[0, rows(h)])

        def emit(y):
            y_ref[0, rows(h)] = y

        return _out_steps(src, woa_ref, wos_ref, wout_ref, wpg_ref, wpp_ref, fg_ref[...], emit)

    (head0, gates0), (head1, gates1) = ((steps[:-2], steps[-2:]) for steps in (proj(0), proj(1)))
    _run(head0)
    _run(_spread(mid(0), gates0 + head1))
    _run(_spread(mid(1), gates1 + out(0)))
    _run(out(1))

    kbuf[0:WINDOW, :] = kbuf[tt:tt + WINDOW, :]
    vbuf[0:WINDOW, :] = vbuf[tt:tt + WINDOW, :]
    ubuf[0:LAGS, :] = ubuf[tt:tt + LAGS, :]


def _layer_fused(x, p, tabs, wts, consts):
    b, t, _ = x.shape
    tt = min(LAYER_ROWS, t)
    assert t % tt == 0 and tt // 2 >= WINDOW and (tt // 2) % BF16_ROWS == 0 and tabs[0].shape[0] == t
    (gain, w_in, sinks, woa, d_skip, w_glu, wos, wout, wpg, wpp, fgain) = wts
    wlag, a8r, a8i, _, _, ck = consts

    def row_spec(w):
        return pl.BlockSpec((1, tt, w), lambda i, j: (i, j, 0))

    tab_spec = pl.BlockSpec((tt, LANES), lambda i, j: (j, 0))
    st_spec = pl.BlockSpec((1, PAIRS, LANES), lambda i, j: (i, 0, 0))
    win_spec = pl.BlockSpec((1, WINDOW, KV_WIDTH), lambda i, j: (i, 0, 0))
    consts_in = (gain, w_in, wlag, a8r, a8i, ck, d_skip, w_glu, woa, wos, wout, wpg, wpp, fgain)
    vmem = (sum(a.size * a.dtype.itemsize for a in consts_in)
            + 2 * tt * (2 * D_MODEL + PLE_DIM + 2 * KV_WIDTH + 3 * LANES) * 4
            + 2 * (WINDOW + tt) * KV2_WIDTH * 2 + tt * ATTN_WIDTH * 2 + (tt + LAGS) * SSM_WIDTH * 4
            + (tt // 2) * N_STATE * 2 + 3 * tt * IN_WIDTH * 4)
    y, k, v, hr, hi = pl.pallas_call(
        functools.partial(_layer_kernel, tt=tt),
        grid=(b, t // tt),
        in_specs=[pl.BlockSpec(memory_space=pltpu.SMEM), row_spec(D_MODEL), row_spec(PLE_DIM),
                  tab_spec, tab_spec, tab_spec]
                 + [_const_spec(a.shape) for a in consts_in],
        out_specs=[row_spec(D_MODEL), win_spec, win_spec, st_spec, st_spec],
        out_shape=[jax.ShapeDtypeStruct((b, t, D_MODEL), F32),
                   jax.ShapeDtypeStruct((b, WINDOW, KV_WIDTH), F32),
                   jax.ShapeDtypeStruct((b, WINDOW, KV_WIDTH), F32),
                   jax.ShapeDtypeStruct((b, PAIRS, LANES), F32),
                   jax.ShapeDtypeStruct((b, PAIRS, LANES), F32)],
        scratch_shapes=[pltpu.VMEM((WINDOW + tt, KV2_WIDTH), BF16),
                        pltpu.VMEM((WINDOW + tt, KV2_WIDTH), BF16),
                        pltpu.VMEM((tt, ATTN_WIDTH), BF16),
                        pltpu.VMEM((tt + LAGS, SSM_WIDTH), F32),
                        pltpu.VMEM((PAIRS, SUBLANES, LANES), F32),
                        pltpu.VMEM((PAIRS, SUBLANES, LANES), F32),
                        pltpu.VMEM((tt // 2, N_STATE), BF16)],
        compiler_params=_params(vmem, 2),
        name="layer_prompt",
    )(sinks, x, p, *tabs, *consts_in)
    return y, k, v, hr, hi


def _sample_kernel(sinks_ref, x_ref, p_ref, cos_ref, sina_ref, sinb_ref, kpre_ref, vpre_ref,
                   h0r_ref, h0i_ref, gain_ref, w_in_ref, wlag_ref, a8r_ref, a8i_ref, pr_ref, pi_ref,
                   ck_ref, d_ref, wglu_ref, woa_ref, wos_ref, wout_ref, wpg_ref, wpp_ref, fg_ref,
                   y_ref, k_ref, v_ref, hr_ref, hi_ref,
                   kbuf, vbuf, xa_s, ubuf, hs, *, n, t):
    rows = n * t
    pj, so = {}, {}

    def store(name):
        if name == "u":
            ubuf[0:LAGS, :] = jnp.zeros((LAGS, SSM_WIDTH), F32)
            ubuf[LAGS:, :] = pj["u"]
            return
        full_ref, pre_ref, buf = (k_ref, kpre_ref, kbuf) if name == "k" else (v_ref, vpre_ref, vbuf)
        full_ref[...] = pj[name]
        for s in range(n):
            buf[s, 0:WINDOW, :] = pre_ref[s]
            for j in range(N_KV_HEADS):
                buf[s, WINDOW:WINDOW + t, j * LANES:(j + 1) * LANES] = pj[name + "2"][j][s * t:(s + 1) * t]

    def get_q(c, head):
        return pj["qh"][head][c * CHUNK:(c + 1) * CHUNK]

    def get_kv(c, kv):
        cols = slice(kv * LANES, (kv + 1) * LANES)
        return kbuf[c, :, cols], vbuf[c, :, cols]

    def emit(c, tile, o):
        r, cols = slice(c * CHUNK, (c + 1) * CHUNK), slice(tile * LANES, (tile + 1) * LANES)
        xa_s[r, cols] = (o * pj["sa"][r, cols]).astype(BF16)

    def glu():
        so["xs"] = _glu(so.pop("y"), pj["u"], pj["sz"], d_ref[...], wglu_ref)

    src = dict(xa=lambda: xa_s[...], xs=lambda: so["xs"], ga=lambda: pj["ga"], gs=lambda: pj["gs"],
               x=lambda: x_ref[...], p=lambda: p_ref[...])

    def emit_y(y):
        y_ref[...] = y

    _run(_proj_steps(lambda: x_ref[...], gain_ref[...],
                     lambda: (cos_ref[...], sina_ref[...], sinb_ref[...]), w_in_ref, pj, store))
    _run(_spread(_attn_steps(sinks_ref, range(n), get_q, get_kv, lambda c: None, emit),
                 _ssm_steps(ubuf, 0, rows, None, None, hs, wlag_ref, a8r_ref, a8i_ref, ck_ref, hr_ref, hi_ref, so,
                            seg=(t, h0r_ref, h0i_ref, pr_ref, pi_ref)) + [glu]))
    _run(_out_steps(src, woa_ref, wos_ref, wout_ref, wpg_ref, wpp_ref, fg_ref[...], emit_y))


def _layer_sample(x, p, tabs, k_prefix, v_prefix, h0r, h0i, wts, consts):
    n, t, _ = x.shape
    assert t == CHUNK and tabs[0].shape[0] == n * t
    rows = n * t
    (gain, w_in, sinks, woa, d_skip, w_glu, wos, wout, wpg, wpp, fgain) = wts
    wlag, a8r, a8i, pr, pi, ck = consts
    operands = (x.reshape(rows, D_MODEL), p.reshape(rows, PLE_DIM), *tabs,
                _both_halves(k_prefix), _both_halves(v_prefix), h0r, h0i,
                gain, w_in, wlag, a8r, a8i, pr, pi, ck, d_skip, w_glu, woa, wos, wout, wpg, wpp, fgain)
    out_shapes = [(rows, D_MODEL), (rows, KV_WIDTH), (rows, KV_WIDTH), h0r.shape, h0r.shape]
    scratch = [((n, KEYS, KV2_WIDTH), BF16), ((n, KEYS, KV2_WIDTH), BF16), ((rows, ATTN_WIDTH), BF16),
               ((rows + LAGS, SSM_WIDTH), F32), ((rows, N_STATE), BF16)]
    vmem = (sum(a.size * a.dtype.itemsize for a in operands)
            + sum(int(np.prod(s)) * 4 for s in out_shapes)
            + sum(int(np.prod(s)) * np.dtype(d).itemsize for s, d in scratch)
            + 3 * rows * IN_WIDTH * 4)
    y, k, v, hr, hi = pl.pallas_call(
        functools.partial(_sample_kernel, n=n, t=t),
        grid=(1,),
        in_specs=[pl.BlockSpec(memory_space=pltpu.SMEM)] + [_const_spec(a.shape) for a in operands],
        out_specs=[pl.BlockSpec(s, lambda i, nd=len(s): (0,) * nd) for s in out_shapes],
        out_shape=[jax.ShapeDtypeStruct(s, F32) for s in out_shapes],
        scratch_shapes=[pltpu.VMEM(s, d) for s, d in scratch],
        compiler_params=_params(vmem, 1),
        name="layer_sample",
    )(sinks, *operands)
    k_new = jnp.concatenate([k_prefix[:, t:], k.reshape(n, t, KV_WIDTH)], axis=1)
    v_new = jnp.concatenate([v_prefix[:, t:], v.reshape(n, t, KV_WIDTH)], axis=1)
    return y.reshape(n, t, D_MODEL), k_new, v_new, hr, hi


def _both_halves(a):
    h0, h1 = a[..., :HEAD_DIM], a[..., HEAD_DIM:]
    return jnp.concatenate([h0, h0, h1, h1], axis=-1).astype(BF16)


def _ssm_constants(a_re, a_im, log_dt, b_re, b_im, c_re, c_im):
    dt = jnp.exp(log_dt.astype(F32))[:, None]
    lr = a_re.astype(F32).reshape(PAIRS, 1, 1, LANES)
    li = a_im.astype(F32).reshape(PAIRS, 1, 1, LANES)
    xr = (a_re.astype(F32) * dt).reshape(PAIRS, 1, 1, LANES)
    xi = (a_im.astype(F32) * dt).reshape(PAIRS, 1, 1, LANES)

    def apow(n):
        mag = jnp.exp(xr * n)
        return mag * jnp.cos(xi * n), mag * jnp.sin(xi * n)

    ar, ai = apow(1.0)
    nr, ni = ar - 1.0, ai
    den = lr * lr + li * li
    fr, fi = (nr * lr + ni * li) / den, (ni * lr - nr * li) / den

    n_slots = PAIR_K // SLOT
    qq, hi_ = np.arange(PAIRS)[:, None], np.arange(n_slots)[None, :]
    lag_tab = (PAIRS_PER_TILE * (hi_ // PAIRS_PER_TILE) + (hi_ % PAIRS_PER_TILE - qq) % PAIRS_PER_TILE)
    lag_tab = lag_tab.astype(np.float32)[:, :, None, None]
    same_group = (np.arange(SLOT)[:, None] // SSM_GROUP == np.arange(LANES)[None, :] // SSM_STATE)
    same_group = same_group.astype(np.float32)

    def b_rows(bm):
        t = jnp.transpose(bm.astype(F32).reshape(PAIRS, 2, SSM_STATE, SSM_GROUP), (0, 3, 1, 2))
        t = t.reshape(PAIRS, 1, 1, SSM_GROUP, LANES)
        t = jnp.broadcast_to(t, (PAIRS, 1, 2, SSM_GROUP, LANES))
        return t.reshape(PAIRS, 1, SLOT, LANES) * same_group

    br, bi = b_rows(b_re), b_rows(b_im)
    bbr, bbi = fr * br - fi * bi, fr * bi + fi * br
    er, ei = apow(lag_tab)
    wlag = jnp.concatenate([(er * bbr - ei * bbi).reshape(PAIRS, PAIR_K, LANES),
                            (er * bbi + ei * bbr).reshape(PAIRS, PAIR_K, LANES)], axis=-1).astype(BF16)

    a8r, a8i = (a.reshape(PAIRS, LANES) for a in apow(float(LAGS)))
    pwr, pwi = (a.reshape(PAIRS, LAGS, LANES)
                for a in apow(np.arange(1, LAGS + 1, dtype=np.float32)[None, :, None, None]))

    def c_cols(c):
        t = jnp.transpose(c.astype(F32).reshape(U_TILES, LANES // SSM_GROUP, SSM_GROUP, SSM_STATE),
                          (0, 3, 1, 2))
        return t.reshape(U_TILES, 1, 1, 1, SSM_STATE, LANES)

    cols_group = np.arange(LANES) // SSM_GROUP
    rows_group = 2 * np.arange(PAIRS_PER_TILE)[:, None] + np.arange(2)[None, :]
    c_mask = (rows_group[:, None, :, None, None] == cols_group[None, None, None, None, :])
    c_mask = c_mask.astype(np.float32)[None]
    ck = jnp.concatenate([c_cols(c_re) * c_mask, -c_cols(c_im) * c_mask], axis=2)
    ck = ck.reshape(U_TILES, PAIRS_PER_TILE * PAIR_N, LANES).astype(BF16)
    return wlag, a8r, a8i, pwr, pwi, ck


def _rope_tables(pos0, t, rows):
    half = ROT_DIM // 2
    d = np.arange(LANES) % HEAD_DIM
    inv = jnp.power(ROPE_THETA, -jnp.arange(half, dtype=F32) * 2.0 / ROT_DIM)
    pos = (pos0 + jnp.arange(t)).astype(F32)
    ang = pos[:, None] * inv[None, :]
    cos, sin = (jnp.tile(a, (1, LANES // half)) for a in (jnp.cos(ang), jnp.sin(ang)))
    cos_t = jnp.where((d < ROT_DIM)[None, :], cos, 1.0)
    sina = jnp.where(((d >= half) & (d < ROT_DIM))[None, :], sin, 0.0)
    sinb = jnp.where((d < half)[None, :], -sin, 0.0)
    reps = (max(rows // t, 1), 1)
    return tuple(jnp.tile(a, reps) for a in (cos_t, sina, sinb))


def kernel(x_prompt, x_sample, p_prompt, p_sample, cache_attn_k, cache_attn_v, state_ssm_re,
           state_ssm_im, norm_gain, w_in, attn_sinks, w_o_attn, ssm_a_re, ssm_a_im, ssm_log_dt,
           ssm_b_re, ssm_b_im, ssm_c_re, ssm_c_im, ssm_d, ssm_w_glu, w_o_ssm, w_out,
           w_ple_gate, w_ple_proj, final_norm_gain):
    assert norm_gain.shape[0] == 1, "single-layer model"
    bp, tp, _ = x_prompt.shape
    bs, ts, _ = x_sample.shape
    wts = (norm_gain[0].reshape(1, D_MODEL).astype(F32), w_in[0].astype(BF16),
           attn_sinks[0].astype(F32), w_o_attn[0].astype(BF16),
           ssm_d[0].reshape(1, SSM_WIDTH).astype(F32), ssm_w_glu[0].astype(BF16),
           w_o_ssm[0].astype(BF16), w_out[0].astype(BF16), w_ple_gate[0].astype(BF16),
           w_ple_proj[0].astype(BF16), final_norm_gain.reshape(1, D_MODEL).astype(F32))
    consts = _ssm_constants(ssm_a_re[0], ssm_a_im[0], ssm_log_dt[0], ssm_b_re[0], ssm_b_im[0],
                            ssm_c_re[0], ssm_c_im[0])

    y_p, k_p, v_p, hr_p, hi_p = _layer_fused(x_prompt, p_prompt[0], _rope_tables(0, tp, tp), wts, consts)

    ck = cache_attn_k[0].reshape(bs, WINDOW, KV_WIDTH).astype(F32)
    cv = cache_attn_v[0].reshape(bs, WINDOW, KV_WIDTH).astype(F32)
    h0r = state_ssm_re[0].reshape(bs, PAIRS, LANES).astype(F32)
    h0i = state_ssm_im[0].reshape(bs, PAIRS, LANES).astype(F32)
    tabs_s = _rope_tables(PAST_LEN, ts, bs * ts)
    y_s, k_s, v_s, hr_s, hi_s = _layer_sample(x_sample, p_sample[0], tabs_s, ck, cv, h0r, h0i, wts, consts)

    def kv_out(a, b):
        return a.reshape(1, b, WINDOW, N_KV_HEADS, HEAD_DIM)

    def st_out(a, b):
        return a.reshape(1, b, SSM_GROUPS, SSM_STATE)

    return (y_p, y_s, kv_out(k_p, bp), kv_out(v_p, bp),
            st_out(hr_p, bp), st_out(hi_p, bp), kv_out(k_s, bs), kv_out(v_s, bs),
            st_out(hr_s, bs), st_out(hi_s, bs))
```
